```python
import math
import jax
import jax.numpy as jnp
from jax import lax
import numpy as np

D_MODEL = 1024
BATCH = 16
SEQ = 256
DEPTH = 4
DEC_BATCH = 4
DEC_SEQ = 1024
PAST_LEN = 256

GRID_W = 64
BRANCH_W = D_MODEL // 2
N_BRANCH = 3
DA_HEADS = 4
DA_HEAD_DIM = BRANCH_W // (2 * DA_HEADS)
DA_V_DIM = 2 * DA_HEAD_DIM
ROPE_BASE = 10000.0
Q_BLOCK = 128
RG_WIDTH = BRANCH_W
RG_BLOCKS = 8
RG_BLOCK_W = RG_WIDTH // RG_BLOCKS
RG_CONV_W = 4
RG_C = 8.0
HG_HEADS = 4
HG_KEY = BRANCH_W // HG_HEADS
HG_VAL = BRANCH_W // HG_HEADS
HG_CHUNK = 16
N_EXPERTS = 32
TOP_K = 4
D_EXPERT = D_MODEL
SWIGLU_ALPHA = 1.702
SWIGLU_LIMIT = 7.0
DN_ALPHA = (2 * DEPTH) ** 0.25
DN_BETA = (8 * DEPTH) ** -0.25
NORM_EPS = 1e-5
_IN_SIZES = (BRANCH_W, BRANCH_W, BRANCH_W, RG_WIDTH, RG_WIDTH,
             HG_HEADS * HG_KEY, HG_HEADS * HG_KEY, HG_HEADS * HG_KEY, HG_HEADS * HG_VAL, HG_HEADS * HG_VAL,
             N_BRANCH * D_MODEL)
D_IN = sum(_IN_SIZES)
_IN_SPLITS = tuple(int(s) for s in np.cumsum(_IN_SIZES)[:-1])

kernel_name = 'hybrid_diffusion_step'

F32 = jnp.float32


def _layer_norm(x, g, b):
    xf = x.astype(F32)
    mu = jnp.mean(xf, -1, keepdims=True)
    var = jnp.mean(jnp.square(xf - mu), -1, keepdims=True)
    return ((xf - mu) * lax.rsqrt(var + NORM_EPS)).astype(x.dtype) * g + b


def _rms_norm(x, w):
    xf = x.astype(F32)
    y = xf * lax.rsqrt(jnp.mean(jnp.square(xf), -1, keepdims=True) + NORM_EPS)
    return y.astype(x.dtype) * w


def _adaln(cond, w, b):
    m = (jax.nn.silu(cond) @ w + b).reshape(cond.shape[0], 1, 6, D_MODEL)
    return [m[:, :, j] for j in range(6)]


def _rope_1d(x, pos):
    d = x.shape[-1]
    half = d // 2
    inv = ROPE_BASE ** (-jnp.arange(0, d, 2, dtype=F32) / d)
    ang = pos[:, None] * inv[None, :]
    shape = (1, pos.shape[0]) + (1,) * (x.ndim - 3) + (half,)
    cos = jnp.cos(ang).reshape(shape)
    sin = jnp.sin(ang).reshape(shape)
    x1 = x[..., :half].astype(F32)
    x2 = x[..., half:].astype(F32)
    return jnp.concatenate([x1 * cos - x2 * sin, x2 * cos + x1 * sin], -1).astype(x.dtype)


def _rope_2d(x):
    S = x.shape[1]
    rows = S // GRID_W
    row = jnp.repeat(jnp.arange(rows, dtype=F32), GRID_W)
    col = jnp.tile(jnp.arange(GRID_W, dtype=F32), rows)
    half = x.shape[-1] // 2
    return jnp.concatenate([_rope_1d(x[..., :half], row), _rope_1d(x[..., half:], col)], -1)


def _diff_attention(q, k, v, lam):
    B, Sq = q.shape[0], q.shape[1]
    nb = Sq // Q_BLOCK
    q_blocks = jnp.swapaxes(q.reshape((B, nb, Q_BLOCK) + q.shape[2:]), 0, 1)
    scale = DA_HEAD_DIM ** -0.5

    def block(qb):
        s = jnp.einsum('bqhmd,bkhmd->bhmqk', qb, k).astype(F32) * scale
        a = jax.nn.softmax(s, axis=-1)
        w = a[:, :, 0] - lam * a[:, :, 1]
        return jnp.einsum('bhqk,bkhe->bqhe', w.astype(v.dtype), v)

    out = lax.map(block, q_blocks)
    return jnp.swapaxes(out, 0, 1).reshape(B, Sq, DA_HEADS, DA_V_DIM)


def _depthwise_conv(x, w, b):
    left = RG_CONV_W // 2
    right = RG_CONV_W - 1 - left
    y = lax.conv_general_dilated(x, w[:, None, :].astype(x.dtype), window_strides=(1,),
                                 padding=[(left, right)], dimension_numbers=('NWC', 'WIO', 'NWC'),
                                 feature_group_count=x.shape[-1])
    return y + b


def _linear_scan(a, b, h0, reverse):
    def combine(left, right):
        return left[0] * right[0], right[0] * left[1] + right[1]
    a_cum, b_cum = lax.associative_scan(combine, (a, b), reverse=reverse, axis=1)
    h = a_cum * h0[:, None, :] + b_cum
    return h, (h[:, 0] if reverse else h[:, -1])


def _rglru(x, gate_w, gate_b, lam, h0, reverse):
    B, S, W = x.shape
    xb = x.reshape(B, S, RG_BLOCKS, RG_BLOCK_W)
    g = jnp.einsum('bsnc,gncd->gbsnd', xb, gate_w).reshape(2, B, S, W) + gate_b[:, None, None, :]
    g = g.astype(F32)
    r = jax.nn.sigmoid(g[0])
    i = jax.nn.sigmoid(g[1])
    log_a = -RG_C * jax.nn.softplus(-lam.astype(F32)) * r
    a = jnp.exp(log_a)
    b = jnp.sqrt(-jnp.expm1(2.0 * log_a)) * i * x.astype(F32)
    return _linear_scan(a, b, h0.astype(F32), reverse)


def _hgrn_lower_bounds(hg_lb):
    pr = jax.nn.softmax(hg_lb.astype(F32), axis=0)
    return jnp.cumsum(pr, axis=0) - pr[0]


def _hgrn2_gates(z, lb):
    B, S, _ = z.shape
    zf = z.astype(F32)
    log_f = jnp.log(lb + (1.0 - lb) * jax.nn.sigmoid(zf))
    k = (1.0 - lb) * jax.nn.sigmoid(-zf)
    shape = (B, S, HG_HEADS, HG_KEY)
    return k.reshape(shape), log_f.reshape(shape)


def _hgrn2_scan(q, k, v, log_f, s0, reverse):
    if reverse:
        o, s_fin = _hgrn2_scan(jnp.flip(q, 1), jnp.flip(k, 1), jnp.flip(v, 1), jnp.flip(log_f, 1), s0, False)
        return jnp.flip(o, 1), s_fin
    B, S, H, _ = q.shape
    n = S // HG_CHUNK

    def chunks(t):
        return t.astype(F32).reshape(B, n, HG_CHUNK, H, t.shape[-1]).transpose(0, 1, 3, 2, 4)

    qc, kc, vc, gc = chunks(q), chunks(k), chunks(v), chunks(log_f)
    bc = jnp.cumsum(gc, axis=3)
    causal = jnp.tril(jnp.ones((HG_CHUNK, HG_CHUNK), bool))[:, :, None]
    diff = bc[:, :, :, :, None, :] - bc[:, :, :, None, :, :]
    decay = jnp.exp(jnp.where(causal, diff, -jnp.inf))
    scores = jnp.einsum('bnhtk,bnhsk,bnhtsk->bnhts', qc, kc, decay)
    o_intra = jnp.einsum('bnhts,bnhsv->bnhtv', scores, vc)
    b_last = bc[:, :, :, -1:, :]
    kv = jnp.einsum('bnhck,bnhcv->bnhkv', kc * jnp.exp(b_last - bc), vc)
    chunk_decay = jnp.exp(b_last[:, :, :, 0, :])

    def step(state, inp):
        dec, kv_n = inp
        return dec[..., None] * state + kv_n, state

    s_fin, s_start = lax.scan(step, s0.astype(F32),
                              (jnp.swapaxes(chunk_decay, 0, 1), jnp.swapaxes(kv, 0, 1)))
    o_inter = jnp.einsum('bnhck,bnhkv->bnhcv', qc * jnp.exp(bc), jnp.swapaxes(s_start, 0, 1))
    o = (o_intra + o_inter).transpose(0, 1, 3, 2, 4).reshape(B, S, H, -1)
    return o, s_fin


def _token_mix(u, l, p, lb, ctx):
    B, S, _ = u.shape
    (aq, ak, av, rx, rgate, hq, hz_f, hz_b, hi, hgate, mg) = jnp.split(u @ p['w_in'][l], _IN_SPLITS, axis=-1)

    q = aq.reshape(B, S, DA_HEADS, 2, DA_HEAD_DIM)
    k = ak.reshape(B, S, DA_HEADS, 2, DA_HEAD_DIM)
    v = av.reshape(B, S, DA_HEADS, DA_V_DIM)
    lambda_init = 0.8 - 0.6 * math.exp(-0.3 * l)
    lq1, lk1, lq2, lk2 = p['da_lambda'][l].astype(F32)
    lam = jnp.exp(jnp.sum(lq1 * lk1)) - jnp.exp(jnp.sum(lq2 * lk2)) + lambda_init
    if ctx is None:
        q_att, k_att, v_att = q, k, v
        h0_rg = jnp.zeros((B, 2, RG_WIDTH), F32)
        s0_hg = jnp.zeros((B, 2, HG_HEADS, HG_KEY, HG_VAL), F32)
    else:
        k_ctx, v_ctx, h0_rg, s0_hg = ctx
        q_att = _rope_2d(q)
        k_att = jnp.concatenate([_rope_2d(k), k_ctx.astype(k.dtype)], axis=1)
        v_att = jnp.concatenate([v, v_ctx.astype(v.dtype)], axis=1)
    att = _diff_attention(q_att, k_att, v_att, lam)
    att = (_rms_norm(att, p['da_subln'][l]) * (1.0 - lambda_init)).reshape(B, S, BRANCH_W)

    xr = _depthwise_conv(rx, p['rg_conv_w'][l], p['rg_conv_b'][l])
    h_f, hl_f = _rglru(xr, p['rg_gate_w'][l, 0], p['rg_gate_b'][l, 0], p['rg_lambda'][l, 0], h0_rg[:, 0], False)
    h_b, hl_b = _rglru(xr, p['rg_gate_w'][l, 1], p['rg_gate_b'][l, 1], p['rg_lambda'][l, 1], h0_rg[:, 1], True)
    rg = (h_f + h_b).astype(u.dtype) * jax.nn.gelu(rgate)

    qh = jax.nn.silu(hq).reshape(B, S, HG_HEADS, HG_KEY)
    vh = hi.reshape(B, S, HG_HEADS, HG_VAL)
    kf, gf = _hgrn2_gates(hz_f, lb[0])
    o_f, sl_f = _hgrn2_scan(qh, kf, vh, gf, s0_hg[:, 0], False)
    kb, gb = _hgrn2_gates(hz_b, lb[1])
    o_b, sl_b = _hgrn2_scan(qh, kb, vh, gb, s0_hg[:, 1], True)
    hg = (_rms_norm(o_f + o_b, p['hg_norm'][l]).astype(u.dtype)
          * jax.nn.silu(hgate.reshape(B, S, HG_HEADS, HG_VAL))).reshape(B, S, BRANCH_W)

    branches = jnp.stack([att, rg, hg], axis=2)
    proj = jnp.einsum('bsnc,ncd->bsnd', branches, p['w_branch'][l])
    gates = jax.nn.sigmoid(mg.reshape(B, S, N_BRANCH, D_MODEL))
    out = jnp.sum(gates * proj, axis=2) @ p['w_out'][l]
    if ctx is None:
        return out, (k, v, jnp.stack([hl_f, hl_b], axis=1), jnp.stack([sl_f, sl_b], axis=1))
    return out, None


def _moe(u, l, p):
    B, S, D = u.shape
    t = u.reshape(-1, D)
    logits = (t @ p['router_w'][l] + p['router_b'][l]).astype(F32)
    top_v, top_i = lax.top_k(logits, TOP_K)
    wts = jax.nn.softmax(top_v, axis=-1)
    gate = jnp.sum(jax.nn.one_hot(top_i, N_EXPERTS, dtype=F32) * wts[..., None], axis=1).astype(u.dtype)
    h = jnp.einsum('td,edf->tef', t, p['w1'][l]) + p['b1'][l]
    glu = jnp.minimum(h[..., ::2], SWIGLU_LIMIT)
    lin = jnp.clip(h[..., 1::2], -SWIGLU_LIMIT, SWIGLU_LIMIT)
    act = glu * jax.nn.sigmoid(SWIGLU_ALPHA * glu) * (lin + 1.0)
    y = jnp.einsum('tef,te,efd->td', act, gate, p['w2'][l]) + gate @ p['b2'][l]
    return y.reshape(B, S, D)


def _trunk_layer(x, cond, l, p, lb, ctx):
    sh1, sc1, g1, sh2, sc2, g2 = _adaln(cond, p['w_ada'][l], p['b_ada'][l])
    mix, new_ctx = _token_mix(x * (1.0 + sc1) + sh1, l, p, lb, ctx)
    x = _layer_norm(DN_ALPHA * x + g1 * mix, p['ln1_g'][l], p['ln1_b'][l])
    ffn = _moe(x * (1.0 + sc2) + sh2, l, p)
    x = _layer_norm(DN_ALPHA * x + g2 * ffn, p['ln2_g'][l], p['ln2_b'][l])
    return x, new_ctx


def setup_inputs(seed: int = 0) -> dict:
    key = jax.random.key(seed)
    keys = iter(jax.random.split(key, 40))

    def nrm(shape, scale):
        return scale * jax.random.normal(next(keys), shape, F32)

    a0 = jax.random.uniform(next(keys), (DEPTH, 2, RG_WIDTH), F32, 0.9, 0.999)
    a_root = a0 ** (1.0 / RG_C)
    rg_lambda = jnp.log(a_root) - jnp.log1p(-a_root)
    return {
        'x_prompt': nrm((BATCH, SEQ, D_MODEL), 1.0),
        'x_sample': nrm((DEC_BATCH, DEC_SEQ, D_MODEL), 1.0),
        'cache_attn_k': nrm((DEC_BATCH, DEPTH, PAST_LEN, DA_HEADS, 2, DA_HEAD_DIM), 1.0),
        'cache_attn_v': nrm((DEC_BATCH, DEPTH, PAST_LEN, DA_HEADS, DA_V_DIM), 1.0),
        'state_rglru': nrm((DEC_BATCH, DEPTH, 2, RG_WIDTH), 0.5),
        'state_hgrn': nrm((DEC_BATCH, DEPTH, 2, HG_HEADS, HG_KEY, HG_VAL), 0.3),
        'c': nrm((DEC_BATCH, D_MODEL), 1.0),
        'c_ctx': nrm((D_MODEL,), 1.0),
        'w_ada': nrm((DEPTH, D_MODEL, 6 * D_MODEL), 0.5 * D_MODEL ** -0.5),
        'b_ada': nrm((DEPTH, 6 * D_MODEL), 0.02),
        'w_in': nrm((DEPTH, D_MODEL, D_IN), D_MODEL ** -0.5),
        'da_lambda': nrm((DEPTH, 4, DA_HEAD_DIM), 0.1),
        'da_subln': 1.0 + nrm((DEPTH, DA_V_DIM), 0.01),
        'rg_conv_w': nrm((DEPTH, RG_CONV_W, RG_WIDTH), RG_CONV_W ** -0.5),
        'rg_conv_b': nrm((DEPTH, RG_WIDTH), 0.01),
        'rg_gate_w': nrm((DEPTH, 2, 2, RG_BLOCKS, RG_BLOCK_W, RG_BLOCK_W), RG_BLOCK_W ** -0.5),
        'rg_gate_b': nrm((DEPTH, 2, 2, RG_WIDTH), 0.01),
        'rg_lambda': rg_lambda,
        'hg_lb': 1.0 + nrm((DEPTH, 2, HG_HEADS * HG_KEY), 0.1),
        'hg_norm': 1.0 + nrm((DEPTH, HG_VAL), 0.01),
        'w_branch': nrm((DEPTH, N_BRANCH, BRANCH_W, D_MODEL), BRANCH_W ** -0.5),
        'w_out': nrm((DEPTH, D_MODEL, D_MODEL), DN_BETA * D_MODEL ** -0.5),
        'ln1_g': 1.0 + nrm((DEPTH, D_MODEL), 0.01),
        'ln1_b': nrm((DEPTH, D_MODEL), 0.01),
        'router_w': nrm((DEPTH, D_MODEL, N_EXPERTS), D_MODEL ** -0.5),
        'router_b': nrm((DEPTH, N_EXPERTS), 0.01),
        'w1': nrm((DEPTH, N_EXPERTS, D_MODEL, 2 * D_EXPERT), D_MODEL ** -0.5),
        'b1': nrm((DEPTH, N_EXPERTS, 2 * D_EXPERT), 0.01),
        'w2': nrm((DEPTH, N_EXPERTS, D_EXPERT, D_MODEL), DN_BETA * D_EXPERT ** -0.5),
        'b2': nrm((DEPTH, N_EXPERTS, D_MODEL), 0.01),
        'ln2_g': 1.0 + nrm((DEPTH, D_MODEL), 0.01),
        'ln2_b': nrm((DEPTH, D_MODEL), 0.01),
    }


def reference(x_prompt, x_sample, cache_attn_k, cache_attn_v, state_rglru, state_hgrn, c, c_ctx,
              w_ada, b_ada, w_in, da_lambda, da_subln, rg_conv_w, rg_conv_b, rg_gate_w, rg_gate_b,
              rg_lambda, hg_lb, hg_norm, w_branch, w_out, ln1_g, ln1_b, router_w, router_b,
              w1, b1, w2, b2, ln2_g, ln2_b):
    p = dict(w_ada=w_ada, b_ada=b_ada, w_in=w_in, da_lambda=da_lambda, da_subln=da_subln,
             rg_conv_w=rg_conv_w, rg_conv_b=rg_conv_b, rg_gate_w=rg_gate_w, rg_gate_b=rg_gate_b,
             rg_lambda=rg_lambda, hg_norm=hg_norm, w_branch=w_branch, w_out=w_out,
             ln1_g=ln1_g, ln1_b=ln1_b, router_w=router_w, router_b=router_b,
             w1=w1, b1=b1, w2=w2, b2=b2, ln2_g=ln2_g, ln2_b=ln2_b)
    lbs = _hgrn_lower_bounds(hg_lb)

    y_prompt = x_prompt
    cond_ctx = c_ctx[None, :]
    ks, vs, rgs, hgs = [], [], [], []
    for l in range(DEPTH):
        y_prompt, (k_l, v_l, rg_l, hg_l) = _trunk_layer(y_prompt, cond_ctx, l, p, lbs[l], None)
        ks.append(k_l)
        vs.append(v_l)
        rgs.append(rg_l)
        hgs.append(hg_l)

    y_sample = x_sample
    for l in range(DEPTH):
        ctx = (cache_attn_k[:, l], cache_attn_v[:, l], state_rglru[:, l], state_hgrn[:, l])
        y_sample, _ = _trunk_layer(y_sample, c, l, p, lbs[l], ctx)

    new_attn_k = jnp.stack(ks, axis=1)
    new_attn_v = jnp.stack(vs, axis=1)
    new_state_rglru = jnp.stack(rgs, axis=1)
    new_state_hgrn = jnp.stack(hgs, axis=1)
    return (y_prompt, y_sample, new_attn_k, new_attn_v, new_state_rglru, new_state_hgrn)
```

```python
import functools
import math

import numpy as np
import jax
import jax.numpy as jnp
from jax import lax
from jax.experimental import pallas as pl
from jax.experimental.pallas import tpu as pltpu

F32 = jnp.float32
BF16 = jnp.bfloat16
HIGHEST = lax.Precision.HIGHEST

D_MODEL = 1024
BATCH = 16
SEQ = 256
DEPTH = 4
DEC_BATCH = 4
DEC_SEQ = 1024
PAST_LEN = 256
GRID_W = 64
BRANCH_W = 512
DA_HEADS = 4
DA_HEAD_DIM = 64
DA_V_DIM = 128
ROPE_BASE = 10000.0
RG_WIDTH = 512
RG_BLOCKS = 8
RG_BLOCK_W = 64
RG_CONV_W = 4
RG_C = 8.0
HG_HEADS = 4
HG_KEY = 128
HG_VAL = 128
N_EXPERTS = 32
TOP_K = 4
D_EXPERT = 1024
SWIGLU_ALPHA = 1.702
SWIGLU_LIMIT = 7.0
DN_ALPHA = (2 * DEPTH) ** 0.25
NORM_EPS = 1e-5
D_IN = 10 * BRANCH_W + 3 * D_MODEL

N_CTX = BATCH * SEQ
N_LAT = DEC_BATCH * DEC_SEQ
N_TOK = N_CTX + N_LAT
N_COND = 8

LANES = 128
VMEM_LIMIT = 56 * 1024 * 1024

HG_CHUNK = 128
HG_LEVELS = (8, 16, 32, 64)
MOE_TM = 256
MOE_TILES = (N_TOK * TOP_K) // MOE_TM + N_EXPERTS


def _cparams(sem):
    return pltpu.CompilerParams(dimension_semantics=sem, vmem_limit_bytes=VMEM_LIMIT)


def _sigmoid(x):
    return 1.0 / (1.0 + jnp.exp(-x))


def _dot(a, b):
    return jnp.dot(a, b, preferred_element_type=F32)


def _dot_nt(a, b):
    return lax.dot_general(a, b, (((1,), (1,)), ((), ())), preferred_element_type=F32)


def _cond_of_row_tile(i, tm):
    r = i * tm
    return jnp.where(r < N_CTX, 0, 1 + (r - N_CTX) // DEC_SEQ)


def _ada_kernel(c_ref, w_ref, b_ref, o_ref):
    c = c_ref[...]
    s = c * _sigmoid(c)
    o_ref[0] = jnp.dot(s, w_ref[0], precision=HIGHEST, preferred_element_type=F32) + b_ref[0]


def _adaln_all(cond, w_ada, b_ada):
    tn = 1536
    return pl.pallas_call(
        _ada_kernel,
        grid=(DEPTH, 6 * D_MODEL // tn),
        in_specs=[pl.BlockSpec((N_COND, D_MODEL), lambda l, j: (0, 0)),
                  pl.BlockSpec((1, D_MODEL, tn), lambda l, j: (l, 0, j)),
                  pl.BlockSpec((1, 1, tn), lambda l, j: (l, 0, j))],
        out_specs=pl.BlockSpec((1, N_COND, tn), lambda l, j: (l, 0, j)),
        out_shape=jax.ShapeDtypeStruct((DEPTH, N_COND, 6 * D_MODEL), F32),
        compiler_params=_cparams(("arbitrary", "arbitrary")),
        name="adaln",
    )(cond, w_ada, b_ada.reshape(DEPTH, 1, 6 * D_MODEL))


def _in_kernel(x_ref, mod_ref, w_ref, o_ref, wbf_ref):
    @pl.when(pl.program_id(1) == 0)
    def _():
        wbf_ref[...] = w_ref[...].astype(BF16)

    u = x_ref[...] * (1.0 + mod_ref[1:2, :]) + mod_ref[0:1, :]
    o_ref[...] = _dot(u.astype(BF16), wbf_ref[...])


def _in_proj(x, mods, w_in, l):
    tm, tn = 1024, 1024
    return pl.pallas_call(
        _in_kernel,
        grid=(D_IN // tn, N_TOK // tm),
        in_specs=[pl.BlockSpec((tm, D_MODEL), lambda j, i: (i, 0)),
                  pl.BlockSpec((None, None, 6, D_MODEL), lambda j, i: (l, _cond_of_row_tile(i, tm), 0, 0)),
                  pl.BlockSpec((None, D_MODEL, tn), lambda j, i: (l, 0, j))],
        out_specs=pl.BlockSpec((tm, tn), lambda j, i: (i, j)),
        out_shape=jax.ShapeDtypeStruct((N_TOK, D_IN), F32),
        scratch_shapes=[pltpu.VMEM((D_MODEL, tn), BF16)],
        compiler_params=_cparams(("arbitrary", "arbitrary")),
        name="in_proj",
    )(x, mods, w_in)


def _rope(x, cos, sin_signed):
    lane = lax.broadcasted_iota(jnp.int32, x.shape, 1)
    first = (lane & 31) < 16
    partner = jnp.where(first, pltpu.roll(x, LANES - 16, 1), pltpu.roll(x, 16, 1))
    return x * cos + partner * sin_signed


def _attn_body(q, keys, vals, lam, subln, out_scale):
    scale = DA_HEAD_DIM ** -0.5
    lane = lax.broadcasted_iota(jnp.int32, q.shape, 1)
    kb = [k.astype(BF16) for k in keys]
    vb = [v.astype(BF16) for v in vals]
    acc = None
    for m in range(2):
        in_map = (lane < DA_HEAD_DIM) if m == 0 else (lane >= DA_HEAD_DIM)
        qm = jnp.where(in_map, q, 0.0).astype(BF16)
        s = [_dot_nt(qm, k) * scale for k in kb]
        mx = s[0].max(axis=-1, keepdims=True)
        for si in s[1:]:
            mx = jnp.maximum(mx, si.max(axis=-1, keepdims=True))
        e = [jnp.exp(si - mx) for si in s]
        den = e[0].sum(axis=-1, keepdims=True)
        for ei in e[1:]:
            den = den + ei.sum(axis=-1, keepdims=True)
        coef = (1.0 / den) if m == 0 else (-lam[:, 0:1] / den)
        for ei, v in zip(e, vb):
            part = _dot((ei * coef).astype(BF16), v)
            acc = part if acc is None else acc + part
    y = acc * lax.rsqrt(jnp.mean(acc * acc, axis=-1, keepdims=True) + NORM_EPS)
    return y * subln * out_scale


def _attn_ctx_kernel(lam_ref, sub_ref, q_ref, k_ref, v_ref, o_ref, *, out_scale):
    o_ref[...] = _attn_body(q_ref[...], [k_ref[...]], [v_ref[...]], lam_ref[...], sub_ref[...], out_scale)


def _attn_lat_kernel(lam_ref, sub_ref, q_ref, k_ref, v_ref, kc_ref, vc_ref, cq_ref, sq_ref, ck_ref, sk_ref,
                     o_ref, *, out_scale):
    q = _rope(q_ref[...], cq_ref[...], sq_ref[...])
    k = _rope(k_ref[...], ck_ref[...], sk_ref[...])
    o_ref[...] = _attn_body(q, [k, kc_ref[...]], [v_ref[...], vc_ref[...]], lam_ref[...], sub_ref[...], out_scale)


def _rope_tables():
    t = np.arange(DEC_SEQ)
    row = (t // GRID_W).astype(np.float64)
    col = (t % GRID_W).astype(np.float64)
    d = DA_HEAD_DIM // 2
    inv = ROPE_BASE ** (-np.arange(0, d, 2, dtype=np.float64) / d)
    a_row = row[:, None] * inv[None, :]
    a_col = col[:, None] * inv[None, :]
    cos = np.concatenate([np.cos(a_row), np.cos(a_row), np.cos(a_col), np.cos(a_col)], -1)
    sin = np.concatenate([-np.sin(a_row), np.sin(a_row), -np.sin(a_col), np.sin(a_col)], -1)
    return (jnp.asarray(np.tile(cos, (1, 2)), F32), jnp.asarray(np.tile(sin, (1, 2)), F32))


def _attention(h, cache_k, cache_v, lam, subln, l, out_scale):
    lam_v = jnp.full((1, LANES), lam, F32)
    sub_v = subln.reshape(1, DA_V_DIM)
    small = pl.BlockSpec((1, LANES), lambda *a: (0, 0))
    att_ctx = pl.pallas_call(
        functools.partial(_attn_ctx_kernel, out_scale=out_scale),
        grid=(BATCH, DA_HEADS),
        in_specs=[small, small,
                  pl.BlockSpec((SEQ, LANES), lambda b, hd: (b, hd)),
                  pl.BlockSpec((SEQ, LANES), lambda b, hd: (b, 4 + hd)),
                  pl.BlockSpec((SEQ, LANES), lambda b, hd: (b, 8 + hd))],
        out_specs=pl.BlockSpec((SEQ, LANES), lambda b, hd: (b, hd)),
        out_shape=jax.ShapeDtypeStruct((N_CTX, BRANCH_W), F32),
        compiler_params=_cparams(("arbitrary", "arbitrary")),
        name="attn_ctx",
    )(lam_v, sub_v, h, h, h)

    tq = 256
    nq = DEC_SEQ // tq
    cos, sin = _rope_tables()
    row0 = N_CTX // DEC_SEQ
    att_lat = pl.pallas_call(
        functools.partial(_attn_lat_kernel, out_scale=out_scale),
        grid=(DEC_BATCH, DA_HEADS, nq),
        in_specs=[small, small,
                  pl.BlockSpec((tq, LANES), lambda b, hd, qi: ((row0 + b) * nq + qi, hd)),
                  pl.BlockSpec((DEC_SEQ, LANES), lambda b, hd, qi: (row0 + b, 4 + hd)),
                  pl.BlockSpec((DEC_SEQ, LANES), lambda b, hd, qi: (row0 + b, 8 + hd)),
                  pl.BlockSpec((None, None, PAST_LEN, LANES), lambda b, hd, qi: (b, l, 0, hd)),
                  pl.BlockSpec((None, None, PAST_LEN, LANES), lambda b, hd, qi: (b, l, 0, hd)),
                  pl.BlockSpec((tq, LANES), lambda b, hd, qi: (qi, 0)),
                  pl.BlockSpec((tq, LANES), lambda b, hd, qi: (qi, 0)),
                  pl.BlockSpec((DEC_SEQ, LANES), lambda b, hd, qi: (0, 0)),
                  pl.BlockSpec((DEC_SEQ, LANES), lambda b, hd, qi: (0, 0))],
        out_specs=pl.BlockSpec((tq, LANES), lambda b, hd, qi: (b * nq + qi, hd)),
        out_shape=jax.ShapeDtypeStruct((N_LAT, BRANCH_W), F32),
        compiler_params=_cparams(("arbitrary", "arbitrary", "arbitrary")),
        name="attn_lat",
    )(lam_v, sub_v, h, h, h, cache_k, cache_v, cos, sin, cos, sin)
    return jnp.concatenate([att_ctx, att_lat], axis=0)


def _gelu_tanh(x):
    return 0.5 * x * (1.0 + jnp.tanh(math.sqrt(2.0 / math.pi) * (x + 0.044715 * (x * x * x))))


def _rg_kernel(*refs, seq, has_h0):
    if has_h0:
        rx_ref, gate_ref, cw_ref, cb_ref, wg_ref, bg_ref, lam_ref, h0_ref = refs[:8]
        rest = refs[8:]
    else:
        rx_ref, gate_ref, cw_ref, cb_ref, wg_ref, bg_ref, lam_ref = refs[:7]
        h0_ref = None
        rest = refs[7:]
    out_ref, hl_ref, a_scr, b_scr, h_scr = rest

    x = rx_ref[...]
    row = lax.broadcasted_iota(jnp.int32, x.shape, 0)
    xr = cb_ref[...] + cw_ref[2:3, :] * x
    for j in (0, 1, 3):
        d = j - RG_CONV_W // 2
        shifted = pltpu.roll(x, (-d) % seq, 0)
        valid = (row + d >= 0) & (row + d < seq)
        xr = xr + cw_ref[j:j + 1, :] * jnp.where(valid, shifted, 0.0)

    g = _dot(xr.astype(BF16), wg_ref[...]) + bg_ref[...]
    for dr in range(2):
        r = _sigmoid(g[:, (2 * dr) * RG_WIDTH:(2 * dr + 1) * RG_WIDTH])
        i = _sigmoid(g[:, (2 * dr + 1) * RG_WIDTH:(2 * dr + 2) * RG_WIDTH])
        lam = lam_ref[dr:dr + 1, :]
        softplus_neg = jnp.maximum(-lam, 0.0) + jnp.log(1.0 + jnp.exp(-jnp.abs(lam)))
        log_a = (-RG_C * softplus_neg) * r
        a_scr[dr] = jnp.exp(log_a)
        b_scr[dr] = jnp.sqrt(1.0 - jnp.exp(2.0 * log_a)) * i * xr

    if has_h0:
        hf0, hb0 = h0_ref[0:1, :], h0_ref[1:2, :]
    else:
        hf0 = hb0 = jnp.zeros((1, RG_WIDTH), F32)

    def step(t, carry):
        hf, hb = carry
        tb = seq - 1 - t
        hf = a_scr[0, pl.ds(t, 1), :] * hf + b_scr[0, pl.ds(t, 1), :]
        h_scr[0, pl.ds(t, 1), :] = hf
        hb = a_scr[1, pl.ds(tb, 1), :] * hb + b_scr[1, pl.ds(tb, 1), :]
        h_scr[1, pl.ds(tb, 1), :] = hb
        return hf, hb

    hf, hb = lax.fori_loop(0, seq, step, (hf0, hb0), unroll=8)
    hl_ref[0:1, :] = hf
    hl_ref[1:2, :] = hb
    out_ref[...] = (h_scr[0] + h_scr[1]) * _gelu_tanh(gate_ref[...])


def _rglru_call(h, conv_w, conv_b, wg, bg, lam, h0, nseq, seq, row_block0):
    has_h0 = h0 is not None
    full = lambda shape: pl.BlockSpec(shape, lambda b: (0,) * len(shape))
    in_specs = [pl.BlockSpec((seq, RG_WIDTH), lambda b: (row_block0 + b, 3)),
                pl.BlockSpec((seq, RG_WIDTH), lambda b: (row_block0 + b, 4)),
                full((RG_CONV_W, RG_WIDTH)), full((1, RG_WIDTH)),
                full((RG_WIDTH, 4 * RG_WIDTH)), full((1, 4 * RG_WIDTH)), full((2, RG_WIDTH))]
    args = [h, h, conv_w, conv_b.reshape(1, RG_WIDTH), wg, bg, lam]
    if has_h0:
        in_specs.append(pl.BlockSpec((None, 2, RG_WIDTH), lambda b: (b, 0, 0)))
        args.append(h0)
    return pl.pallas_call(
        functools.partial(_rg_kernel, seq=seq, has_h0=has_h0),
        grid=(nseq,),
        in_specs=in_specs,
        out_specs=[pl.BlockSpec((seq, RG_WIDTH), lambda b: (b, 0)),
                   pl.BlockSpec((None, 2, RG_WIDTH), lambda b: (b, 0, 0))],
        out_shape=[jax.ShapeDtypeStruct((nseq * seq, RG_WIDTH), F32),
                   jax.ShapeDtypeStruct((nseq, 2, RG_WIDTH), F32)],
        scratch_shapes=[pltpu.VMEM((2, seq, RG_WIDTH), F32)] * 3,
        compiler_params=_cparams(("arbitrary",)),
        name="rglru_lat" if has_h0 else "rglru_ctx",
    )(*args)


def _rg_gate_dense(gate_w, gate_b):
    eye = jnp.eye(RG_BLOCKS, dtype=F32)
    dense = jnp.einsum('dgncf,nm->dgncmf', gate_w, eye).reshape(2, 2, RG_WIDTH, RG_WIDTH)
    wg = jnp.transpose(dense, (2, 0, 1, 3)).reshape(RG_WIDTH, 4 * RG_WIDTH)
    return wg.astype(BF16), gate_b.reshape(1, 4 * RG_WIDTH)


def _hg_masks():
    c = HG_CHUNK
    t = np.arange(c)[:, None]
    s = np.arange(c)[None, :]
    tri = np.stack([s <= t, s >= t]).astype(np.float32)
    lvl = np.zeros((2, len(HG_LEVELS), c, c), np.float32)
    for n, h in enumerate(HG_LEVELS):
        same = (t // (2 * h)) == (s // (2 * h))
        lvl[0, n] = same & ((t // h) % 2 == 1) & ((s // h) % 2 == 0)
        lvl[1, n] = same & ((t // h) % 2 == 0) & ((s // h) % 2 == 1)
    same8 = ((t // 8) == (s // 8)).astype(np.float32)
    return jnp.asarray(tri, BF16), jnp.asarray(lvl), jnp.asarray(same8)


def _hg_chunk(q_raw, z, v, lb, st, tri, lvl, same8, rev):
    c = HG_CHUNK
    q = q_raw * _sigmoid(q_raw)
    sig = _sigmoid(z)
    g = jnp.log(lb + (1.0 - lb) * sig)
    kk = (1.0 - lb) * (1.0 - sig)

    g1 = g.astype(BF16)
    r1 = g - g1.astype(F32)
    g2 = r1.astype(BF16)
    g3 = (r1 - g2.astype(F32)).astype(BF16)
    gi = _dot(tri, g1) + _dot(tri, g2) + _dot(tri, g3)
    gx = gi - g

    row = lax.broadcasted_iota(jnp.int32, (c, c), 0)
    col = lax.broadcasted_iota(jnp.int32, (c, c), 1)
    off = (row - col) if not rev else (col - row)

    scores = jnp.where(off == 0, jnp.sum(q * kk, axis=-1, keepdims=True), 0.0)
    for d in range(1, 8):
        sh = d if not rev else c - d
        kk_s = pltpu.roll(kk, sh, 0)
        g_s = pltpu.roll(gi, sh, 0)
        e = jnp.exp(jnp.minimum(gi - g_s, 0.0))
        band = jnp.sum(q * kk_s * e, axis=-1, keepdims=True)
        scores = scores + jnp.where(off == d, band, 0.0)
    scores = scores * same8

    for n, h in enumerate(HG_LEVELS):
        nb = c // h
        gi3 = gi.reshape(nb, h, HG_KEY)
        gx3 = gx.reshape(nb, h, HG_KEY)
        if not rev:
            a = jnp.exp(jnp.minimum(gi3 - gx3[:, 0:1, :], 0.0))
            b = jnp.exp(jnp.minimum(gi3[:, h - 1:h, :] - gi3, 0.0))
        else:
            a = jnp.exp(jnp.minimum(gi3 - gx3[:, h - 1:h, :], 0.0))
            b = jnp.exp(jnp.minimum(gi3[:, 0:1, :] - gi3, 0.0))
        qa = (q * a.reshape(c, HG_KEY)).astype(BF16)
        kb = (kk * b.reshape(c, HG_KEY)).astype(BF16)
        scores = scores + lvl[n] * _dot_nt(qa, kb)

    vb = v.astype(BF16)
    o = _dot(scores.astype(BF16), vb)
    o = o + _dot_nt((q * jnp.exp(gi)).astype(BF16), st.astype(BF16))
    g_end = gi[c - 1:c, :] if not rev else gi[0:1, :]
    kd = (kk * jnp.exp(jnp.minimum(g_end - gi, 0.0))).astype(BF16)
    st_new = st * jnp.exp(g_end) + _dot(v.T.astype(BF16), kd)
    return o, st_new


def _hg_kernel(*refs, has_s0, nchunk):
    if has_s0:
        (qf_ref, qb_ref, zf_ref, zb_ref, vf_ref, vb_ref, lb_ref, tri_ref, lvl_ref, s8_ref, s0_ref,
         of_ref, ob_ref, sfin_ref, st_scr) = refs
    else:
        (qf_ref, qb_ref, zf_ref, zb_ref, vf_ref, vb_ref, lb_ref, tri_ref, lvl_ref, s8_ref,
         of_ref, ob_ref, sfin_ref, st_scr) = refs
        s0_ref = None
    ci = pl.program_id(2)

    @pl.when(ci == 0)
    def _():
        for dr in range(2):
            st_scr[dr] = s0_ref[dr].T if has_s0 else jnp.zeros((HG_VAL, HG_KEY), F32)

    s8 = s8_ref[...]
    o_f, st_f = _hg_chunk(qf_ref[...], zf_ref[...], vf_ref[...], lb_ref[0:1, :], st_scr[0],
                          tri_ref[0], lvl_ref[0], s8, False)
    o_b, st_b = _hg_chunk(qb_ref[...], zb_ref[...], vb_ref[...], lb_ref[1:2, :], st_scr[1],
                          tri_ref[1], lvl_ref[1], s8, True)
    of_ref[...] = o_f
    ob_ref[...] = o_b
    st_scr[0] = st_f
    st_scr[1] = st_b

    @pl.when(ci == nchunk - 1)
    def _():
        sfin_ref[0] = st_f.T
        sfin_ref[1] = st_b.T


def _hgrn_call(h, lbs_l, s0, l, nseq, seq, row0):
    c = HG_CHUNK
    nchunk = seq // c
    has_s0 = s0 is not None
    tri, lvl, same8 = _hg_masks()
    rf = lambda b, hd, ci: row0 // c + b * nchunk + ci
    rb = lambda b, hd, ci: row0 // c + b * nchunk + (nchunk - 1 - ci)
    blk = lambda rfun, cb: pl.BlockSpec((c, LANES), lambda b, hd, ci: (rfun(b, hd, ci), cb + hd))
    const = lambda shape: pl.BlockSpec(shape, lambda b, hd, ci: (0,) * len(shape))
    in_specs = [blk(rf, 20), blk(rb, 20), blk(rf, 24), blk(rb, 28), blk(rf, 32), blk(rb, 32),
                pl.BlockSpec((2, LANES), lambda b, hd, ci: (0, hd)),
                const((2, c, c)), const((2, len(HG_LEVELS), c, c)), const((c, c))]
    args = [h, h, h, h, h, h, lbs_l, tri, lvl, same8]
    if has_s0:
        in_specs.append(pl.BlockSpec((None, None, 2, None, HG_KEY, HG_VAL), lambda b, hd, ci: (b, l, 0, hd, 0, 0)))
        args.append(s0)
    orow_f = lambda b, hd, ci: (b * nchunk + ci, hd)
    orow_b = lambda b, hd, ci: (b * nchunk + (nchunk - 1 - ci), hd)
    return pl.pallas_call(
        functools.partial(_hg_kernel, has_s0=has_s0, nchunk=nchunk),
        grid=(nseq, HG_HEADS, nchunk),
        in_specs=in_specs,
        out_specs=[pl.BlockSpec((c, LANES), orow_f), pl.BlockSpec((c, LANES), orow_b),
                   pl.BlockSpec((None, 2, None, HG_KEY, HG_VAL), lambda b, hd, ci: (b, 0, hd, 0, 0))],
        out_shape=[jax.ShapeDtypeStruct((nseq * seq, BRANCH_W), F32),
                   jax.ShapeDtypeStruct((nseq * seq, BRANCH_W), F32),
                   jax.ShapeDtypeStruct((nseq, 2, HG_HEADS, HG_KEY, HG_VAL), F32)],
        scratch_shapes=[pltpu.VMEM((2, HG_VAL, HG_KEY), F32)],
        compiler_params=_cparams(("arbitrary", "arbitrary", "arbitrary")),
        name="hgrn_lat" if has_s0 else "hgrn_ctx",
    )(*args)


def _layer_norm(y, g, b):
    mu = jnp.mean(y, axis=-1, keepdims=True)
    yc = y - mu
    var = jnp.mean(yc * yc, axis=-1, keepdims=True)
    return yc * lax.rsqrt(var + NORM_EPS) * g + b


def _merge_kernel(att_ref, rg_ref, of_ref, ob_ref, hgate_ref, mg0_ref, mg1_ref, mg2_ref, x_ref, mod_ref,
                  hgn_ref, wbr_ref, wout_ref, lng_ref, lnb_ref, rw_ref, rb_ref,
                  x1_ref, u2_ref, idx_ref, wts_ref, wbr_bf, wout_bf):
    @pl.when(pl.program_id(0) == 0)
    def _():
        wbr_bf[...] = wbr_ref[...].astype(BF16)
        wout_bf[...] = wout_ref[...].astype(BF16)

    o = of_ref[...] + ob_ref[...]
    hgate = hgate_ref[...]
    hg_parts = []
    for hd in range(HG_HEADS):
        oh = o[:, hd * HG_VAL:(hd + 1) * HG_VAL]
        gh = hgate[:, hd * HG_VAL:(hd + 1) * HG_VAL]
        yh = oh * lax.rsqrt(jnp.mean(oh * oh, axis=-1, keepdims=True) + NORM_EPS) * hgn_ref[...]
        hg_parts.append(yh * (gh * _sigmoid(gh)))
    hg = jnp.concatenate(hg_parts, axis=-1)

    proj = _sigmoid(mg0_ref[...]) * _dot(att_ref[...].astype(BF16), wbr_bf[0])
    proj = proj + _sigmoid(mg1_ref[...]) * _dot(rg_ref[...].astype(BF16), wbr_bf[1])
    proj = proj + _sigmoid(mg2_ref[...]) * _dot(hg.astype(BF16), wbr_bf[2])
    mix = _dot(proj.astype(BF16), wout_bf[...])

    x1 = _layer_norm(DN_ALPHA * x_ref[...] + mod_ref[2:3, :] * mix, lng_ref[...], lnb_ref[...])
    x1_ref[...] = x1
    u2 = x1 * (1.0 + mod_ref[4:5, :]) + mod_ref[3:4, :]
    u2_ref[...] = u2.astype(BF16)

    logits = jnp.dot(u2, rw_ref[...], precision=HIGHEST, preferred_element_type=F32) + rb_ref[...]
    lane = lax.broadcasted_iota(jnp.int32, logits.shape, 1).astype(F32)
    idx_out = jnp.zeros(logits.shape, F32)
    wts_out = jnp.zeros(logits.shape, F32)
    top0 = None
    den = None
    for k in range(TOP_K):
        m = jnp.max(logits, axis=-1, keepdims=True)
        sel = jnp.min(jnp.where(logits == m, lane, float(LANES)), axis=-1, keepdims=True)
        if k == 0:
            top0 = m
        e = jnp.exp(m - top0)
        den = e if den is None else den + e
        idx_out = jnp.where(lane == k, sel, idx_out)
        wts_out = jnp.where(lane == k, e, wts_out)
        logits = jnp.where(lane == sel, -jnp.inf, logits)
    idx_ref[...] = idx_out.astype(jnp.int32)
    wts_ref[...] = wts_out * (1.0 / den)


def _merge_call(att, rg, o_f, o_b, h, x, mods, p, l):
    tm = 256
    rowb = lambda w, cb: pl.BlockSpec((tm, w), lambda i: (i, cb))
    const = lambda shape: pl.BlockSpec(shape, lambda i: (0,) * len(shape))
    rw = jnp.zeros((D_MODEL, LANES), F32).at[:, :N_EXPERTS].set(p['router_w'][l])
    rb = jnp.full((1, LANES), -1e30, F32).at[0, :N_EXPERTS].set(p['router_b'][l])
    return pl.pallas_call(
        _merge_kernel,
        grid=(N_TOK // tm,),
        in_specs=[rowb(BRANCH_W, 0), rowb(BRANCH_W, 0), rowb(BRANCH_W, 0), rowb(BRANCH_W, 0),
                  rowb(BRANCH_W, 9), rowb(D_MODEL, 5), rowb(D_MODEL, 6), rowb(D_MODEL, 7),
                  rowb(D_MODEL, 0),
                  pl.BlockSpec((None, None, 6, D_MODEL), lambda i: (l, _cond_of_row_tile(i, tm), 0, 0)),
                  const((1, HG_VAL)), const((3, BRANCH_W, D_MODEL)), const((D_MODEL, D_MODEL)),
                  const((1, D_MODEL)), const((1, D_MODEL)), const((D_MODEL, LANES)), const((1, LANES))],
        out_specs=[rowb(D_MODEL, 0), rowb(D_MODEL, 0), rowb(LANES, 0), rowb(LANES, 0)],
        out_shape=[jax.ShapeDtypeStruct((N_TOK, D_MODEL), F32),
                   jax.ShapeDtypeStruct((N_TOK, D_MODEL), BF16),
                   jax.ShapeDtypeStruct((N_TOK, LANES), jnp.int32),
                   jax.ShapeDtypeStruct((N_TOK, LANES), F32)],
        scratch_shapes=[pltpu.VMEM((3, BRANCH_W, D_MODEL), BF16), pltpu.VMEM((D_MODEL, D_MODEL), BF16)],
        compiler_params=_cparams(("arbitrary",)),
        name="merge",
    )(att, rg, o_f, o_b, h, h, h, h, x, mods, p['hg_norm'][l].reshape(1, HG_VAL), p['w_branch'][l],
      p['w_out'][l], p['ln1_g'][l].reshape(1, D_MODEL), p['ln1_b'][l].reshape(1, D_MODEL), rw, rb)


def _moe_kernel(te_ref, nt_ref, x_ref, w1g_ref, w1l_ref, b1g_ref, b1l_ref, w2_ref, b2_ref, y_ref):
    i = pl.program_id(0)

    @pl.when(i < nt_ref[0])
    def _():
        x = x_ref[...]
        hg = _dot(x, w1g_ref[...]) + b1g_ref[...]
        hl = _dot(x, w1l_ref[...]) + b1l_ref[...]
        glu = jnp.minimum(hg, SWIGLU_LIMIT)
        lin = jnp.clip(hl, -SWIGLU_LIMIT, SWIGLU_LIMIT)
        act = glu * _sigmoid(SWIGLU_ALPHA * glu) * (lin + 1.0)
        y_ref[...] = _dot(act.astype(BF16), w2_ref[...]) + b2_ref[...]

    @pl.when(i >= nt_ref[0])
    def _():
        y_ref[...] = jnp.zeros(y_ref.shape, F32)


def _moe_call(x_sorted, tile_expert, n_used, w1g, w1l, b1g, b1l, w2, b2, l):
    tm = MOE_TM
    wspec = pl.BlockSpec((None, None, D_MODEL, D_EXPERT), lambda i, te, nt: (l, te[i], 0, 0))
    bspec = pl.BlockSpec((None, None, 1, D_EXPERT), lambda i, te, nt: (l, te[i], 0, 0))
    grid_spec = pltpu.PrefetchScalarGridSpec(
        num_scalar_prefetch=2,
        grid=(MOE_TILES,),
        in_specs=[pl.BlockSpec((tm, D_MODEL), lambda i, te, nt: (i, 0)),
                  wspec, wspec, bspec, bspec, wspec, bspec],
        out_specs=pl.BlockSpec((tm, D_MODEL), lambda i, te, nt: (i, 0)),
    )
    return pl.pallas_call(
        _moe_kernel,
        grid_spec=grid_spec,
        out_shape=jax.ShapeDtypeStruct((MOE_TILES * tm, D_MODEL), F32),
        compiler_params=_cparams(("arbitrary",)),
        name="moe",
    )(tile_expert, n_used, x_sorted, w1g, w1l, b1g, b1l, w2, b2)


def _route(top_i):
    tm = MOE_TM
    flat_e = top_i.reshape(-1)
    onehot = (flat_e[:, None] == jnp.arange(N_EXPERTS, dtype=jnp.int32)[None, :]).astype(jnp.int32)
    csum = jnp.cumsum(onehot, axis=0)
    rank = jnp.sum((csum - 1) * onehot, axis=1)
    counts = csum[-1]
    tiles_e = (counts + tm - 1) // tm
    tile_end = jnp.cumsum(tiles_e)
    tile_start = tile_end - tiles_e
    pos = tile_start[flat_e] * tm + rank
    n_used = tile_end[-1]
    tile_ids = jnp.arange(MOE_TILES, dtype=jnp.int32)
    tile_expert = jnp.searchsorted(tile_end, jnp.minimum(tile_ids, n_used - 1), side='right').astype(jnp.int32)
    tile_expert = jnp.minimum(tile_expert, N_EXPERTS - 1)
    src_tok = jnp.zeros((MOE_TILES * tm,), jnp.int32).at[pos].set(
        jnp.arange(N_TOK * TOP_K, dtype=jnp.int32) // TOP_K)
    return pos, src_tok, tile_expert, n_used.reshape(1).astype(jnp.int32)


def _final_kernel(yg_ref, wts_ref, x1_ref, mod_ref, lng_ref, lnb_ref, o_ref):
    wts = wts_ref[...]
    ffn = wts[:, 0:1] * yg_ref[0]
    for k in range(1, TOP_K):
        ffn = ffn + wts[:, k:k + 1] * yg_ref[k]
    o_ref[...] = _layer_norm(DN_ALPHA * x1_ref[...] + mod_ref[5:6, :] * ffn, lng_ref[...], lnb_ref[...])


def _final_call(yg, wts, x1, mods, ln_g, ln_b, l):
    tm = 256
    const = lambda shape: pl.BlockSpec(shape, lambda i: (0,) * len(shape))
    return pl.pallas_call(
        _final_kernel,
        grid=(N_TOK // tm,),
        in_specs=[pl.BlockSpec((TOP_K, tm, D_MODEL), lambda i: (0, i, 0)),
                  pl.BlockSpec((tm, LANES), lambda i: (i, 0)),
                  pl.BlockSpec((tm, D_MODEL), lambda i: (i, 0)),
                  pl.BlockSpec((None, None, 6, D_MODEL), lambda i: (l, _cond_of_row_tile(i, tm), 0, 0)),
                  const((1, D_MODEL)), const((1, D_MODEL))],
        out_specs=pl.BlockSpec((tm, D_MODEL), lambda i: (i, 0)),
        out_shape=jax.ShapeDtypeStruct((N_TOK, D_MODEL), F32),
        compiler_params=_cparams(("arbitrary",)),
        name="final",
    )(yg, wts, x1, mods, ln_g.reshape(1, D_MODEL), ln_b.reshape(1, D_MODEL))


def kernel(x_prompt, x_sample, cache_attn_k, cache_attn_v, state_rglru, state_hgrn, c, c_ctx, w_ada, b_ada, w_in, da_lambda, da_subln, rg_conv_w, rg_conv_b, rg_gate_w, rg_gate_b, rg_lambda, hg_lb, hg_norm, w_branch, w_out, ln1_g, ln1_b, router_w, router_b, w1, b1, w2, b2, ln2_g, ln2_b):
    p = dict(hg_norm=hg_norm, w_branch=w_branch, w_out=w_out, ln1_g=ln1_g, ln1_b=ln1_b,
             router_w=router_w, router_b=router_b)

    x = jnp.concatenate([x_prompt.reshape(N_CTX, D_MODEL), x_sample.reshape(N_LAT, D_MODEL)], axis=0)
    cond = jnp.concatenate([c_ctx[None, :], c, jnp.zeros((N_COND - 1 - DEC_BATCH, D_MODEL), F32)], axis=0)
    mods = _adaln_all(cond, w_ada, b_ada).reshape(DEPTH, N_COND, 6, D_MODEL)

    pr = jax.nn.softmax(hg_lb.astype(F32), axis=0)
    lbs = jnp.cumsum(pr, axis=0) - pr[0]
    dl = da_lambda.astype(F32)
    lam_all = jnp.exp(jnp.sum(dl[:, 0] * dl[:, 1], -1)) - jnp.exp(jnp.sum(dl[:, 2] * dl[:, 3], -1))

    cache_k = cache_attn_k.reshape(DEC_BATCH, DEPTH, PAST_LEN, BRANCH_W)
    cache_v = cache_attn_v.reshape(DEC_BATCH, DEPTH, PAST_LEN, BRANCH_W)

    w1g = w1[..., 0::2].astype(BF16)
    w1l = w1[..., 1::2].astype(BF16)
    b1g = b1[..., 0::2].reshape(DEPTH, N_EXPERTS, 1, D_EXPERT)
    b1l = b1[..., 1::2].reshape(DEPTH, N_EXPERTS, 1, D_EXPERT)
    w2b = w2.astype(BF16)
    b2r = b2.reshape(DEPTH, N_EXPERTS, 1, D_MODEL)

    ks, vs, rgs, hgs = [], [], [], []
    for l in range(DEPTH):
        lambda_init = 0.8 - 0.6 * math.exp(-0.3 * l)
        h = _in_proj(x, mods, w_in, l)
        ks.append(h[:N_CTX, BRANCH_W:2 * BRANCH_W].reshape(BATCH, SEQ, DA_HEADS, 2, DA_HEAD_DIM))
        vs.append(h[:N_CTX, 2 * BRANCH_W:3 * BRANCH_W].reshape(BATCH, SEQ, DA_HEADS, DA_V_DIM))

        att = _attention(h, cache_k, cache_v, lam_all[l] + lambda_init, da_subln[l], l, 1.0 - lambda_init)

        wg, bg = _rg_gate_dense(rg_gate_w[l], rg_gate_b[l])
        rg_c, hl_c = _rglru_call(h, rg_conv_w[l], rg_conv_b[l], wg, bg, rg_lambda[l], None, BATCH, SEQ, 0)
        rg_l, _ = _rglru_call(h, rg_conv_w[l], rg_conv_b[l], wg, bg, rg_lambda[l], state_rglru[:, l],
                              DEC_BATCH, DEC_SEQ, N_CTX // DEC_SEQ)
        rg = jnp.concatenate([rg_c, rg_l], axis=0)
        rgs.append(hl_c)

        of_c, ob_c, sl_c = _hgrn_call(h, lbs[l], None, l, BATCH, SEQ, 0)
        of_l, ob_l, _ = _hgrn_call(h, lbs[l], state_hgrn, l, DEC_BATCH, DEC_SEQ, N_CTX)
        hgs.append(sl_c)
        o_f = jnp.concatenate([of_c, of_l], axis=0)
        o_b = jnp.concatenate([ob_c, ob_l], axis=0)

        x1, u2, idx, wts = _merge_call(att, rg, o_f, o_b, h, x, mods, p, l)

        pos, src_tok, tile_expert, n_used = _route(idx[:, :TOP_K])
        x_sorted = jnp.take(u2, src_tok, axis=0)
        y_sorted = _moe_call(x_sorted, tile_expert, n_used, w1g, w1l, b1g, b1l, w2b, b2r, l)
        yg = jnp.take(y_sorted, pos.reshape(N_TOK, TOP_K).T, axis=0)
        x = _final_call(yg, wts, x1, mods, ln2_g[l], ln2_b[l], l)

    y_prompt = x[:N_CTX].reshape(BATCH, SEQ, D_MODEL)
    y_sample = x[N_CTX:].reshape(DEC_BATCH, DEC_SEQ, D_MODEL)
    return (y_prompt, y_sample, jnp.stack(ks, axis=1), jnp.stack(vs, axis=1),
            jnp.stack(rgs, axis=1), jnp.stack(hgs, axis=1))
```

```python
import functools
import math

import numpy as np
import jax
import jax.numpy as jnp
from jax import lax
from jax.experimental import pallas as pl
from jax.experimental.pallas import tpu as pltpu

F32 = jnp.float32
BF16 = jnp.bfloat16
HIGHEST = lax.Precision.HIGHEST

D_MODEL = 1024
BATCH = 16
SEQ = 256
DEPTH = 4
DEC_BATCH = 4
DEC_SEQ = 1024
PAST_LEN = 256
GRID_W = 64
BRANCH_W = 512
DA_HEADS = 4
DA_HEAD_DIM = 64
DA_V_DIM = 128
ROPE_BASE = 10000.0
RG_WIDTH = 512
RG_BLOCKS = 8
RG_BLOCK_W = 64
RG_CONV_W = 4
RG_C = 8.0
HG_HEADS = 4
HG_KEY = 128
HG_VAL = 128
N_EXPERTS = 32
TOP_K = 4
D_EXPERT = 1024
SWIGLU_ALPHA = 1.702
SWIGLU_LIMIT = 7.0
DN_ALPHA = (2 * DEPTH) ** 0.25
NORM_EPS = 1e-5
D_IN = 10 * BRANCH_W + 3 * D_MODEL

N_CTX = BATCH * SEQ
N_LAT = DEC_BATCH * DEC_SEQ
N_TOK = N_CTX + N_LAT
N_COND = 8

LANES = 128
VMEM_LIMIT = 56 * 1024 * 1024

HG_CHUNK = 128
HG_LEVELS = (8, 16, 32, 64)
MOE_TM = 256
MOE_TILES = (N_TOK * TOP_K) // MOE_TM + N_EXPERTS


def _cparams(sem):
    return pltpu.CompilerParams(dimension_semantics=sem, vmem_limit_bytes=VMEM_LIMIT)


def _sigmoid(x):
    return 1.0 / (1.0 + jnp.exp(-x))


def _dot(a, b):
    return jnp.dot(a, b, preferred_element_type=F32)


def _dot_nt(a, b):
    return lax.dot_general(a, b, (((1,), (1,)), ((), ())), preferred_element_type=F32)


def _cond_of_row_tile(i, tm):
    r = i * tm
    return jnp.where(r < N_CTX, 0, 1 + (r - N_CTX) // DEC_SEQ)


def _ada_kernel(c_ref, w_ref, b_ref, o_ref):
    c = c_ref[...]
    s = c * _sigmoid(c)
    o_ref[0] = jnp.dot(s, w_ref[0], precision=HIGHEST, preferred_element_type=F32) + b_ref[0]


def _adaln_all(cond, w_ada, b_ada):
    tn = 1536
    return pl.pallas_call(
        _ada_kernel,
        grid=(DEPTH, 6 * D_MODEL // tn),
        in_specs=[pl.BlockSpec((N_COND, D_MODEL), lambda l, j: (0, 0)),
                  pl.BlockSpec((1, D_MODEL, tn), lambda l, j: (l, 0, j)),
                  pl.BlockSpec((1, 1, tn), lambda l, j: (l, 0, j))],
        out_specs=pl.BlockSpec((1, N_COND, tn), lambda l, j: (l, 0, j)),
        out_shape=jax.ShapeDtypeStruct((DEPTH, N_COND, 6 * D_MODEL), F32),
        compiler_params=_cparams(("arbitrary", "arbitrary")),
        name="adaln",
    )(cond, w_ada, b_ada.reshape(DEPTH, 1, 6 * D_MODEL))


def _in_kernel(x_ref, mod_ref, w_ref, o_ref, wbf_ref):
    @pl.when(pl.program_id(1) == 0)
    def _():
        wbf_ref[...] = w_ref[...].astype(BF16)

    u = x_ref[...] * (1.0 + mod_ref[1:2, :]) + mod_ref[0:1, :]
    o_ref[...] = _dot(u.astype(BF16), wbf_ref[...])


def _in_proj(x, mods, w_in, l):
    tm, tn = 1024, 1024
    return pl.pallas_call(
        _in_kernel,
        grid=(D_IN // tn, N_TOK // tm),
        in_specs=[pl.BlockSpec((tm, D_MODEL), lambda j, i: (i, 0)),
                  pl.BlockSpec((None, None, 6, D_MODEL), lambda j, i: (l, _cond_of_row_tile(i, tm), 0, 0)),
                  pl.BlockSpec((None, D_MODEL, tn), lambda j, i: (l, 0, j))],
        out_specs=pl.BlockSpec((tm, tn), lambda j, i: (i, j)),
        out_shape=jax.ShapeDtypeStruct((N_TOK, D_IN), F32),
        scratch_shapes=[pltpu.VMEM((D_MODEL, tn), BF16)],
        compiler_params=_cparams(("arbitrary", "arbitrary")),
        name="in_proj",
    )(x, mods, w_in)


def _rope(x, cos, sin_signed):
    lane = lax.broadcasted_iota(jnp.int32, x.shape, 1)
    first = (lane & 31) < 16
    partner = jnp.where(first, pltpu.roll(x, LANES - 16, 1), pltpu.roll(x, 16, 1))
    return x * cos + partner * sin_signed


def _attn_body(q, keys, vals, lam, subln, out_scale):
    scale = DA_HEAD_DIM ** -0.5
    lane = lax.broadcasted_iota(jnp.int32, q.shape, 1)
    kb = [k.astype(BF16) for k in keys]
    vb = [v.astype(BF16) for v in vals]
    acc = None
    for m in range(2):
        in_map = (lane < DA_HEAD_DIM) if m == 0 else (lane >= DA_HEAD_DIM)
        qm = jnp.where(in_map, q, 0.0).astype(BF16)
        s = [_dot_nt(qm, k) * scale for k in kb]
        mx = s[0].max(axis=-1, keepdims=True)
        for si in s[1:]:
            mx = jnp.maximum(mx, si.max(axis=-1, keepdims=True))
        e = [jnp.exp(si - mx) for si in s]
        den = e[0].sum(axis=-1, keepdims=True)
        for ei in e[1:]:
            den = den + ei.sum(axis=-1, keepdims=True)
        coef = (1.0 / den) if m == 0 else (-lam[:, 0:1] / den)
        for ei, v in zip(e, vb):
            part = _dot((ei * coef).astype(BF16), v)
            acc = part if acc is None else acc + part
    y = acc * lax.rsqrt(jnp.mean(acc * acc, axis=-1, keepdims=True) + NORM_EPS)
    return y * subln * out_scale


def _attn_ctx_kernel(lam_ref, sub_ref, q_ref, k_ref, v_ref, o_ref, *, out_scale):
    o_ref[...] = _attn_body(q_ref[...], [k_ref[...]], [v_ref[...]], lam_ref[...], sub_ref[...], out_scale)


def _attn_lat_kernel(lam_ref, sub_ref, q_ref, k_ref, v_ref, kc_ref, vc_ref, cq_ref, sq_ref, ck_ref, sk_ref,
                     o_ref, *, out_scale):
    q = _rope(q_ref[...], cq_ref[...], sq_ref[...])
    k = _rope(k_ref[...], ck_ref[...], sk_ref[...])
    o_ref[...] = _attn_body(q, [k, kc_ref[...]], [v_ref[...], vc_ref[...]], lam_ref[...], sub_ref[...], out_scale)


def _rope_tables():
    t = np.arange(DEC_SEQ)
    row = (t // GRID_W).astype(np.float64)
    col = (t % GRID_W).astype(np.float64)
    d = DA_HEAD_DIM // 2
    inv = ROPE_BASE ** (-np.arange(0, d, 2, dtype=np.float64) / d)
    a_row = row[:, None] * inv[None, :]
    a_col = col[:, None] * inv[None, :]
    cos = np.concatenate([np.cos(a_row), np.cos(a_row), np.cos(a_col), np.cos(a_col)], -1)
    sin = np.concatenate([-np.sin(a_row), np.sin(a_row), -np.sin(a_col), np.sin(a_col)], -1)
    return (jnp.asarray(np.tile(cos, (1, 2)), F32), jnp.asarray(np.tile(sin, (1, 2)), F32))


def _attention(h, cache_k, cache_v, lam, subln, l, out_scale):
    lam_v = jnp.full((1, LANES), lam, F32)
    sub_v = subln.reshape(1, DA_V_DIM)
    small = pl.BlockSpec((1, LANES), lambda *a: (0, 0))
    att_ctx = pl.pallas_call(
        functools.partial(_attn_ctx_kernel, out_scale=out_scale),
        grid=(BATCH, DA_HEADS),
        in_specs=[small, small,
                  pl.BlockSpec((SEQ, LANES), lambda b, hd: (b, hd)),
                  pl.BlockSpec((SEQ, LANES), lambda b, hd: (b, 4 + hd)),
                  pl.BlockSpec((SEQ, LANES), lambda b, hd: (b, 8 + hd))],
        out_specs=pl.BlockSpec((SEQ, LANES), lambda b, hd: (b, hd)),
        out_shape=jax.ShapeDtypeStruct((N_CTX, BRANCH_W), F32),
        compiler_params=_cparams(("arbitrary", "arbitrary")),
        name="attn_ctx",
    )(lam_v, sub_v, h, h, h)

    tq = 256
    nq = DEC_SEQ // tq
    cos, sin = _rope_tables()
    row0 = N_CTX // DEC_SEQ
    att_lat = pl.pallas_call(
        functools.partial(_attn_lat_kernel, out_scale=out_scale),
        grid=(DEC_BATCH, DA_HEADS, nq),
        in_specs=[small, small,
                  pl.BlockSpec((tq, LANES), lambda b, hd, qi: ((row0 + b) * nq + qi, hd)),
                  pl.BlockSpec((DEC_SEQ, LANES), lambda b, hd, qi: (row0 + b, 4 + hd)),
                  pl.BlockSpec((DEC_SEQ, LANES), lambda b, hd, qi: (row0 + b, 8 + hd)),
                  pl.BlockSpec((None, None, PAST_LEN, LANES), lambda b, hd, qi: (b, l, 0, hd)),
                  pl.BlockSpec((None, None, PAST_LEN, LANES), lambda b, hd, qi: (b, l, 0, hd)),
                  pl.BlockSpec((tq, LANES), lambda b, hd, qi: (qi, 0)),
                  pl.BlockSpec((tq, LANES), lambda b, hd, qi: (qi, 0)),
                  pl.BlockSpec((DEC_SEQ, LANES), lambda b, hd, qi: (0, 0)),
                  pl.BlockSpec((DEC_SEQ, LANES), lambda b, hd, qi: (0, 0))],
        out_specs=pl.BlockSpec((tq, LANES), lambda b, hd, qi: (b * nq + qi, hd)),
        out_shape=jax.ShapeDtypeStruct((N_LAT, BRANCH_W), F32),
        compiler_params=_cparams(("arbitrary", "arbitrary", "arbitrary")),
        name="attn_lat",
    )(lam_v, sub_v, h, h, h, cache_k, cache_v, cos, sin, cos, sin)
    return att_ctx, att_lat


def _gelu_tanh(x):
    return 0.5 * x * (1.0 + jnp.tanh(math.sqrt(2.0 / math.pi) * (x + 0.044715 * (x * x * x))))


def _rg_kernel(*refs, seq, has_h0):
    if has_h0:
        rx_ref, gate_ref, cw_ref, cb_ref, wg_ref, bg_ref, lam_ref, h0_ref = refs[:8]
        rest = refs[8:]
    else:
        rx_ref, gate_ref, cw_ref, cb_ref, wg_ref, bg_ref, lam_ref = refs[:7]
        h0_ref = None
        rest = refs[7:]
    out_ref, hl_ref, a_scr, b_scr, h_scr = rest

    x = rx_ref[...]
    row = lax.broadcasted_iota(jnp.int32, x.shape, 0)
    xr = cb_ref[...] + cw_ref[2:3, :] * x
    for j in (0, 1, 3):
        d = j - RG_CONV_W // 2
        shifted = pltpu.roll(x, (-d) % seq, 0)
        valid = (row + d >= 0) & (row + d < seq)
        xr = xr + cw_ref[j:j + 1, :] * jnp.where(valid, shifted, 0.0)

    g = _dot(xr.astype(BF16), wg_ref[...]) + bg_ref[...]
    for dr in range(2):
        r = _sigmoid(g[:, (2 * dr) * RG_WIDTH:(2 * dr + 1) * RG_WIDTH])
        i = _sigmoid(g[:, (2 * dr + 1) * RG_WIDTH:(2 * dr + 2) * RG_WIDTH])
        lam = lam_ref[dr:dr + 1, :]
        softplus_neg = jnp.maximum(-lam, 0.0) + jnp.log(1.0 + jnp.exp(-jnp.abs(lam)))
        log_a = (-RG_C * softplus_neg) * r
        a_scr[dr] = jnp.exp(log_a)
        b_scr[dr] = jnp.sqrt(1.0 - jnp.exp(2.0 * log_a)) * i * xr

    if has_h0:
        hf0, hb0 = h0_ref[0:1, :], h0_ref[1:2, :]
    else:
        hf0 = hb0 = jnp.zeros((1, RG_WIDTH), F32)

    def step(t, carry):
        hf, hb = carry
        tb = seq - 1 - t
        hf = a_scr[0, pl.ds(t, 1), :] * hf + b_scr[0, pl.ds(t, 1), :]
        h_scr[0, pl.ds(t, 1), :] = hf
        hb = a_scr[1, pl.ds(tb, 1), :] * hb + b_scr[1, pl.ds(tb, 1), :]
        h_scr[1, pl.ds(tb, 1), :] = hb
        return hf, hb

    hf, hb = lax.fori_loop(0, seq, step, (hf0, hb0), unroll=8)
    hl_ref[0:1, :] = hf
    hl_ref[1:2, :] = hb
    out_ref[...] = (h_scr[0] + h_scr[1]) * _gelu_tanh(gate_ref[...])


def _rglru_call(h, conv_w, conv_b, wg, bg, lam, h0, nseq, seq, row_block0):
    has_h0 = h0 is not None
    full = lambda shape: pl.BlockSpec(shape, lambda b: (0,) * len(shape))
    in_specs = [pl.BlockSpec((seq, RG_WIDTH), lambda b: (row_block0 + b, 3)),
                pl.BlockSpec((seq, RG_WIDTH), lambda b: (row_block0 + b, 4)),
                full((RG_CONV_W, RG_WIDTH)), full((1, RG_WIDTH)),
                full((RG_WIDTH, 4 * RG_WIDTH)), full((1, 4 * RG_WIDTH)), full((2, RG_WIDTH))]
    args = [h, h, conv_w, conv_b.reshape(1, RG_WIDTH), wg, bg, lam]
    if has_h0:
        in_specs.append(pl.BlockSpec((None, 2, RG_WIDTH), lambda b: (b, 0, 0)))
        args.append(h0)
    return pl.pallas_call(
        functools.partial(_rg_kernel, seq=seq, has_h0=has_h0),
        grid=(nseq,),
        in_specs=in_specs,
        out_specs=[pl.BlockSpec((seq, RG_WIDTH), lambda b: (b, 0)),
                   pl.BlockSpec((None, 2, RG_WIDTH), lambda b: (b, 0, 0))],
        out_shape=[jax.ShapeDtypeStruct((nseq * seq, RG_WIDTH), F32),
                   jax.ShapeDtypeStruct((nseq, 2, RG_WIDTH), F32)],
        scratch_shapes=[pltpu.VMEM((2, seq, RG_WIDTH), F32)] * 3,
        compiler_params=_cparams(("arbitrary",)),
        name="rglru_lat" if has_h0 else "rglru_ctx",
    )(*args)


def _rg_gate_dense(gate_w, gate_b):
    eye = jnp.eye(RG_BLOCKS, dtype=F32)
    dense = jnp.einsum('dgncf,nm->dgncmf', gate_w, eye).reshape(2, 2, RG_WIDTH, RG_WIDTH)
    wg = jnp.transpose(dense, (2, 0, 1, 3)).reshape(RG_WIDTH, 4 * RG_WIDTH)
    return wg.astype(BF16), gate_b.reshape(1, 4 * RG_WIDTH)


def _hg_masks():
    c = HG_CHUNK
    t = np.arange(c)[:, None]
    s = np.arange(c)[None, :]
    tri = np.stack([s <= t, s >= t]).astype(np.float32)
    lvl = np.zeros((2, len(HG_LEVELS), c, c), np.float32)
    for n, h in enumerate(HG_LEVELS):
        same = (t // (2 * h)) == (s // (2 * h))
        lvl[0, n] = same & ((t // h) % 2 == 1) & ((s // h) % 2 == 0)
        lvl[1, n] = same & ((t // h) % 2 == 0) & ((s // h) % 2 == 1)
    same8 = ((t // 8) == (s // 8)).astype(np.float32)
    return jnp.asarray(tri, BF16), jnp.asarray(lvl), jnp.asarray(same8)


def _hg_chunk(q_raw, z, v, lb, st, tri, lvl, same8, rev):
    c = HG_CHUNK
    q = q_raw * _sigmoid(q_raw)
    sig = _sigmoid(z)
    g = jnp.log(lb + (1.0 - lb) * sig)
    kk = (1.0 - lb) * (1.0 - sig)

    g1 = g.astype(BF16)
    r1 = g - g1.astype(F32)
    g2 = r1.astype(BF16)
    g3 = (r1 - g2.astype(F32)).astype(BF16)
    gi = _dot(tri, g1) + _dot(tri, g2) + _dot(tri, g3)
    gx = gi - g

    row = lax.broadcasted_iota(jnp.int32, (c, c), 0)
    col = lax.broadcasted_iota(jnp.int32, (c, c), 1)
    off = (row - col) if not rev else (col - row)

    scores = jnp.where(off == 0, jnp.sum(q * kk, axis=-1, keepdims=True), 0.0)
    for d in range(1, 8):
        sh = d if not rev else c - d
        kk_s = pltpu.roll(kk, sh, 0)
        g_s = pltpu.roll(gi, sh, 0)
        e = jnp.exp(jnp.minimum(gi - g_s, 0.0))
        band = jnp.sum(q * kk_s * e, axis=-1, keepdims=True)
        scores = scores + jnp.where(off == d, band, 0.0)
    scores = scores * same8

    for n, h in enumerate(HG_LEVELS):
        nb = c // h
        gi3 = gi.reshape(nb, h, HG_KEY)
        gx3 = gx.reshape(nb, h, HG_KEY)
        if not rev:
            a = jnp.exp(jnp.minimum(gi3 - gx3[:, 0:1, :], 0.0))
            b = jnp.exp(jnp.minimum(gi3[:, h - 1:h, :] - gi3, 0.0))
        else:
            a = jnp.exp(jnp.minimum(gi3 - gx3[:, h - 1:h, :], 0.0))
            b = jnp.exp(jnp.minimum(gi3[:, 0:1, :] - gi3, 0.0))
        qa = (q * a.reshape(c, HG_KEY)).astype(BF16)
        kb = (kk * b.reshape(c, HG_KEY)).astype(BF16)
        scores = scores + lvl[n] * _dot_nt(qa, kb)

    vb = v.astype(BF16)
    o = _dot(scores.astype(BF16), vb)
    o = o + _dot_nt((q * jnp.exp(gi)).astype(BF16), st.astype(BF16))
    g_end = gi[c - 1:c, :] if not rev else gi[0:1, :]
    kd = (kk * jnp.exp(jnp.minimum(g_end - gi, 0.0))).astype(BF16)
    st_new = st * jnp.exp(g_end) + _dot(v.T.astype(BF16), kd)
    return o, st_new


def _hg_kernel(*refs, has_s0, nchunk):
    if has_s0:
        (qf_ref, qb_ref, zf_ref, zb_ref, vf_ref, vb_ref, lb_ref, tri_ref, lvl_ref, s8_ref, s0_ref,
         of_ref, ob_ref, sfin_ref, st_scr) = refs
    else:
        (qf_ref, qb_ref, zf_ref, zb_ref, vf_ref, vb_ref, lb_ref, tri_ref, lvl_ref, s8_ref,
         of_ref, ob_ref, sfin_ref, st_scr) = refs
        s0_ref = None
    ci = pl.program_id(2)

    @pl.when(ci == 0)
    def _():
        for dr in range(2):
            st_scr[dr] = s0_ref[dr].T if has_s0 else jnp.zeros((HG_VAL, HG_KEY), F32)

    s8 = s8_ref[...]
    o_f, st_f = _hg_chunk(qf_ref[...], zf_ref[...], vf_ref[...], lb_ref[0:1, :], st_scr[0],
                          tri_ref[0], lvl_ref[0], s8, False)
    o_b, st_b = _hg_chunk(qb_ref[...], zb_ref[...], vb_ref[...], lb_ref[1:2, :], st_scr[1],
                          tri_ref[1], lvl_ref[1], s8, True)
    of_ref[...] = o_f
    ob_ref[...] = o_b
    st_scr[0] = st_f
    st_scr[1] = st_b

    @pl.when(ci == nchunk - 1)
    def _():
        sfin_ref[0] = st_f.T
        sfin_ref[1] = st_b.T


def _hgrn_call(h, lbs_l, s0, l, nseq, seq, row0):
    c = HG_CHUNK
    nchunk = seq // c
    has_s0 = s0 is not None
    tri, lvl, same8 = _hg_masks()
    rf = lambda b, hd, ci: row0 // c + b * nchunk + ci
    rb = lambda b, hd, ci: row0 // c + b * nchunk + (nchunk - 1 - ci)
    blk = lambda rfun, cb: pl.BlockSpec((c, LANES), lambda b, hd, ci: (rfun(b, hd, ci), cb + hd))
    const = lambda shape: pl.BlockSpec(shape, lambda b, hd, ci: (0,) * len(shape))
    in_specs = [blk(rf, 20), blk(rb, 20), blk(rf, 24), blk(rb, 28), blk(rf, 32), blk(rb, 32),
                pl.BlockSpec((2, LANES), lambda b, hd, ci: (0, hd)),
                const((2, c, c)), const((2, len(HG_LEVELS), c, c)), const((c, c))]
    args = [h, h, h, h, h, h, lbs_l, tri, lvl, same8]
    if has_s0:
        in_specs.append(pl.BlockSpec((None, None, 2, None, HG_KEY, HG_VAL), lambda b, hd, ci: (b, l, 0, hd, 0, 0)))
        args.append(s0)
    orow_f = lambda b, hd, ci: (b * nchunk + ci, hd)
    orow_b = lambda b, hd, ci: (b * nchunk + (nchunk - 1 - ci), hd)
    return pl.pallas_call(
        functools.partial(_hg_kernel, has_s0=has_s0, nchunk=nchunk),
        grid=(nseq, HG_HEADS, nchunk),
        in_specs=in_specs,
        out_specs=[pl.BlockSpec((c, LANES), orow_f), pl.BlockSpec((c, LANES), orow_b),
                   pl.BlockSpec((None, 2, None, HG_KEY, HG_VAL), lambda b, hd, ci: (b, 0, hd, 0, 0))],
        out_shape=[jax.ShapeDtypeStruct((nseq * seq, BRANCH_W), F32),
                   jax.ShapeDtypeStruct((nseq * seq, BRANCH_W), F32),
                   jax.ShapeDtypeStruct((nseq, 2, HG_HEADS, HG_KEY, HG_VAL), F32)],
        scratch_shapes=[pltpu.VMEM((2, HG_VAL, HG_KEY), F32)],
        compiler_params=_cparams(("arbitrary", "arbitrary", "arbitrary")),
        name="hgrn_lat" if has_s0 else "hgrn_ctx",
    )(*args)


def _layer_norm(y, g, b):
    mu = jnp.mean(y, axis=-1, keepdims=True)
    yc = y - mu
    var = jnp.mean(yc * yc, axis=-1, keepdims=True)
    return yc * lax.rsqrt(var + NORM_EPS) * g + b


def _merge_kernel(attc_ref, attl_ref, rgc_ref, rgl_ref, ofc_ref, ofl_ref, obc_ref, obl_ref,
                  hgate_ref, mg0_ref, mg1_ref, mg2_ref, x_ref, mod_ref,
                  hgn_ref, wbr_ref, wout_ref, lng_ref, lnb_ref, rw_ref, rb_ref, tri_ref,
                  x1_ref, u2_ref, idx_ref, rank_ref, wts_ref, cnt_ref, wbr_bf, wout_bf, cnt_scr, *, ctx_tiles):
    @pl.when(pl.program_id(0) == 0)
    def _():
        wbr_bf[...] = wbr_ref[...].astype(BF16)
        wout_bf[...] = wout_ref[...].astype(BF16)
        cnt_scr[...] = jnp.zeros(cnt_scr.shape, F32)

    is_ctx = pl.program_id(0) < ctx_tiles
    pick = lambda c_ref, l_ref: jnp.where(is_ctx, c_ref[...], l_ref[...])
    att = pick(attc_ref, attl_ref)
    rg = pick(rgc_ref, rgl_ref)
    o = pick(ofc_ref, ofl_ref) + pick(obc_ref, obl_ref)
    hgate = hgate_ref[...]
    hg_parts = []
    for hd in range(HG_HEADS):
        oh = o[:, hd * HG_VAL:(hd + 1) * HG_VAL]
        gh = hgate[:, hd * HG_VAL:(hd + 1) * HG_VAL]
        yh = oh * lax.rsqrt(jnp.mean(oh * oh, axis=-1, keepdims=True) + NORM_EPS) * hgn_ref[...]
        hg_parts.append(yh * (gh * _sigmoid(gh)))
    hg = jnp.concatenate(hg_parts, axis=-1)

    proj = _sigmoid(mg0_ref[...]) * _dot(att.astype(BF16), wbr_bf[0])
    proj = proj + _sigmoid(mg1_ref[...]) * _dot(rg.astype(BF16), wbr_bf[1])
    proj = proj + _sigmoid(mg2_ref[...]) * _dot(hg.astype(BF16), wbr_bf[2])
    mix = _dot(proj.astype(BF16), wout_bf[...])

    x1 = _layer_norm(DN_ALPHA * x_ref[...] + mod_ref[2:3, :] * mix, lng_ref[...], lnb_ref[...])
    x1_ref[...] = x1
    u2 = x1 * (1.0 + mod_ref[4:5, :]) + mod_ref[3:4, :]
    u2_ref[...] = u2.astype(BF16)

    logits = jnp.dot(u2, rw_ref[...], precision=HIGHEST, preferred_element_type=F32) + rb_ref[...]
    lane = lax.broadcasted_iota(jnp.int32, logits.shape, 1).astype(F32)
    idx_out = jnp.zeros(logits.shape, F32)
    wts_out = jnp.zeros(logits.shape, F32)
    chosen = jnp.zeros(logits.shape, F32)
    sels = []
    top0 = None
    den = None
    for k in range(TOP_K):
        m = jnp.max(logits, axis=-1, keepdims=True)
        sel = jnp.min(jnp.where(logits == m, lane, float(LANES)), axis=-1, keepdims=True)
        sels.append(sel)
        if k == 0:
            top0 = m
        e = jnp.exp(m - top0)
        den = e if den is None else den + e
        idx_out = jnp.where(lane == k, sel, idx_out)
        wts_out = jnp.where(lane == k, e, wts_out)
        chosen = jnp.where(lane == sel, 1.0, chosen)
        logits = jnp.where(lane == sel, -jnp.inf, logits)
    idx_ref[...] = idx_out.astype(jnp.int32)
    wts_ref[...] = wts_out * (1.0 / den)

    before = _dot(tri_ref[...], chosen.astype(BF16)) + cnt_scr[...]
    rank_out = jnp.zeros(logits.shape, F32)
    for k in range(TOP_K):
        rk = jnp.sum(jnp.where(lane == sels[k], before, 0.0), axis=-1, keepdims=True)
        rank_out = jnp.where(lane == k, rk, rank_out)
    rank_ref[...] = rank_out.astype(jnp.int32)
    cnt_scr[...] = cnt_scr[...] + jnp.sum(chosen, axis=0, keepdims=True)
    cnt_ref[...] = jnp.broadcast_to(cnt_scr[...], cnt_ref.shape).astype(jnp.int32)


def _merge_call(att, rg, o_f, o_b, h, x, mods, p, l):
    tm = 256
    ctx_tiles = N_CTX // tm
    rowb = lambda w, cb: pl.BlockSpec((tm, w), lambda i: (i, cb))
    ctxb = pl.BlockSpec((tm, BRANCH_W), lambda i: (jnp.minimum(i, ctx_tiles - 1), 0))
    latb = pl.BlockSpec((tm, BRANCH_W), lambda i: (jnp.maximum(i - ctx_tiles, 0), 0))
    const = lambda shape: pl.BlockSpec(shape, lambda i: (0,) * len(shape))
    rw = jnp.zeros((D_MODEL, LANES), F32).at[:, :N_EXPERTS].set(p['router_w'][l])
    rb = jnp.full((1, LANES), -1e30, F32).at[0, :N_EXPERTS].set(p['router_b'][l])
    tri = jnp.asarray(np.tril(np.ones((tm, tm), np.float32), -1), BF16)
    return pl.pallas_call(
        functools.partial(_merge_kernel, ctx_tiles=ctx_tiles),
        grid=(N_TOK // tm,),
        in_specs=[ctxb, latb, ctxb, latb, ctxb, latb, ctxb, latb,
                  rowb(BRANCH_W, 9), rowb(D_MODEL, 5), rowb(D_MODEL, 6), rowb(D_MODEL, 7),
                  rowb(D_MODEL, 0),
                  pl.BlockSpec((None, None, 6, D_MODEL), lambda i: (l, _cond_of_row_tile(i, tm), 0, 0)),
                  const((1, HG_VAL)), const((3, BRANCH_W, D_MODEL)), const((D_MODEL, D_MODEL)),
                  const((1, D_MODEL)), const((1, D_MODEL)), const((D_MODEL, LANES)), const((1, LANES)),
                  const((tm, tm))],
        out_specs=[rowb(D_MODEL, 0), rowb(D_MODEL, 0), rowb(LANES, 0), rowb(LANES, 0), rowb(LANES, 0),
                   const((8, LANES))],
        out_shape=[jax.ShapeDtypeStruct((N_TOK, D_MODEL), F32),
                   jax.ShapeDtypeStruct((N_TOK, D_MODEL), BF16),
                   jax.ShapeDtypeStruct((N_TOK, LANES), jnp.int32),
                   jax.ShapeDtypeStruct((N_TOK, LANES), jnp.int32),
                   jax.ShapeDtypeStruct((N_TOK, LANES), F32),
                   jax.ShapeDtypeStruct((8, LANES), jnp.int32)],
        scratch_shapes=[pltpu.VMEM((3, BRANCH_W, D_MODEL), BF16), pltpu.VMEM((D_MODEL, D_MODEL), BF16),
                        pltpu.VMEM((1, LANES), F32)],
        compiler_params=_cparams(("arbitrary",)),
        name="merge",
    )(att[0], att[1], rg[0], rg[1], o_f[0], o_f[1], o_b[0], o_b[1], h, h, h, h, x, mods,
      p['hg_norm'][l].reshape(1, HG_VAL), p['w_branch'][l], p['w_out'][l],
      p['ln1_g'][l].reshape(1, D_MODEL), p['ln1_b'][l].reshape(1, D_MODEL), rw, rb, tri)


MOE_NB = 256


def _moe_kernel(te_ref, tf_ref, nt_ref, x_ref, w1_ref, b1_ref, w2_ref, b2_ref, perm_ref, y_ref, w1_bf, w2_bf):
    i = pl.program_id(0)
    half = MOE_NB // 2

    @pl.when(tf_ref[i] == 1)
    def _():
        for b in range(2 * D_EXPERT // MOE_NB):
            blk = w1_ref[:, b * MOE_NB:(b + 1) * MOE_NB].astype(BF16)
            w1_bf[:, b * MOE_NB:(b + 1) * MOE_NB] = _dot(blk, perm_ref[...]).astype(BF16)
        w2_bf[...] = w2_ref[...].astype(BF16)

    @pl.when(i < nt_ref[0])
    def _():
        h = _dot(x_ref[...], w1_bf[...]) + b1_ref[...]
        acts = []
        for b in range(2 * D_EXPERT // MOE_NB):
            glu = jnp.minimum(h[:, b * MOE_NB:b * MOE_NB + half], SWIGLU_LIMIT)
            lin = jnp.clip(h[:, b * MOE_NB + half:(b + 1) * MOE_NB], -SWIGLU_LIMIT, SWIGLU_LIMIT)
            acts.append((glu * _sigmoid(SWIGLU_ALPHA * glu) * (lin + 1.0)).astype(BF16))
        act = jnp.concatenate(acts, axis=-1)
        y_ref[...] = _dot(act, w2_bf[...]) + b2_ref[...]

    @pl.when(i >= nt_ref[0])
    def _():
        y_ref[...] = jnp.zeros(y_ref.shape, F32)


def _moe_perm():
    half = MOE_NB // 2
    pm = np.zeros((MOE_NB, MOE_NB), np.float32)
    pm[2 * np.arange(half), np.arange(half)] = 1.0
    pm[2 * np.arange(half) + 1, half + np.arange(half)] = 1.0
    return jnp.asarray(pm, BF16)


def _moe_call(x_sorted, tile_expert, tile_first, n_used, w1, b1p, w2, b2, l):
    tm = MOE_TM
    emap = lambda i, te, tf, nt: (l, te[i], 0, 0)
    grid_spec = pltpu.PrefetchScalarGridSpec(
        num_scalar_prefetch=3,
        grid=(MOE_TILES,),
        in_specs=[pl.BlockSpec((tm, D_MODEL), lambda i, te, tf, nt: (i, 0)),
                  pl.BlockSpec((None, None, D_MODEL, 2 * D_EXPERT), emap),
                  pl.BlockSpec((None, None, 1, 2 * D_EXPERT), emap),
                  pl.BlockSpec((None, None, D_EXPERT, D_MODEL), emap),
                  pl.BlockSpec((None, None, 1, D_MODEL), emap),
                  pl.BlockSpec((MOE_NB, MOE_NB), lambda i, te, tf, nt: (0, 0))],
        out_specs=pl.BlockSpec((tm, D_MODEL), lambda i, te, tf, nt: (i, 0)),
        scratch_shapes=[pltpu.VMEM((D_MODEL, 2 * D_EXPERT), BF16), pltpu.VMEM((D_EXPERT, D_MODEL), BF16)],
    )
    return pl.pallas_call(
        _moe_kernel,
        grid_spec=grid_spec,
        out_shape=jax.ShapeDtypeStruct((MOE_TILES * tm, D_MODEL), F32),
        compiler_params=_cparams(("arbitrary",)),
        name="moe",
    )(tile_expert, tile_first, n_used, x_sorted, w1, b1p, w2, b2, _moe_perm())


def _route(idx, rank, counts):
    tm = MOE_TM
    tiles_e = (counts + tm - 1) // tm
    earlier = np.tril(np.ones((N_EXPERTS, N_EXPERTS), np.int32))
    tile_end = jnp.sum(earlier * tiles_e[None, :], axis=1)
    tile_start = tile_end - tiles_e
    pos = jnp.take(tile_start, idx) * tm + rank
    n_used = tile_end[N_EXPERTS - 1]
    tile_ids = jnp.arange(MOE_TILES, dtype=jnp.int32)
    tid = jnp.minimum(tile_ids, n_used - 1)
    tile_expert = jnp.sum((tile_end[None, :] <= tid[:, None]).astype(jnp.int32), axis=1)
    tile_first = ((tile_ids == jnp.take(tile_start, tile_expert)) & (tile_ids < n_used)).astype(jnp.int32)
    src_pair = jnp.zeros((MOE_TILES * tm,), jnp.int32).at[pos.reshape(-1)].set(
        jnp.arange(N_TOK * TOP_K, dtype=jnp.int32))
    return pos, src_pair // TOP_K, tile_expert, tile_first, n_used.reshape(1)


def _final_kernel(yg_ref, wts_ref, x1_ref, mod_ref, lng_ref, lnb_ref, o_ref):
    wts = wts_ref[...]
    ffn = wts[:, 0:1] * yg_ref[0]
    for k in range(1, TOP_K):
        ffn = ffn + wts[:, k:k + 1] * yg_ref[k]
    o_ref[...] = _layer_norm(DN_ALPHA * x1_ref[...] + mod_ref[5:6, :] * ffn, lng_ref[...], lnb_ref[...])


def _final_call(yg, wts, x1, mods, ln_g, ln_b, l):
    tm = 256
    const = lambda shape: pl.BlockSpec(shape, lambda i: (0,) * len(shape))
    return pl.pallas_call(
        _final_kernel,
        grid=(N_TOK // tm,),
        in_specs=[pl.BlockSpec((TOP_K, tm, D_MODEL), lambda i: (0, i, 0)),
                  pl.BlockSpec((tm, LANES), lambda i: (i, 0)),
                  pl.BlockSpec((tm, D_MODEL), lambda i: (i, 0)),
                  pl.BlockSpec((None, None, 6, D_MODEL), lambda i: (l, _cond_of_row_tile(i, tm), 0, 0)),
                  const((1, D_MODEL)), const((1, D_MODEL))],
        out_specs=pl.BlockSpec((tm, D_MODEL), lambda i: (i, 0)),
        out_shape=jax.ShapeDtypeStruct((N_TOK, D_MODEL), F32),
        compiler_params=_cparams(("arbitrary",)),
        name="final",
    )(yg, wts, x1, mods, ln_g.reshape(1, D_MODEL), ln_b.reshape(1, D_MODEL))


def kernel(x_prompt, x_sample, cache_attn_k, cache_attn_v, state_rglru, state_hgrn, c, c_ctx, w_ada, b_ada, w_in, da_lambda, da_subln, rg_conv_w, rg_conv_b, rg_gate_w, rg_gate_b, rg_lambda, hg_lb, hg_norm, w_branch, w_out, ln1_g, ln1_b, router_w, router_b, w1, b1, w2, b2, ln2_g, ln2_b):
    p = dict(hg_norm=hg_norm, w_branch=w_branch, w_out=w_out, ln1_g=ln1_g, ln1_b=ln1_b,
             router_w=router_w, router_b=router_b)

    x = jnp.concatenate([x_prompt.reshape(N_CTX, D_MODEL), x_sample.reshape(N_LAT, D_MODEL)], axis=0)
    cond = jnp.concatenate([c_ctx[None, :], c, jnp.zeros((N_COND - 1 - DEC_BATCH, D_MODEL), F32)], axis=0)
    mods = _adaln_all(cond, w_ada, b_ada).reshape(DEPTH, N_COND, 6, D_MODEL)

    pr = jax.nn.softmax(hg_lb.astype(F32), axis=0)
    lbs = jnp.cumsum(pr, axis=0) - pr[0]
    dl = da_lambda.astype(F32)
    lam_all = jnp.exp(jnp.sum(dl[:, 0] * dl[:, 1], -1)) - jnp.exp(jnp.sum(dl[:, 2] * dl[:, 3], -1))

    cache_k = cache_attn_k.reshape(DEC_BATCH, DEPTH, PAST_LEN, BRANCH_W)
    cache_v = cache_attn_v.reshape(DEC_BATCH, DEPTH, PAST_LEN, BRANCH_W)

    b1p = b1.reshape(DEPTH, N_EXPERTS, 2 * D_EXPERT // MOE_NB, MOE_NB // 2, 2)
    b1p = jnp.swapaxes(b1p, -1, -2).reshape(DEPTH, N_EXPERTS, 1, 2 * D_EXPERT)
    b2r = b2.reshape(DEPTH, N_EXPERTS, 1, D_MODEL)

    ks, vs, rgs, hgs = [], [], [], []
    for l in range(DEPTH):
        lambda_init = 0.8 - 0.6 * math.exp(-0.3 * l)
        h = _in_proj(x, mods, w_in, l)
        ks.append(h[:N_CTX, BRANCH_W:2 * BRANCH_W].reshape(BATCH, SEQ, DA_HEADS, 2, DA_HEAD_DIM))
        vs.append(h[:N_CTX, 2 * BRANCH_W:3 * BRANCH_W].reshape(BATCH, SEQ, DA_HEADS, DA_V_DIM))

        att = _attention(h, cache_k, cache_v, lam_all[l] + lambda_init, da_subln[l], l, 1.0 - lambda_init)

        wg, bg = _rg_gate_dense(rg_gate_w[l], rg_gate_b[l])
        rg_c, hl_c = _rglru_call(h, rg_conv_w[l], rg_conv_b[l], wg, bg, rg_lambda[l], None, BATCH, SEQ, 0)
        rg_l, _ = _rglru_call(h, rg_conv_w[l], rg_conv_b[l], wg, bg, rg_lambda[l], state_rglru[:, l],
                              DEC_BATCH, DEC_SEQ, N_CTX // DEC_SEQ)
        rgs.append(hl_c)

        of_c, ob_c, sl_c = _hgrn_call(h, lbs[l], None, l, BATCH, SEQ, 0)
        of_l, ob_l, _ = _hgrn_call(h, lbs[l], state_hgrn, l, DEC_BATCH, DEC_SEQ, N_CTX)
        hgs.append(sl_c)

        x1, u2, idx, rank, wts, cnt = _merge_call(att, (rg_c, rg_l), (of_c, of_l), (ob_c, ob_l), h, x, mods, p, l)

        pos, src_tok, tile_expert, tile_first, n_used = _route(idx[:, :TOP_K], rank[:, :TOP_K], cnt[0, :N_EXPERTS])
        x_sorted = jnp.take(u2, src_tok, axis=0)
        y_sorted = _moe_call(x_sorted, tile_expert, tile_first, n_used, w1, b1p, w2, b2r, l)
        yg = jnp.take(y_sorted, pos.T.reshape(-1), axis=0).reshape(TOP_K, N_TOK, D_MODEL)
        x = _final_call(yg, wts, x1, mods, ln2_g[l], ln2_b[l], l)

    y_prompt = x[:N_CTX].reshape(BATCH, SEQ, D_MODEL)
    y_sample = x[N_CTX:].reshape(DEC_BATCH, DEC_SEQ, D_MODEL)
    return (y_prompt, y_sample, jnp.stack(ks, axis=1), jnp.stack(vs, axis=1),
            jnp.stack(rgs, axis=1), jnp.stack(hgs, axis=1))
```

```python
import functools
import math

import numpy as np
import jax
import jax.numpy as jnp
from jax import lax
from jax.experimental import pallas as pl
from jax.experimental.pallas import tpu as pltpu

F32 = jnp.float32
BF16 = jnp.bfloat16
HIGHEST = lax.Precision.HIGHEST

D_MODEL = 1024
BATCH = 16
SEQ = 256
DEPTH = 4
DEC_BATCH = 4
DEC_SEQ = 1024
PAST_LEN = 256
GRID_W = 64
BRANCH_W = 512
DA_HEADS = 4
DA_HEAD_DIM = 64
DA_V_DIM = 128
ROPE_BASE = 10000.0
RG_WIDTH = 512
RG_BLOCKS = 8
RG_BLOCK_W = 64
RG_CONV_W = 4
RG_C = 8.0
HG_HEADS = 4
HG_KEY = 128
HG_VAL = 128
N_EXPERTS = 32
TOP_K = 4
D_EXPERT = 1024
SWIGLU_ALPHA = 1.702
SWIGLU_LIMIT = 7.0
DN_ALPHA = (2 * DEPTH) ** 0.25
NORM_EPS = 1e-5
D_IN = 10 * BRANCH_W + 3 * D_MODEL

N_CTX = BATCH * SEQ
N_LAT = DEC_BATCH * DEC_SEQ
N_TOK = N_CTX + N_LAT
N_COND = 8

LANES = 128
VMEM_LIMIT = 56 * 1024 * 1024

HG_CHUNK = 128
HG_LEVELS = (8, 16, 32, 64)
MOE_TM = 256
MOE_TILES = (N_TOK * TOP_K) // MOE_TM + N_EXPERTS


def _cparams(sem):
    return pltpu.CompilerParams(dimension_semantics=sem, vmem_limit_bytes=VMEM_LIMIT)


def _sigmoid(x):
    return 1.0 / (1.0 + jnp.exp(-x))


def _dot(a, b):
    return jnp.dot(a, b, preferred_element_type=F32)


def _dot_nt(a, b):
    return lax.dot_general(a, b, (((1,), (1,)), ((), ())), preferred_element_type=F32)


def _cond_of_row_tile(i, tm):
    r = i * tm
    return jnp.where(r < N_CTX, 0, 1 + (r - N_CTX) // DEC_SEQ)


def _ada_kernel(c_ref, w_ref, b_ref, o_ref):
    c = c_ref[...]
    s = c * _sigmoid(c)
    o_ref[0] = jnp.dot(s, w_ref[0], precision=HIGHEST, preferred_element_type=F32) + b_ref[0]


def _adaln_all(cond, w_ada, b_ada):
    tn = 1536
    return pl.pallas_call(
        _ada_kernel,
        grid=(DEPTH, 6 * D_MODEL // tn),
        in_specs=[pl.BlockSpec((N_COND, D_MODEL), lambda l, j: (0, 0)),
                  pl.BlockSpec((1, D_MODEL, tn), lambda l, j: (l, 0, j)),
                  pl.BlockSpec((1, 1, tn), lambda l, j: (l, 0, j))],
        out_specs=pl.BlockSpec((1, N_COND, tn), lambda l, j: (l, 0, j)),
        out_shape=jax.ShapeDtypeStruct((DEPTH, N_COND, 6 * D_MODEL), F32),
        compiler_params=_cparams(("arbitrary", "arbitrary")),
        name="adaln",
    )(cond, w_ada, b_ada.reshape(DEPTH, 1, 6 * D_MODEL))


def _in_kernel(x_ref, mod_ref, w_ref, o_ref, wbf_ref):
    @pl.when(pl.program_id(1) == 0)
    def _():
        wbf_ref[...] = w_ref[...].astype(BF16)

    u = x_ref[...] * (1.0 + mod_ref[1:2, :]) + mod_ref[0:1, :]
    o_ref[...] = _dot(u.astype(BF16), wbf_ref[...])


def _in_proj(x, mods, w_in, l):
    tm, tn = 1024, 1024
    return pl.pallas_call(
        _in_kernel,
        grid=(D_IN // tn, N_TOK // tm),
        in_specs=[pl.BlockSpec((tm, D_MODEL), lambda j, i: (i, 0)),
                  pl.BlockSpec((None, None, 6, D_MODEL), lambda j, i: (l, _cond_of_row_tile(i, tm), 0, 0)),
                  pl.BlockSpec((None, D_MODEL, tn), lambda j, i: (l, 0, j))],
        out_specs=pl.BlockSpec((tm, tn), lambda j, i: (i, j)),
        out_shape=jax.ShapeDtypeStruct((N_TOK, D_IN), F32),
        scratch_shapes=[pltpu.VMEM((D_MODEL, tn), BF16)],
        compiler_params=_cparams(("arbitrary", "arbitrary")),
        name="in_proj",
    )(x, mods, w_in)


def _rope(x, cos, sin_signed):
    lane = lax.broadcasted_iota(jnp.int32, x.shape, 1)
    first = (lane & 31) < 16
    partner = jnp.where(first, pltpu.roll(x, LANES - 16, 1), pltpu.roll(x, 16, 1))
    return x * cos + partner * sin_signed


def _attn_body(q, keys, vals, lam, subln, out_scale):
    scale = DA_HEAD_DIM ** -0.5
    lane = lax.broadcasted_iota(jnp.int32, q.shape, 1)
    kb = [k.astype(BF16) for k in keys]
    vb = [v.astype(BF16) for v in vals]
    acc = None
    for m in range(2):
        in_map = (lane < DA_HEAD_DIM) if m == 0 else (lane >= DA_HEAD_DIM)
        qm = jnp.where(in_map, q, 0.0).astype(BF16)
        s = [_dot_nt(qm, k) * scale for k in kb]
        mx = s[0].max(axis=-1, keepdims=True)
        for si in s[1:]:
            mx = jnp.maximum(mx, si.max(axis=-1, keepdims=True))
        e = [jnp.exp(si - mx) for si in s]
        den = e[0].sum(axis=-1, keepdims=True)
        for ei in e[1:]:
            den = den + ei.sum(axis=-1, keepdims=True)
        coef = (1.0 / den) if m == 0 else (-lam[:, 0:1] / den)
        for ei, v in zip(e, vb):
            part = _dot((ei * coef).astype(BF16), v)
            acc = part if acc is None else acc + part
    y = acc * lax.rsqrt(jnp.mean(acc * acc, axis=-1, keepdims=True) + NORM_EPS)
    return y * subln * out_scale


def _attn_ctx_kernel(lam_ref, sub_ref, q_ref, k_ref, v_ref, o_ref, *, out_scale):
    o_ref[...] = _attn_body(q_ref[...], [k_ref[...]], [v_ref[...]], lam_ref[...], sub_ref[...], out_scale)


def _attn_lat_kernel(lam_ref, sub_ref, q_ref, k_ref, v_ref, kc_ref, vc_ref, cq_ref, sq_ref, ck_ref, sk_ref,
                     o_ref, *, out_scale):
    q = _rope(q_ref[...], cq_ref[...], sq_ref[...])
    k = _rope(k_ref[...], ck_ref[...], sk_ref[...])
    o_ref[...] = _attn_body(q, [k, kc_ref[...]], [v_ref[...], vc_ref[...]], lam_ref[...], sub_ref[...], out_scale)


def _rope_tables():
    t = np.arange(DEC_SEQ)
    row = (t // GRID_W).astype(np.float64)
    col = (t % GRID_W).astype(np.float64)
    d = DA_HEAD_DIM // 2
    inv = ROPE_BASE ** (-np.arange(0, d, 2, dtype=np.float64) / d)
    a_row = row[:, None] * inv[None, :]
    a_col = col[:, None] * inv[None, :]
    cos = np.concatenate([np.cos(a_row), np.cos(a_row), np.cos(a_col), np.cos(a_col)], -1)
    sin = np.concatenate([-np.sin(a_row), np.sin(a_row), -np.sin(a_col), np.sin(a_col)], -1)
    return (jnp.asarray(np.tile(cos, (1, 2)), F32), jnp.asarray(np.tile(sin, (1, 2)), F32))


def _attention(h, cache_k, cache_v, lam, subln, l, out_scale):
    lam_v = jnp.full((1, LANES), lam, F32)
    sub_v = subln.reshape(1, DA_V_DIM)
    small = pl.BlockSpec((1, LANES), lambda *a: (0, 0))
    att_ctx = pl.pallas_call(
        functools.partial(_attn_ctx_kernel, out_scale=out_scale),
        grid=(BATCH, DA_HEADS),
        in_specs=[small, small,
                  pl.BlockSpec((SEQ, LANES), lambda b, hd: (b, hd)),
                  pl.BlockSpec((SEQ, LANES), lambda b, hd: (b, 4 + hd)),
                  pl.BlockSpec((SEQ, LANES), lambda b, hd: (b, 8 + hd))],
        out_specs=pl.BlockSpec((SEQ, LANES), lambda b, hd: (b, hd)),
        out_shape=jax.ShapeDtypeStruct((N_CTX, BRANCH_W), F32),
        compiler_params=_cparams(("arbitrary", "arbitrary")),
        name="attn_ctx",
    )(lam_v, sub_v, h, h, h)

    tq = 256
    nq = DEC_SEQ // tq
    cos, sin = _rope_tables()
    row0 = N_CTX // DEC_SEQ
    att_lat = pl.pallas_call(
        functools.partial(_attn_lat_kernel, out_scale=out_scale),
        grid=(DEC_BATCH, DA_HEADS, nq),
        in_specs=[small, small,
                  pl.BlockSpec((tq, LANES), lambda b, hd, qi: ((row0 + b) * nq + qi, hd)),
                  pl.BlockSpec((DEC_SEQ, LANES), lambda b, hd, qi: (row0 + b, 4 + hd)),
                  pl.BlockSpec((DEC_SEQ, LANES), lambda b, hd, qi: (row0 + b, 8 + hd)),
                  pl.BlockSpec((None, None, PAST_LEN, LANES), lambda b, hd, qi: (b, l, 0, hd)),
                  pl.BlockSpec((None, None, PAST_LEN, LANES), lambda b, hd, qi: (b, l, 0, hd)),
                  pl.BlockSpec((tq, LANES), lambda b, hd, qi: (qi, 0)),
                  pl.BlockSpec((tq, LANES), lambda b, hd, qi: (qi, 0)),
                  pl.BlockSpec((DEC_SEQ, LANES), lambda b, hd, qi: (0, 0)),
                  pl.BlockSpec((DEC_SEQ, LANES), lambda b, hd, qi: (0, 0))],
        out_specs=pl.BlockSpec((tq, LANES), lambda b, hd, qi: (b * nq + qi, hd)),
        out_shape=jax.ShapeDtypeStruct((N_LAT, BRANCH_W), F32),
        compiler_params=_cparams(("arbitrary", "arbitrary", "arbitrary")),
        name="attn_lat",
    )(lam_v, sub_v, h, h, h, cache_k, cache_v, cos, sin, cos, sin)
    return att_ctx, att_lat


def _gelu_tanh(x):
    return 0.5 * x * (1.0 + jnp.tanh(math.sqrt(2.0 / math.pi) * (x + 0.044715 * (x * x * x))))


def _rg_kernel(*refs, seq, has_h0):
    if has_h0:
        rx_ref, gate_ref, cw_ref, cb_ref, wg_ref, bg_ref, lam_ref, h0_ref = refs[:8]
        rest = refs[8:]
    else:
        rx_ref, gate_ref, cw_ref, cb_ref, wg_ref, bg_ref, lam_ref = refs[:7]
        h0_ref = None
        rest = refs[7:]
    out_ref, hl_ref, a_scr, b_scr, h_scr = rest

    x = rx_ref[...]
    row = lax.broadcasted_iota(jnp.int32, x.shape, 0)
    xr = cb_ref[...] + cw_ref[2:3, :] * x
    for j in (0, 1, 3):
        d = j - RG_CONV_W // 2
        shifted = pltpu.roll(x, (-d) % seq, 0)
        valid = (row + d >= 0) & (row + d < seq)
        xr = xr + cw_ref[j:j + 1, :] * jnp.where(valid, shifted, 0.0)

    g = _dot(xr.astype(BF16), wg_ref[...]) + bg_ref[...]
    for dr in range(2):
        r = _sigmoid(g[:, (2 * dr) * RG_WIDTH:(2 * dr + 1) * RG_WIDTH])
        i = _sigmoid(g[:, (2 * dr + 1) * RG_WIDTH:(2 * dr + 2) * RG_WIDTH])
        lam = lam_ref[dr:dr + 1, :]
        softplus_neg = jnp.maximum(-lam, 0.0) + jnp.log(1.0 + jnp.exp(-jnp.abs(lam)))
        log_a = (-RG_C * softplus_neg) * r
        a_scr[dr] = jnp.exp(log_a)
        b_scr[dr] = jnp.sqrt(1.0 - jnp.exp(2.0 * log_a)) * i * xr

    if has_h0:
        hf0, hb0 = h0_ref[0:1, :], h0_ref[1:2, :]
    else:
        hf0 = hb0 = jnp.zeros((1, RG_WIDTH), F32)

    def step(t, carry):
        hf, hb = carry
        tb = seq - 1 - t
        hf = a_scr[0, pl.ds(t, 1), :] * hf + b_scr[0, pl.ds(t, 1), :]
        h_scr[0, pl.ds(t, 1), :] = hf
        hb = a_scr[1, pl.ds(tb, 1), :] * hb + b_scr[1, pl.ds(tb, 1), :]
        h_scr[1, pl.ds(tb, 1), :] = hb
        return hf, hb

    hf, hb = lax.fori_loop(0, seq, step, (hf0, hb0), unroll=8)
    hl_ref[0:1, :] = hf
    hl_ref[1:2, :] = hb
    out_ref[...] = (h_scr[0] + h_scr[1]) * _gelu_tanh(gate_ref[...])


def _rglru_call(h, conv_w, conv_b, wg, bg, lam, h0, nseq, seq, row_block0):
    has_h0 = h0 is not None
    full = lambda shape: pl.BlockSpec(shape, lambda b: (0,) * len(shape))
    in_specs = [pl.BlockSpec((seq, RG_WIDTH), lambda b: (row_block0 + b, 3)),
                pl.BlockSpec((seq, RG_WIDTH), lambda b: (row_block0 + b, 4)),
                full((RG_CONV_W, RG_WIDTH)), full((1, RG_WIDTH)),
                full((RG_WIDTH, 4 * RG_WIDTH)), full((1, 4 * RG_WIDTH)), full((2, RG_WIDTH))]
    args = [h, h, conv_w, conv_b.reshape(1, RG_WIDTH), wg, bg, lam]
    if has_h0:
        in_specs.append(pl.BlockSpec((None, 2, RG_WIDTH), lambda b: (b, 0, 0)))
        args.append(h0)
    return pl.pallas_call(
        functools.partial(_rg_kernel, seq=seq, has_h0=has_h0),
        grid=(nseq,),
        in_specs=in_specs,
        out_specs=[pl.BlockSpec((seq, RG_WIDTH), lambda b: (b, 0)),
                   pl.BlockSpec((None, 2, RG_WIDTH), lambda b: (b, 0, 0))],
        out_shape=[jax.ShapeDtypeStruct((nseq * seq, RG_WIDTH), F32),
                   jax.ShapeDtypeStruct((nseq, 2, RG_WIDTH), F32)],
        scratch_shapes=[pltpu.VMEM((2, seq, RG_WIDTH), F32)] * 3,
        compiler_params=_cparams(("arbitrary",)),
        name="rglru_lat" if has_h0 else "rglru_ctx",
    )(*args)


def _rg_gate_dense(gate_w, gate_b):
    eye = jnp.eye(RG_BLOCKS, dtype=F32)
    dense = jnp.einsum('dgncf,nm->dgncmf', gate_w, eye).reshape(2, 2, RG_WIDTH, RG_WIDTH)
    wg = jnp.transpose(dense, (2, 0, 1, 3)).reshape(RG_WIDTH, 4 * RG_WIDTH)
    return wg.astype(BF16), gate_b.reshape(1, 4 * RG_WIDTH)


def _hg_masks():
    c = HG_CHUNK
    t = np.arange(c)[:, None]
    s = np.arange(c)[None, :]
    tri = np.stack([s <= t, s >= t]).astype(np.float32)
    lvl = np.zeros((2, len(HG_LEVELS), c, c), np.float32)
    for n, h in enumerate(HG_LEVELS):
        same = (t // (2 * h)) == (s // (2 * h))
        lvl[0, n] = same & ((t // h) % 2 == 1) & ((s // h) % 2 == 0)
        lvl[1, n] = same & ((t // h) % 2 == 0) & ((s // h) % 2 == 1)
    same8 = ((t // 8) == (s // 8)).astype(np.float32)
    return jnp.asarray(tri, BF16), jnp.asarray(lvl), jnp.asarray(same8)


def _hg_chunk(q_raw, z, v, lb, st, tri, lvl, same8, rev):
    c = HG_CHUNK
    q = q_raw * _sigmoid(q_raw)
    sig = _sigmoid(z)
    g = jnp.log(lb + (1.0 - lb) * sig)
    kk = (1.0 - lb) * (1.0 - sig)

    g1 = g.astype(BF16)
    r1 = g - g1.astype(F32)
    g2 = r1.astype(BF16)
    g3 = (r1 - g2.astype(F32)).astype(BF16)
    gi = _dot(tri, g1) + _dot(tri, g2) + _dot(tri, g3)
    gx = gi - g

    row = lax.broadcasted_iota(jnp.int32, (c, c), 0)
    col = lax.broadcasted_iota(jnp.int32, (c, c), 1)
    off = (row - col) if not rev else (col - row)

    scores = jnp.where(off == 0, jnp.sum(q * kk, axis=-1, keepdims=True), 0.0)
    for d in range(1, 8):
        sh = d if not rev else c - d
        kk_s = pltpu.roll(kk, sh, 0)
        g_s = pltpu.roll(gi, sh, 0)
        e = jnp.exp(jnp.minimum(gi - g_s, 0.0))
        band = jnp.sum(q * kk_s * e, axis=-1, keepdims=True)
        scores = scores + jnp.where(off == d, band, 0.0)
    scores = scores * same8

    for n, h in enumerate(HG_LEVELS):
        nb = c // h
        gi3 = gi.reshape(nb, h, HG_KEY)
        gx3 = gx.reshape(nb, h, HG_KEY)
        if not rev:
            a = jnp.exp(jnp.minimum(gi3 - gx3[:, 0:1, :], 0.0))
            b = jnp.exp(jnp.minimum(gi3[:, h - 1:h, :] - gi3, 0.0))
        else:
            a = jnp.exp(jnp.minimum(gi3 - gx3[:, h - 1:h, :], 0.0))
            b = jnp.exp(jnp.minimum(gi3[:, 0:1, :] - gi3, 0.0))
        qa = (q * a.reshape(c, HG_KEY)).astype(BF16)
        kb = (kk * b.reshape(c, HG_KEY)).astype(BF16)
        scores = scores + lvl[n] * _dot_nt(qa, kb)

    vb = v.astype(BF16)
    o = _dot(scores.astype(BF16), vb)
    o = o + _dot_nt((q * jnp.exp(gi)).astype(BF16), st.astype(BF16))
    g_end = gi[c - 1:c, :] if not rev else gi[0:1, :]
    kd = (kk * jnp.exp(jnp.minimum(g_end - gi, 0.0))).astype(BF16)
    st_new = st * jnp.exp(g_end) + _dot(v.T.astype(BF16), kd)
    return o, st_new


def _hg_kernel(*refs, has_s0, nchunk):
    if has_s0:
        (qf_ref, qb_ref, zf_ref, zb_ref, vf_ref, vb_ref, lb_ref, tri_ref, lvl_ref, s8_ref, s0_ref,
         of_ref, ob_ref, sfin_ref, st_scr) = refs
    else:
        (qf_ref, qb_ref, zf_ref, zb_ref, vf_ref, vb_ref, lb_ref, tri_ref, lvl_ref, s8_ref,
         of_ref, ob_ref, sfin_ref, st_scr) = refs
        s0_ref = None
    ci = pl.program_id(2)

    @pl.when(ci == 0)
    def _():
        for dr in range(2):
            st_scr[dr] = s0_ref[dr].T if has_s0 else jnp.zeros((HG_VAL, HG_KEY), F32)

    s8 = s8_ref[...]
    o_f, st_f = _hg_chunk(qf_ref[...], zf_ref[...], vf_ref[...], lb_ref[0:1, :], st_scr[0],
                          tri_ref[0], lvl_ref[0], s8, False)
    o_b, st_b = _hg_chunk(qb_ref[...], zb_ref[...], vb_ref[...], lb_ref[1:2, :], st_scr[1],
                          tri_ref[1], lvl_ref[1], s8, True)
    of_ref[...] = o_f
    ob_ref[...] = o_b
    st_scr[0] = st_f
    st_scr[1] = st_b

    @pl.when(ci == nchunk - 1)
    def _():
        sfin_ref[0] = st_f.T
        sfin_ref[1] = st_b.T


def _hgrn_call(h, lbs_l, s0, l, nseq, seq, row0):
    c = HG_CHUNK
    nchunk = seq // c
    has_s0 = s0 is not None
    tri, lvl, same8 = _hg_masks()
    rf = lambda b, hd, ci: row0 // c + b * nchunk + ci
    rb = lambda b, hd, ci: row0 // c + b * nchunk + (nchunk - 1 - ci)
    blk = lambda rfun, cb: pl.BlockSpec((c, LANES), lambda b, hd, ci: (rfun(b, hd, ci), cb + hd))
    const = lambda shape: pl.BlockSpec(shape, lambda b, hd, ci: (0,) * len(shape))
    in_specs = [blk(rf, 20), blk(rb, 20), blk(rf, 24), blk(rb, 28), blk(rf, 32), blk(rb, 32),
                pl.BlockSpec((2, LANES), lambda b, hd, ci: (0, hd)),
                const((2, c, c)), const((2, len(HG_LEVELS), c, c)), const((c, c))]
    args = [h, h, h, h, h, h, lbs_l, tri, lvl, same8]
    if has_s0:
        in_specs.append(pl.BlockSpec((None, None, 2, None, HG_KEY, HG_VAL), lambda b, hd, ci: (b, l, 0, hd, 0, 0)))
        args.append(s0)
    orow_f = lambda b, hd, ci: (b * nchunk + ci, hd)
    orow_b = lambda b, hd, ci: (b * nchunk + (nchunk - 1 - ci), hd)
    return pl.pallas_call(
        functools.partial(_hg_kernel, has_s0=has_s0, nchunk=nchunk),
        grid=(nseq, HG_HEADS, nchunk),
        in_specs=in_specs,
        out_specs=[pl.BlockSpec((c, LANES), orow_f), pl.BlockSpec((c, LANES), orow_b),
                   pl.BlockSpec((None, 2, None, HG_KEY, HG_VAL), lambda b, hd, ci: (b, 0, hd, 0, 0))],
        out_shape=[jax.ShapeDtypeStruct((nseq * seq, BRANCH_W), F32),
                   jax.ShapeDtypeStruct((nseq * seq, BRANCH_W), F32),
                   jax.ShapeDtypeStruct((nseq, 2, HG_HEADS, HG_KEY, HG_VAL), F32)],
        scratch_shapes=[pltpu.VMEM((2, HG_VAL, HG_KEY), F32)],
        compiler_params=_cparams(("arbitrary", "arbitrary", "arbitrary")),
        name="hgrn_lat" if has_s0 else "hgrn_ctx",
    )(*args)


def _layer_norm(y, g, b):
    mu = jnp.mean(y, axis=-1, keepdims=True)
    yc = y - mu
    var = jnp.mean(yc * yc, axis=-1, keepdims=True)
    return yc * lax.rsqrt(var + NORM_EPS) * g + b


def _merge_kernel(attc_ref, attl_ref, rgc_ref, rgl_ref, ofc_ref, ofl_ref, obc_ref, obl_ref,
                  hgate_ref, mg0_ref, mg1_ref, mg2_ref, x_ref, mod_ref,
                  hgn_ref, wbr_ref, wout_ref, lng_ref, lnb_ref, rw_ref, rb_ref, tri_ref,
                  x1_ref, u2_ref, idx_ref, rank_ref, wts_ref, cnt_ref, wbr_bf, wout_bf, cnt_scr, *, ctx_tiles):
    @pl.when(pl.program_id(0) == 0)
    def _():
        wbr_bf[...] = wbr_ref[...].astype(BF16)
        wout_bf[...] = wout_ref[...].astype(BF16)
        cnt_scr[...] = jnp.zeros(cnt_scr.shape, F32)

    is_ctx = pl.program_id(0) < ctx_tiles
    pick = lambda c_ref, l_ref: jnp.where(is_ctx, c_ref[...], l_ref[...])
    att = pick(attc_ref, attl_ref)
    rg = pick(rgc_ref, rgl_ref)
    o = pick(ofc_ref, ofl_ref) + pick(obc_ref, obl_ref)
    hgate = hgate_ref[...]
    hg_parts = []
    for hd in range(HG_HEADS):
        oh = o[:, hd * HG_VAL:(hd + 1) * HG_VAL]
        gh = hgate[:, hd * HG_VAL:(hd + 1) * HG_VAL]
        yh = oh * lax.rsqrt(jnp.mean(oh * oh, axis=-1, keepdims=True) + NORM_EPS) * hgn_ref[...]
        hg_parts.append(yh * (gh * _sigmoid(gh)))
    hg = jnp.concatenate(hg_parts, axis=-1)

    proj = _sigmoid(mg0_ref[...]) * _dot(att.astype(BF16), wbr_bf[0])
    proj = proj + _sigmoid(mg1_ref[...]) * _dot(rg.astype(BF16), wbr_bf[1])
    proj = proj + _sigmoid(mg2_ref[...]) * _dot(hg.astype(BF16), wbr_bf[2])
    mix = _dot(proj.astype(BF16), wout_bf[...])

    x1 = _layer_norm(DN_ALPHA * x_ref[...] + mod_ref[2:3, :] * mix, lng_ref[...], lnb_ref[...])
    x1_ref[...] = x1
    u2 = x1 * (1.0 + mod_ref[4:5, :]) + mod_ref[3:4, :]
    u2_ref[...] = u2.astype(BF16)

    logits = jnp.dot(u2, rw_ref[...], precision=HIGHEST, preferred_element_type=F32) + rb_ref[...]
    lane = lax.broadcasted_iota(jnp.int32, logits.shape, 1).astype(F32)
    idx_out = jnp.zeros(logits.shape, F32)
    wts_out = jnp.zeros(logits.shape, F32)
    chosen = jnp.zeros(logits.shape, F32)
    sels = []
    top0 = None
    den = None
    for k in range(TOP_K):
        m = jnp.max(logits, axis=-1, keepdims=True)
        sel = jnp.min(jnp.where(logits == m, lane, float(LANES)), axis=-1, keepdims=True)
        sels.append(sel)
        if k == 0:
            top0 = m
        e = jnp.exp(m - top0)
        den = e if den is None else den + e
        idx_out = jnp.where(lane == k, sel, idx_out)
        wts_out = jnp.where(lane == k, e, wts_out)
        chosen = jnp.where(lane == sel, 1.0, chosen)
        logits = jnp.where(lane == sel, -jnp.inf, logits)
    idx_ref[...] = idx_out.astype(jnp.int32)
    wts_ref[...] = wts_out * (1.0 / den)

    before = _dot(tri_ref[...], chosen.astype(BF16)) + cnt_scr[...]
    rank_out = jnp.zeros(logits.shape, F32)
    for k in range(TOP_K):
        rk = jnp.sum(jnp.where(lane == sels[k], before, 0.0), axis=-1, keepdims=True)
        rank_out = jnp.where(lane == k, rk, rank_out)
    rank_ref[...] = rank_out.astype(jnp.int32)
    cnt_scr[...] = cnt_scr[...] + jnp.sum(chosen, axis=0, keepdims=True)
    cnt_ref[...] = jnp.broadcast_to(cnt_scr[...], cnt_ref.shape).astype(jnp.int32)


def _merge_call(att, rg, o_f, o_b, h, x, mods, p, l):
    tm = 256
    ctx_tiles = N_CTX // tm
    rowb = lambda w, cb: pl.BlockSpec((tm, w), lambda i: (i, cb))
    ctxb = pl.BlockSpec((tm, BRANCH_W), lambda i: (jnp.minimum(i, ctx_tiles - 1), 0))
    latb = pl.BlockSpec((tm, BRANCH_W), lambda i: (jnp.maximum(i - ctx_tiles, 0), 0))
    const = lambda shape: pl.BlockSpec(shape, lambda i: (0,) * len(shape))
    rw = jnp.zeros((D_MODEL, LANES), F32).at[:, :N_EXPERTS].set(p['router_w'][l])
    rb = jnp.full((1, LANES), -1e30, F32).at[0, :N_EXPERTS].set(p['router_b'][l])
    tri = jnp.asarray(np.tril(np.ones((tm, tm), np.float32), -1), BF16)
    return pl.pallas_call(
        functools.partial(_merge_kernel, ctx_tiles=ctx_tiles),
        grid=(N_TOK // tm,),
        in_specs=[ctxb, latb, ctxb, latb, ctxb, latb, ctxb, latb,
                  rowb(BRANCH_W, 9), rowb(D_MODEL, 5), rowb(D_MODEL, 6), rowb(D_MODEL, 7),
                  rowb(D_MODEL, 0),
                  pl.BlockSpec((None, None, 6, D_MODEL), lambda i: (l, _cond_of_row_tile(i, tm), 0, 0)),
                  const((1, HG_VAL)), const((3, BRANCH_W, D_MODEL)), const((D_MODEL, D_MODEL)),
                  const((1, D_MODEL)), const((1, D_MODEL)), const((D_MODEL, LANES)), const((1, LANES)),
                  const((tm, tm))],
        out_specs=[rowb(D_MODEL, 0), rowb(D_MODEL, 0), rowb(LANES, 0), rowb(LANES, 0), rowb(LANES, 0),
                   const((8, LANES))],
        out_shape=[jax.ShapeDtypeStruct((N_TOK, D_MODEL), F32),
                   jax.ShapeDtypeStruct((N_TOK, D_MODEL), BF16),
                   jax.ShapeDtypeStruct((N_TOK, LANES), jnp.int32),
                   jax.ShapeDtypeStruct((N_TOK, LANES), jnp.int32),
                   jax.ShapeDtypeStruct((N_TOK, LANES), F32),
                   jax.ShapeDtypeStruct((8, LANES), jnp.int32)],
        scratch_shapes=[pltpu.VMEM((3, BRANCH_W, D_MODEL), BF16), pltpu.VMEM((D_MODEL, D_MODEL), BF16),
                        pltpu.VMEM((1, LANES), F32)],
        compiler_params=_cparams(("arbitrary",)),
        name="merge",
    )(att[0], att[1], rg[0], rg[1], o_f[0], o_f[1], o_b[0], o_b[1], h, h, h, h, x, mods,
      p['hg_norm'][l].reshape(1, HG_VAL), p['w_branch'][l], p['w_out'][l],
      p['ln1_g'][l].reshape(1, D_MODEL), p['ln1_b'][l].reshape(1, D_MODEL), rw, rb, tri)


MOE_NB = 256


def _moe_kernel(te_ref, tf_ref, tsl_ref, tnx_ref, nt_ref, x_ref, w1_hbm, b1_ref, w2_hbm, b2_ref, perm_ref, y_ref,
                w1_f, w2_f, w1_bf, w2_bf, sem, *, layer):
    i = pl.program_id(0)
    half = MOE_NB // 2

    def weight_copies(e, s):
        return (pltpu.make_async_copy(w1_hbm.at[layer, e], w1_f.at[s], sem.at[0, s]),
                pltpu.make_async_copy(w2_hbm.at[layer, e], w2_f.at[s], sem.at[1, s]))

    @pl.when(i == 0)
    def _():
        for cp in weight_copies(te_ref[0], 0):
            cp.start()

    @pl.when(tf_ref[i] == 1)
    def _():
        s = tsl_ref[i]
        for cp in weight_copies(te_ref[i], s):
            cp.wait()
        nxt = tnx_ref[i]

        @pl.when(nxt >= 0)
        def _():
            for cp in weight_copies(nxt, 1 - s):
                cp.start()

        for b in range(2 * D_EXPERT // MOE_NB):
            blk = w1_f[s, :, b * MOE_NB:(b + 1) * MOE_NB].astype(BF16)
            w1_bf[:, b * MOE_NB:(b + 1) * MOE_NB] = _dot(blk, perm_ref[...]).astype(BF16)
        w2_bf[...] = w2_f[s].astype(BF16)

    @pl.when(i < nt_ref[0])
    def _():
        h = _dot(x_ref[...], w1_bf[...]) + b1_ref[...]
        acts = []
        for b in range(2 * D_EXPERT // MOE_NB):
            glu = jnp.minimum(h[:, b * MOE_NB:b * MOE_NB + half], SWIGLU_LIMIT)
            lin = jnp.clip(h[:, b * MOE_NB + half:(b + 1) * MOE_NB], -SWIGLU_LIMIT, SWIGLU_LIMIT)
            acts.append((glu * _sigmoid(SWIGLU_ALPHA * glu) * (lin + 1.0)).astype(BF16))
        act = jnp.concatenate(acts, axis=-1)
        y_ref[...] = _dot(act, w2_bf[...]) + b2_ref[...]

    @pl.when(i >= nt_ref[0])
    def _():
        y_ref[...] = jnp.zeros(y_ref.shape, F32)


def _moe_perm():
    half = MOE_NB // 2
    pm = np.zeros((MOE_NB, MOE_NB), np.float32)
    pm[2 * np.arange(half), np.arange(half)] = 1.0
    pm[2 * np.arange(half) + 1, half + np.arange(half)] = 1.0
    return jnp.asarray(pm, BF16)


def _moe_call(x_sorted, sched, w1, b1p, w2, b2, l):
    tm = MOE_TM
    emap = lambda i, te, *_: (l, te[i], 0, 0)
    grid_spec = pltpu.PrefetchScalarGridSpec(
        num_scalar_prefetch=5,
        grid=(MOE_TILES,),
        in_specs=[pl.BlockSpec((tm, D_MODEL), lambda i, *_: (i, 0)),
                  pl.BlockSpec(memory_space=pl.ANY),
                  pl.BlockSpec((None, None, 1, 2 * D_EXPERT), emap),
                  pl.BlockSpec(memory_space=pl.ANY),
                  pl.BlockSpec((None, None, 1, D_MODEL), emap),
                  pl.BlockSpec((MOE_NB, MOE_NB), lambda i, *_: (0, 0))],
        out_specs=pl.BlockSpec((tm, D_MODEL), lambda i, *_: (i, 0)),
        scratch_shapes=[pltpu.VMEM((2, D_MODEL, 2 * D_EXPERT), F32), pltpu.VMEM((2, D_EXPERT, D_MODEL), F32),
                        pltpu.VMEM((D_MODEL, 2 * D_EXPERT), BF16), pltpu.VMEM((D_EXPERT, D_MODEL), BF16),
                        pltpu.SemaphoreType.DMA((2, 2))],
    )
    return pl.pallas_call(
        functools.partial(_moe_kernel, layer=l),
        grid_spec=grid_spec,
        out_shape=jax.ShapeDtypeStruct((MOE_TILES * tm, D_MODEL), F32),
        compiler_params=_cparams(("arbitrary",)),
        name="moe",
    )(*sched, x_sorted, w1, b1p, w2, b2, _moe_perm())


def _route(idx, rank, counts):
    tm = MOE_TM
    tiles_e = (counts + tm - 1) // tm
    eid = np.arange(N_EXPERTS, dtype=np.int32)
    earlier = (eid[None, :] <= eid[:, None]).astype(np.int32)
    tile_end = jnp.sum(earlier * tiles_e[None, :], axis=1)
    tile_start = tile_end - tiles_e
    pos = jnp.take(tile_start, idx) * tm + rank
    n_used = tile_end[N_EXPERTS - 1]
    tile_ids = jnp.arange(MOE_TILES, dtype=jnp.int32)
    tid = jnp.minimum(tile_ids, n_used - 1)
    tile_expert = jnp.sum((tile_end[None, :] <= tid[:, None]).astype(jnp.int32), axis=1)
    tile_first = ((tile_ids == jnp.take(tile_start, tile_expert)) & (tile_ids < n_used)).astype(jnp.int32)
    has_rows = (tiles_e > 0).astype(jnp.int32)
    slot_e = (jnp.sum(earlier * has_rows[None, :], axis=1) - 1) & 1
    later = jnp.where((eid[None, :] > eid[:, None]) & (has_rows[None, :] > 0), eid[None, :], N_EXPERTS)
    next_e = jnp.min(later, axis=1)
    next_e = jnp.where(next_e >= N_EXPERTS, -1, next_e).astype(jnp.int32)
    sched = (tile_expert, tile_first, jnp.take(slot_e, tile_expert).astype(jnp.int32),
             jnp.take(next_e, tile_expert), n_used.reshape(1))
    src_pair = jnp.zeros((MOE_TILES * tm,), jnp.int32).at[pos.reshape(-1)].set(
        jnp.arange(N_TOK * TOP_K, dtype=jnp.int32), mode='promise_in_bounds', unique_indices=True)
    return pos, src_pair // TOP_K, sched


def _final_kernel(yg_ref, wts_ref, x1_ref, mod_ref, lng_ref, lnb_ref, o_ref):
    wts = wts_ref[...]
    ffn = wts[:, 0:1] * yg_ref[0]
    for k in range(1, TOP_K):
        ffn = ffn + wts[:, k:k + 1] * yg_ref[k]
    o_ref[...] = _layer_norm(DN_ALPHA * x1_ref[...] + mod_ref[5:6, :] * ffn, lng_ref[...], lnb_ref[...])


def _final_call(yg, wts, x1, mods, ln_g, ln_b, l):
    tm = 256
    const = lambda shape: pl.BlockSpec(shape, lambda i: (0,) * len(shape))
    return pl.pallas_call(
        _final_kernel,
        grid=(N_TOK // tm,),
        in_specs=[pl.BlockSpec((TOP_K, tm, D_MODEL), lambda i: (0, i, 0)),
                  pl.BlockSpec((tm, LANES), lambda i: (i, 0)),
                  pl.BlockSpec((tm, D_MODEL), lambda i: (i, 0)),
                  pl.BlockSpec((None, None, 6, D_MODEL), lambda i: (l, _cond_of_row_tile(i, tm), 0, 0)),
                  const((1, D_MODEL)), const((1, D_MODEL))],
        out_specs=pl.BlockSpec((tm, D_MODEL), lambda i: (i, 0)),
        out_shape=jax.ShapeDtypeStruct((N_TOK, D_MODEL), F32),
        compiler_params=_cparams(("arbitrary",)),
        name="final",
    )(yg, wts, x1, mods, ln_g.reshape(1, D_MODEL), ln_b.reshape(1, D_MODEL))


def kernel(x_prompt, x_sample, cache_attn_k, cache_attn_v, state_rglru, state_hgrn, c, c_ctx, w_ada, b_ada, w_in, da_lambda, da_subln, rg_conv_w, rg_conv_b, rg_gate_w, rg_gate_b, rg_lambda, hg_lb, hg_norm, w_branch, w_out, ln1_g, ln1_b, router_w, router_b, w1, b1, w2, b2, ln2_g, ln2_b):
    p = dict(hg_norm=hg_norm, w_branch=w_branch, w_out=w_out, ln1_g=ln1_g, ln1_b=ln1_b,
             router_w=router_w, router_b=router_b)

    x = jnp.concatenate([x_prompt.reshape(N_CTX, D_MODEL), x_sample.reshape(N_LAT, D_MODEL)], axis=0)
    cond = jnp.concatenate([c_ctx[None, :], c, jnp.zeros((N_COND - 1 - DEC_BATCH, D_MODEL), F32)], axis=0)
    mods = _adaln_all(cond, w_ada, b_ada).reshape(DEPTH, N_COND, 6, D_MODEL)

    pr = jax.nn.softmax(hg_lb.astype(F32), axis=0)
    lbs = jnp.cumsum(pr, axis=0) - pr[0]
    dl = da_lambda.astype(F32)
    lam_all = jnp.exp(jnp.sum(dl[:, 0] * dl[:, 1], -1)) - jnp.exp(jnp.sum(dl[:, 2] * dl[:, 3], -1))

    cache_k = cache_attn_k.reshape(DEC_BATCH, DEPTH, PAST_LEN, BRANCH_W)
    cache_v = cache_attn_v.reshape(DEC_BATCH, DEPTH, PAST_LEN, BRANCH_W)

    b1p = b1.reshape(DEPTH, N_EXPERTS, 2 * D_EXPERT // MOE_NB, MOE_NB // 2, 2)
    b1p = jnp.swapaxes(b1p, -1, -2).reshape(DEPTH, N_EXPERTS, 1, 2 * D_EXPERT)
    b2r = b2.reshape(DEPTH, N_EXPERTS, 1, D_MODEL)

    ks, vs, rgs, hgs = [], [], [], []
    for l in range(DEPTH):
        lambda_init = 0.8 - 0.6 * math.exp(-0.3 * l)
        h = _in_proj(x, mods, w_in, l)
        ks.append(h[:N_CTX, BRANCH_W:2 * BRANCH_W].reshape(BATCH, SEQ, DA_HEADS, 2, DA_HEAD_DIM))
        vs.append(h[:N_CTX, 2 * BRANCH_W:3 * BRANCH_W].reshape(BATCH, SEQ, DA_HEADS, DA_V_DIM))

        att = _attention(h, cache_k, cache_v, lam_all[l] + lambda_init, da_subln[l], l, 1.0 - lambda_init)

        wg, bg = _rg_gate_dense(rg_gate_w[l], rg_gate_b[l])
        rg_c, hl_c = _rglru_call(h, rg_conv_w[l], rg_conv_b[l], wg, bg, rg_lambda[l], None, BATCH, SEQ, 0)
        rg_l, _ = _rglru_call(h, rg_conv_w[l], rg_conv_b[l], wg, bg, rg_lambda[l], state_rglru[:, l],
                              DEC_BATCH, DEC_SEQ, N_CTX // DEC_SEQ)
        rgs.append(hl_c)

        of_c, ob_c, sl_c = _hgrn_call(h, lbs[l], None, l, BATCH, SEQ, 0)
        of_l, ob_l, _ = _hgrn_call(h, lbs[l], state_hgrn, l, DEC_BATCH, DEC_SEQ, N_CTX)
        hgs.append(sl_c)

        x1, u2, idx, rank, wts, cnt = _merge_call(att, (rg_c, rg_l), (of_c, of_l), (ob_c, ob_l), h, x, mods, p, l)

        pos, src_tok, sched = _route(idx[:, :TOP_K], rank[:, :TOP_K], cnt[0, :N_EXPERTS])
        x_sorted = u2.at[src_tok].get(mode='promise_in_bounds')
        y_sorted = _moe_call(x_sorted, sched, w1, b1p, w2, b2r, l)
        yg = y_sorted.at[pos.T.reshape(-1)].get(mode='promise_in_bounds').reshape(TOP_K, N_TOK, D_MODEL)
        x = _final_call(yg, wts, x1, mods, ln2_g[l], ln2_b[l], l)

    y_prompt = x[:N_CTX].reshape(BATCH, SEQ, D_MODEL)
    y_sample = x[N_CTX:].reshape(DEC_BATCH, DEC_SEQ, D_MODEL)
    return (y_prompt, y_sample, jnp.stack(ks, axis=1), jnp.stack(vs, axis=1),
            jnp.stack(rgs, axis=1), jnp.stack(hgs, axis=1))
```

```python
import functools
import math

import numpy as np
import jax
import jax.numpy as jnp
from jax import lax
from jax.experimental import pallas as pl
from jax.experimental.pallas import tpu as pltpu

F32 = jnp.float32
BF16 = jnp.bfloat16
HIGHEST = lax.Precision.HIGHEST

D_MODEL = 1024
BATCH = 16
SEQ = 256
DEPTH = 4
DEC_BATCH = 4
DEC_SEQ = 1024
PAST_LEN = 256
GRID_W = 64
BRANCH_W = 512
DA_HEADS = 4
DA_HEAD_DIM = 64
DA_V_DIM = 128
ROPE_BASE = 10000.0
RG_WIDTH = 512
RG_BLOCKS = 8
RG_BLOCK_W = 64
RG_CONV_W = 4
RG_C = 8.0
HG_HEADS = 4
HG_KEY = 128
HG_VAL = 128
N_EXPERTS = 32
TOP_K = 4
D_EXPERT = 1024
SWIGLU_ALPHA = 1.702
SWIGLU_LIMIT = 7.0
DN_ALPHA = (2 * DEPTH) ** 0.25
NORM_EPS = 1e-5
D_IN = 10 * BRANCH_W + 3 * D_MODEL

N_CTX = BATCH * SEQ
N_LAT = DEC_BATCH * DEC_SEQ
N_TOK = N_CTX + N_LAT
N_COND = 8

LANES = 128
VMEM_LIMIT = 56 * 1024 * 1024

HG_CHUNK = 128
HG_LEVELS = (8, 16, 32, 64)
MOE_TM = 256
MOE_TILES = (N_TOK * TOP_K) // MOE_TM + N_EXPERTS


def _cparams(sem):
    return pltpu.CompilerParams(dimension_semantics=sem, vmem_limit_bytes=VMEM_LIMIT)


def _sigmoid(x):
    return 1.0 / (1.0 + jnp.exp(-x))


def _dot(a, b):
    return jnp.dot(a, b, preferred_element_type=F32)


def _dot_nt(a, b):
    return lax.dot_general(a, b, (((1,), (1,)), ((), ())), preferred_element_type=F32)


def _cond_of_row_tile(i, tm):
    r = i * tm
    return jnp.where(r < N_CTX, 0, 1 + (r - N_CTX) // DEC_SEQ)


def _ada_kernel(c_ref, w_ref, b_ref, o_ref):
    c = c_ref[...]
    s = c * _sigmoid(c)
    o_ref[0] = jnp.dot(s, w_ref[0], precision=HIGHEST, preferred_element_type=F32) + b_ref[0]


def _adaln_all(cond, w_ada, b_ada):
    tn = 1536
    return pl.pallas_call(
        _ada_kernel,
        grid=(DEPTH, 6 * D_MODEL // tn),
        in_specs=[pl.BlockSpec((N_COND, D_MODEL), lambda l, j: (0, 0)),
                  pl.BlockSpec((1, D_MODEL, tn), lambda l, j: (l, 0, j)),
                  pl.BlockSpec((1, 1, tn), lambda l, j: (l, 0, j))],
        out_specs=pl.BlockSpec((1, N_COND, tn), lambda l, j: (l, 0, j)),
        out_shape=jax.ShapeDtypeStruct((DEPTH, N_COND, 6 * D_MODEL), F32),
        compiler_params=_cparams(("arbitrary", "arbitrary")),
        name="adaln",
    )(cond, w_ada, b_ada.reshape(DEPTH, 1, 6 * D_MODEL))


def _modulate_kernel(x_ref, mod_ref, u_ref):
    u_ref[...] = (x_ref[...] * (1.0 + mod_ref[1:2, :]) + mod_ref[0:1, :]).astype(BF16)


def _modulate(x, mods, l):
    tm = 1024
    return pl.pallas_call(
        _modulate_kernel,
        grid=(N_TOK // tm,),
        in_specs=[pl.BlockSpec((tm, D_MODEL), lambda i: (i, 0)),
                  pl.BlockSpec((None, None, 6, D_MODEL), lambda i: (l, _cond_of_row_tile(i, tm), 0, 0))],
        out_specs=pl.BlockSpec((tm, D_MODEL), lambda i: (i, 0)),
        out_shape=jax.ShapeDtypeStruct((N_TOK, D_MODEL), BF16),
        compiler_params=_cparams(("arbitrary",)),
        name="modulate",
    )(x, mods)


def _in_kernel(u_ref, w_ref, o_ref, wbf_ref):
    @pl.when(pl.program_id(1) == 0)
    def _():
        wbf_ref[...] = w_ref[...].astype(BF16)

    o_ref[...] = _dot(u_ref[...], wbf_ref[...]).astype(BF16)


def _in_proj(u, w_in, l):
    tm, tn = 1024, 1024
    return pl.pallas_call(
        _in_kernel,
        grid=(D_IN // tn, N_TOK // tm),
        in_specs=[pl.BlockSpec((tm, D_MODEL), lambda j, i: (i, 0)),
                  pl.BlockSpec((None, D_MODEL, tn), lambda j, i: (l, 0, j))],
        out_specs=pl.BlockSpec((tm, tn), lambda j, i: (i, j)),
        out_shape=jax.ShapeDtypeStruct((N_TOK, D_IN), BF16),
        scratch_shapes=[pltpu.VMEM((D_MODEL, tn), BF16)],
        compiler_params=_cparams(("arbitrary", "arbitrary")),
        name="in_proj",
    )(u, w_in)


def _rope(x, cos, sin_signed):
    lane = lax.broadcasted_iota(jnp.int32, x.shape, 1)
    first = (lane & 31) < 16
    partner = jnp.where(first, pltpu.roll(x, LANES - 16, 1), pltpu.roll(x, 16, 1))
    return x * cos + partner * sin_signed


def _attn_body(q, keys, vals, lam, subln, out_scale):
    scale = DA_HEAD_DIM ** -0.5
    lane = lax.broadcasted_iota(jnp.int32, q.shape, 1)
    kb = [k.astype(BF16) for k in keys]
    vb = [v.astype(BF16) for v in vals]
    acc = None
    for m in range(2):
        in_map = (lane < DA_HEAD_DIM) if m == 0 else (lane >= DA_HEAD_DIM)
        qm = jnp.where(in_map, q, 0.0).astype(BF16)
        s = [_dot_nt(qm, k) * scale for k in kb]
        mx = s[0].max(axis=-1, keepdims=True)
        for si in s[1:]:
            mx = jnp.maximum(mx, si.max(axis=-1, keepdims=True))
        e = [jnp.exp(si - mx) for si in s]
        den = e[0].sum(axis=-1, keepdims=True)
        for ei in e[1:]:
            den = den + ei.sum(axis=-1, keepdims=True)
        coef = (1.0 / den) if m == 0 else (-lam[:, 0:1] / den)
        for ei, v in zip(e, vb):
            part = _dot((ei * coef).astype(BF16), v)
            acc = part if acc is None else acc + part
    y = acc * lax.rsqrt(jnp.mean(acc * acc, axis=-1, keepdims=True) + NORM_EPS)
    return y * subln * out_scale


def _attn_ctx_kernel(lam_ref, sub_ref, q_ref, k_ref, v_ref, o_ref, *, out_scale):
    o_ref[...] = _attn_body(q_ref[...], [k_ref[...]], [v_ref[...]], lam_ref[...], sub_ref[...], out_scale)


def _attn_lat_kernel(lam_ref, sub_ref, q_ref, k_ref, v_ref, kc_ref, vc_ref, cq_ref, sq_ref, ck_ref, sk_ref,
                     o_ref, *, out_scale):
    q = _rope(q_ref[...].astype(F32), cq_ref[...], sq_ref[...])
    k = _rope(k_ref[...].astype(F32), ck_ref[...], sk_ref[...])
    o_ref[...] = _attn_body(q, [k, kc_ref[...]], [v_ref[...], vc_ref[...]], lam_ref[...], sub_ref[...], out_scale)


def _rope_tables():
    t = np.arange(DEC_SEQ)
    row = (t // GRID_W).astype(np.float64)
    col = (t % GRID_W).astype(np.float64)
    d = DA_HEAD_DIM // 2
    inv = ROPE_BASE ** (-np.arange(0, d, 2, dtype=np.float64) / d)
    a_row = row[:, None] * inv[None, :]
    a_col = col[:, None] * inv[None, :]
    cos = np.concatenate([np.cos(a_row), np.cos(a_row), np.cos(a_col), np.cos(a_col)], -1)
    sin = np.concatenate([-np.sin(a_row), np.sin(a_row), -np.sin(a_col), np.sin(a_col)], -1)
    return (jnp.asarray(np.tile(cos, (1, 2)), F32), jnp.asarray(np.tile(sin, (1, 2)), F32))


def _attention(h, cache_k, cache_v, lam, subln, l, out_scale):
    lam_v = jnp.full((1, LANES), lam, F32)
    sub_v = subln.reshape(1, DA_V_DIM)
    small = pl.BlockSpec((1, LANES), lambda *a: (0, 0))
    att_ctx = pl.pallas_call(
        functools.partial(_attn_ctx_kernel, out_scale=out_scale),
        grid=(BATCH, DA_HEADS),
        in_specs=[small, small,
                  pl.BlockSpec((SEQ, LANES), lambda b, hd: (b, hd)),
                  pl.BlockSpec((SEQ, LANES), lambda b, hd: (b, 4 + hd)),
                  pl.BlockSpec((SEQ, LANES), lambda b, hd: (b, 8 + hd))],
        out_specs=pl.BlockSpec((SEQ, LANES), lambda b, hd: (b, hd)),
        out_shape=jax.ShapeDtypeStruct((N_CTX, BRANCH_W), F32),
        compiler_params=_cparams(("arbitrary", "arbitrary")),
        name="attn_ctx",
    )(lam_v, sub_v, h, h, h)

    tq = 256
    nq = DEC_SEQ // tq
    cos, sin = _rope_tables()
    row0 = N_CTX // DEC_SEQ
    att_lat = pl.pallas_call(
        functools.partial(_attn_lat_kernel, out_scale=out_scale),
        grid=(DEC_BATCH, DA_HEADS, nq),
        in_specs=[small, small,
                  pl.BlockSpec((tq, LANES), lambda b, hd, qi: ((row0 + b) * nq + qi, hd)),
                  pl.BlockSpec((DEC_SEQ, LANES), lambda b, hd, qi: (row0 + b, 4 + hd)),
                  pl.BlockSpec((DEC_SEQ, LANES), lambda b, hd, qi: (row0 + b, 8 + hd)),
                  pl.BlockSpec((None, None, PAST_LEN, LANES), lambda b, hd, qi: (b, l, 0, hd)),
                  pl.BlockSpec((None, None, PAST_LEN, LANES), lambda b, hd, qi: (b, l, 0, hd)),
                  pl.BlockSpec((tq, LANES), lambda b, hd, qi: (qi, 0)),
                  pl.BlockSpec((tq, LANES), lambda b, hd, qi: (qi, 0)),
                  pl.BlockSpec((DEC_SEQ, LANES), lambda b, hd, qi: (0, 0)),
                  pl.BlockSpec((DEC_SEQ, LANES), lambda b, hd, qi: (0, 0))],
        out_specs=pl.BlockSpec((tq, LANES), lambda b, hd, qi: (b * nq + qi, hd)),
        out_shape=jax.ShapeDtypeStruct((N_LAT, BRANCH_W), F32),
        compiler_params=_cparams(("arbitrary", "arbitrary", "arbitrary")),
        name="attn_lat",
    )(lam_v, sub_v, h, h, h, cache_k, cache_v, cos, sin, cos, sin)
    return att_ctx, att_lat


def _gelu_tanh(x):
    return 0.5 * x * (1.0 + jnp.tanh(math.sqrt(2.0 / math.pi) * (x + 0.044715 * (x * x * x))))


def _rg_kernel(*refs, seq, has_h0):
    if has_h0:
        rx_ref, gate_ref, cw_ref, cb_ref, wg_ref, bg_ref, lam_ref, h0_ref = refs[:8]
        rest = refs[8:]
    else:
        rx_ref, gate_ref, cw_ref, cb_ref, wg_ref, bg_ref, lam_ref = refs[:7]
        h0_ref = None
        rest = refs[7:]
    out_ref, hl_ref, a_scr, b_scr, h_scr = rest

    x = rx_ref[...].astype(F32)
    row = lax.broadcasted_iota(jnp.int32, x.shape, 0)
    xr = cb_ref[...] + cw_ref[2:3, :] * x
    for j in (0, 1, 3):
        d = j - RG_CONV_W // 2
        shifted = pltpu.roll(x, (-d) % seq, 0)
        valid = (row + d >= 0) & (row + d < seq)
        xr = xr + cw_ref[j:j + 1, :] * jnp.where(valid, shifted, 0.0)

    g = _dot(xr.astype(BF16), wg_ref[...]) + bg_ref[...]
    for dr in range(2):
        r = _sigmoid(g[:, (2 * dr) * RG_WIDTH:(2 * dr + 1) * RG_WIDTH])
        i = _sigmoid(g[:, (2 * dr + 1) * RG_WIDTH:(2 * dr + 2) * RG_WIDTH])
        lam = lam_ref[dr:dr + 1, :]
        softplus_neg = jnp.maximum(-lam, 0.0) + jnp.log(1.0 + jnp.exp(-jnp.abs(lam)))
        log_a = (-RG_C * softplus_neg) * r
        a_scr[dr] = jnp.exp(log_a)
        b_scr[dr] = jnp.sqrt(1.0 - jnp.exp(2.0 * log_a)) * i * xr

    if has_h0:
        hf0, hb0 = h0_ref[0:1, :], h0_ref[1:2, :]
    else:
        hf0 = hb0 = jnp.zeros((1, RG_WIDTH), F32)

    def step(t, carry):
        hf, hb = carry
        tb = seq - 1 - t
        hf = a_scr[0, pl.ds(t, 1), :] * hf + b_scr[0, pl.ds(t, 1), :]
        h_scr[0, pl.ds(t, 1), :] = hf
        hb = a_scr[1, pl.ds(tb, 1), :] * hb + b_scr[1, pl.ds(tb, 1), :]
        h_scr[1, pl.ds(tb, 1), :] = hb
        return hf, hb

    hf, hb = lax.fori_loop(0, seq, step, (hf0, hb0), unroll=8)
    hl_ref[0:1, :] = hf
    hl_ref[1:2, :] = hb
    out_ref[...] = (h_scr[0] + h_scr[1]) * _gelu_tanh(gate_ref[...].astype(F32))


def _rglru_call(h, conv_w, conv_b, wg, bg, lam, h0, nseq, seq, row_block0):
    has_h0 = h0 is not None
    full = lambda shape: pl.BlockSpec(shape, lambda b: (0,) * len(shape))
    in_specs = [pl.BlockSpec((seq, RG_WIDTH), lambda b: (row_block0 + b, 3)),
                pl.BlockSpec((seq, RG_WIDTH), lambda b: (row_block0 + b, 4)),
                full((RG_CONV_W, RG_WIDTH)), full((1, RG_WIDTH)),
                full((RG_WIDTH, 4 * RG_WIDTH)), full((1, 4 * RG_WIDTH)), full((2, RG_WIDTH))]
    args = [h, h, conv_w, conv_b.reshape(1, RG_WIDTH), wg, bg, lam]
    if has_h0:
        in_specs.append(pl.BlockSpec((None, 2, RG_WIDTH), lambda b: (b, 0, 0)))
        args.append(h0)
    return pl.pallas_call(
        functools.partial(_rg_kernel, seq=seq, has_h0=has_h0),
        grid=(nseq,),
        in_specs=in_specs,
        out_specs=[pl.BlockSpec((seq, RG_WIDTH), lambda b: (b, 0)),
                   pl.BlockSpec((None, 2, RG_WIDTH), lambda b: (b, 0, 0))],
        out_shape=[jax.ShapeDtypeStruct((nseq * seq, RG_WIDTH), F32),
                   jax.ShapeDtypeStruct((nseq, 2, RG_WIDTH), F32)],
        scratch_shapes=[pltpu.VMEM((2, seq, RG_WIDTH), F32)] * 3,
        compiler_params=_cparams(("arbitrary",)),
        name="rglru_lat" if has_h0 else "rglru_ctx",
    )(*args)


def _rg_gate_dense(gate_w, gate_b):
    eye = jnp.eye(RG_BLOCKS, dtype=F32)
    dense = jnp.einsum('dgncf,nm->dgncmf', gate_w, eye).reshape(2, 2, RG_WIDTH, RG_WIDTH)
    wg = jnp.transpose(dense, (2, 0, 1, 3)).reshape(RG_WIDTH, 4 * RG_WIDTH)
    return wg.astype(BF16), gate_b.reshape(1, 4 * RG_WIDTH)


def _hg_masks():
    c = HG_CHUNK
    t = np.arange(c)[:, None]
    s = np.arange(c)[None, :]
    tri = np.stack([s <= t, s >= t]).astype(np.float32)
    lvl = np.zeros((2, len(HG_LEVELS), c, c), np.float32)
    for n, h in enumerate(HG_LEVELS):
        same = (t // (2 * h)) == (s // (2 * h))
        lvl[0, n] = same & ((t // h) % 2 == 1) & ((s // h) % 2 == 0)
        lvl[1, n] = same & ((t // h) % 2 == 0) & ((s // h) % 2 == 1)
    same8 = ((t // 8) == (s // 8)).astype(np.float32)
    return jnp.asarray(tri, BF16), jnp.asarray(lvl), jnp.asarray(same8)


def _hg_chunk(q_raw, z, v, lb, st, tri, lvl, same8, rev):
    c = HG_CHUNK
    q = q_raw * _sigmoid(q_raw)
    sig = _sigmoid(z)
    g = jnp.log(lb + (1.0 - lb) * sig)
    kk = (1.0 - lb) * (1.0 - sig)

    g1 = g.astype(BF16)
    r1 = g - g1.astype(F32)
    g2 = r1.astype(BF16)
    g3 = (r1 - g2.astype(F32)).astype(BF16)
    gi = _dot(tri, g1) + _dot(tri, g2) + _dot(tri, g3)
    gx = gi - g

    row = lax.broadcasted_iota(jnp.int32, (c, c), 0)
    col = lax.broadcasted_iota(jnp.int32, (c, c), 1)
    off = (row - col) if not rev else (col - row)

    scores = jnp.where(off == 0, jnp.sum(q * kk, axis=-1, keepdims=True), 0.0)
    for d in range(1, 8):
        sh = d if not rev else c - d
        kk_s = pltpu.roll(kk, sh, 0)
        g_s = pltpu.roll(gi, sh, 0)
        e = jnp.exp(jnp.minimum(gi - g_s, 0.0))
        band = jnp.sum(q * kk_s * e, axis=-1, keepdims=True)
        scores = scores + jnp.where(off == d, band, 0.0)
    scores = scores * same8

    for n, h in enumerate(HG_LEVELS):
        nb = c // h
        gi3 = gi.reshape(nb, h, HG_KEY)
        gx3 = gx.reshape(nb, h, HG_KEY)
        if not rev:
            a = jnp.exp(jnp.minimum(gi3 - gx3[:, 0:1, :], 0.0))
            b = jnp.exp(jnp.minimum(gi3[:, h - 1:h, :] - gi3, 0.0))
        else:
            a = jnp.exp(jnp.minimum(gi3 - gx3[:, h - 1:h, :], 0.0))
            b = jnp.exp(jnp.minimum(gi3[:, 0:1, :] - gi3, 0.0))
        qa = (q * a.reshape(c, HG_KEY)).astype(BF16)
        kb = (kk * b.reshape(c, HG_KEY)).astype(BF16)
        scores = scores + lvl[n] * _dot_nt(qa, kb)

    vb = v.astype(BF16)
    o = _dot(scores.astype(BF16), vb)
    o = o + _dot_nt((q * jnp.exp(gi)).astype(BF16), st.astype(BF16))
    g_end = gi[c - 1:c, :] if not rev else gi[0:1, :]
    kd = (kk * jnp.exp(jnp.minimum(g_end - gi, 0.0))).astype(BF16)
    st_new = st * jnp.exp(g_end) + _dot(v.T.astype(BF16), kd)
    return o, st_new


def _hg_kernel(*refs, has_s0, nchunk):
    if has_s0:
        (qf_ref, qb_ref, zf_ref, zb_ref, vf_ref, vb_ref, lb_ref, tri_ref, lvl_ref, s8_ref, s0_ref,
         of_ref, ob_ref, sfin_ref, st_scr) = refs
    else:
        (qf_ref, qb_ref, zf_ref, zb_ref, vf_ref, vb_ref, lb_ref, tri_ref, lvl_ref, s8_ref,
         of_ref, ob_ref, sfin_ref, st_scr) = refs
        s0_ref = None
    ci = pl.program_id(2)

    @pl.when(ci == 0)
    def _():
        for dr in range(2):
            st_scr[dr] = s0_ref[dr].T if has_s0 else jnp.zeros((HG_VAL, HG_KEY), F32)

    s8 = s8_ref[...]
    ld = lambda ref: ref[...].astype(F32)
    o_f, st_f = _hg_chunk(ld(qf_ref), ld(zf_ref), ld(vf_ref), lb_ref[0:1, :], st_scr[0],
                          tri_ref[0], lvl_ref[0], s8, False)
    o_b, st_b = _hg_chunk(ld(qb_ref), ld(zb_ref), ld(vb_ref), lb_ref[1:2, :], st_scr[1],
                          tri_ref[1], lvl_ref[1], s8, True)
    of_ref[...] = o_f
    ob_ref[...] = o_b
    st_scr[0] = st_f
    st_scr[1] = st_b

    @pl.when(ci == nchunk - 1)
    def _():
        sfin_ref[0] = st_f.T
        sfin_ref[1] = st_b.T


def _hgrn_call(h, lbs_l, s0, l, nseq, seq, row0):
    c = HG_CHUNK
    nchunk = seq // c
    has_s0 = s0 is not None
    tri, lvl, same8 = _hg_masks()
    rf = lambda b, hd, ci: row0 // c + b * nchunk + ci
    rb = lambda b, hd, ci: row0 // c + b * nchunk + (nchunk - 1 - ci)
    blk = lambda rfun, cb: pl.BlockSpec((c, LANES), lambda b, hd, ci: (rfun(b, hd, ci), cb + hd))
    const = lambda shape: pl.BlockSpec(shape, lambda b, hd, ci: (0,) * len(shape))
    in_specs = [blk(rf, 20), blk(rb, 20), blk(rf, 24), blk(rb, 28), blk(rf, 32), blk(rb, 32),
                pl.BlockSpec((2, LANES), lambda b, hd, ci: (0, hd)),
                const((2, c, c)), const((2, len(HG_LEVELS), c, c)), const((c, c))]
    args = [h, h, h, h, h, h, lbs_l, tri, lvl, same8]
    if has_s0:
        in_specs.append(pl.BlockSpec((None, None, 2, None, HG_KEY, HG_VAL), lambda b, hd, ci: (b, l, 0, hd, 0, 0)))
        args.append(s0)
    orow_f = lambda b, hd, ci: (b * nchunk + ci, hd)
    orow_b = lambda b, hd, ci: (b * nchunk + (nchunk - 1 - ci), hd)
    return pl.pallas_call(
        functools.partial(_hg_kernel, has_s0=has_s0, nchunk=nchunk),
        grid=(nseq, HG_HEADS, nchunk),
        in_specs=in_specs,
        out_specs=[pl.BlockSpec((c, LANES), orow_f), pl.BlockSpec((c, LANES), orow_b),
                   pl.BlockSpec((None, 2, None, HG_KEY, HG_VAL), lambda b, hd, ci: (b, 0, hd, 0, 0))],
        out_shape=[jax.ShapeDtypeStruct((nseq * seq, BRANCH_W), F32),
                   jax.ShapeDtypeStruct((nseq * seq, BRANCH_W), F32),
                   jax.ShapeDtypeStruct((nseq, 2, HG_HEADS, HG_KEY, HG_VAL), F32)],
        scratch_shapes=[pltpu.VMEM((2, HG_VAL, HG_KEY), F32)],
        compiler_params=_cparams(("arbitrary", "arbitrary", "arbitrary")),
        name="hgrn_lat" if has_s0 else "hgrn_ctx",
    )(*args)


def _layer_norm(y, g, b):
    mu = jnp.mean(y, axis=-1, keepdims=True)
    yc = y - mu
    var = jnp.mean(yc * yc, axis=-1, keepdims=True)
    return yc * lax.rsqrt(var + NORM_EPS) * g + b


def _merge_kernel(attc_ref, attl_ref, rgc_ref, rgl_ref, ofc_ref, ofl_ref, obc_ref, obl_ref,
                  hgate_ref, mg0_ref, mg1_ref, mg2_ref, x_ref, mod_ref,
                  hgn_ref, wbr_ref, wout_ref, lng_ref, lnb_ref, rw_ref, rb_ref, tri_ref,
                  x1_ref, u2_ref, idx_ref, rank_ref, wts_ref, cnt_ref, wbr_bf, wout_bf, cnt_scr, *, ctx_tiles):
    @pl.when(pl.program_id(0) == 0)
    def _():
        wbr_bf[...] = wbr_ref[...].astype(BF16)
        wout_bf[...] = wout_ref[...].astype(BF16)
        cnt_scr[...] = jnp.zeros(cnt_scr.shape, F32)

    is_ctx = pl.program_id(0) < ctx_tiles
    pick = lambda c_ref, l_ref: jnp.where(is_ctx, c_ref[...], l_ref[...])
    att = pick(attc_ref, attl_ref)
    rg = pick(rgc_ref, rgl_ref)
    o = pick(ofc_ref, ofl_ref) + pick(obc_ref, obl_ref)
    hgate = hgate_ref[...].astype(F32)
    hg_parts = []
    for hd in range(HG_HEADS):
        oh = o[:, hd * HG_VAL:(hd + 1) * HG_VAL]
        gh = hgate[:, hd * HG_VAL:(hd + 1) * HG_VAL]
        yh = oh * lax.rsqrt(jnp.mean(oh * oh, axis=-1, keepdims=True) + NORM_EPS) * hgn_ref[...]
        hg_parts.append(yh * (gh * _sigmoid(gh)))
    hg = jnp.concatenate(hg_parts, axis=-1)

    proj = _sigmoid(mg0_ref[...].astype(F32)) * _dot(att.astype(BF16), wbr_bf[0])
    proj = proj + _sigmoid(mg1_ref[...].astype(F32)) * _dot(rg.astype(BF16), wbr_bf[1])
    proj = proj + _sigmoid(mg2_ref[...].astype(F32)) * _dot(hg.astype(BF16), wbr_bf[2])
    mix = _dot(proj.astype(BF16), wout_bf[...])

    x1 = _layer_norm(DN_ALPHA * x_ref[...] + mod_ref[2:3, :] * mix, lng_ref[...], lnb_ref[...])
    x1_ref[...] = x1
    u2 = x1 * (1.0 + mod_ref[4:5, :]) + mod_ref[3:4, :]
    u2_ref[...] = u2.astype(BF16)

    u2_hi = u2.astype(BF16)
    u2_lo = (u2 - u2_hi.astype(F32)).astype(BF16)
    logits = (_dot(u2_hi, rw_ref[0]) + _dot(u2_lo, rw_ref[0]) + _dot(u2_hi, rw_ref[1])) + rb_ref[...]
    lane = lax.broadcasted_iota(jnp.int32, logits.shape, 1).astype(F32)
    idx_out = jnp.zeros(logits.shape, F32)
    wts_out = jnp.zeros(logits.shape, F32)
    chosen = jnp.zeros(logits.shape, F32)
    sels = []
    top0 = None
    den = None
    for k in range(TOP_K):
        m = jnp.max(logits, axis=-1, keepdims=True)
        sel = jnp.min(jnp.where(logits == m, lane, float(LANES)), axis=-1, keepdims=True)
        sels.append(sel)
        if k == 0:
            top0 = m
        e = jnp.exp(m - top0)
        den = e if den is None else den + e
        idx_out = jnp.where(lane == k, sel, idx_out)
        wts_out = jnp.where(lane == k, e, wts_out)
        chosen = jnp.where(lane == sel, 1.0, chosen)
        logits = jnp.where(lane == sel, -jnp.inf, logits)
    idx_ref[...] = idx_out.astype(jnp.int32)
    wts_ref[...] = wts_out * (1.0 / den)

    before = _dot(tri_ref[...], chosen.astype(BF16)) + cnt_scr[...]
    rank_out = jnp.zeros(logits.shape, F32)
    for k in range(TOP_K):
        rk = jnp.sum(jnp.where(lane == sels[k], before, 0.0), axis=-1, keepdims=True)
        rank_out = jnp.where(lane == k, rk, rank_out)
    rank_ref[...] = rank_out.astype(jnp.int32)
    cnt_scr[...] = cnt_scr[...] + jnp.sum(chosen, axis=0, keepdims=True)
    cnt_ref[...] = jnp.broadcast_to(cnt_scr[...], cnt_ref.shape).astype(jnp.int32)


def _merge_call(att, rg, o_f, o_b, h, x, mods, p, l):
    tm = 256
    ctx_tiles = N_CTX // tm
    rowb = lambda w, cb: pl.BlockSpec((tm, w), lambda i: (i, cb))
    ctxb = pl.BlockSpec((tm, BRANCH_W), lambda i: (jnp.minimum(i, ctx_tiles - 1), 0))
    latb = pl.BlockSpec((tm, BRANCH_W), lambda i: (jnp.maximum(i - ctx_tiles, 0), 0))
    const = lambda shape: pl.BlockSpec(shape, lambda i: (0,) * len(shape))
    rw = jnp.zeros((D_MODEL, LANES), F32).at[:, :N_EXPERTS].set(p['router_w'][l])
    rb = jnp.full((1, LANES), -1e30, F32).at[0, :N_EXPERTS].set(p['router_b'][l])
    rw_hi = rw.astype(BF16)
    rw = jnp.stack([rw_hi, (rw - rw_hi.astype(F32)).astype(BF16)])
    tri = jnp.asarray(np.tril(np.ones((tm, tm), np.float32), -1), BF16)
    return pl.pallas_call(
        functools.partial(_merge_kernel, ctx_tiles=ctx_tiles),
        grid=(N_TOK // tm,),
        in_specs=[ctxb, latb, ctxb, latb, ctxb, latb, ctxb, latb,
                  rowb(BRANCH_W, 9), rowb(D_MODEL, 5), rowb(D_MODEL, 6), rowb(D_MODEL, 7),
                  rowb(D_MODEL, 0),
                  pl.BlockSpec((None, None, 6, D_MODEL), lambda i: (l, _cond_of_row_tile(i, tm), 0, 0)),
                  const((1, HG_VAL)), const((3, BRANCH_W, D_MODEL)), const((D_MODEL, D_MODEL)),
                  const((1, D_MODEL)), const((1, D_MODEL)), const((2, D_MODEL, LANES)), const((1, LANES)),
                  const((tm, tm))],
        out_specs=[rowb(D_MODEL, 0), rowb(D_MODEL, 0), rowb(LANES, 0), rowb(LANES, 0), rowb(LANES, 0),
                   const((8, LANES))],
        out_shape=[jax.ShapeDtypeStruct((N_TOK, D_MODEL), F32),
                   jax.ShapeDtypeStruct((N_TOK, D_MODEL), BF16),
                   jax.ShapeDtypeStruct((N_TOK, LANES), jnp.int32),
                   jax.ShapeDtypeStruct((N_TOK, LANES), jnp.int32),
                   jax.ShapeDtypeStruct((N_TOK, LANES), F32),
                   jax.ShapeDtypeStruct((8, LANES), jnp.int32)],
        scratch_shapes=[pltpu.VMEM((3, BRANCH_W, D_MODEL), BF16), pltpu.VMEM((D_MODEL, D_MODEL), BF16),
                        pltpu.VMEM((1, LANES), F32)],
        compiler_params=_cparams(("arbitrary",)),
        name="merge",
    )(att[0], att[1], rg[0], rg[1], o_f[0], o_f[1], o_b[0], o_b[1], h, h, h, h, x, mods,
      p['hg_norm'][l].reshape(1, HG_VAL), p['w_branch'][l], p['w_out'][l],
      p['ln1_g'][l].reshape(1, D_MODEL), p['ln1_b'][l].reshape(1, D_MODEL), rw, rb, tri)


MOE_NB = 256


def _moe_kernel(te_ref, tf_ref, tsl_ref, tnx_ref, nt_ref, x_ref, w1_hbm, b1_ref, w2_hbm, b2_ref, perm_ref,
                y_ref, w1_f, w2_f, w1_bf, w2_bf, sem, *, layer):
    i = pl.program_id(0)
    half = MOE_NB // 2

    def weight_copies(e, s):
        return (pltpu.make_async_copy(w1_hbm.at[layer, e], w1_f.at[s], sem.at[0, s]),
                pltpu.make_async_copy(w2_hbm.at[layer, e], w2_f.at[s], sem.at[1, s]))

    @pl.when(i == 0)
    def _():
        for cp in weight_copies(te_ref[0], 0):
            cp.start()

    @pl.when(tf_ref[i] == 1)
    def _():
        s = tsl_ref[i]
        for cp in weight_copies(te_ref[i], s):
            cp.wait()
        nxt = tnx_ref[i]

        @pl.when(nxt >= 0)
        def _():
            for cp in weight_copies(nxt, 1 - s):
                cp.start()

        for b in range(2 * D_EXPERT // MOE_NB):
            blk = w1_f[s, :, b * MOE_NB:(b + 1) * MOE_NB].astype(BF16)
            w1_bf[:, b * MOE_NB:(b + 1) * MOE_NB] = _dot(blk, perm_ref[...]).astype(BF16)
        w2_bf[...] = w2_f[s].astype(BF16)

    @pl.when(i < nt_ref[0])
    def _():
        h = _dot(x_ref[...], w1_bf[...]) + b1_ref[...]
        acts = []
        for b in range(2 * D_EXPERT // MOE_NB):
            glu = jnp.minimum(h[:, b * MOE_NB:b * MOE_NB + half], SWIGLU_LIMIT)
            lin = jnp.clip(h[:, b * MOE_NB + half:(b + 1) * MOE_NB], -SWIGLU_LIMIT, SWIGLU_LIMIT)
            acts.append((glu * _sigmoid(SWIGLU_ALPHA * glu) * (lin + 1.0)).astype(BF16))
        act = jnp.concatenate(acts, axis=-1)
        y_ref[...] = _dot(act, w2_bf[...]) + b2_ref[...]

    @pl.when(i >= nt_ref[0])
    def _():
        y_ref[...] = jnp.zeros(y_ref.shape, F32)


def _moe_perm():
    half = MOE_NB // 2
    pm = np.zeros((MOE_NB, MOE_NB), np.float32)
    pm[2 * np.arange(half), np.arange(half)] = 1.0
    pm[2 * np.arange(half) + 1, half + np.arange(half)] = 1.0
    return jnp.asarray(pm, BF16)


def _moe_call(x_sorted, sched, w1, b1p, w2, b2, l):
    tm = MOE_TM
    emap = lambda i, te, *_: (l, te[i], 0, 0)
    grid_spec = pltpu.PrefetchScalarGridSpec(
        num_scalar_prefetch=5,
        grid=(MOE_TILES,),
        in_specs=[pl.BlockSpec((tm, D_MODEL), lambda i, *_: (i, 0)),
                  pl.BlockSpec(memory_space=pl.ANY),
                  pl.BlockSpec((None, None, 1, 2 * D_EXPERT), emap),
                  pl.BlockSpec(memory_space=pl.ANY),
                  pl.BlockSpec((None, None, 1, D_MODEL), emap),
                  pl.BlockSpec((MOE_NB, MOE_NB), lambda i, *_: (0, 0))],
        out_specs=pl.BlockSpec((tm, D_MODEL), lambda i, *_: (i, 0)),
        scratch_shapes=[pltpu.VMEM((2, D_MODEL, 2 * D_EXPERT), F32), pltpu.VMEM((2, D_EXPERT, D_MODEL), F32),
                        pltpu.VMEM((D_MODEL, 2 * D_EXPERT), BF16), pltpu.VMEM((D_EXPERT, D_MODEL), BF16),
                        pltpu.SemaphoreType.DMA((2, 2))],
    )
    return pl.pallas_call(
        functools.partial(_moe_kernel, layer=l),
        grid_spec=grid_spec,
        out_shape=jax.ShapeDtypeStruct((MOE_TILES * tm, D_MODEL), F32),
        compiler_params=_cparams(("arbitrary",)),
        name="moe",
    )(*sched, x_sorted, w1, b1p, w2, b2, _moe_perm())


def _route(idx, rank, counts):
    tm = MOE_TM
    tiles_e = (counts + tm - 1) // tm
    eid = np.arange(N_EXPERTS, dtype=np.int32)
    earlier = (eid[None, :] <= eid[:, None]).astype(np.int32)
    tile_end = jnp.sum(earlier * tiles_e[None, :], axis=1)
    tile_start = tile_end - tiles_e
    pos = jnp.take(tile_start, idx) * tm + rank
    n_used = tile_end[N_EXPERTS - 1]
    tile_ids = jnp.arange(MOE_TILES, dtype=jnp.int32)
    tid = jnp.minimum(tile_ids, n_used - 1)
    tile_expert = jnp.sum((tile_end[None, :] <= tid[:, None]).astype(jnp.int32), axis=1)
    tile_first = ((tile_ids == jnp.take(tile_start, tile_expert)) & (tile_ids < n_used)).astype(jnp.int32)
    has_rows = (tiles_e > 0).astype(jnp.int32)
    slot_e = (jnp.sum(earlier * has_rows[None, :], axis=1) - 1) & 1
    later = jnp.where((eid[None, :] > eid[:, None]) & (has_rows[None, :] > 0), eid[None, :], N_EXPERTS)
    next_e = jnp.min(later, axis=1)
    next_e = jnp.where(next_e >= N_EXPERTS, -1, next_e).astype(jnp.int32)
    sched = (tile_expert, tile_first, jnp.take(slot_e, tile_expert).astype(jnp.int32),
             jnp.take(next_e, tile_expert), n_used.reshape(1))
    src_pair = jnp.zeros((MOE_TILES * tm,), jnp.int32).at[pos.reshape(-1)].set(
        jnp.arange(N_TOK * TOP_K, dtype=jnp.int32), mode='promise_in_bounds', unique_indices=True)
    return pos, src_pair // TOP_K, sched


def _final_kernel(yg_ref, wts_ref, x1_ref, mod_ref, modn_ref, lng_ref, lnb_ref, o_ref, un_ref):
    wts = wts_ref[...]
    ffn = wts[:, 0:1] * yg_ref[0]
    for k in range(1, TOP_K):
        ffn = ffn + wts[:, k:k + 1] * yg_ref[k]
    x2 = _layer_norm(DN_ALPHA * x1_ref[...] + mod_ref[5:6, :] * ffn, lng_ref[...], lnb_ref[...])
    o_ref[...] = x2
    un_ref[...] = (x2 * (1.0 + modn_ref[1:2, :]) + modn_ref[0:1, :]).astype(BF16)


def _final_call(yg, wts, x1, mods, ln_g, ln_b, l):
    tm = 256
    ln = min(l + 1, DEPTH - 1)
    const = lambda shape: pl.BlockSpec(shape, lambda i: (0,) * len(shape))
    modspec = lambda lyr: pl.BlockSpec((None, None, 6, D_MODEL), lambda i: (lyr, _cond_of_row_tile(i, tm), 0, 0))
    return pl.pallas_call(
        _final_kernel,
        grid=(N_TOK // tm,),
        in_specs=[pl.BlockSpec((TOP_K, tm, D_MODEL), lambda i: (0, i, 0)),
                  pl.BlockSpec((tm, LANES), lambda i: (i, 0)),
                  pl.BlockSpec((tm, D_MODEL), lambda i: (i, 0)),
                  modspec(l), modspec(ln),
                  const((1, D_MODEL)), const((1, D_MODEL))],
        out_specs=[pl.BlockSpec((tm, D_MODEL), lambda i: (i, 0)), pl.BlockSpec((tm, D_MODEL), lambda i: (i, 0))],
        out_shape=[jax.ShapeDtypeStruct((N_TOK, D_MODEL), F32), jax.ShapeDtypeStruct((N_TOK, D_MODEL), BF16)],
        compiler_params=_cparams(("arbitrary",)),
        name="final",
    )(yg, wts, x1, mods, mods, ln_g.reshape(1, D_MODEL), ln_b.reshape(1, D_MODEL))


def kernel(x_prompt, x_sample, cache_attn_k, cache_attn_v, state_rglru, state_hgrn, c, c_ctx, w_ada, b_ada, w_in, da_lambda, da_subln, rg_conv_w, rg_conv_b, rg_gate_w, rg_gate_b, rg_lambda, hg_lb, hg_norm, w_branch, w_out, ln1_g, ln1_b, router_w, router_b, w1, b1, w2, b2, ln2_g, ln2_b):
    p = dict(hg_norm=hg_norm, w_branch=w_branch, w_out=w_out, ln1_g=ln1_g, ln1_b=ln1_b,
             router_w=router_w, router_b=router_b)

    x = jnp.concatenate([x_prompt.reshape(N_CTX, D_MODEL), x_sample.reshape(N_LAT, D_MODEL)], axis=0)
    cond = jnp.concatenate([c_ctx[None, :], c, jnp.zeros((N_COND - 1 - DEC_BATCH, D_MODEL), F32)], axis=0)
    mods = _adaln_all(cond, w_ada, b_ada).reshape(DEPTH, N_COND, 6, D_MODEL)

    pr = jax.nn.softmax(hg_lb.astype(F32), axis=0)
    lbs = jnp.cumsum(pr, axis=0) - pr[0]
    dl = da_lambda.astype(F32)
    lam_all = jnp.exp(jnp.sum(dl[:, 0] * dl[:, 1], -1)) - jnp.exp(jnp.sum(dl[:, 2] * dl[:, 3], -1))

    cache_k = cache_attn_k.reshape(DEC_BATCH, DEPTH, PAST_LEN, BRANCH_W)
    cache_v = cache_attn_v.reshape(DEC_BATCH, DEPTH, PAST_LEN, BRANCH_W)

    b1p = b1.reshape(DEPTH, N_EXPERTS, 2 * D_EXPERT // MOE_NB, MOE_NB // 2, 2)
    b1p = jnp.swapaxes(b1p, -1, -2).reshape(DEPTH, N_EXPERTS, 1, 2 * D_EXPERT)
    b2r = b2.reshape(DEPTH, N_EXPERTS, 1, D_MODEL)

    ks, vs, rgs, hgs = [], [], [], []
    u = _modulate(x, mods, 0)
    for l in range(DEPTH):
        lambda_init = 0.8 - 0.6 * math.exp(-0.3 * l)
        h = _in_proj(u, w_in, l)
        ks.append(h[:N_CTX, BRANCH_W:2 * BRANCH_W].astype(F32).reshape(BATCH, SEQ, DA_HEADS, 2, DA_HEAD_DIM))
        vs.append(h[:N_CTX, 2 * BRANCH_W:3 * BRANCH_W].astype(F32).reshape(BATCH, SEQ, DA_HEADS, DA_V_DIM))

        att = _attention(h, cache_k, cache_v, lam_all[l] + lambda_init, da_subln[l], l, 1.0 - lambda_init)

        wg, bg = _rg_gate_dense(rg_gate_w[l], rg_gate_b[l])
        rg_c, hl_c = _rglru_call(h, rg_conv_w[l], rg_conv_b[l], wg, bg, rg_lambda[l], None, BATCH, SEQ, 0)
        rg_l, _ = _rglru_call(h, rg_conv_w[l], rg_conv_b[l], wg, bg, rg_lambda[l], state_rglru[:, l],
                              DEC_BATCH, DEC_SEQ, N_CTX // DEC_SEQ)
        rgs.append(hl_c)

        of_c, ob_c, sl_c = _hgrn_call(h, lbs[l], None, l, BATCH, SEQ, 0)
        of_l, ob_l, _ = _hgrn_call(h, lbs[l], state_hgrn, l, DEC_BATCH, DEC_SEQ, N_CTX)
        hgs.append(sl_c)

        x1, u2, idx, rank, wts, cnt = _merge_call(att, (rg_c, rg_l), (of_c, of_l), (ob_c, ob_l), h, x, mods, p, l)

        pos, src_tok, sched = _route(idx[:, :TOP_K], rank[:, :TOP_K], cnt[0, :N_EXPERTS])
        x_sorted = u2.at[src_tok].get(mode='promise_in_bounds')
        y_sorted = _moe_call(x_sorted, sched, w1, b1p, w2, b2r, l)
        yg = y_sorted.at[pos.T.reshape(-1)].get(mode='promise_in_bounds').reshape(TOP_K, N_TOK, D_MODEL)
        x, u = _final_call(yg, wts, x1, mods, ln2_g[l], ln2_b[l], l)

    y_prompt = x[:N_CTX].reshape(BATCH, SEQ, D_MODEL)
    y_sample = x[N_CTX:].reshape(DEC_BATCH, DEC_SEQ, D_MODEL)
    return (y_prompt, y_sample, jnp.stack(ks, axis=1), jnp.stack(vs, axis=1),
            jnp.stack(rgs, axis=1), jnp.stack(hgs, axis=1))
```

```python
import functools
import math

import numpy as np
import jax
import jax.numpy as jnp
from jax import lax
from jax.experimental import pallas as pl
from jax.experimental.pallas import tpu as pltpu

F32 = jnp.float32
BF16 = jnp.bfloat16
HIGHEST = lax.Precision.HIGHEST

D_MODEL = 1024
BATCH = 16
SEQ = 256
DEPTH = 4
DEC_BATCH = 4
DEC_SEQ = 1024
PAST_LEN = 256
GRID_W = 64
BRANCH_W = 512
DA_HEADS = 4
DA_HEAD_DIM = 64
DA_V_DIM = 128
ROPE_BASE = 10000.0
RG_WIDTH = 512
RG_BLOCKS = 8
RG_BLOCK_W = 64
RG_CONV_W = 4
RG_C = 8.0
HG_HEADS = 4
HG_KEY = 128
HG_VAL = 128
N_EXPERTS = 32
TOP_K = 4
D_EXPERT = 1024
SWIGLU_ALPHA = 1.702
SWIGLU_LIMIT = 7.0
DN_ALPHA = (2 * DEPTH) ** 0.25
NORM_EPS = 1e-5
D_IN = 10 * BRANCH_W + 3 * D_MODEL

N_CTX = BATCH * SEQ
N_LAT = DEC_BATCH * DEC_SEQ
N_TOK = N_CTX + N_LAT
N_COND = 8

LANES = 128
VMEM_LIMIT = 56 * 1024 * 1024

HG_CHUNK = 128
HG_LEVELS = (8, 16, 32, 64)
MOE_TM = 256
MOE_TILES = (N_TOK * TOP_K) // MOE_TM + N_EXPERTS


def _cparams(sem):
    return pltpu.CompilerParams(dimension_semantics=sem, vmem_limit_bytes=VMEM_LIMIT)


def _sigmoid(x):
    return 1.0 / (1.0 + jnp.exp(-x))


def _dot(a, b):
    return jnp.dot(a, b, preferred_element_type=F32)


def _dot_nt(a, b):
    return lax.dot_general(a, b, (((1,), (1,)), ((), ())), preferred_element_type=F32)


def _cond_of_row_tile(i, tm):
    r = i * tm
    return jnp.where(r < N_CTX, 0, 1 + (r - N_CTX) // DEC_SEQ)


def _ada_kernel(c_ref, w_ref, b_ref, o_ref):
    c = c_ref[...]
    s = c * _sigmoid(c)
    o_ref[0] = jnp.dot(s, w_ref[0], precision=HIGHEST, preferred_element_type=F32) + b_ref[0]


def _adaln_all(cond, w_ada, b_ada):
    tn = 1536
    return pl.pallas_call(
        _ada_kernel,
        grid=(DEPTH, 6 * D_MODEL // tn),
        in_specs=[pl.BlockSpec((N_COND, D_MODEL), lambda l, j: (0, 0)),
                  pl.BlockSpec((1, D_MODEL, tn), lambda l, j: (l, 0, j)),
                  pl.BlockSpec((1, 1, tn), lambda l, j: (l, 0, j))],
        out_specs=pl.BlockSpec((1, N_COND, tn), lambda l, j: (l, 0, j)),
        out_shape=jax.ShapeDtypeStruct((DEPTH, N_COND, 6 * D_MODEL), F32),
        compiler_params=_cparams(("arbitrary", "arbitrary")),
        name="adaln",
    )(cond, w_ada, b_ada.reshape(DEPTH, 1, 6 * D_MODEL))


def _modulate_kernel(x_ref, mod_ref, u_ref):
    u_ref[...] = (x_ref[...] * (1.0 + mod_ref[1:2, :]) + mod_ref[0:1, :]).astype(BF16)


def _modulate(x, mods, l):
    tm = 1024
    return pl.pallas_call(
        _modulate_kernel,
        grid=(N_TOK // tm,),
        in_specs=[pl.BlockSpec((tm, D_MODEL), lambda i: (i, 0)),
                  pl.BlockSpec((None, None, 6, D_MODEL), lambda i: (l, _cond_of_row_tile(i, tm), 0, 0))],
        out_specs=pl.BlockSpec((tm, D_MODEL), lambda i: (i, 0)),
        out_shape=jax.ShapeDtypeStruct((N_TOK, D_MODEL), BF16),
        compiler_params=_cparams(("arbitrary",)),
        name="modulate",
    )(x, mods)


def _in_kernel(u_ref, w_ref, o_ref, wbf_ref):
    @pl.when(pl.program_id(1) == 0)
    def _():
        wbf_ref[...] = w_ref[...].astype(BF16)

    o_ref[...] = _dot(u_ref[...], wbf_ref[...]).astype(BF16)


def _in_proj(u, w_in, l):
    tm, tn = 1024, 1024
    return pl.pallas_call(
        _in_kernel,
        grid=(D_IN // tn, N_TOK // tm),
        in_specs=[pl.BlockSpec((tm, D_MODEL), lambda j, i: (i, 0)),
                  pl.BlockSpec((None, D_MODEL, tn), lambda j, i: (l, 0, j))],
        out_specs=pl.BlockSpec((tm, tn), lambda j, i: (i, j)),
        out_shape=jax.ShapeDtypeStruct((N_TOK, D_IN), BF16),
        scratch_shapes=[pltpu.VMEM((D_MODEL, tn), BF16)],
        compiler_params=_cparams(("arbitrary", "arbitrary")),
        name="in_proj",
    )(u, w_in)


def _rope(x, cos, sin_signed):
    lane = lax.broadcasted_iota(jnp.int32, x.shape, 1)
    first = (lane & 31) < 16
    partner = jnp.where(first, pltpu.roll(x, LANES - 16, 1), pltpu.roll(x, 16, 1))
    return x * cos + partner * sin_signed


LOG2E = 1.4426950408889634


def _attn_body(q, keys, vals, lam, subln, out_scale):
    lane = lax.broadcasted_iota(jnp.int32, q.shape, 1)
    qs = q * (DA_HEAD_DIM ** -0.5 * LOG2E)
    acc = None
    for m in range(2):
        in_map = (lane < DA_HEAD_DIM) if m == 0 else (lane >= DA_HEAD_DIM)
        qm = jnp.where(in_map, qs, 0.0).astype(BF16)
        s = [_dot_nt(qm, k) for k in keys]
        mx = s[0].max(axis=-1, keepdims=True)
        for si in s[1:]:
            mx = jnp.maximum(mx, si.max(axis=-1, keepdims=True))
        e = [jnp.exp2(si - mx) for si in s]
        den = e[0].sum(axis=-1, keepdims=True)
        for ei in e[1:]:
            den = den + ei.sum(axis=-1, keepdims=True)
        pv = _dot(e[0].astype(BF16), vals[0])
        for ei, v in zip(e[1:], vals[1:]):
            pv = pv + _dot(ei.astype(BF16), v)
        coef = (1.0 / den) if m == 0 else (-lam[:, 0:1] / den)
        acc = pv * coef if acc is None else acc + pv * coef
    y = acc * lax.rsqrt(jnp.mean(acc * acc, axis=-1, keepdims=True) + NORM_EPS)
    return y * subln * out_scale


def _attn_ctx_kernel(lam_ref, sub_ref, q_ref, k_ref, v_ref, o_ref, *, out_scale):
    for hd in range(DA_HEADS):
        cols = slice(hd * LANES, (hd + 1) * LANES)
        o_ref[:, cols] = _attn_body(q_ref[:, cols].astype(F32), [k_ref[:, cols]], [v_ref[:, cols]],
                                    lam_ref[...], sub_ref[...], out_scale)


def _attn_lat_kernel(lam_ref, sub_ref, q_ref, k_ref, v_ref, kc_ref, vc_ref, cos_ref, sin_ref,
                     o_ref, k_scr, kc_scr, vc_scr, *, out_scale, tq):
    k_scr[...] = _rope(k_ref[...].astype(F32), cos_ref[...], sin_ref[...]).astype(BF16)
    kc_scr[...] = kc_ref[...].astype(BF16)
    vc_scr[...] = vc_ref[...].astype(BF16)

    def q_block(qi, carry):
        rows = pl.ds(pl.multiple_of(qi * tq, tq), tq)
        q = _rope(q_ref[rows, :].astype(F32), cos_ref[rows, :], sin_ref[rows, :])
        o_ref[rows, :] = _attn_body(q, [k_scr[...], kc_scr[...]], [v_ref[...], vc_scr[...]],
                                    lam_ref[...], sub_ref[...], out_scale)
        return carry

    lax.fori_loop(0, DEC_SEQ // tq, q_block, 0)


def _rope_tables():
    t = np.arange(DEC_SEQ)
    row = (t // GRID_W).astype(np.float64)
    col = (t % GRID_W).astype(np.float64)
    d = DA_HEAD_DIM // 2
    inv = ROPE_BASE ** (-np.arange(0, d, 2, dtype=np.float64) / d)
    a_row = row[:, None] * inv[None, :]
    a_col = col[:, None] * inv[None, :]
    cos = np.concatenate([np.cos(a_row), np.cos(a_row), np.cos(a_col), np.cos(a_col)], -1)
    sin = np.concatenate([-np.sin(a_row), np.sin(a_row), -np.sin(a_col), np.sin(a_col)], -1)
    return (jnp.asarray(np.tile(cos, (1, 2)), F32), jnp.asarray(np.tile(sin, (1, 2)), F32))


def _attention(h, cache_k, cache_v, lam, subln, l, out_scale):
    lam_v = jnp.full((1, LANES), lam, F32)
    sub_v = subln.reshape(1, DA_V_DIM)
    small = pl.BlockSpec((1, LANES), lambda *a: (0, 0))
    att_ctx = pl.pallas_call(
        functools.partial(_attn_ctx_kernel, out_scale=out_scale),
        grid=(BATCH,),
        in_specs=[small, small,
                  pl.BlockSpec((SEQ, BRANCH_W), lambda b: (b, 0)),
                  pl.BlockSpec((SEQ, BRANCH_W), lambda b: (b, 1)),
                  pl.BlockSpec((SEQ, BRANCH_W), lambda b: (b, 2))],
        out_specs=pl.BlockSpec((SEQ, BRANCH_W), lambda b: (b, 0)),
        out_shape=jax.ShapeDtypeStruct((N_CTX, BRANCH_W), F32),
        compiler_params=_cparams(("arbitrary",)),
        name="attn_ctx",
    )(lam_v, sub_v, h, h, h)

    tq = 256
    cos, sin = _rope_tables()
    row0 = N_CTX // DEC_SEQ
    seq_blk = lambda cb: pl.BlockSpec((DEC_SEQ, LANES), lambda b, hd: (row0 + b, cb + hd))
    cache_blk = pl.BlockSpec((None, None, PAST_LEN, LANES), lambda b, hd: (b, l, 0, hd))
    table = pl.BlockSpec((DEC_SEQ, LANES), lambda b, hd: (0, 0))
    att_lat = pl.pallas_call(
        functools.partial(_attn_lat_kernel, out_scale=out_scale, tq=tq),
        grid=(DEC_BATCH, DA_HEADS),
        in_specs=[small, small, seq_blk(0), seq_blk(4), seq_blk(8), cache_blk, cache_blk, table, table],
        out_specs=pl.BlockSpec((DEC_SEQ, LANES), lambda b, hd: (b, hd)),
        out_shape=jax.ShapeDtypeStruct((N_LAT, BRANCH_W), F32),
        scratch_shapes=[pltpu.VMEM((DEC_SEQ, LANES), BF16), pltpu.VMEM((PAST_LEN, LANES), BF16),
                        pltpu.VMEM((PAST_LEN, LANES), BF16)],
        compiler_params=_cparams(("arbitrary", "arbitrary")),
        name="attn_lat",
    )(lam_v, sub_v, h, h, h, cache_k, cache_v, cos, sin)
    return att_ctx, att_lat


def _gelu_tanh(x):
    return 0.5 * x * (1.0 + jnp.tanh(math.sqrt(2.0 / math.pi) * (x + 0.044715 * (x * x * x))))


def _rg_kernel(*refs, seq, has_h0):
    if has_h0:
        rx_ref, gate_ref, cw_ref, cb_ref, wg_ref, bg_ref, lam_ref, h0_ref = refs[:8]
        rest = refs[8:]
    else:
        rx_ref, gate_ref, cw_ref, cb_ref, wg_ref, bg_ref, lam_ref = refs[:7]
        h0_ref = None
        rest = refs[7:]
    out_ref, hl_ref, a_scr, b_scr, h_scr = rest

    x = rx_ref[...].astype(F32)
    row = lax.broadcasted_iota(jnp.int32, x.shape, 0)
    xr = cb_ref[...] + cw_ref[2:3, :] * x
    for j in (0, 1, 3):
        d = j - RG_CONV_W // 2
        shifted = pltpu.roll(x, (-d) % seq, 0)
        valid = (row + d >= 0) & (row + d < seq)
        xr = xr + cw_ref[j:j + 1, :] * jnp.where(valid, shifted, 0.0)

    g = _dot(xr.astype(BF16), wg_ref[...]) + bg_ref[...]
    for dr in range(2):
        r = _sigmoid(g[:, (2 * dr) * RG_WIDTH:(2 * dr + 1) * RG_WIDTH])
        i = _sigmoid(g[:, (2 * dr + 1) * RG_WIDTH:(2 * dr + 2) * RG_WIDTH])
        lam = lam_ref[dr:dr + 1, :]
        softplus_neg = jnp.maximum(-lam, 0.0) + jnp.log(1.0 + jnp.exp(-jnp.abs(lam)))
        log_a = (-RG_C * softplus_neg) * r
        a_scr[dr] = jnp.exp(log_a)
        b_scr[dr] = jnp.sqrt(1.0 - jnp.exp(2.0 * log_a)) * i * xr

    if has_h0:
        hf0, hb0 = h0_ref[0:1, :], h0_ref[1:2, :]
    else:
        hf0 = hb0 = jnp.zeros((1, RG_WIDTH), F32)

    def step(t, carry):
        hf, hb = carry
        tb = seq - 1 - t
        hf = a_scr[0, pl.ds(t, 1), :] * hf + b_scr[0, pl.ds(t, 1), :]
        h_scr[0, pl.ds(t, 1), :] = hf
        hb = a_scr[1, pl.ds(tb, 1), :] * hb + b_scr[1, pl.ds(tb, 1), :]
        h_scr[1, pl.ds(tb, 1), :] = hb
        return hf, hb

    hf, hb = lax.fori_loop(0, seq, step, (hf0, hb0), unroll=8)
    hl_ref[0:1, :] = hf
    hl_ref[1:2, :] = hb
    out_ref[...] = (h_scr[0] + h_scr[1]) * _gelu_tanh(gate_ref[...].astype(F32))


def _rglru_call(h, conv_w, conv_b, wg, bg, lam, h0, nseq, seq, row_block0):
    has_h0 = h0 is not None
    full = lambda shape: pl.BlockSpec(shape, lambda b: (0,) * len(shape))
    in_specs = [pl.BlockSpec((seq, RG_WIDTH), lambda b: (row_block0 + b, 3)),
                pl.BlockSpec((seq, RG_WIDTH), lambda b: (row_block0 + b, 4)),
                full((RG_CONV_W, RG_WIDTH)), full((1, RG_WIDTH)),
                full((RG_WIDTH, 4 * RG_WIDTH)), full((1, 4 * RG_WIDTH)), full((2, RG_WIDTH))]
    args = [h, h, conv_w, conv_b.reshape(1, RG_WIDTH), wg, bg, lam]
    if has_h0:
        in_specs.append(pl.BlockSpec((None, 2, RG_WIDTH), lambda b: (b, 0, 0)))
        args.append(h0)
    return pl.pallas_call(
        functools.partial(_rg_kernel, seq=seq, has_h0=has_h0),
        grid=(nseq,),
        in_specs=in_specs,
        out_specs=[pl.BlockSpec((seq, RG_WIDTH), lambda b: (b, 0)),
                   pl.BlockSpec((None, 2, RG_WIDTH), lambda b: (b, 0, 0))],
        out_shape=[jax.ShapeDtypeStruct((nseq * seq, RG_WIDTH), F32),
                   jax.ShapeDtypeStruct((nseq, 2, RG_WIDTH), F32)],
        scratch_shapes=[pltpu.VMEM((2, seq, RG_WIDTH), F32)] * 3,
        compiler_params=_cparams(("arbitrary",)),
        name="rglru_lat" if has_h0 else "rglru_ctx",
    )(*args)


def _rg_gate_dense(gate_w, gate_b):
    eye = jnp.eye(RG_BLOCKS, dtype=F32)
    dense = jnp.einsum('dgncf,nm->dgncmf', gate_w, eye).reshape(2, 2, RG_WIDTH, RG_WIDTH)
    wg = jnp.transpose(dense, (2, 0, 1, 3)).reshape(RG_WIDTH, 4 * RG_WIDTH)
    return wg.astype(BF16), gate_b.reshape(1, 4 * RG_WIDTH)


def _hg_masks():
    c = HG_CHUNK
    t = np.arange(c)[:, None]
    s = np.arange(c)[None, :]
    tri = np.stack([s <= t, s >= t]).astype(np.float32)
    lvl = np.zeros((2, len(HG_LEVELS), c, c), np.float32)
    for n, h in enumerate(HG_LEVELS):
        same = (t // (2 * h)) == (s // (2 * h))
        lvl[0, n] = same & ((t // h) % 2 == 1) & ((s // h) % 2 == 0)
        lvl[1, n] = same & ((t // h) % 2 == 0) & ((s // h) % 2 == 1)
    same8 = ((t // 8) == (s // 8)).astype(np.float32)
    return jnp.asarray(tri, BF16), jnp.asarray(lvl), jnp.asarray(same8)


def _hg_chunk(q_raw, z, v, lb, st, tri, lvl, same8, rev):
    c = HG_CHUNK
    q = q_raw * _sigmoid(q_raw)
    sig = _sigmoid(z)
    g = jnp.log(lb + (1.0 - lb) * sig)
    kk = (1.0 - lb) * (1.0 - sig)

    g1 = g.astype(BF16)
    r1 = g - g1.astype(F32)
    g2 = r1.astype(BF16)
    g3 = (r1 - g2.astype(F32)).astype(BF16)
    gi = _dot(tri, g1) + _dot(tri, g2) + _dot(tri, g3)
    gx = gi - g

    row = lax.broadcasted_iota(jnp.int32, (c, c), 0)
    col = lax.broadcasted_iota(jnp.int32, (c, c), 1)
    off = (row - col) if not rev else (col - row)

    scores = jnp.where(off == 0, jnp.sum(q * kk, axis=-1, keepdims=True), 0.0)
    for d in range(1, 8):
        sh = d if not rev else c - d
        kk_s = pltpu.roll(kk, sh, 0)
        g_s = pltpu.roll(gi, sh, 0)
        e = jnp.exp(jnp.minimum(gi - g_s, 0.0))
        band = jnp.sum(q * kk_s * e, axis=-1, keepdims=True)
        scores = scores + jnp.where(off == d, band, 0.0)
    scores = scores * same8

    for n, h in enumerate(HG_LEVELS):
        nb = c // h
        gi3 = gi.reshape(nb, h, HG_KEY)
        gx3 = gx.reshape(nb, h, HG_KEY)
        if not rev:
            a = jnp.exp(jnp.minimum(gi3 - gx3[:, 0:1, :], 0.0))
            b = jnp.exp(jnp.minimum(gi3[:, h - 1:h, :] - gi3, 0.0))
        else:
            a = jnp.exp(jnp.minimum(gi3 - gx3[:, h - 1:h, :], 0.0))
            b = jnp.exp(jnp.minimum(gi3[:, 0:1, :] - gi3, 0.0))
        qa = (q * a.reshape(c, HG_KEY)).astype(BF16)
        kb = (kk * b.reshape(c, HG_KEY)).astype(BF16)
        scores = scores + lvl[n] * _dot_nt(qa, kb)

    vb = v.astype(BF16)
    o = _dot(scores.astype(BF16), vb)
    o = o + _dot_nt((q * jnp.exp(gi)).astype(BF16), st.astype(BF16))
    g_end = gi[c - 1:c, :] if not rev else gi[0:1, :]
    kd = (kk * jnp.exp(jnp.minimum(g_end - gi, 0.0))).astype(BF16)
    st_new = st * jnp.exp(g_end) + _dot(v.T.astype(BF16), kd)
    return o, st_new


def _hg_kernel(*refs, has_s0, nchunk):
    if has_s0:
        (qf_ref, qb_ref, zf_ref, zb_ref, vf_ref, vb_ref, lb_ref, tri_ref, lvl_ref, s8_ref, s0_ref,
         of_ref, ob_ref, sfin_ref, st_scr) = refs
    else:
        (qf_ref, qb_ref, zf_ref, zb_ref, vf_ref, vb_ref, lb_ref, tri_ref, lvl_ref, s8_ref,
         of_ref, ob_ref, sfin_ref, st_scr) = refs
        s0_ref = None
    ci = pl.program_id(1)

    @pl.when(ci == 0)
    def _():
        for dr in range(2):
            for hd in range(HG_HEADS):
                st_scr[dr, hd] = s0_ref[dr, hd].T if has_s0 else jnp.zeros((HG_VAL, HG_KEY), F32)

    s8 = s8_ref[...]
    last = ci == nchunk - 1
    for hd in range(HG_HEADS):
        cols = slice(hd * LANES, (hd + 1) * LANES)
        ld = lambda ref: ref[:, cols].astype(F32)
        o_f, st_f = _hg_chunk(ld(qf_ref), ld(zf_ref), ld(vf_ref), lb_ref[0:1, cols], st_scr[0, hd],
                              tri_ref[0], lvl_ref[0], s8, False)
        o_b, st_b = _hg_chunk(ld(qb_ref), ld(zb_ref), ld(vb_ref), lb_ref[1:2, cols], st_scr[1, hd],
                              tri_ref[1], lvl_ref[1], s8, True)
        of_ref[:, cols] = o_f
        ob_ref[:, cols] = o_b
        st_scr[0, hd] = st_f
        st_scr[1, hd] = st_b

        @pl.when(last)
        def _():
            sfin_ref[0, hd] = st_f.T
            sfin_ref[1, hd] = st_b.T


def _hgrn_call(h, lbs_l, s0, l, nseq, seq, row0):
    c = HG_CHUNK
    nchunk = seq // c
    has_s0 = s0 is not None
    tri, lvl, same8 = _hg_masks()
    rf = lambda b, ci: row0 // c + b * nchunk + ci
    rb = lambda b, ci: row0 // c + b * nchunk + (nchunk - 1 - ci)
    blk = lambda rfun, cb: pl.BlockSpec((c, BRANCH_W), lambda b, ci: (rfun(b, ci), cb))
    const = lambda shape: pl.BlockSpec(shape, lambda b, ci: (0,) * len(shape))
    in_specs = [blk(rf, 5), blk(rb, 5), blk(rf, 6), blk(rb, 7), blk(rf, 8), blk(rb, 8),
                const((2, BRANCH_W)), const((2, c, c)), const((2, len(HG_LEVELS), c, c)), const((c, c))]
    args = [h, h, h, h, h, h, lbs_l, tri, lvl, same8]
    if has_s0:
        in_specs.append(pl.BlockSpec((None, None, 2, HG_HEADS, HG_KEY, HG_VAL), lambda b, ci: (b, l, 0, 0, 0, 0)))
        args.append(s0)
    return pl.pallas_call(
        functools.partial(_hg_kernel, has_s0=has_s0, nchunk=nchunk),
        grid=(nseq, nchunk),
        in_specs=in_specs,
        out_specs=[pl.BlockSpec((c, BRANCH_W), lambda b, ci: (b * nchunk + ci, 0)),
                   pl.BlockSpec((c, BRANCH_W), lambda b, ci: (b * nchunk + (nchunk - 1 - ci), 0)),
                   pl.BlockSpec((None, 2, HG_HEADS, HG_KEY, HG_VAL), lambda b, ci: (b, 0, 0, 0, 0))],
        out_shape=[jax.ShapeDtypeStruct((nseq * seq, BRANCH_W), F32),
                   jax.ShapeDtypeStruct((nseq * seq, BRANCH_W), F32),
                   jax.ShapeDtypeStruct((nseq, 2, HG_HEADS, HG_KEY, HG_VAL), F32)],
        scratch_shapes=[pltpu.VMEM((2, HG_HEADS, HG_VAL, HG_KEY), F32)],
        compiler_params=_cparams(("arbitrary", "arbitrary")),
        name="hgrn_lat" if has_s0 else "hgrn_ctx",
    )(*args)


def _layer_norm(y, g, b):
    mu = jnp.mean(y, axis=-1, keepdims=True)
    yc = y - mu
    var = jnp.mean(yc * yc, axis=-1, keepdims=True)
    return yc * lax.rsqrt(var + NORM_EPS) * g + b


def _merge_kernel(attc_ref, attl_ref, rgc_ref, rgl_ref, ofc_ref, ofl_ref, obc_ref, obl_ref,
                  hgate_ref, mg0_ref, mg1_ref, mg2_ref, x_ref, mod_ref,
                  hgn_ref, wbr_ref, wout_ref, lng_ref, lnb_ref, rw_ref, rb_ref, tri_ref,
                  x1_ref, u2_ref, idx_ref, rank_ref, wts_ref, cnt_ref, wbr_bf, wout_bf, cnt_scr, *, ctx_tiles):
    @pl.when(pl.program_id(0) == 0)
    def _():
        wbr_bf[...] = wbr_ref[...].astype(BF16)
        wout_bf[...] = wout_ref[...].astype(BF16)
        cnt_scr[...] = jnp.zeros(cnt_scr.shape, F32)

    is_ctx = pl.program_id(0) < ctx_tiles
    pick = lambda c_ref, l_ref: jnp.where(is_ctx, c_ref[...], l_ref[...])
    att = pick(attc_ref, attl_ref)
    rg = pick(rgc_ref, rgl_ref)
    o = pick(ofc_ref, ofl_ref) + pick(obc_ref, obl_ref)
    hgate = hgate_ref[...].astype(F32)
    hg_parts = []
    for hd in range(HG_HEADS):
        oh = o[:, hd * HG_VAL:(hd + 1) * HG_VAL]
        gh = hgate[:, hd * HG_VAL:(hd + 1) * HG_VAL]
        yh = oh * lax.rsqrt(jnp.mean(oh * oh, axis=-1, keepdims=True) + NORM_EPS) * hgn_ref[...]
        hg_parts.append(yh * (gh * _sigmoid(gh)))
    hg = jnp.concatenate(hg_parts, axis=-1)

    proj = _sigmoid(mg0_ref[...].astype(F32)) * _dot(att.astype(BF16), wbr_bf[0])
    proj = proj + _sigmoid(mg1_ref[...].astype(F32)) * _dot(rg.astype(BF16), wbr_bf[1])
    proj = proj + _sigmoid(mg2_ref[...].astype(F32)) * _dot(hg.astype(BF16), wbr_bf[2])
    mix = _dot(proj.astype(BF16), wout_bf[...])

    x1 = _layer_norm(DN_ALPHA * x_ref[...] + mod_ref[2:3, :] * mix, lng_ref[...], lnb_ref[...])
    x1_ref[...] = x1
    u2 = x1 * (1.0 + mod_ref[4:5, :]) + mod_ref[3:4, :]
    u2_ref[...] = u2.astype(BF16)

    u2_hi = u2.astype(BF16)
    u2_lo = (u2 - u2_hi.astype(F32)).astype(BF16)
    logits = (_dot(u2_hi, rw_ref[0]) + _dot(u2_lo, rw_ref[0]) + _dot(u2_hi, rw_ref[1])) + rb_ref[...]
    lane = lax.broadcasted_iota(jnp.int32, logits.shape, 1).astype(F32)
    idx_out = jnp.zeros(logits.shape, F32)
    wts_out = jnp.zeros(logits.shape, F32)
    chosen = jnp.zeros(logits.shape, F32)
    sels = []
    top0 = None
    den = None
    for k in range(TOP_K):
        m = jnp.max(logits, axis=-1, keepdims=True)
        sel = jnp.min(jnp.where(logits == m, lane, float(LANES)), axis=-1, keepdims=True)
        sels.append(sel)
        if k == 0:
            top0 = m
        e = jnp.exp(m - top0)
        den = e if den is None else den + e
        idx_out = jnp.where(lane == k, sel, idx_out)
        wts_out = jnp.where(lane == k, e, wts_out)
        chosen = jnp.where(lane == sel, 1.0, chosen)
        logits = jnp.where(lane == sel, -jnp.inf, logits)
    idx_ref[...] = idx_out.astype(jnp.int32)
    wts_ref[...] = wts_out * (1.0 / den)

    before = _dot(tri_ref[...], chosen.astype(BF16)) + cnt_scr[...]
    rank_out = jnp.zeros(logits.shape, F32)
    for k in range(TOP_K):
        rk = jnp.sum(jnp.where(lane == sels[k], before, 0.0), axis=-1, keepdims=True)
        rank_out = jnp.where(lane == k, rk, rank_out)
    rank_ref[...] = rank_out.astype(jnp.int32)
    cnt_scr[...] = cnt_scr[...] + jnp.sum(chosen, axis=0, keepdims=True)
    cnt_ref[...] = jnp.broadcast_to(cnt_scr[...], cnt_ref.shape).astype(jnp.int32)


def _merge_call(att, rg, o_f, o_b, h, x, mods, p, l):
    tm = 256
    ctx_tiles = N_CTX // tm
    rowb = lambda w, cb: pl.BlockSpec((tm, w), lambda i: (i, cb))
    ctxb = pl.BlockSpec((tm, BRANCH_W), lambda i: (jnp.minimum(i, ctx_tiles - 1), 0))
    latb = pl.BlockSpec((tm, BRANCH_W), lambda i: (jnp.maximum(i - ctx_tiles, 0), 0))
    const = lambda shape: pl.BlockSpec(shape, lambda i: (0,) * len(shape))
    rw = jnp.zeros((D_MODEL, LANES), F32).at[:, :N_EXPERTS].set(p['router_w'][l])
    rb = jnp.full((1, LANES), -1e30, F32).at[0, :N_EXPERTS].set(p['router_b'][l])
    rw_hi = rw.astype(BF16)
    rw = jnp.stack([rw_hi, (rw - rw_hi.astype(F32)).astype(BF16)])
    tri = jnp.asarray(np.tril(np.ones((tm, tm), np.float32), -1), BF16)
    return pl.pallas_call(
        functools.partial(_merge_kernel, ctx_tiles=ctx_tiles),
        grid=(N_TOK // tm,),
        in_specs=[ctxb, latb, ctxb, latb, ctxb, latb, ctxb, latb,
                  rowb(BRANCH_W, 9), rowb(D_MODEL, 5), rowb(D_MODEL, 6), rowb(D_MODEL, 7),
                  rowb(D_MODEL, 0),
                  pl.BlockSpec((None, None, 6, D_MODEL), lambda i: (l, _cond_of_row_tile(i, tm), 0, 0)),
                  const((1, HG_VAL)), const((3, BRANCH_W, D_MODEL)), const((D_MODEL, D_MODEL)),
                  const((1, D_MODEL)), const((1, D_MODEL)), const((2, D_MODEL, LANES)), const((1, LANES)),
                  const((tm, tm))],
        out_specs=[rowb(D_MODEL, 0), rowb(D_MODEL, 0), rowb(LANES, 0), rowb(LANES, 0), rowb(LANES, 0),
                   const((8, LANES))],
        out_shape=[jax.ShapeDtypeStruct((N_TOK, D_MODEL), F32),
                   jax.ShapeDtypeStruct((N_TOK, D_MODEL), BF16),
                   jax.ShapeDtypeStruct((N_TOK, LANES), jnp.int32),
                   jax.ShapeDtypeStruct((N_TOK, LANES), jnp.int32),
                   jax.ShapeDtypeStruct((N_TOK, LANES), F32),
                   jax.ShapeDtypeStruct((8, LANES), jnp.int32)],
        scratch_shapes=[pltpu.VMEM((3, BRANCH_W, D_MODEL), BF16), pltpu.VMEM((D_MODEL, D_MODEL), BF16),
                        pltpu.VMEM((1, LANES), F32)],
        compiler_params=_cparams(("arbitrary",)),
        name="merge",
    )(att[0], att[1], rg[0], rg[1], o_f[0], o_f[1], o_b[0], o_b[1], h, h, h, h, x, mods,
      p['hg_norm'][l].reshape(1, HG_VAL), p['w_branch'][l], p['w_out'][l],
      p['ln1_g'][l].reshape(1, D_MODEL), p['ln1_b'][l].reshape(1, D_MODEL), rw, rb, tri)


MOE_NB = 256


def _moe_kernel(te_ref, tf_ref, tsl_ref, tnx_ref, nt_ref, x_ref, w1_hbm, b1_ref, w2_hbm, b2_ref, perm_ref,
                y_ref, w1_f, w2_f, w1_bf, w2_bf, sem, *, layer):
    i = pl.program_id(0)
    half = MOE_NB // 2

    def weight_copies(e, s):
        return (pltpu.make_async_copy(w1_hbm.at[layer, e], w1_f.at[s], sem.at[0, s]),
                pltpu.make_async_copy(w2_hbm.at[layer, e], w2_f.at[s], sem.at[1, s]))

    @pl.when(i == 0)
    def _():
        for cp in weight_copies(te_ref[0], 0):
            cp.start()

    @pl.when(tf_ref[i] == 1)
    def _():
        s = tsl_ref[i]
        for cp in weight_copies(te_ref[i], s):
            cp.wait()
        nxt = tnx_ref[i]

        @pl.when(nxt >= 0)
        def _():
            for cp in weight_copies(nxt, 1 - s):
                cp.start()

        for b in range(2 * D_EXPERT // MOE_NB):
            blk = w1_f[s, :, b * MOE_NB:(b + 1) * MOE_NB].astype(BF16)
            w1_bf[:, b * MOE_NB:(b + 1) * MOE_NB] = _dot(blk, perm_ref[...]).astype(BF16)
        w2_bf[...] = w2_f[s].astype(BF16)

    @pl.when(i < nt_ref[0])
    def _():
        h = _dot(x_ref[...], w1_bf[...]) + b1_ref[...]
        acts = []
        for b in range(2 * D_EXPERT // MOE_NB):
            glu = jnp.minimum(h[:, b * MOE_NB:b * MOE_NB + half], SWIGLU_LIMIT)
            lin = jnp.clip(h[:, b * MOE_NB + half:(b + 1) * MOE_NB], -SWIGLU_LIMIT, SWIGLU_LIMIT)
            acts.append((glu * _sigmoid(SWIGLU_ALPHA * glu) * (lin + 1.0)).astype(BF16))
        act = jnp.concatenate(acts, axis=-1)
        y_ref[...] = _dot(act, w2_bf[...]) + b2_ref[...]

    @pl.when(i >= nt_ref[0])
    def _():
        y_ref[...] = jnp.zeros(y_ref.shape, F32)


def _moe_perm():
    half = MOE_NB // 2
    pm = np.zeros((MOE_NB, MOE_NB), np.float32)
    pm[2 * np.arange(half), np.arange(half)] = 1.0
    pm[2 * np.arange(half) + 1, half + np.arange(half)] = 1.0
    return jnp.asarray(pm, BF16)


def _moe_call(x_sorted, sched, w1, b1p, w2, b2, l):
    tm = MOE_TM
    emap = lambda i, te, *_: (l, te[i], 0, 0)
    grid_spec = pltpu.PrefetchScalarGridSpec(
        num_scalar_prefetch=5,
        grid=(MOE_TILES,),
        in_specs=[pl.BlockSpec((tm, D_MODEL), lambda i, *_: (i, 0)),
                  pl.BlockSpec(memory_space=pl.ANY),
                  pl.BlockSpec((None, None, 1, 2 * D_EXPERT), emap),
                  pl.BlockSpec(memory_space=pl.ANY),
                  pl.BlockSpec((None, None, 1, D_MODEL), emap),
                  pl.BlockSpec((MOE_NB, MOE_NB), lambda i, *_: (0, 0))],
        out_specs=pl.BlockSpec((tm, D_MODEL), lambda i, *_: (i, 0)),
        scratch_shapes=[pltpu.VMEM((2, D_MODEL, 2 * D_EXPERT), F32), pltpu.VMEM((2, D_EXPERT, D_MODEL), F32),
                        pltpu.VMEM((D_MODEL, 2 * D_EXPERT), BF16), pltpu.VMEM((D_EXPERT, D_MODEL), BF16),
                        pltpu.SemaphoreType.DMA((2, 2))],
    )
    return pl.pallas_call(
        functools.partial(_moe_kernel, layer=l),
        grid_spec=grid_spec,
        out_shape=jax.ShapeDtypeStruct((MOE_TILES * tm, D_MODEL), F32),
        compiler_params=_cparams(("arbitrary",)),
        name="moe",
    )(*sched, x_sorted, w1, b1p, w2, b2, _moe_perm())


def _route(idx, rank, counts):
    tm = MOE_TM
    tiles_e = (counts + tm - 1) // tm
    eid = np.arange(N_EXPERTS, dtype=np.int32)
    earlier = (eid[None, :] <= eid[:, None]).astype(np.int32)
    tile_end = jnp.sum(earlier * tiles_e[None, :], axis=1)
    tile_start = tile_end - tiles_e
    pos = jnp.take(tile_start, idx) * tm + rank
    n_used = tile_end[N_EXPERTS - 1]
    tile_ids = jnp.arange(MOE_TILES, dtype=jnp.int32)
    tid = jnp.minimum(tile_ids, n_used - 1)
    tile_expert = jnp.sum((tile_end[None, :] <= tid[:, None]).astype(jnp.int32), axis=1)
    tile_first = ((tile_ids == jnp.take(tile_start, tile_expert)) & (tile_ids < n_used)).astype(jnp.int32)
    has_rows = (tiles_e > 0).astype(jnp.int32)
    slot_e = (jnp.sum(earlier * has_rows[None, :], axis=1) - 1) & 1
    later = jnp.where((eid[None, :] > eid[:, None]) & (has_rows[None, :] > 0), eid[None, :], N_EXPERTS)
    next_e = jnp.min(later, axis=1)
    next_e = jnp.where(next_e >= N_EXPERTS, -1, next_e).astype(jnp.int32)
    sched = (tile_expert, tile_first, jnp.take(slot_e, tile_expert).astype(jnp.int32),
             jnp.take(next_e, tile_expert), n_used.reshape(1))
    src_pair = jnp.zeros((MOE_TILES * tm,), jnp.int32).at[pos.reshape(-1)].set(
        jnp.arange(N_TOK * TOP_K, dtype=jnp.int32), mode='promise_in_bounds', unique_indices=True)
    return pos, src_pair // TOP_K, sched


def _final_kernel(yg_ref, wts_ref, x1_ref, mod_ref, modn_ref, lng_ref, lnb_ref, o_ref, un_ref):
    wts = wts_ref[...]
    ffn = wts[:, 0:1] * yg_ref[0]
    for k in range(1, TOP_K):
        ffn = ffn + wts[:, k:k + 1] * yg_ref[k]
    x2 = _layer_norm(DN_ALPHA * x1_ref[...] + mod_ref[5:6, :] * ffn, lng_ref[...], lnb_ref[...])
    o_ref[...] = x2
    un_ref[...] = (x2 * (1.0 + modn_ref[1:2, :]) + modn_ref[0:1, :]).astype(BF16)


def _final_call(yg, wts, x1, mods, ln_g, ln_b, l):
    tm = 256
    ln = min(l + 1, DEPTH - 1)
    const = lambda shape: pl.BlockSpec(shape, lambda i: (0,) * len(shape))
    modspec = lambda lyr: pl.BlockSpec((None, None, 6, D_MODEL), lambda i: (lyr, _cond_of_row_tile(i, tm), 0, 0))
    return pl.pallas_call(
        _final_kernel,
        grid=(N_TOK // tm,),
        in_specs=[pl.BlockSpec((TOP_K, tm, D_MODEL), lambda i: (0, i, 0)),
                  pl.BlockSpec((tm, LANES), lambda i: (i, 0)),
                  pl.BlockSpec((tm, D_MODEL), lambda i: (i, 0)),
                  modspec(l), modspec(ln),
                  const((1, D_MODEL)), const((1, D_MODEL))],
        out_specs=[pl.BlockSpec((tm, D_MODEL), lambda i: (i, 0)), pl.BlockSpec((tm, D_MODEL), lambda i: (i, 0))],
        out_shape=[jax.ShapeDtypeStruct((N_TOK, D_MODEL), F32), jax.ShapeDtypeStruct((N_TOK, D_MODEL), BF16)],
        compiler_params=_cparams(("arbitrary",)),
        name="final",
    )(yg, wts, x1, mods, mods, ln_g.reshape(1, D_MODEL), ln_b.reshape(1, D_MODEL))


def kernel(x_prompt, x_sample, cache_attn_k, cache_attn_v, state_rglru, state_hgrn, c, c_ctx, w_ada, b_ada, w_in, da_lambda, da_subln, rg_conv_w, rg_conv_b, rg_gate_w, rg_gate_b, rg_lambda, hg_lb, hg_norm, w_branch, w_out, ln1_g, ln1_b, router_w, router_b, w1, b1, w2, b2, ln2_g, ln2_b):
    p = dict(hg_norm=hg_norm, w_branch=w_branch, w_out=w_out, ln1_g=ln1_g, ln1_b=ln1_b,
             router_w=router_w, router_b=router_b)

    x = jnp.concatenate([x_prompt.reshape(N_CTX, D_MODEL), x_sample.reshape(N_LAT, D_MODEL)], axis=0)
    cond = jnp.concatenate([c_ctx[None, :], c, jnp.zeros((N_COND - 1 - DEC_BATCH, D_MODEL), F32)], axis=0)
    mods = _adaln_all(cond, w_ada, b_ada).reshape(DEPTH, N_COND, 6, D_MODEL)

    pr = jax.nn.softmax(hg_lb.astype(F32), axis=0)
    lbs = jnp.cumsum(pr, axis=0) - pr[0]
    dl = da_lambda.astype(F32)
    lam_all = jnp.exp(jnp.sum(dl[:, 0] * dl[:, 1], -1)) - jnp.exp(jnp.sum(dl[:, 2] * dl[:, 3], -1))

    cache_k = cache_attn_k.reshape(DEC_BATCH, DEPTH, PAST_LEN, BRANCH_W)
    cache_v = cache_attn_v.reshape(DEC_BATCH, DEPTH, PAST_LEN, BRANCH_W)

    b1p = b1.reshape(DEPTH, N_EXPERTS, 2 * D_EXPERT // MOE_NB, MOE_NB // 2, 2)
    b1p = jnp.swapaxes(b1p, -1, -2).reshape(DEPTH, N_EXPERTS, 1, 2 * D_EXPERT)
    b2r = b2.reshape(DEPTH, N_EXPERTS, 1, D_MODEL)

    ks, vs, rgs, hgs = [], [], [], []
    u = _modulate(x, mods, 0)
    for l in range(DEPTH):
        lambda_init = 0.8 - 0.6 * math.exp(-0.3 * l)
        h = _in_proj(u, w_in, l)
        ks.append(h[:N_CTX, BRANCH_W:2 * BRANCH_W].astype(F32).reshape(BATCH, SEQ, DA_HEADS, 2, DA_HEAD_DIM))
        vs.append(h[:N_CTX, 2 * BRANCH_W:3 * BRANCH_W].astype(F32).reshape(BATCH, SEQ, DA_HEADS, DA_V_DIM))

        att = _attention(h, cache_k, cache_v, lam_all[l] + lambda_init, da_subln[l], l, 1.0 - lambda_init)

        wg, bg = _rg_gate_dense(rg_gate_w[l], rg_gate_b[l])
        rg_c, hl_c = _rglru_call(h, rg_conv_w[l], rg_conv_b[l], wg, bg, rg_lambda[l], None, BATCH, SEQ, 0)
        rg_l, _ = _rglru_call(h, rg_conv_w[l], rg_conv_b[l], wg, bg, rg_lambda[l], state_rglru[:, l],
                              DEC_BATCH, DEC_SEQ, N_CTX // DEC_SEQ)
        rgs.append(hl_c)

        of_c, ob_c, sl_c = _hgrn_call(h, lbs[l], None, l, BATCH, SEQ, 0)
        of_l, ob_l, _ = _hgrn_call(h, lbs[l], state_hgrn, l, DEC_BATCH, DEC_SEQ, N_CTX)
        hgs.append(sl_c)

        x1, u2, idx, rank, wts, cnt = _merge_call(att, (rg_c, rg_l), (of_c, of_l), (ob_c, ob_l), h, x, mods, p, l)

        pos, src_tok, sched = _route(idx[:, :TOP_K], rank[:, :TOP_K], cnt[0, :N_EXPERTS])
        x_sorted = u2.at[src_tok].get(mode='promise_in_bounds')
        y_sorted = _moe_call(x_sorted, sched, w1, b1p, w2, b2r, l)
        yg = y_sorted.at[pos.T.reshape(-1)].get(mode='promise_in_bounds').reshape(TOP_K, N_TOK, D_MODEL)
        x, u = _final_call(yg, wts, x1, mods, ln2_g[l], ln2_b[l], l)

    y_prompt = x[:N_CTX].reshape(BATCH, SEQ, D_MODEL)
    y_sample = x[N_CTX:].reshape(DEC_BATCH, DEC_SEQ, D_MODEL)
    return (y_prompt, y_sample, jnp.stack(ks, axis=1), jnp.stack(vs, axis=1),
            jnp.stack(rgs, axis=1), jnp.stack(hgs, axis=1))
```

```python
import functools
import math

import numpy as np
import jax
import jax.numpy as jnp
from jax import lax
from jax.experimental import pallas as pl
from jax.experimental.pallas import tpu as pltpu
from jax.experimental.pallas import tpu_sc as plsc

F32 = jnp.float32
BF16 = jnp.bfloat16
HIGHEST = lax.Precision.HIGHEST

D_MODEL = 1024
BATCH = 16
SEQ = 256
DEPTH = 4
DEC_BATCH = 4
DEC_SEQ = 1024
PAST_LEN = 256
GRID_W = 64
BRANCH_W = 512
DA_HEADS = 4
DA_HEAD_DIM = 64
DA_V_DIM = 128
ROPE_BASE = 10000.0
RG_WIDTH = 512
RG_BLOCKS = 8
RG_BLOCK_W = 64
RG_CONV_W = 4
RG_C = 8.0
HG_HEADS = 4
HG_KEY = 128
HG_VAL = 128
N_EXPERTS = 32
TOP_K = 4
D_EXPERT = 1024
SWIGLU_ALPHA = 1.702
SWIGLU_LIMIT = 7.0
DN_ALPHA = (2 * DEPTH) ** 0.25
NORM_EPS = 1e-5
D_IN = 10 * BRANCH_W + 3 * D_MODEL

N_CTX = BATCH * SEQ
N_LAT = DEC_BATCH * DEC_SEQ
N_TOK = N_CTX + N_LAT
N_COND = 8

LANES = 128
VMEM_LIMIT = 56 * 1024 * 1024

HG_CHUNK = 128
HG_LEVELS = (8, 16, 32, 64)
MOE_TM = 256
MOE_TILES = (N_TOK * TOP_K) // MOE_TM + N_EXPERTS


def _cparams(sem):
    return pltpu.CompilerParams(dimension_semantics=sem, vmem_limit_bytes=VMEM_LIMIT)


def _sigmoid(x):
    return 1.0 / (1.0 + jnp.exp(-x))


def _dot(a, b):
    return jnp.dot(a, b, preferred_element_type=F32)


def _dot_nt(a, b):
    return lax.dot_general(a, b, (((1,), (1,)), ((), ())), preferred_element_type=F32)


def _cond_of_row_tile(i, tm):
    r = i * tm
    return jnp.where(r < N_CTX, 0, 1 + (r - N_CTX) // DEC_SEQ)


def _ada_kernel(c_ref, w_ref, b_ref, o_ref):
    c = c_ref[...]
    s = c * _sigmoid(c)
    o_ref[0] = jnp.dot(s, w_ref[0], precision=HIGHEST, preferred_element_type=F32) + b_ref[0]


def _adaln_all(cond, w_ada, b_ada):
    tn = 1536
    return pl.pallas_call(
        _ada_kernel,
        grid=(DEPTH, 6 * D_MODEL // tn),
        in_specs=[pl.BlockSpec((N_COND, D_MODEL), lambda l, j: (0, 0)),
                  pl.BlockSpec((1, D_MODEL, tn), lambda l, j: (l, 0, j)),
                  pl.BlockSpec((1, 1, tn), lambda l, j: (l, 0, j))],
        out_specs=pl.BlockSpec((1, N_COND, tn), lambda l, j: (l, 0, j)),
        out_shape=jax.ShapeDtypeStruct((DEPTH, N_COND, 6 * D_MODEL), F32),
        compiler_params=_cparams(("arbitrary", "arbitrary")),
        name="adaln",
    )(cond, w_ada, b_ada.reshape(DEPTH, 1, 6 * D_MODEL))


def _modulate_kernel(x_ref, mod_ref, u_ref):
    u_ref[...] = (x_ref[...] * (1.0 + mod_ref[1:2, :]) + mod_ref[0:1, :]).astype(BF16)


def _modulate(x, mods, l):
    tm = 1024
    return pl.pallas_call(
        _modulate_kernel,
        grid=(N_TOK // tm,),
        in_specs=[pl.BlockSpec((tm, D_MODEL), lambda i: (i, 0)),
                  pl.BlockSpec((None, None, 6, D_MODEL), lambda i: (l, _cond_of_row_tile(i, tm), 0, 0))],
        out_specs=pl.BlockSpec((tm, D_MODEL), lambda i: (i, 0)),
        out_shape=jax.ShapeDtypeStruct((N_TOK, D_MODEL), BF16),
        compiler_params=_cparams(("arbitrary",)),
        name="modulate",
    )(x, mods)


def _in_kernel(u_ref, w_ref, o_ref, wbf_ref):
    @pl.when(pl.program_id(1) == 0)
    def _():
        wbf_ref[...] = w_ref[...].astype(BF16)

    o_ref[...] = _dot(u_ref[...], wbf_ref[...]).astype(BF16)


def _in_proj(u, w_in, l):
    tm, tn = 1024, 1024
    return pl.pallas_call(
        _in_kernel,
        grid=(D_IN // tn, N_TOK // tm),
        in_specs=[pl.BlockSpec((tm, D_MODEL), lambda j, i: (i, 0)),
                  pl.BlockSpec((None, D_MODEL, tn), lambda j, i: (l, 0, j))],
        out_specs=pl.BlockSpec((tm, tn), lambda j, i: (i, j)),
        out_shape=jax.ShapeDtypeStruct((N_TOK, D_IN), BF16),
        scratch_shapes=[pltpu.VMEM((D_MODEL, tn), BF16)],
        compiler_params=_cparams(("arbitrary", "arbitrary")),
        name="in_proj",
    )(u, w_in)


def _rope(x, cos, sin_signed):
    lane = lax.broadcasted_iota(jnp.int32, x.shape, 1)
    first = (lane & 31) < 16
    partner = jnp.where(first, pltpu.roll(x, LANES - 16, 1), pltpu.roll(x, 16, 1))
    return x * cos + partner * sin_signed


LOG2E = 1.4426950408889634


def _attn_body(q, keys, vals, lam, subln, out_scale):
    lane = lax.broadcasted_iota(jnp.int32, q.shape, 1)
    qs = q * (DA_HEAD_DIM ** -0.5 * LOG2E)
    acc = None
    for m in range(2):
        in_map = (lane < DA_HEAD_DIM) if m == 0 else (lane >= DA_HEAD_DIM)
        qm = jnp.where(in_map, qs, 0.0).astype(BF16)
        s = [_dot_nt(qm, k) for k in keys]
        mx = s[0].max(axis=-1, keepdims=True)
        for si in s[1:]:
            mx = jnp.maximum(mx, si.max(axis=-1, keepdims=True))
        e = [jnp.exp2(si - mx) for si in s]
        den = e[0].sum(axis=-1, keepdims=True)
        for ei in e[1:]:
            den = den + ei.sum(axis=-1, keepdims=True)
        pv = _dot(e[0].astype(BF16), vals[0])
        for ei, v in zip(e[1:], vals[1:]):
            pv = pv + _dot(ei.astype(BF16), v)
        coef = (1.0 / den) if m == 0 else (-lam[:, 0:1] / den)
        acc = pv * coef if acc is None else acc + pv * coef
    y = acc * lax.rsqrt(jnp.mean(acc * acc, axis=-1, keepdims=True) + NORM_EPS)
    return y * subln * out_scale


def _attn_ctx_kernel(lam_ref, sub_ref, q_ref, k_ref, v_ref, o_ref, *, out_scale):
    for hd in range(DA_HEADS):
        cols = slice(hd * LANES, (hd + 1) * LANES)
        o_ref[:, cols] = _attn_body(q_ref[:, cols].astype(F32), [k_ref[:, cols]], [v_ref[:, cols]],
                                    lam_ref[...], sub_ref[...], out_scale)


def _attn_lat_kernel(lam_ref, sub_ref, q_ref, k_ref, v_ref, kc_ref, vc_ref, cos_ref, sin_ref,
                     o_ref, k_scr, kc_scr, vc_scr, *, out_scale, tq):
    k_scr[...] = _rope(k_ref[...].astype(F32), cos_ref[...], sin_ref[...]).astype(BF16)
    kc_scr[...] = kc_ref[...].astype(BF16)
    vc_scr[...] = vc_ref[...].astype(BF16)

    def q_block(qi, carry):
        rows = pl.ds(pl.multiple_of(qi * tq, tq), tq)
        q = _rope(q_ref[rows, :].astype(F32), cos_ref[rows, :], sin_ref[rows, :])
        o_ref[rows, :] = _attn_body(q, [k_scr[...], kc_scr[...]], [v_ref[...], vc_scr[...]],
                                    lam_ref[...], sub_ref[...], out_scale)
        return carry

    lax.fori_loop(0, DEC_SEQ // tq, q_block, 0)


def _rope_tables():
    t = np.arange(DEC_SEQ)
    row = (t // GRID_W).astype(np.float64)
    col = (t % GRID_W).astype(np.float64)
    d = DA_HEAD_DIM // 2
    inv = ROPE_BASE ** (-np.arange(0, d, 2, dtype=np.float64) / d)
    a_row = row[:, None] * inv[None, :]
    a_col = col[:, None] * inv[None, :]
    cos = np.concatenate([np.cos(a_row), np.cos(a_row), np.cos(a_col), np.cos(a_col)], -1)
    sin = np.concatenate([-np.sin(a_row), np.sin(a_row), -np.sin(a_col), np.sin(a_col)], -1)
    return (jnp.asarray(np.tile(cos, (1, 2)), F32), jnp.asarray(np.tile(sin, (1, 2)), F32))


def _attention(h, cache_k, cache_v, lam, subln, l, out_scale):
    lam_v = jnp.full((1, LANES), lam, F32)
    sub_v = subln.reshape(1, DA_V_DIM)
    small = pl.BlockSpec((1, LANES), lambda *a: (0, 0))
    att_ctx = pl.pallas_call(
        functools.partial(_attn_ctx_kernel, out_scale=out_scale),
        grid=(BATCH,),
        in_specs=[small, small,
                  pl.BlockSpec((SEQ, BRANCH_W), lambda b: (b, 0)),
                  pl.BlockSpec((SEQ, BRANCH_W), lambda b: (b, 1)),
                  pl.BlockSpec((SEQ, BRANCH_W), lambda b: (b, 2))],
        out_specs=pl.BlockSpec((SEQ, BRANCH_W), lambda b: (b, 0)),
        out_shape=jax.ShapeDtypeStruct((N_CTX, BRANCH_W), F32),
        compiler_params=_cparams(("arbitrary",)),
        name="attn_ctx",
    )(lam_v, sub_v, h, h, h)

    tq = 256
    cos, sin = _rope_tables()
    row0 = N_CTX // DEC_SEQ
    seq_blk = lambda cb: pl.BlockSpec((DEC_SEQ, LANES), lambda b, hd: (row0 + b, cb + hd))
    cache_blk = pl.BlockSpec((None, None, PAST_LEN, LANES), lambda b, hd: (b, l, 0, hd))
    table = pl.BlockSpec((DEC_SEQ, LANES), lambda b, hd: (0, 0))
    att_lat = pl.pallas_call(
        functools.partial(_attn_lat_kernel, out_scale=out_scale, tq=tq),
        grid=(DEC_BATCH, DA_HEADS),
        in_specs=[small, small, seq_blk(0), seq_blk(4), seq_blk(8), cache_blk, cache_blk, table, table],
        out_specs=pl.BlockSpec((DEC_SEQ, LANES), lambda b, hd: (b, hd)),
        out_shape=jax.ShapeDtypeStruct((N_LAT, BRANCH_W), F32),
        scratch_shapes=[pltpu.VMEM((DEC_SEQ, LANES), BF16), pltpu.VMEM((PAST_LEN, LANES), BF16),
                        pltpu.VMEM((PAST_LEN, LANES), BF16)],
        compiler_params=_cparams(("arbitrary", "arbitrary")),
        name="attn_lat",
    )(lam_v, sub_v, h, h, h, cache_k, cache_v, cos, sin)
    return att_ctx, att_lat


def _gelu_tanh(x):
    return 0.5 * x * (1.0 + jnp.tanh(math.sqrt(2.0 / math.pi) * (x + 0.044715 * (x * x * x))))


def _rg_kernel(*refs, seq, has_h0):
    if has_h0:
        rx_ref, gate_ref, cw_ref, cb_ref, wg_ref, bg_ref, lam_ref, h0_ref = refs[:8]
        rest = refs[8:]
    else:
        rx_ref, gate_ref, cw_ref, cb_ref, wg_ref, bg_ref, lam_ref = refs[:7]
        h0_ref = None
        rest = refs[7:]
    out_ref, hl_ref, a_scr, b_scr, h_scr = rest

    x = rx_ref[...].astype(F32)
    row = lax.broadcasted_iota(jnp.int32, x.shape, 0)
    xr = cb_ref[...] + cw_ref[2:3, :] * x
    for j in (0, 1, 3):
        d = j - RG_CONV_W // 2
        shifted = pltpu.roll(x, (-d) % seq, 0)
        valid = (row + d >= 0) & (row + d < seq)
        xr = xr + cw_ref[j:j + 1, :] * jnp.where(valid, shifted, 0.0)

    g = _dot(xr.astype(BF16), wg_ref[...]) + bg_ref[...]
    for dr in range(2):
        r = _sigmoid(g[:, (2 * dr) * RG_WIDTH:(2 * dr + 1) * RG_WIDTH])
        i = _sigmoid(g[:, (2 * dr + 1) * RG_WIDTH:(2 * dr + 2) * RG_WIDTH])
        lam = lam_ref[dr:dr + 1, :]
        softplus_neg = jnp.maximum(-lam, 0.0) + jnp.log(1.0 + jnp.exp(-jnp.abs(lam)))
        log_a = (-RG_C * softplus_neg) * r
        a_scr[dr] = jnp.exp(log_a)
        b_scr[dr] = jnp.sqrt(1.0 - jnp.exp(2.0 * log_a)) * i * xr

    if has_h0:
        hf0, hb0 = h0_ref[0:1, :], h0_ref[1:2, :]
    else:
        hf0 = hb0 = jnp.zeros((1, RG_WIDTH), F32)

    def step(t, carry):
        hf, hb = carry
        tb = seq - 1 - t
        hf = a_scr[0, pl.ds(t, 1), :] * hf + b_scr[0, pl.ds(t, 1), :]
        h_scr[0, pl.ds(t, 1), :] = hf
        hb = a_scr[1, pl.ds(tb, 1), :] * hb + b_scr[1, pl.ds(tb, 1), :]
        h_scr[1, pl.ds(tb, 1), :] = hb
        return hf, hb

    hf, hb = lax.fori_loop(0, seq, step, (hf0, hb0), unroll=8)
    hl_ref[0:1, :] = hf
    hl_ref[1:2, :] = hb
    out_ref[...] = (h_scr[0] + h_scr[1]) * _gelu_tanh(gate_ref[...].astype(F32))


def _rglru_call(h, conv_w, conv_b, wg, bg, lam, h0, nseq, seq, row_block0):
    has_h0 = h0 is not None
    full = lambda shape: pl.BlockSpec(shape, lambda b: (0,) * len(shape))
    in_specs = [pl.BlockSpec((seq, RG_WIDTH), lambda b: (row_block0 + b, 3)),
                pl.BlockSpec((seq, RG_WIDTH), lambda b: (row_block0 + b, 4)),
                full((RG_CONV_W, RG_WIDTH)), full((1, RG_WIDTH)),
                full((RG_WIDTH, 4 * RG_WIDTH)), full((1, 4 * RG_WIDTH)), full((2, RG_WIDTH))]
    args = [h, h, conv_w, conv_b.reshape(1, RG_WIDTH), wg, bg, lam]
    if has_h0:
        in_specs.append(pl.BlockSpec((None, 2, RG_WIDTH), lambda b: (b, 0, 0)))
        args.append(h0)
    return pl.pallas_call(
        functools.partial(_rg_kernel, seq=seq, has_h0=has_h0),
        grid=(nseq,),
        in_specs=in_specs,
        out_specs=[pl.BlockSpec((seq, RG_WIDTH), lambda b: (b, 0)),
                   pl.BlockSpec((None, 2, RG_WIDTH), lambda b: (b, 0, 0))],
        out_shape=[jax.ShapeDtypeStruct((nseq * seq, RG_WIDTH), F32),
                   jax.ShapeDtypeStruct((nseq, 2, RG_WIDTH), F32)],
        scratch_shapes=[pltpu.VMEM((2, seq, RG_WIDTH), F32)] * 3,
        compiler_params=_cparams(("arbitrary",)),
        name="rglru_lat" if has_h0 else "rglru_ctx",
    )(*args)


def _rg_gate_dense(gate_w, gate_b):
    eye = jnp.eye(RG_BLOCKS, dtype=F32)
    dense = jnp.einsum('dgncf,nm->dgncmf', gate_w, eye).reshape(2, 2, RG_WIDTH, RG_WIDTH)
    wg = jnp.transpose(dense, (2, 0, 1, 3)).reshape(RG_WIDTH, 4 * RG_WIDTH)
    return wg.astype(BF16), gate_b.reshape(1, 4 * RG_WIDTH)


def _hg_masks():
    c = HG_CHUNK
    t = np.arange(c)[:, None]
    s = np.arange(c)[None, :]
    tri = np.stack([s <= t, s >= t]).astype(np.float32)
    lvl = np.zeros((2, len(HG_LEVELS), c, c), np.float32)
    for n, h in enumerate(HG_LEVELS):
        same = (t // (2 * h)) == (s // (2 * h))
        lvl[0, n] = same & ((t // h) % 2 == 1) & ((s // h) % 2 == 0)
        lvl[1, n] = same & ((t // h) % 2 == 0) & ((s // h) % 2 == 1)
    same8 = ((t // 8) == (s // 8)).astype(np.float32)
    return jnp.asarray(tri, BF16), jnp.asarray(lvl), jnp.asarray(same8)


def _hg_chunk(q_raw, z, v, lb, st, tri, lvl, same8, rev):
    c = HG_CHUNK
    q = q_raw * _sigmoid(q_raw)
    sig = _sigmoid(z)
    g = jnp.log(lb + (1.0 - lb) * sig)
    kk = (1.0 - lb) * (1.0 - sig)

    g1 = g.astype(BF16)
    r1 = g - g1.astype(F32)
    g2 = r1.astype(BF16)
    g3 = (r1 - g2.astype(F32)).astype(BF16)
    gi = _dot(tri, g1) + _dot(tri, g2) + _dot(tri, g3)
    gx = gi - g

    row = lax.broadcasted_iota(jnp.int32, (c, c), 0)
    col = lax.broadcasted_iota(jnp.int32, (c, c), 1)
    off = (row - col) if not rev else (col - row)

    scores = jnp.where(off == 0, jnp.sum(q * kk, axis=-1, keepdims=True), 0.0)
    for d in range(1, 8):
        sh = d if not rev else c - d
        kk_s = pltpu.roll(kk, sh, 0)
        g_s = pltpu.roll(gi, sh, 0)
        e = jnp.exp(jnp.minimum(gi - g_s, 0.0))
        band = jnp.sum(q * kk_s * e, axis=-1, keepdims=True)
        scores = scores + jnp.where(off == d, band, 0.0)
    scores = scores * same8

    for n, h in enumerate(HG_LEVELS):
        nb = c // h
        gi3 = gi.reshape(nb, h, HG_KEY)
        gx3 = gx.reshape(nb, h, HG_KEY)
        if not rev:
            a = jnp.exp(jnp.minimum(gi3 - gx3[:, 0:1, :], 0.0))
            b = jnp.exp(jnp.minimum(gi3[:, h - 1:h, :] - gi3, 0.0))
        else:
            a = jnp.exp(jnp.minimum(gi3 - gx3[:, h - 1:h, :], 0.0))
            b = jnp.exp(jnp.minimum(gi3[:, 0:1, :] - gi3, 0.0))
        qa = (q * a.reshape(c, HG_KEY)).astype(BF16)
        kb = (kk * b.reshape(c, HG_KEY)).astype(BF16)
        scores = scores + lvl[n] * _dot_nt(qa, kb)

    vb = v.astype(BF16)
    o = _dot(scores.astype(BF16), vb)
    o = o + _dot_nt((q * jnp.exp(gi)).astype(BF16), st.astype(BF16))
    g_end = gi[c - 1:c, :] if not rev else gi[0:1, :]
    kd = (kk * jnp.exp(jnp.minimum(g_end - gi, 0.0))).astype(BF16)
    st_new = st * jnp.exp(g_end) + _dot(v.T.astype(BF16), kd)
    return o, st_new


def _hg_kernel(*refs, has_s0, nchunk):
    if has_s0:
        (qf_ref, qb_ref, zf_ref, zb_ref, vf_ref, vb_ref, lb_ref, tri_ref, lvl_ref, s8_ref, s0_ref,
         of_ref, ob_ref, sfin_ref, st_scr) = refs
    else:
        (qf_ref, qb_ref, zf_ref, zb_ref, vf_ref, vb_ref, lb_ref, tri_ref, lvl_ref, s8_ref,
         of_ref, ob_ref, sfin_ref, st_scr) = refs
        s0_ref = None
    ci = pl.program_id(1)

    @pl.when(ci == 0)
    def _():
        for dr in range(2):
            for hd in range(HG_HEADS):
                st_scr[dr, hd] = s0_ref[dr, hd].T if has_s0 else jnp.zeros((HG_VAL, HG_KEY), F32)

    s8 = s8_ref[...]
    last = ci == nchunk - 1
    for hd in range(HG_HEADS):
        cols = slice(hd * LANES, (hd + 1) * LANES)
        ld = lambda ref: ref[:, cols].astype(F32)
        o_f, st_f = _hg_chunk(ld(qf_ref), ld(zf_ref), ld(vf_ref), lb_ref[0:1, cols], st_scr[0, hd],
                              tri_ref[0], lvl_ref[0], s8, False)
        o_b, st_b = _hg_chunk(ld(qb_ref), ld(zb_ref), ld(vb_ref), lb_ref[1:2, cols], st_scr[1, hd],
                              tri_ref[1], lvl_ref[1], s8, True)
        of_ref[:, cols] = o_f
        ob_ref[:, cols] = o_b
        st_scr[0, hd] = st_f
        st_scr[1, hd] = st_b

        @pl.when(last)
        def _():
            sfin_ref[0, hd] = st_f.T
            sfin_ref[1, hd] = st_b.T


def _hgrn_call(h, lbs_l, s0, l, nseq, seq, row0):
    c = HG_CHUNK
    nchunk = seq // c
    has_s0 = s0 is not None
    tri, lvl, same8 = _hg_masks()
    rf = lambda b, ci: row0 // c + b * nchunk + ci
    rb = lambda b, ci: row0 // c + b * nchunk + (nchunk - 1 - ci)
    blk = lambda rfun, cb: pl.BlockSpec((c, BRANCH_W), lambda b, ci: (rfun(b, ci), cb))
    const = lambda shape: pl.BlockSpec(shape, lambda b, ci: (0,) * len(shape))
    in_specs = [blk(rf, 5), blk(rb, 5), blk(rf, 6), blk(rb, 7), blk(rf, 8), blk(rb, 8),
                const((2, BRANCH_W)), const((2, c, c)), const((2, len(HG_LEVELS), c, c)), const((c, c))]
    args = [h, h, h, h, h, h, lbs_l, tri, lvl, same8]
    if has_s0:
        in_specs.append(pl.BlockSpec((None, None, 2, HG_HEADS, HG_KEY, HG_VAL), lambda b, ci: (b, l, 0, 0, 0, 0)))
        args.append(s0)
    return pl.pallas_call(
        functools.partial(_hg_kernel, has_s0=has_s0, nchunk=nchunk),
        grid=(nseq, nchunk),
        in_specs=in_specs,
        out_specs=[pl.BlockSpec((c, BRANCH_W), lambda b, ci: (b * nchunk + ci, 0)),
                   pl.BlockSpec((c, BRANCH_W), lambda b, ci: (b * nchunk + (nchunk - 1 - ci), 0)),
                   pl.BlockSpec((None, 2, HG_HEADS, HG_KEY, HG_VAL), lambda b, ci: (b, 0, 0, 0, 0))],
        out_shape=[jax.ShapeDtypeStruct((nseq * seq, BRANCH_W), F32),
                   jax.ShapeDtypeStruct((nseq * seq, BRANCH_W), F32),
                   jax.ShapeDtypeStruct((nseq, 2, HG_HEADS, HG_KEY, HG_VAL), F32)],
        scratch_shapes=[pltpu.VMEM((2, HG_HEADS, HG_VAL, HG_KEY), F32)],
        compiler_params=_cparams(("arbitrary", "arbitrary")),
        name="hgrn_lat" if has_s0 else "hgrn_ctx",
    )(*args)


def _layer_norm(y, g, b):
    mu = jnp.mean(y, axis=-1, keepdims=True)
    yc = y - mu
    var = jnp.mean(yc * yc, axis=-1, keepdims=True)
    return yc * lax.rsqrt(var + NORM_EPS) * g + b


def _merge_kernel(attc_ref, attl_ref, rgc_ref, rgl_ref, ofc_ref, ofl_ref, obc_ref, obl_ref,
                  hgate_ref, mg0_ref, mg1_ref, mg2_ref, x_ref, mod_ref,
                  hgn_ref, wbr_ref, wout_ref, lng_ref, lnb_ref, rw_ref, rb_ref, tri_ref,
                  x1_ref, u2_ref, idx_ref, rank_ref, wts_ref, cnt_ref, wbr_bf, wout_bf, cnt_scr, *, ctx_tiles):
    @pl.when(pl.program_id(0) == 0)
    def _():
        wbr_bf[...] = wbr_ref[...].astype(BF16)
        wout_bf[...] = wout_ref[...].astype(BF16)
        cnt_scr[...] = jnp.zeros(cnt_scr.shape, F32)

    is_ctx = pl.program_id(0) < ctx_tiles
    pick = lambda c_ref, l_ref: jnp.where(is_ctx, c_ref[...], l_ref[...])
    att = pick(attc_ref, attl_ref)
    rg = pick(rgc_ref, rgl_ref)
    o = pick(ofc_ref, ofl_ref) + pick(obc_ref, obl_ref)
    hgate = hgate_ref[...].astype(F32)
    hg_parts = []
    for hd in range(HG_HEADS):
        oh = o[:, hd * HG_VAL:(hd + 1) * HG_VAL]
        gh = hgate[:, hd * HG_VAL:(hd + 1) * HG_VAL]
        yh = oh * lax.rsqrt(jnp.mean(oh * oh, axis=-1, keepdims=True) + NORM_EPS) * hgn_ref[...]
        hg_parts.append(yh * (gh * _sigmoid(gh)))
    hg = jnp.concatenate(hg_parts, axis=-1)

    proj = _sigmoid(mg0_ref[...].astype(F32)) * _dot(att.astype(BF16), wbr_bf[0])
    proj = proj + _sigmoid(mg1_ref[...].astype(F32)) * _dot(rg.astype(BF16), wbr_bf[1])
    proj = proj + _sigmoid(mg2_ref[...].astype(F32)) * _dot(hg.astype(BF16), wbr_bf[2])
    mix = _dot(proj.astype(BF16), wout_bf[...])

    x1 = _layer_norm(DN_ALPHA * x_ref[...] + mod_ref[2:3, :] * mix, lng_ref[...], lnb_ref[...])
    x1_ref[...] = x1
    u2 = x1 * (1.0 + mod_ref[4:5, :]) + mod_ref[3:4, :]
    for j in range(u2_ref.shape[0]):
        u2_ref[j] = u2[:, j * u2_ref.shape[2]:(j + 1) * u2_ref.shape[2]]

    u2_hi = u2.astype(BF16)
    u2_lo = (u2 - u2_hi.astype(F32)).astype(BF16)
    logits = (_dot(u2_hi, rw_ref[0]) + _dot(u2_lo, rw_ref[0]) + _dot(u2_hi, rw_ref[1])) + rb_ref[...]
    lane = lax.broadcasted_iota(jnp.int32, logits.shape, 1).astype(F32)
    idx_out = jnp.zeros(logits.shape, F32)
    wts_out = jnp.zeros(logits.shape, F32)
    chosen = jnp.zeros(logits.shape, F32)
    sels = []
    top0 = None
    den = None
    for k in range(TOP_K):
        m = jnp.max(logits, axis=-1, keepdims=True)
        sel = jnp.min(jnp.where(logits == m, lane, float(LANES)), axis=-1, keepdims=True)
        sels.append(sel)
        if k == 0:
            top0 = m
        e = jnp.exp(m - top0)
        den = e if den is None else den + e
        idx_out = jnp.where(lane == k, sel, idx_out)
        wts_out = jnp.where(lane == k, e, wts_out)
        chosen = jnp.where(lane == sel, 1.0, chosen)
        logits = jnp.where(lane == sel, -jnp.inf, logits)
    idx_ref[...] = idx_out.astype(jnp.int32)
    wts_ref[...] = wts_out * (1.0 / den)

    before = _dot(tri_ref[...], chosen.astype(BF16)) + cnt_scr[...]
    rank_out = jnp.zeros(logits.shape, F32)
    for k in range(TOP_K):
        rk = jnp.sum(jnp.where(lane == sels[k], before, 0.0), axis=-1, keepdims=True)
        rank_out = jnp.where(lane == k, rk, rank_out)
    rank_ref[...] = rank_out.astype(jnp.int32)
    cnt_scr[...] = cnt_scr[...] + jnp.sum(chosen, axis=0, keepdims=True)
    cnt_ref[...] = jnp.broadcast_to(cnt_scr[...], cnt_ref.shape).astype(jnp.int32)


def _merge_call(att, rg, o_f, o_b, h, x, mods, p, l):
    tm = 256
    ctx_tiles = N_CTX // tm
    rowb = lambda w, cb: pl.BlockSpec((tm, w), lambda i: (i, cb))
    ctxb = pl.BlockSpec((tm, BRANCH_W), lambda i: (jnp.minimum(i, ctx_tiles - 1), 0))
    latb = pl.BlockSpec((tm, BRANCH_W), lambda i: (jnp.maximum(i - ctx_tiles, 0), 0))
    const = lambda shape: pl.BlockSpec(shape, lambda i: (0,) * len(shape))
    rw = jnp.zeros((D_MODEL, LANES), F32).at[:, :N_EXPERTS].set(p['router_w'][l])
    rb = jnp.full((1, LANES), -1e30, F32).at[0, :N_EXPERTS].set(p['router_b'][l])
    rw_hi = rw.astype(BF16)
    rw = jnp.stack([rw_hi, (rw - rw_hi.astype(F32)).astype(BF16)])
    tri = jnp.asarray(np.tril(np.ones((tm, tm), np.float32), -1), BF16)
    return pl.pallas_call(
        functools.partial(_merge_kernel, ctx_tiles=ctx_tiles),
        grid=(N_TOK // tm,),
        in_specs=[ctxb, latb, ctxb, latb, ctxb, latb, ctxb, latb,
                  rowb(BRANCH_W, 9), rowb(D_MODEL, 5), rowb(D_MODEL, 6), rowb(D_MODEL, 7),
                  rowb(D_MODEL, 0),
                  pl.BlockSpec((None, None, 6, D_MODEL), lambda i: (l, _cond_of_row_tile(i, tm), 0, 0)),
                  const((1, HG_VAL)), const((3, BRANCH_W, D_MODEL)), const((D_MODEL, D_MODEL)),
                  const((1, D_MODEL)), const((1, D_MODEL)), const((2, D_MODEL, LANES)), const((1, LANES)),
                  const((tm, tm))],
        out_specs=[rowb(D_MODEL, 0), pl.BlockSpec((SC_SLABS, tm, SC_SLAB_W), lambda i: (0, i, 0)),
                   rowb(LANES, 0), rowb(LANES, 0), rowb(LANES, 0), const((8, LANES))],
        out_shape=[jax.ShapeDtypeStruct((N_TOK, D_MODEL), F32),
                   jax.ShapeDtypeStruct((SC_SLABS, N_TOK, SC_SLAB_W), F32),
                   jax.ShapeDtypeStruct((N_TOK, LANES), jnp.int32),
                   jax.ShapeDtypeStruct((N_TOK, LANES), jnp.int32),
                   jax.ShapeDtypeStruct((N_TOK, LANES), F32),
                   jax.ShapeDtypeStruct((8, LANES), jnp.int32)],
        scratch_shapes=[pltpu.VMEM((3, BRANCH_W, D_MODEL), BF16), pltpu.VMEM((D_MODEL, D_MODEL), BF16),
                        pltpu.VMEM((1, LANES), F32)],
        compiler_params=_cparams(("arbitrary",)),
        name="merge",
    )(att[0], att[1], rg[0], rg[1], o_f[0], o_f[1], o_b[0], o_b[1], h, h, h, h, x, mods,
      p['hg_norm'][l].reshape(1, HG_VAL), p['w_branch'][l], p['w_out'][l],
      p['ln1_g'][l].reshape(1, D_MODEL), p['ln1_b'][l].reshape(1, D_MODEL), rw, rb, tri)


MOE_NB = 256


def _moe_kernel(te_ref, tf_ref, tsl_ref, tnx_ref, nt_ref, x_ref, w1_hbm, b1_ref, w2_hbm, b2_ref, perm_ref,
                y_ref, w1_f, w2_f, w1_bf, w2_bf, sem, *, layer):
    i = pl.program_id(0)
    half = MOE_NB // 2

    def weight_copies(e, s):
        return (pltpu.make_async_copy(w1_hbm.at[layer, e], w1_f.at[s], sem.at[0, s]),
                pltpu.make_async_copy(w2_hbm.at[layer, e], w2_f.at[s], sem.at[1, s]))

    @pl.when(i == 0)
    def _():
        for cp in weight_copies(te_ref[0], 0):
            cp.start()

    @pl.when(tf_ref[i] == 1)
    def _():
        s = tsl_ref[i]
        for cp in weight_copies(te_ref[i], s):
            cp.wait()
        nxt = tnx_ref[i]

        @pl.when(nxt >= 0)
        def _():
            for cp in weight_copies(nxt, 1 - s):
                cp.start()

        for b in range(2 * D_EXPERT // MOE_NB):
            blk = w1_f[s, :, b * MOE_NB:(b + 1) * MOE_NB].astype(BF16)
            w1_bf[:, b * MOE_NB:(b + 1) * MOE_NB] = _dot(blk, perm_ref[...]).astype(BF16)
        w2_bf[...] = w2_f[s].astype(BF16)

    @pl.when(i < nt_ref[0])
    def _():
        x = jnp.concatenate([x_ref[j].astype(BF16) for j in range(x_ref.shape[0])], axis=-1)
        h = _dot(x, w1_bf[...]) + b1_ref[...]
        acts = []
        for b in range(2 * D_EXPERT // MOE_NB):
            glu = jnp.minimum(h[:, b * MOE_NB:b * MOE_NB + half], SWIGLU_LIMIT)
            lin = jnp.clip(h[:, b * MOE_NB + half:(b + 1) * MOE_NB], -SWIGLU_LIMIT, SWIGLU_LIMIT)
            acts.append((glu * _sigmoid(SWIGLU_ALPHA * glu) * (lin + 1.0)).astype(BF16))
        act = jnp.concatenate(acts, axis=-1)
        y_ref[...] = _dot(act, w2_bf[...]) + b2_ref[...]

    @pl.when(i >= nt_ref[0])
    def _():
        y_ref[...] = jnp.zeros(y_ref.shape, F32)


def _moe_perm():
    half = MOE_NB // 2
    pm = np.zeros((MOE_NB, MOE_NB), np.float32)
    pm[2 * np.arange(half), np.arange(half)] = 1.0
    pm[2 * np.arange(half) + 1, half + np.arange(half)] = 1.0
    return jnp.asarray(pm, BF16)


def _moe_call(x_sorted, sched, w1, b1p, w2, b2, l):
    tm = MOE_TM
    emap = lambda i, te, *_: (l, te[i], 0, 0)
    grid_spec = pltpu.PrefetchScalarGridSpec(
        num_scalar_prefetch=5,
        grid=(MOE_TILES,),
        in_specs=[pl.BlockSpec((SC_SLABS, tm, SC_SLAB_W), lambda i, *_: (0, i, 0)),
                  pl.BlockSpec(memory_space=pl.ANY),
                  pl.BlockSpec((None, None, 1, 2 * D_EXPERT), emap),
                  pl.BlockSpec(memory_space=pl.ANY),
                  pl.BlockSpec((None, None, 1, D_MODEL), emap),
                  pl.BlockSpec((MOE_NB, MOE_NB), lambda i, *_: (0, 0))],
        out_specs=pl.BlockSpec((tm, D_MODEL), lambda i, *_: (i, 0)),
        scratch_shapes=[pltpu.VMEM((2, D_MODEL, 2 * D_EXPERT), F32), pltpu.VMEM((2, D_EXPERT, D_MODEL), F32),
                        pltpu.VMEM((D_MODEL, 2 * D_EXPERT), BF16), pltpu.VMEM((D_EXPERT, D_MODEL), BF16),
                        pltpu.SemaphoreType.DMA((2, 2))],
    )
    return pl.pallas_call(
        functools.partial(_moe_kernel, layer=l),
        grid_spec=grid_spec,
        out_shape=jax.ShapeDtypeStruct((MOE_TILES * tm, D_MODEL), F32),
        compiler_params=_cparams(("arbitrary",)),
        name="moe",
    )(*sched, x_sorted, w1, b1p, w2, b2, _moe_perm())


def _route(idx, rank, counts):
    tm = MOE_TM
    tiles_e = (counts + tm - 1) // tm
    eid = np.arange(N_EXPERTS, dtype=np.int32)
    earlier = (eid[None, :] <= eid[:, None]).astype(np.int32)
    tile_end = jnp.sum(earlier * tiles_e[None, :], axis=1)
    tile_start = tile_end - tiles_e
    pos = jnp.take(tile_start, idx) * tm + rank
    n_used = tile_end[N_EXPERTS - 1]
    tile_ids = jnp.arange(MOE_TILES, dtype=jnp.int32)
    tid = jnp.minimum(tile_ids, n_used - 1)
    tile_expert = jnp.sum((tile_end[None, :] <= tid[:, None]).astype(jnp.int32), axis=1)
    tile_first = ((tile_ids == jnp.take(tile_start, tile_expert)) & (tile_ids < n_used)).astype(jnp.int32)
    has_rows = (tiles_e > 0).astype(jnp.int32)
    slot_e = (jnp.sum(earlier * has_rows[None, :], axis=1) - 1) & 1
    later = jnp.where((eid[None, :] > eid[:, None]) & (has_rows[None, :] > 0), eid[None, :], N_EXPERTS)
    next_e = jnp.min(later, axis=1)
    next_e = jnp.where(next_e >= N_EXPERTS, -1, next_e).astype(jnp.int32)
    sched = (tile_expert, tile_first, jnp.take(slot_e, tile_expert).astype(jnp.int32),
             jnp.take(next_e, tile_expert), n_used.reshape(1))
    return pos, sched


SC_WINDOW = 128
SC_SLABS = 4
SC_SLAB_W = D_MODEL // SC_SLABS


def _dispatch_rows(u2, pos_t):
    mesh = plsc.VectorSubcoreMesh(core_axis_name="core", subcore_axis_name="subcore")

    @functools.partial(pl.kernel, mesh=mesh, scratch_types=[],
                       out_type=jax.ShapeDtypeStruct((SC_SLABS, MOE_TILES * MOE_TM, SC_SLAB_W), F32))
    def dispatch(x_hbm, i_hbm, o_hbm):
        for j in range(SC_SLABS):
            def body(x_vmem, i_vmem, j=j):
                for k in range(TOP_K):
                    pltpu.sync_copy(x_vmem, o_hbm.at[j].at[i_vmem.at[k]])

            pltpu.emit_pipeline(
                body,
                grid=(N_TOK // SC_WINDOW,),
                in_specs=[pl.BlockSpec((SC_WINDOW, SC_SLAB_W), index_map=lambda i: (i, 0)),
                          pl.BlockSpec((TOP_K, SC_WINDOW), index_map=lambda i: (0, i))],
                out_specs=[],
                core_axis_name=("core", "subcore"),
                dimension_semantics=(pltpu.PARALLEL,),
            )(x_hbm.at[j], i_hbm)

    return dispatch(u2, pos_t)


def _final_kernel(yg_ref, wts_ref, x1_ref, mod_ref, modn_ref, lng_ref, lnb_ref, o_ref, un_ref):
    wts = wts_ref[...]
    ffn = wts[:, 0:1] * yg_ref[0]
    for k in range(1, TOP_K):
        ffn = ffn + wts[:, k:k + 1] * yg_ref[k]
    x2 = _layer_norm(DN_ALPHA * x1_ref[...] + mod_ref[5:6, :] * ffn, lng_ref[...], lnb_ref[...])
    o_ref[...] = x2
    un_ref[...] = (x2 * (1.0 + modn_ref[1:2, :]) + modn_ref[0:1, :]).astype(BF16)


def _final_call(yg, wts, x1, mods, ln_g, ln_b, l):
    tm = 256
    ln = min(l + 1, DEPTH - 1)
    const = lambda shape: pl.BlockSpec(shape, lambda i: (0,) * len(shape))
    modspec = lambda lyr: pl.BlockSpec((None, None, 6, D_MODEL), lambda i: (lyr, _cond_of_row_tile(i, tm), 0, 0))
    return pl.pallas_call(
        _final_kernel,
        grid=(N_TOK // tm,),
        in_specs=[pl.BlockSpec((TOP_K, tm, D_MODEL), lambda i: (0, i, 0)),
                  pl.BlockSpec((tm, LANES), lambda i: (i, 0)),
                  pl.BlockSpec((tm, D_MODEL), lambda i: (i, 0)),
                  modspec(l), modspec(ln),
                  const((1, D_MODEL)), const((1, D_MODEL))],
        out_specs=[pl.BlockSpec((tm, D_MODEL), lambda i: (i, 0)), pl.BlockSpec((tm, D_MODEL), lambda i: (i, 0))],
        out_shape=[jax.ShapeDtypeStruct((N_TOK, D_MODEL), F32), jax.ShapeDtypeStruct((N_TOK, D_MODEL), BF16)],
        compiler_params=_cparams(("arbitrary",)),
        name="final",
    )(yg, wts, x1, mods, mods, ln_g.reshape(1, D_MODEL), ln_b.reshape(1, D_MODEL))


def kernel(x_prompt, x_sample, cache_attn_k, cache_attn_v, state_rglru, state_hgrn, c, c_ctx, w_ada, b_ada, w_in, da_lambda, da_subln, rg_conv_w, rg_conv_b, rg_gate_w, rg_gate_b, rg_lambda, hg_lb, hg_norm, w_branch, w_out, ln1_g, ln1_b, router_w, router_b, w1, b1, w2, b2, ln2_g, ln2_b):
    p = dict(hg_norm=hg_norm, w_branch=w_branch, w_out=w_out, ln1_g=ln1_g, ln1_b=ln1_b,
             router_w=router_w, router_b=router_b)

    x = jnp.concatenate([x_prompt.reshape(N_CTX, D_MODEL), x_sample.reshape(N_LAT, D_MODEL)], axis=0)
    cond = jnp.concatenate([c_ctx[None, :], c, jnp.zeros((N_COND - 1 - DEC_BATCH, D_MODEL), F32)], axis=0)
    mods = _adaln_all(cond, w_ada, b_ada).reshape(DEPTH, N_COND, 6, D_MODEL)

    pr = jax.nn.softmax(hg_lb.astype(F32), axis=0)
    lbs = jnp.cumsum(pr, axis=0) - pr[0]
    dl = da_lambda.astype(F32)
    lam_all = jnp.exp(jnp.sum(dl[:, 0] * dl[:, 1], -1)) - jnp.exp(jnp.sum(dl[:, 2] * dl[:, 3], -1))

    cache_k = cache_attn_k.reshape(DEC_BATCH, DEPTH, PAST_LEN, BRANCH_W)
    cache_v = cache_attn_v.reshape(DEC_BATCH, DEPTH, PAST_LEN, BRANCH_W)

    b1p = b1.reshape(DEPTH, N_EXPERTS, 2 * D_EXPERT // MOE_NB, MOE_NB // 2, 2)
    b1p = jnp.swapaxes(b1p, -1, -2).reshape(DEPTH, N_EXPERTS, 1, 2 * D_EXPERT)
    b2r = b2.reshape(DEPTH, N_EXPERTS, 1, D_MODEL)

    ks, vs, rgs, hgs = [], [], [], []
    u = _modulate(x, mods, 0)
    for l in range(DEPTH):
        lambda_init = 0.8 - 0.6 * math.exp(-0.3 * l)
        h = _in_proj(u, w_in, l)
        ks.append(h[:N_CTX, BRANCH_W:2 * BRANCH_W].astype(F32).reshape(BATCH, SEQ, DA_HEADS, 2, DA_HEAD_DIM))
        vs.append(h[:N_CTX, 2 * BRANCH_W:3 * BRANCH_W].astype(F32).reshape(BATCH, SEQ, DA_HEADS, DA_V_DIM))

        att = _attention(h, cache_k, cache_v, lam_all[l] + lambda_init, da_subln[l], l, 1.0 - lambda_init)

        wg, bg = _rg_gate_dense(rg_gate_w[l], rg_gate_b[l])
        rg_c, hl_c = _rglru_call(h, rg_conv_w[l], rg_conv_b[l], wg, bg, rg_lambda[l], None, BATCH, SEQ, 0)
        rg_l, _ = _rglru_call(h, rg_conv_w[l], rg_conv_b[l], wg, bg, rg_lambda[l], state_rglru[:, l],
                              DEC_BATCH, DEC_SEQ, N_CTX // DEC_SEQ)
        rgs.append(hl_c)

        of_c, ob_c, sl_c = _hgrn_call(h, lbs[l], None, l, BATCH, SEQ, 0)
        of_l, ob_l, _ = _hgrn_call(h, lbs[l], state_hgrn, l, DEC_BATCH, DEC_SEQ, N_CTX)
        hgs.append(sl_c)

        x1, u2, idx, rank, wts, cnt = _merge_call(att, (rg_c, rg_l), (of_c, of_l), (ob_c, ob_l), h, x, mods, p, l)

        pos, sched = _route(idx[:, :TOP_K], rank[:, :TOP_K], cnt[0, :N_EXPERTS])
        x_sorted = _dispatch_rows(u2, pos.T)
        y_sorted = _moe_call(x_sorted, sched, w1, b1p, w2, b2r, l)
        yg = y_sorted.at[pos.T.reshape(-1)].get(mode='promise_in_bounds').reshape(TOP_K, N_TOK, D_MODEL)
        x, u = _final_call(yg, wts, x1, mods, ln2_g[l], ln2_b[l], l)

    y_prompt = x[:N_CTX].reshape(BATCH, SEQ, D_MODEL)
    y_sample = x[N_CTX:].reshape(DEC_BATCH, DEC_SEQ, D_MODEL)
    return (y_prompt, y_sample, jnp.stack(ks, axis=1), jnp.stack(vs, axis=1),
            jnp.stack(rgs, axis=1), jnp.stack(hgs, axis=1))
```

```python
import functools
import math

import numpy as np
import jax
import jax.numpy as jnp
from jax import lax
from jax.experimental import pallas as pl
from jax.experimental.pallas import tpu as pltpu
from jax.experimental.pallas import tpu_sc as plsc

F32 = jnp.float32
BF16 = jnp.bfloat16
HIGHEST = lax.Precision.HIGHEST

D_MODEL = 1024
BATCH = 16
SEQ = 256
DEPTH = 4
DEC_BATCH = 4
DEC_SEQ = 1024
PAST_LEN = 256
GRID_W = 64
BRANCH_W = 512
DA_HEADS = 4
DA_HEAD_DIM = 64
DA_V_DIM = 128
ROPE_BASE = 10000.0
RG_WIDTH = 512
RG_BLOCKS = 8
RG_BLOCK_W = 64
RG_CONV_W = 4
RG_C = 8.0
HG_HEADS = 4
HG_KEY = 128
HG_VAL = 128
N_EXPERTS = 32
TOP_K = 4
D_EXPERT = 1024
SWIGLU_ALPHA = 1.702
SWIGLU_LIMIT = 7.0
DN_ALPHA = (2 * DEPTH) ** 0.25
NORM_EPS = 1e-5
D_IN = 10 * BRANCH_W + 3 * D_MODEL

N_CTX = BATCH * SEQ
N_LAT = DEC_BATCH * DEC_SEQ
N_TOK = N_CTX + N_LAT
N_COND = 8

LANES = 128
VMEM_LIMIT = 56 * 1024 * 1024

HG_CHUNK = 128
HG_LEVELS = (1, 2, 4, 8, 16, 32, 64)
MOE_TM = 256
MOE_TILES = (N_TOK * TOP_K) // MOE_TM + N_EXPERTS


def _cparams(sem):
    return pltpu.CompilerParams(dimension_semantics=sem, vmem_limit_bytes=VMEM_LIMIT)


def _sigmoid(x):
    return 1.0 / (1.0 + jnp.exp(-x))


def _dot(a, b):
    return jnp.dot(a, b, preferred_element_type=F32)


def _dot_nt(a, b):
    return lax.dot_general(a, b, (((1,), (1,)), ((), ())), preferred_element_type=F32)


def _cond_of_row_tile(i, tm):
    r = i * tm
    return jnp.where(r < N_CTX, 0, 1 + (r - N_CTX) // DEC_SEQ)


def _ada_kernel(c_ref, w_ref, b_ref, o_ref):
    c = c_ref[...]
    s = c * _sigmoid(c)
    o_ref[0] = jnp.dot(s, w_ref[0], precision=HIGHEST, preferred_element_type=F32) + b_ref[0]


def _adaln_all(cond, w_ada, b_ada):
    tn = 1536
    return pl.pallas_call(
        _ada_kernel,
        grid=(DEPTH, 6 * D_MODEL // tn),
        in_specs=[pl.BlockSpec((N_COND, D_MODEL), lambda l, j: (0, 0)),
                  pl.BlockSpec((1, D_MODEL, tn), lambda l, j: (l, 0, j)),
                  pl.BlockSpec((1, 1, tn), lambda l, j: (l, 0, j))],
        out_specs=pl.BlockSpec((1, N_COND, tn), lambda l, j: (l, 0, j)),
        out_shape=jax.ShapeDtypeStruct((DEPTH, N_COND, 6 * D_MODEL), F32),
        compiler_params=_cparams(("arbitrary", "arbitrary")),
        name="adaln",
    )(cond, w_ada, b_ada.reshape(DEPTH, 1, 6 * D_MODEL))


def _modulate_kernel(x_ref, mod_ref, u_ref):
    u_ref[...] = (x_ref[...] * (1.0 + mod_ref[1:2, :]) + mod_ref[0:1, :]).astype(BF16)


def _modulate(x, mods, l):
    tm = 1024
    return pl.pallas_call(
        _modulate_kernel,
        grid=(N_TOK // tm,),
        in_specs=[pl.BlockSpec((tm, D_MODEL), lambda i: (i, 0)),
                  pl.BlockSpec((None, None, 6, D_MODEL), lambda i: (l, _cond_of_row_tile(i, tm), 0, 0))],
        out_specs=pl.BlockSpec((tm, D_MODEL), lambda i: (i, 0)),
        out_shape=jax.ShapeDtypeStruct((N_TOK, D_MODEL), BF16),
        compiler_params=_cparams(("arbitrary",)),
        name="modulate",
    )(x, mods)


def _in_kernel(u_ref, w_ref, o_ref, wbf_ref):
    @pl.when(pl.program_id(1) == 0)
    def _():
        wbf_ref[...] = w_ref[...].astype(BF16)

    o_ref[...] = _dot(u_ref[...], wbf_ref[...]).astype(BF16)


def _in_proj(u, w_in, l):
    tm, tn = 1024, 1024
    return pl.pallas_call(
        _in_kernel,
        grid=(D_IN // tn, N_TOK // tm),
        in_specs=[pl.BlockSpec((tm, D_MODEL), lambda j, i: (i, 0)),
                  pl.BlockSpec((None, D_MODEL, tn), lambda j, i: (l, 0, j))],
        out_specs=pl.BlockSpec((tm, tn), lambda j, i: (i, j)),
        out_shape=jax.ShapeDtypeStruct((N_TOK, D_IN), BF16),
        scratch_shapes=[pltpu.VMEM((D_MODEL, tn), BF16)],
        compiler_params=_cparams(("arbitrary", "arbitrary")),
        name="in_proj",
    )(u, w_in)


def _rope(x, cos, sin_signed):
    lane = lax.broadcasted_iota(jnp.int32, x.shape, 1)
    first = (lane & 31) < 16
    partner = jnp.where(first, pltpu.roll(x, LANES - 16, 1), pltpu.roll(x, 16, 1))
    return x * cos + partner * sin_signed


LOG2E = 1.4426950408889634


def _attn_body(q, keys, vals, lam, subln, out_scale):
    lane = lax.broadcasted_iota(jnp.int32, q.shape, 1)
    qs = q * (DA_HEAD_DIM ** -0.5 * LOG2E)
    acc = None
    for m in range(2):
        in_map = (lane < DA_HEAD_DIM) if m == 0 else (lane >= DA_HEAD_DIM)
        qm = jnp.where(in_map, qs, 0.0).astype(BF16)
        s = [_dot_nt(qm, k) for k in keys]
        mx = s[0].max(axis=-1, keepdims=True)
        for si in s[1:]:
            mx = jnp.maximum(mx, si.max(axis=-1, keepdims=True))
        e = [jnp.exp2(si - mx) for si in s]
        den = e[0].sum(axis=-1, keepdims=True)
        for ei in e[1:]:
            den = den + ei.sum(axis=-1, keepdims=True)
        pv = _dot(e[0].astype(BF16), vals[0])
        for ei, v in zip(e[1:], vals[1:]):
            pv = pv + _dot(ei.astype(BF16), v)
        coef = (1.0 / den) if m == 0 else (-lam[:, 0:1] / den)
        acc = pv * coef if acc is None else acc + pv * coef
    y = acc * lax.rsqrt(jnp.mean(acc * acc, axis=-1, keepdims=True) + NORM_EPS)
    return y * subln * out_scale


def _attn_ctx_kernel(lam_ref, sub_ref, q_ref, k_ref, v_ref, o_ref, *, out_scale):
    for hd in range(DA_HEADS):
        cols = slice(hd * LANES, (hd + 1) * LANES)
        o_ref[:, cols] = _attn_body(q_ref[:, cols].astype(F32), [k_ref[:, cols]], [v_ref[:, cols]],
                                    lam_ref[...], sub_ref[...], out_scale)


def _attn_lat_kernel(lam_ref, sub_ref, q_ref, k_ref, v_ref, kc_ref, vc_ref, cos_ref, sin_ref,
                     o_ref, k_scr, kc_scr, vc_scr, *, out_scale, tq):
    k_scr[...] = _rope(k_ref[...].astype(F32), cos_ref[...], sin_ref[...]).astype(BF16)
    kc_scr[...] = kc_ref[...].astype(BF16)
    vc_scr[...] = vc_ref[...].astype(BF16)

    def q_block(qi, carry):
        rows = pl.ds(pl.multiple_of(qi * tq, tq), tq)
        q = _rope(q_ref[rows, :].astype(F32), cos_ref[rows, :], sin_ref[rows, :])
        o_ref[rows, :] = _attn_body(q, [k_scr[...], kc_scr[...]], [v_ref[...], vc_scr[...]],
                                    lam_ref[...], sub_ref[...], out_scale)
        return carry

    lax.fori_loop(0, DEC_SEQ // tq, q_block, 0)


def _rope_tables():
    t = np.arange(DEC_SEQ)
    row = (t // GRID_W).astype(np.float64)
    col = (t % GRID_W).astype(np.float64)
    d = DA_HEAD_DIM // 2
    inv = ROPE_BASE ** (-np.arange(0, d, 2, dtype=np.float64) / d)
    a_row = row[:, None] * inv[None, :]
    a_col = col[:, None] * inv[None, :]
    cos = np.concatenate([np.cos(a_row), np.cos(a_row), np.cos(a_col), np.cos(a_col)], -1)
    sin = np.concatenate([-np.sin(a_row), np.sin(a_row), -np.sin(a_col), np.sin(a_col)], -1)
    return (jnp.asarray(np.tile(cos, (1, 2)), F32), jnp.asarray(np.tile(sin, (1, 2)), F32))


def _attention(h, cache_k, cache_v, lam, subln, l, out_scale):
    lam_v = jnp.full((1, LANES), lam, F32)
    sub_v = subln.reshape(1, DA_V_DIM)
    small = pl.BlockSpec((1, LANES), lambda *a: (0, 0))
    att_ctx = pl.pallas_call(
        functools.partial(_attn_ctx_kernel, out_scale=out_scale),
        grid=(BATCH,),
        in_specs=[small, small,
                  pl.BlockSpec((SEQ, BRANCH_W), lambda b: (b, 0)),
                  pl.BlockSpec((SEQ, BRANCH_W), lambda b: (b, 1)),
                  pl.BlockSpec((SEQ, BRANCH_W), lambda b: (b, 2))],
        out_specs=pl.BlockSpec((SEQ, BRANCH_W), lambda b: (b, 0)),
        out_shape=jax.ShapeDtypeStruct((N_CTX, BRANCH_W), F32),
        compiler_params=_cparams(("arbitrary",)),
        name="attn_ctx",
    )(lam_v, sub_v, h, h, h)

    tq = 256
    cos, sin = _rope_tables()
    row0 = N_CTX // DEC_SEQ
    seq_blk = lambda cb: pl.BlockSpec((DEC_SEQ, LANES), lambda b, hd: (row0 + b, cb + hd))
    cache_blk = pl.BlockSpec((None, None, PAST_LEN, LANES), lambda b, hd: (b, l, 0, hd))
    table = pl.BlockSpec((DEC_SEQ, LANES), lambda b, hd: (0, 0))
    att_lat = pl.pallas_call(
        functools.partial(_attn_lat_kernel, out_scale=out_scale, tq=tq),
        grid=(DEC_BATCH, DA_HEADS),
        in_specs=[small, small, seq_blk(0), seq_blk(4), seq_blk(8), cache_blk, cache_blk, table, table],
        out_specs=pl.BlockSpec((DEC_SEQ, LANES), lambda b, hd: (b, hd)),
        out_shape=jax.ShapeDtypeStruct((N_LAT, BRANCH_W), F32),
        scratch_shapes=[pltpu.VMEM((DEC_SEQ, LANES), BF16), pltpu.VMEM((PAST_LEN, LANES), BF16),
                        pltpu.VMEM((PAST_LEN, LANES), BF16)],
        compiler_params=_cparams(("arbitrary", "arbitrary")),
        name="attn_lat",
    )(lam_v, sub_v, h, h, h, cache_k, cache_v, cos, sin)
    return att_ctx, att_lat


def _gelu_tanh(x):
    return 0.5 * x * (1.0 + jnp.tanh(math.sqrt(2.0 / math.pi) * (x + 0.044715 * (x * x * x))))


def _rg_kernel(*refs, seq, has_h0):
    if has_h0:
        rx_ref, gate_ref, cw_ref, cb_ref, wg_ref, bg_ref, lam_ref, h0_ref = refs[:8]
        rest = refs[8:]
    else:
        rx_ref, gate_ref, cw_ref, cb_ref, wg_ref, bg_ref, lam_ref = refs[:7]
        h0_ref = None
        rest = refs[7:]
    out_ref, hl_ref, a_scr, b_scr, h_scr = rest

    x = rx_ref[...].astype(F32)
    row = lax.broadcasted_iota(jnp.int32, x.shape, 0)
    xr = cb_ref[...] + cw_ref[2:3, :] * x
    for j in (0, 1, 3):
        d = j - RG_CONV_W // 2
        shifted = pltpu.roll(x, (-d) % seq, 0)
        valid = (row + d >= 0) & (row + d < seq)
        xr = xr + cw_ref[j:j + 1, :] * jnp.where(valid, shifted, 0.0)

    g = _dot(xr.astype(BF16), wg_ref[...]) + bg_ref[...]
    for dr in range(2):
        r = _sigmoid(g[:, (2 * dr) * RG_WIDTH:(2 * dr + 1) * RG_WIDTH])
        i = _sigmoid(g[:, (2 * dr + 1) * RG_WIDTH:(2 * dr + 2) * RG_WIDTH])
        lam = lam_ref[dr:dr + 1, :]
        softplus_neg = jnp.maximum(-lam, 0.0) + jnp.log(1.0 + jnp.exp(-jnp.abs(lam)))
        log_a = (-RG_C * softplus_neg) * r
        a_scr[dr] = jnp.exp(log_a)
        b_scr[dr] = jnp.sqrt(1.0 - jnp.exp(2.0 * log_a)) * i * xr

    if has_h0:
        hf0, hb0 = h0_ref[0:1, :], h0_ref[1:2, :]
    else:
        hf0 = hb0 = jnp.zeros((1, RG_WIDTH), F32)

    def step(t, carry):
        hf, hb = carry
        tb = seq - 1 - t
        hf = a_scr[0, pl.ds(t, 1), :] * hf + b_scr[0, pl.ds(t, 1), :]
        h_scr[0, pl.ds(t, 1), :] = hf
        hb = a_scr[1, pl.ds(tb, 1), :] * hb + b_scr[1, pl.ds(tb, 1), :]
        h_scr[1, pl.ds(tb, 1), :] = hb
        return hf, hb

    hf, hb = lax.fori_loop(0, seq, step, (hf0, hb0), unroll=8)
    hl_ref[0:1, :] = hf
    hl_ref[1:2, :] = hb
    out_ref[...] = (h_scr[0] + h_scr[1]) * _gelu_tanh(gate_ref[...].astype(F32))


def _rglru_call(h, conv_w, conv_b, wg, bg, lam, h0, nseq, seq, row_block0):
    has_h0 = h0 is not None
    full = lambda shape: pl.BlockSpec(shape, lambda b: (0,) * len(shape))
    in_specs = [pl.BlockSpec((seq, RG_WIDTH), lambda b: (row_block0 + b, 3)),
                pl.BlockSpec((seq, RG_WIDTH), lambda b: (row_block0 + b, 4)),
                full((RG_CONV_W, RG_WIDTH)), full((1, RG_WIDTH)),
                full((RG_WIDTH, 4 * RG_WIDTH)), full((1, 4 * RG_WIDTH)), full((2, RG_WIDTH))]
    args = [h, h, conv_w, conv_b.reshape(1, RG_WIDTH), wg, bg, lam]
    if has_h0:
        in_specs.append(pl.BlockSpec((None, 2, RG_WIDTH), lambda b: (b, 0, 0)))
        args.append(h0)
    return pl.pallas_call(
        functools.partial(_rg_kernel, seq=seq, has_h0=has_h0),
        grid=(nseq,),
        in_specs=in_specs,
        out_specs=[pl.BlockSpec((seq, RG_WIDTH), lambda b: (b, 0)),
                   pl.BlockSpec((None, 2, RG_WIDTH), lambda b: (b, 0, 0))],
        out_shape=[jax.ShapeDtypeStruct((nseq * seq, RG_WIDTH), F32),
                   jax.ShapeDtypeStruct((nseq, 2, RG_WIDTH), F32)],
        scratch_shapes=[pltpu.VMEM((2, seq, RG_WIDTH), F32)] * 3,
        compiler_params=_cparams(("arbitrary",)),
        name="rglru_lat" if has_h0 else "rglru_ctx",
    )(*args)


def _rg_gate_dense(gate_w, gate_b):
    eye = jnp.eye(RG_BLOCKS, dtype=F32)
    dense = jnp.einsum('dgncf,nm->dgncmf', gate_w, eye).reshape(2, 2, RG_WIDTH, RG_WIDTH)
    wg = jnp.transpose(dense, (2, 0, 1, 3)).reshape(RG_WIDTH, 4 * RG_WIDTH)
    return wg.astype(BF16), gate_b.reshape(1, 4 * RG_WIDTH)


def _hg_masks():
    c = HG_CHUNK
    t = np.arange(c)[:, None]
    s = np.arange(c)[None, :]
    lvl = np.zeros((2, len(HG_LEVELS), c, c), np.float32)
    for n, h in enumerate(HG_LEVELS):
        same = (t // (2 * h)) == (s // (2 * h))
        lvl[0, n] = same & ((t // h) % 2 == 1) & ((s // h) % 2 == 0)
        lvl[1, n] = same & ((t // h) % 2 == 0) & ((s // h) % 2 == 1)
    return jnp.asarray(lvl)


def _hg_chunk(q_raw, z, v, lb, st, lvl, rev):
    c = HG_CHUNK
    q = q_raw * _sigmoid(q_raw)
    sig = _sigmoid(z)
    f = lb + (1.0 - lb) * sig
    kk = (1.0 - lb) * (1.0 - sig)

    rowk = lax.broadcasted_iota(jnp.int32, (c, HG_KEY), 0)
    pos = rowk if not rev else (c - 1) - rowk
    earlier = lambda x, d: pltpu.roll(x, d if not rev else c - d, 0)
    later = lambda x, d: pltpu.roll(x, c - d if not rev else d, 0)

    def block_row(x, n, p):
        r = p if not rev else n - 1 - p
        x3 = x.reshape(c // n, n, HG_KEY)
        return jnp.broadcast_to(x3[:, r:r + 1, :], x3.shape).reshape(c, HG_KEY)

    pair = f * earlier(f, 1)
    f_next = later(f, 1)
    p4 = pos & 3
    a = {1: f, 2: jnp.where((pos & 1) == 1, pair, f)}
    b = {2: jnp.where((pos & 1) == 0, f_next, 1.0)}
    a[4] = a[2] * jnp.where(p4 == 2, earlier(pair, 1), jnp.where(p4 == 3, earlier(pair, 2), 1.0))
    b[4] = jnp.where(p4 == 3, 1.0, jnp.where(p4 == 2, f_next,
                                             jnp.where(p4 == 1, later(pair, 2), f_next * later(pair, 3))))
    h = 4
    while h < c:
        in_later_half = (pos & h) != 0
        a[2 * h] = jnp.where(in_later_half, a[h] * block_row(a[h], 2 * h, h - 1), a[h])
        b[2 * h] = jnp.where(in_later_half, b[h], b[h] * block_row(a[h], 2 * h, 2 * h - 1))
        h *= 2

    row = lax.broadcasted_iota(jnp.int32, (c, c), 0)
    col = lax.broadcasted_iota(jnp.int32, (c, c), 1)
    scores = jnp.where(row == col, jnp.sum(q * kk, axis=-1, keepdims=True), 0.0)
    for n, h in enumerate(HG_LEVELS):
        qa = (q * a[h]).astype(BF16)
        kb = (kk if h == 1 else kk * b[h]).astype(BF16)
        scores = scores + lvl[n] * _dot_nt(qa, kb)

    vb = v.astype(BF16)
    o = _dot(scores.astype(BF16), vb)
    o = o + _dot_nt((q * a[c]).astype(BF16), st.astype(BF16))
    total = a[c][c - 1:c, :] if not rev else a[c][0:1, :]
    st_new = st * total + _dot(v.T.astype(BF16), (kk * b[c]).astype(BF16))
    return o, st_new


def _hg_kernel(*refs, has_s0, nchunk):
    if has_s0:
        (qf_ref, qb_ref, zf_ref, zb_ref, vf_ref, vb_ref, lb_ref, lvl_ref, s0_ref,
         of_ref, ob_ref, sfin_ref, st_scr) = refs
    else:
        (qf_ref, qb_ref, zf_ref, zb_ref, vf_ref, vb_ref, lb_ref, lvl_ref,
         of_ref, ob_ref, sfin_ref, st_scr) = refs
        s0_ref = None
    ci = pl.program_id(1)

    @pl.when(ci == 0)
    def _():
        for dr in range(2):
            for hd in range(HG_HEADS):
                st_scr[dr, hd] = s0_ref[dr, hd].T if has_s0 else jnp.zeros((HG_VAL, HG_KEY), F32)

    last = ci == nchunk - 1
    for hd in range(HG_HEADS):
        cols = slice(hd * LANES, (hd + 1) * LANES)
        ld = lambda ref: ref[:, cols].astype(F32)
        o_f, st_f = _hg_chunk(ld(qf_ref), ld(zf_ref), ld(vf_ref), lb_ref[0:1, cols], st_scr[0, hd],
                              lvl_ref[0], False)
        o_b, st_b = _hg_chunk(ld(qb_ref), ld(zb_ref), ld(vb_ref), lb_ref[1:2, cols], st_scr[1, hd],
                              lvl_ref[1], True)
        of_ref[:, cols] = o_f
        ob_ref[:, cols] = o_b
        st_scr[0, hd] = st_f
        st_scr[1, hd] = st_b

        @pl.when(last)
        def _():
            sfin_ref[0, hd] = st_f.T
            sfin_ref[1, hd] = st_b.T


def _hgrn_call(h, lbs_l, s0, l, nseq, seq, row0):
    c = HG_CHUNK
    nchunk = seq // c
    has_s0 = s0 is not None
    lvl = _hg_masks()
    rf = lambda b, ci: row0 // c + b * nchunk + ci
    rb = lambda b, ci: row0 // c + b * nchunk + (nchunk - 1 - ci)
    blk = lambda rfun, cb: pl.BlockSpec((c, BRANCH_W), lambda b, ci: (rfun(b, ci), cb))
    const = lambda shape: pl.BlockSpec(shape, lambda b, ci: (0,) * len(shape))
    in_specs = [blk(rf, 5), blk(rb, 5), blk(rf, 6), blk(rb, 7), blk(rf, 8), blk(rb, 8),
                const((2, BRANCH_W)), const((2, len(HG_LEVELS), c, c))]
    args = [h, h, h, h, h, h, lbs_l, lvl]
    if has_s0:
        in_specs.append(pl.BlockSpec((None, None, 2, HG_HEADS, HG_KEY, HG_VAL), lambda b, ci: (b, l, 0, 0, 0, 0)))
        args.append(s0)
    return pl.pallas_call(
        functools.partial(_hg_kernel, has_s0=has_s0, nchunk=nchunk),
        grid=(nseq, nchunk),
        in_specs=in_specs,
        out_specs=[pl.BlockSpec((c, BRANCH_W), lambda b, ci: (b * nchunk + ci, 0)),
                   pl.BlockSpec((c, BRANCH_W), lambda b, ci: (b * nchunk + (nchunk - 1 - ci), 0)),
                   pl.BlockSpec((None, 2, HG_HEADS, HG_KEY, HG_VAL), lambda b, ci: (b, 0, 0, 0, 0))],
        out_shape=[jax.ShapeDtypeStruct((nseq * seq, BRANCH_W), F32),
                   jax.ShapeDtypeStruct((nseq * seq, BRANCH_W), F32),
                   jax.ShapeDtypeStruct((nseq, 2, HG_HEADS, HG_KEY, HG_VAL), F32)],
        scratch_shapes=[pltpu.VMEM((2, HG_HEADS, HG_VAL, HG_KEY), F32)],
        compiler_params=_cparams(("arbitrary", "arbitrary")),
        name="hgrn_lat" if has_s0 else "hgrn_ctx",
    )(*args)


def _layer_norm(y, g, b):
    mu = jnp.mean(y, axis=-1, keepdims=True)
    yc = y - mu
    var = jnp.mean(yc * yc, axis=-1, keepdims=True)
    return yc * lax.rsqrt(var + NORM_EPS) * g + b


def _merge_kernel(attc_ref, attl_ref, rgc_ref, rgl_ref, ofc_ref, ofl_ref, obc_ref, obl_ref,
                  hgate_ref, mg0_ref, mg1_ref, mg2_ref, x_ref, mod_ref,
                  hgn_ref, wbr_ref, wout_ref, lng_ref, lnb_ref, rw_ref, rb_ref, tri_ref,
                  x1_ref, u2_ref, idx_ref, rank_ref, wts_ref, cnt_ref, wbr_bf, wout_bf, cnt_scr, *, ctx_tiles):
    @pl.when(pl.program_id(0) == 0)
    def _():
        wbr_bf[...] = wbr_ref[...].astype(BF16)
        wout_bf[...] = wout_ref[...].astype(BF16)
        cnt_scr[...] = jnp.zeros(cnt_scr.shape, F32)

    is_ctx = pl.program_id(0) < ctx_tiles
    pick = lambda c_ref, l_ref: jnp.where(is_ctx, c_ref[...], l_ref[...])
    att = pick(attc_ref, attl_ref)
    rg = pick(rgc_ref, rgl_ref)
    o = pick(ofc_ref, ofl_ref) + pick(obc_ref, obl_ref)
    hgate = hgate_ref[...].astype(F32)
    hg_parts = []
    for hd in range(HG_HEADS):
        oh = o[:, hd * HG_VAL:(hd + 1) * HG_VAL]
        gh = hgate[:, hd * HG_VAL:(hd + 1) * HG_VAL]
        yh = oh * lax.rsqrt(jnp.mean(oh * oh, axis=-1, keepdims=True) + NORM_EPS) * hgn_ref[...]
        hg_parts.append(yh * (gh * _sigmoid(gh)))
    hg = jnp.concatenate(hg_parts, axis=-1)

    proj = _sigmoid(mg0_ref[...].astype(F32)) * _dot(att.astype(BF16), wbr_bf[0])
    proj = proj + _sigmoid(mg1_ref[...].astype(F32)) * _dot(rg.astype(BF16), wbr_bf[1])
    proj = proj + _sigmoid(mg2_ref[...].astype(F32)) * _dot(hg.astype(BF16), wbr_bf[2])
    mix = _dot(proj.astype(BF16), wout_bf[...])

    x1 = _layer_norm(DN_ALPHA * x_ref[...] + mod_ref[2:3, :] * mix, lng_ref[...], lnb_ref[...])
    x1_ref[...] = x1
    u2 = x1 * (1.0 + mod_ref[4:5, :]) + mod_ref[3:4, :]
    for j in range(u2_ref.shape[0]):
        u2_ref[j] = u2[:, j * u2_ref.shape[2]:(j + 1) * u2_ref.shape[2]]

    u2_hi = u2.astype(BF16)
    u2_lo = (u2 - u2_hi.astype(F32)).astype(BF16)
    logits = (_dot(u2_hi, rw_ref[0]) + _dot(u2_lo, rw_ref[0]) + _dot(u2_hi, rw_ref[1])) + rb_ref[...]
    lane = lax.broadcasted_iota(jnp.int32, logits.shape, 1).astype(F32)
    idx_out = jnp.zeros(logits.shape, F32)
    wts_out = jnp.zeros(logits.shape, F32)
    chosen = jnp.zeros(logits.shape, F32)
    sels = []
    top0 = None
    den = None
    for k in range(TOP_K):
        m = jnp.max(logits, axis=-1, keepdims=True)
        sel = jnp.min(jnp.where(logits == m, lane, float(LANES)), axis=-1, keepdims=True)
        sels.append(sel)
        if k == 0:
            top0 = m
        e = jnp.exp(m - top0)
        den = e if den is None else den + e
        idx_out = jnp.where(lane == k, sel, idx_out)
        wts_out = jnp.where(lane == k, e, wts_out)
        chosen = jnp.where(lane == sel, 1.0, chosen)
        logits = jnp.where(lane == sel, -jnp.inf, logits)
    idx_ref[...] = idx_out.astype(jnp.int32)
    wts_ref[...] = wts_out * (1.0 / den)

    before = _dot(tri_ref[...], chosen.astype(BF16)) + cnt_scr[...]
    rank_out = jnp.zeros(logits.shape, F32)
    for k in range(TOP_K):
        rk = jnp.sum(jnp.where(lane == sels[k], before, 0.0), axis=-1, keepdims=True)
        rank_out = jnp.where(lane == k, rk, rank_out)
    rank_ref[...] = rank_out.astype(jnp.int32)
    cnt_scr[...] = cnt_scr[...] + jnp.sum(chosen, axis=0, keepdims=True)
    cnt_ref[...] = jnp.broadcast_to(cnt_scr[...], cnt_ref.shape).astype(jnp.int32)


def _merge_call(att, rg, o_f, o_b, h, x, mods, p, l):
    tm = 256
    ctx_tiles = N_CTX // tm
    rowb = lambda w, cb: pl.BlockSpec((tm, w), lambda i: (i, cb))
    ctxb = pl.BlockSpec((tm, BRANCH_W), lambda i: (jnp.minimum(i, ctx_tiles - 1), 0))
    latb = pl.BlockSpec((tm, BRANCH_W), lambda i: (jnp.maximum(i - ctx_tiles, 0), 0))
    const = lambda shape: pl.BlockSpec(shape, lambda i: (0,) * len(shape))
    rw = jnp.zeros((D_MODEL, LANES), F32).at[:, :N_EXPERTS].set(p['router_w'][l])
    rb = jnp.full((1, LANES), -1e30, F32).at[0, :N_EXPERTS].set(p['router_b'][l])
    rw_hi = rw.astype(BF16)
    rw = jnp.stack([rw_hi, (rw - rw_hi.astype(F32)).astype(BF16)])
    tri = jnp.asarray(np.tril(np.ones((tm, tm), np.float32), -1), BF16)
    return pl.pallas_call(
        functools.partial(_merge_kernel, ctx_tiles=ctx_tiles),
        grid=(N_TOK // tm,),
        in_specs=[ctxb, latb, ctxb, latb, ctxb, latb, ctxb, latb,
                  rowb(BRANCH_W, 9), rowb(D_MODEL, 5), rowb(D_MODEL, 6), rowb(D_MODEL, 7),
                  rowb(D_MODEL, 0),
                  pl.BlockSpec((None, None, 6, D_MODEL), lambda i: (l, _cond_of_row_tile(i, tm), 0, 0)),
                  const((1, HG_VAL)), const((3, BRANCH_W, D_MODEL)), const((D_MODEL, D_MODEL)),
                  const((1, D_MODEL)), const((1, D_MODEL)), const((2, D_MODEL, LANES)), const((1, LANES)),
                  const((tm, tm))],
        out_specs=[rowb(D_MODEL, 0), pl.BlockSpec((SC_SLABS, tm, SC_SLAB_W), lambda i: (0, i, 0)),
                   rowb(LANES, 0), rowb(LANES, 0), rowb(LANES, 0), const((8, LANES))],
        out_shape=[jax.ShapeDtypeStruct((N_TOK, D_MODEL), F32),
                   jax.ShapeDtypeStruct((SC_SLABS, N_TOK, SC_SLAB_W), F32),
                   jax.ShapeDtypeStruct((N_TOK, LANES), jnp.int32),
                   jax.ShapeDtypeStruct((N_TOK, LANES), jnp.int32),
                   jax.ShapeDtypeStruct((N_TOK, LANES), F32),
                   jax.ShapeDtypeStruct((8, LANES), jnp.int32)],
        scratch_shapes=[pltpu.VMEM((3, BRANCH_W, D_MODEL), BF16), pltpu.VMEM((D_MODEL, D_MODEL), BF16),
                        pltpu.VMEM((1, LANES), F32)],
        compiler_params=_cparams(("arbitrary",)),
        name="merge",
    )(att[0], att[1], rg[0], rg[1], o_f[0], o_f[1], o_b[0], o_b[1], h, h, h, h, x, mods,
      p['hg_norm'][l].reshape(1, HG_VAL), p['w_branch'][l], p['w_out'][l],
      p['ln1_g'][l].reshape(1, D_MODEL), p['ln1_b'][l].reshape(1, D_MODEL), rw, rb, tri)


MOE_NB = 256


def _moe_kernel(te_ref, tf_ref, tsl_ref, tnx_ref, nt_ref, x_ref, w1_hbm, b1_ref, w2_hbm, b2_ref, perm_ref,
                y_ref, w1_f, w2_f, w1_bf, w2_bf, sem, *, layer):
    i = pl.program_id(0)
    half = MOE_NB // 2

    def weight_copies(e, s):
        return (pltpu.make_async_copy(w1_hbm.at[layer, e], w1_f.at[s], sem.at[0, s]),
                pltpu.make_async_copy(w2_hbm.at[layer, e], w2_f.at[s], sem.at[1, s]))

    @pl.when(i == 0)
    def _():
        for cp in weight_copies(te_ref[0], 0):
            cp.start()

    @pl.when(tf_ref[i] == 1)
    def _():
        s = tsl_ref[i]
        for cp in weight_copies(te_ref[i], s):
            cp.wait()
        nxt = tnx_ref[i]

        @pl.when(nxt >= 0)
        def _():
            for cp in weight_copies(nxt, 1 - s):
                cp.start()

        for b in range(2 * D_EXPERT // MOE_NB):
            blk = w1_f[s, :, b * MOE_NB:(b + 1) * MOE_NB].astype(BF16)
            w1_bf[:, b * MOE_NB:(b + 1) * MOE_NB] = _dot(blk, perm_ref[...]).astype(BF16)
        w2_bf[...] = w2_f[s].astype(BF16)

    @pl.when(i < nt_ref[0])
    def _():
        x = jnp.concatenate([x_ref[j].astype(BF16) for j in range(x_ref.shape[0])], axis=-1)
        h = _dot(x, w1_bf[...]) + b1_ref[...]
        acts = []
        for b in range(2 * D_EXPERT // MOE_NB):
            glu = jnp.minimum(h[:, b * MOE_NB:b * MOE_NB + half], SWIGLU_LIMIT)
            lin = jnp.clip(h[:, b * MOE_NB + half:(b + 1) * MOE_NB], -SWIGLU_LIMIT, SWIGLU_LIMIT)
            acts.append((glu * _sigmoid(SWIGLU_ALPHA * glu) * (lin + 1.0)).astype(BF16))
        act = jnp.concatenate(acts, axis=-1)
        y_ref[...] = _dot(act, w2_bf[...]) + b2_ref[...]

    @pl.when(i >= nt_ref[0])
    def _():
        y_ref[...] = jnp.zeros(y_ref.shape, F32)


def _moe_perm():
    half = MOE_NB // 2
    pm = np.zeros((MOE_NB, MOE_NB), np.float32)
    pm[2 * np.arange(half), np.arange(half)] = 1.0
    pm[2 * np.arange(half) + 1, half + np.arange(half)] = 1.0
    return jnp.asarray(pm, BF16)


def _moe_call(x_sorted, sched, w1, b1p, w2, b2, l):
    tm = MOE_TM
    emap = lambda i, te, *_: (l, te[i], 0, 0)
    grid_spec = pltpu.PrefetchScalarGridSpec(
        num_scalar_prefetch=5,
        grid=(MOE_TILES,),
        in_specs=[pl.BlockSpec((SC_SLABS, tm, SC_SLAB_W), lambda i, *_: (0, i, 0)),
                  pl.BlockSpec(memory_space=pl.ANY),
                  pl.BlockSpec((None, None, 1, 2 * D_EXPERT), emap),
                  pl.BlockSpec(memory_space=pl.ANY),
                  pl.BlockSpec((None, None, 1, D_MODEL), emap),
                  pl.BlockSpec((MOE_NB, MOE_NB), lambda i, *_: (0, 0))],
        out_specs=pl.BlockSpec((tm, D_MODEL), lambda i, *_: (i, 0)),
        scratch_shapes=[pltpu.VMEM((2, D_MODEL, 2 * D_EXPERT), F32), pltpu.VMEM((2, D_EXPERT, D_MODEL), F32),
                        pltpu.VMEM((D_MODEL, 2 * D_EXPERT), BF16), pltpu.VMEM((D_EXPERT, D_MODEL), BF16),
                        pltpu.SemaphoreType.DMA((2, 2))],
    )
    return pl.pallas_call(
        functools.partial(_moe_kernel, layer=l),
        grid_spec=grid_spec,
        out_shape=jax.ShapeDtypeStruct((MOE_TILES * tm, D_MODEL), F32),
        compiler_params=_cparams(("arbitrary",)),
        name="moe",
    )(*sched, x_sorted, w1, b1p, w2, b2, _moe_perm())


def _route(idx, rank, counts):
    tm = MOE_TM
    tiles_e = (counts + tm - 1) // tm
    eid = np.arange(N_EXPERTS, dtype=np.int32)
    earlier = (eid[None, :] <= eid[:, None]).astype(np.int32)
    tile_end = jnp.sum(earlier * tiles_e[None, :], axis=1)
    tile_start = tile_end - tiles_e
    lookup = lambda table, keys: jnp.sum(jnp.where(keys[..., None] == eid, table, 0), axis=-1)
    pos_t = lookup(tile_start, idx.T) * tm + rank.T
    n_used = tile_end[N_EXPERTS - 1]
    tile_ids = jnp.arange(MOE_TILES, dtype=jnp.int32)
    tid = jnp.minimum(tile_ids, n_used - 1)
    tile_expert = jnp.sum((tile_end[None, :] <= tid[:, None]).astype(jnp.int32), axis=1)
    tile_first = ((tile_ids == lookup(tile_start, tile_expert)) & (tile_ids < n_used)).astype(jnp.int32)
    has_rows = (tiles_e > 0).astype(jnp.int32)
    slot_e = (jnp.sum(earlier * has_rows[None, :], axis=1) - 1) & 1
    later = jnp.where((eid[None, :] > eid[:, None]) & (has_rows[None, :] > 0), eid[None, :], N_EXPERTS)
    next_e = jnp.min(later, axis=1)
    next_e = jnp.where(next_e >= N_EXPERTS, -1, next_e).astype(jnp.int32)
    sched = (tile_expert, tile_first, lookup(slot_e, tile_expert).astype(jnp.int32),
             lookup(next_e, tile_expert).astype(jnp.int32), n_used.reshape(1))
    return pos_t, sched


SC_WINDOW = 128
SC_SLABS = 4
SC_SLAB_W = D_MODEL // SC_SLABS


def _dispatch_rows(u2, pos_t):
    mesh = plsc.VectorSubcoreMesh(core_axis_name="core", subcore_axis_name="subcore")

    @functools.partial(pl.kernel, mesh=mesh, scratch_types=[],
                       out_type=jax.ShapeDtypeStruct((SC_SLABS, MOE_TILES * MOE_TM, SC_SLAB_W), F32))
    def dispatch(x_hbm, i_hbm, o_hbm):
        for j in range(SC_SLABS):
            def body(x_vmem, i_vmem, j=j):
                for k in range(TOP_K):
                    pltpu.sync_copy(x_vmem, o_hbm.at[j].at[i_vmem.at[k]])

            pltpu.emit_pipeline(
                body,
                grid=(N_TOK // SC_WINDOW,),
                in_specs=[pl.BlockSpec((SC_WINDOW, SC_SLAB_W), index_map=lambda i: (i, 0)),
                          pl.BlockSpec((TOP_K, SC_WINDOW), index_map=lambda i: (0, i))],
                out_specs=[],
                core_axis_name=("core", "subcore"),
                dimension_semantics=(pltpu.PARALLEL,),
            )(x_hbm.at[j], i_hbm)

    return dispatch(u2, pos_t)


def _final_kernel(yg_ref, wts_ref, x1_ref, mod_ref, modn_ref, lng_ref, lnb_ref, o_ref, un_ref):
    wts = wts_ref[...]
    ffn = wts[:, 0:1] * yg_ref[0]
    for k in range(1, TOP_K):
        ffn = ffn + wts[:, k:k + 1] * yg_ref[k]
    x2 = _layer_norm(DN_ALPHA * x1_ref[...] + mod_ref[5:6, :] * ffn, lng_ref[...], lnb_ref[...])
    o_ref[...] = x2
    un_ref[...] = (x2 * (1.0 + modn_ref[1:2, :]) + modn_ref[0:1, :]).astype(BF16)


def _final_call(yg, wts, x1, mods, ln_g, ln_b, l):
    tm = 256
    ln = min(l + 1, DEPTH - 1)
    const = lambda shape: pl.BlockSpec(shape, lambda i: (0,) * len(shape))
    modspec = lambda lyr: pl.BlockSpec((None, None, 6, D_MODEL), lambda i: (lyr, _cond_of_row_tile(i, tm), 0, 0))
    return pl.pallas_call(
        _final_kernel,
        grid=(N_TOK // tm,),
        in_specs=[pl.BlockSpec((TOP_K, tm, D_MODEL), lambda i: (0, i, 0)),
                  pl.BlockSpec((tm, LANES), lambda i: (i, 0)),
                  pl.BlockSpec((tm, D_MODEL), lambda i: (i, 0)),
                  modspec(l), modspec(ln),
                  const((1, D_MODEL)), const((1, D_MODEL))],
        out_specs=[pl.BlockSpec((tm, D_MODEL), lambda i: (i, 0)), pl.BlockSpec((tm, D_MODEL), lambda i: (i, 0))],
        out_shape=[jax.ShapeDtypeStruct((N_TOK, D_MODEL), F32), jax.ShapeDtypeStruct((N_TOK, D_MODEL), BF16)],
        compiler_params=_cparams(("arbitrary",)),
        name="final",
    )(yg, wts, x1, mods, mods, ln_g.reshape(1, D_MODEL), ln_b.reshape(1, D_MODEL))


def kernel(x_prompt, x_sample, cache_attn_k, cache_attn_v, state_rglru, state_hgrn, c, c_ctx, w_ada, b_ada, w_in, da_lambda, da_subln, rg_conv_w, rg_conv_b, rg_gate_w, rg_gate_b, rg_lambda, hg_lb, hg_norm, w_branch, w_out, ln1_g, ln1_b, router_w, router_b, w1, b1, w2, b2, ln2_g, ln2_b):
    p = dict(hg_norm=hg_norm, w_branch=w_branch, w_out=w_out, ln1_g=ln1_g, ln1_b=ln1_b,
             router_w=router_w, router_b=router_b)

    x = jnp.concatenate([x_prompt.reshape(N_CTX, D_MODEL), x_sample.reshape(N_LAT, D_MODEL)], axis=0)
    cond = jnp.concatenate([c_ctx[None, :], c, jnp.zeros((N_COND - 1 - DEC_BATCH, D_MODEL), F32)], axis=0)
    mods = _adaln_all(cond, w_ada, b_ada).reshape(DEPTH, N_COND, 6, D_MODEL)

    pr = jax.nn.softmax(hg_lb.astype(F32), axis=0)
    lbs = jnp.cumsum(pr, axis=0) - pr[0]
    dl = da_lambda.astype(F32)
    lam_all = jnp.exp(jnp.sum(dl[:, 0] * dl[:, 1], -1)) - jnp.exp(jnp.sum(dl[:, 2] * dl[:, 3], -1))

    cache_k = cache_attn_k.reshape(DEC_BATCH, DEPTH, PAST_LEN, BRANCH_W)
    cache_v = cache_attn_v.reshape(DEC_BATCH, DEPTH, PAST_LEN, BRANCH_W)

    b1p = b1.reshape(DEPTH, N_EXPERTS, 2 * D_EXPERT // MOE_NB, MOE_NB // 2, 2)
    b1p = jnp.swapaxes(b1p, -1, -2).reshape(DEPTH, N_EXPERTS, 1, 2 * D_EXPERT)
    b2r = b2.reshape(DEPTH, N_EXPERTS, 1, D_MODEL)

    ks, vs, rgs, hgs = [], [], [], []
    u = _modulate(x, mods, 0)
    for l in range(DEPTH):
        lambda_init = 0.8 - 0.6 * math.exp(-0.3 * l)
        h = _in_proj(u, w_in, l)
        ks.append(h[:N_CTX, BRANCH_W:2 * BRANCH_W].astype(F32).reshape(BATCH, SEQ, DA_HEADS, 2, DA_HEAD_DIM))
        vs.append(h[:N_CTX, 2 * BRANCH_W:3 * BRANCH_W].astype(F32).reshape(BATCH, SEQ, DA_HEADS, DA_V_DIM))

        att = _attention(h, cache_k, cache_v, lam_all[l] + lambda_init, da_subln[l], l, 1.0 - lambda_init)

        wg, bg = _rg_gate_dense(rg_gate_w[l], rg_gate_b[l])
        rg_c, hl_c = _rglru_call(h, rg_conv_w[l], rg_conv_b[l], wg, bg, rg_lambda[l], None, BATCH, SEQ, 0)
        rg_l, _ = _rglru_call(h, rg_conv_w[l], rg_conv_b[l], wg, bg, rg_lambda[l], state_rglru[:, l],
                              DEC_BATCH, DEC_SEQ, N_CTX // DEC_SEQ)
        rgs.append(hl_c)

        of_c, ob_c, sl_c = _hgrn_call(h, lbs[l], None, l, BATCH, SEQ, 0)
        of_l, ob_l, _ = _hgrn_call(h, lbs[l], state_hgrn, l, DEC_BATCH, DEC_SEQ, N_CTX)
        hgs.append(sl_c)

        x1, u2, idx, rank, wts, cnt = _merge_call(att, (rg_c, rg_l), (of_c, of_l), (ob_c, ob_l), h, x, mods, p, l)

        pos_t, sched = _route(idx[:, :TOP_K], rank[:, :TOP_K], cnt[0, :N_EXPERTS])
        x_sorted = _dispatch_rows(u2, pos_t)
        y_sorted = _moe_call(x_sorted, sched, w1, b1p, w2, b2r, l)
        yg = y_sorted.at[pos_t.reshape(-1)].get(mode='promise_in_bounds').reshape(TOP_K, N_TOK, D_MODEL)
        x, u = _final_call(yg, wts, x1, mods, ln2_g[l], ln2_b[l], l)

    y_prompt = x[:N_CTX].reshape(BATCH, SEQ, D_MODEL)
    y_sample = x[N_CTX:].reshape(DEC_BATCH, DEC_SEQ, D_MODEL)
    return (y_prompt, y_sample, jnp.stack(ks, axis=1), jnp.stack(vs, axis=1),
            jnp.stack(rgs, axis=1), jnp.stack(hgs, axis=1))
```

```python
import functools
import math

import numpy as np
import jax
import jax.numpy as jnp
from jax import lax
from jax.experimental import pallas as pl
from jax.experimental.pallas import tpu as pltpu
from jax.experimental.pallas import tpu_sc as plsc

F32 = jnp.float32
BF16 = jnp.bfloat16
HIGHEST = lax.Precision.HIGHEST

D_MODEL = 1024
BATCH = 16
SEQ = 256
DEPTH = 4
DEC_BATCH = 4
DEC_SEQ = 1024
PAST_LEN = 256
GRID_W = 64
BRANCH_W = 512
DA_HEADS = 4
DA_HEAD_DIM = 64
DA_V_DIM = 128
ROPE_BASE = 10000.0
RG_WIDTH = 512
RG_BLOCKS = 8
RG_BLOCK_W = 64
RG_CONV_W = 4
RG_C = 8.0
HG_HEADS = 4
HG_KEY = 128
HG_VAL = 128
N_EXPERTS = 32
TOP_K = 4
D_EXPERT = 1024
SWIGLU_ALPHA = 1.702
SWIGLU_LIMIT = 7.0
DN_ALPHA = (2 * DEPTH) ** 0.25
NORM_EPS = 1e-5
D_IN = 10 * BRANCH_W + 3 * D_MODEL

N_CTX = BATCH * SEQ
N_LAT = DEC_BATCH * DEC_SEQ
N_TOK = N_CTX + N_LAT
N_COND = 8

LANES = 128
VMEM_LIMIT = 56 * 1024 * 1024

HG_CHUNK = 128
HG_LEVELS = (1, 2, 4, 8, 16, 32, 64)
MOE_TM = 256
MOE_TILES = (N_TOK * TOP_K) // MOE_TM + N_EXPERTS


def _cparams(sem):
    return pltpu.CompilerParams(dimension_semantics=sem, vmem_limit_bytes=VMEM_LIMIT)


def _sigmoid(x):
    return 0.5 * jnp.tanh(0.5 * x) + 0.5


def _pack_bf16_pair(lo, hi):
    lo_bits = pltpu.bitcast(lo.astype(BF16).astype(F32), jnp.uint32) >> 16
    hi_bits = pltpu.bitcast(hi.astype(BF16).astype(F32), jnp.uint32) & jnp.uint32(0xFFFF0000)
    return pltpu.bitcast(hi_bits | lo_bits, F32)


def _unpack_bf16_pair(words):
    bits = pltpu.bitcast(words, jnp.uint32)
    return pltpu.bitcast(bits << 16, F32), pltpu.bitcast(bits & jnp.uint32(0xFFFF0000), F32)


def _dot(a, b):
    return jnp.dot(a, b, preferred_element_type=F32)


def _dot_nt(a, b):
    return lax.dot_general(a, b, (((1,), (1,)), ((), ())), preferred_element_type=F32)


def _cond_of_row_tile(i, tm):
    r = i * tm
    return jnp.where(r < N_CTX, 0, 1 + (r - N_CTX) // DEC_SEQ)


def _ada_kernel(c_ref, w_ref, b_ref, o_ref):
    c = c_ref[...]
    s = c * _sigmoid(c)
    o_ref[0] = jnp.dot(s, w_ref[0], precision=HIGHEST, preferred_element_type=F32) + b_ref[0]


def _adaln_all(cond, w_ada, b_ada):
    tn = 1536
    return pl.pallas_call(
        _ada_kernel,
        grid=(DEPTH, 6 * D_MODEL // tn),
        in_specs=[pl.BlockSpec((N_COND, D_MODEL), lambda l, j: (0, 0)),
                  pl.BlockSpec((1, D_MODEL, tn), lambda l, j: (l, 0, j)),
                  pl.BlockSpec((1, 1, tn), lambda l, j: (l, 0, j))],
        out_specs=pl.BlockSpec((1, N_COND, tn), lambda l, j: (l, 0, j)),
        out_shape=jax.ShapeDtypeStruct((DEPTH, N_COND, 6 * D_MODEL), F32),
        compiler_params=_cparams(("arbitrary", "arbitrary")),
        name="adaln",
    )(cond, w_ada, b_ada.reshape(DEPTH, 1, 6 * D_MODEL))


def _modulate_kernel(x_ref, mod_ref, u_ref):
    u_ref[...] = (x_ref[...] * (1.0 + mod_ref[1:2, :]) + mod_ref[0:1, :]).astype(BF16)


def _modulate(x, mods, l):
    tm = 1024
    return pl.pallas_call(
        _modulate_kernel,
        grid=(N_TOK // tm,),
        in_specs=[pl.BlockSpec((tm, D_MODEL), lambda i: (i, 0)),
                  pl.BlockSpec((None, None, 6, D_MODEL), lambda i: (l, _cond_of_row_tile(i, tm), 0, 0))],
        out_specs=pl.BlockSpec((tm, D_MODEL), lambda i: (i, 0)),
        out_shape=jax.ShapeDtypeStruct((N_TOK, D_MODEL), BF16),
        compiler_params=_cparams(("arbitrary",)),
        name="modulate",
    )(x, mods)


def _in_kernel(u_ref, w_ref, o_ref, wbf_ref):
    @pl.when(pl.program_id(1) == 0)
    def _():
        wbf_ref[...] = w_ref[...].astype(BF16)

    o_ref[...] = _dot(u_ref[...], wbf_ref[...]).astype(BF16)


def _in_proj(u, w_in, l):
    tm, tn = 1024, 1024
    return pl.pallas_call(
        _in_kernel,
        grid=(D_IN // tn, N_TOK // tm),
        in_specs=[pl.BlockSpec((tm, D_MODEL), lambda j, i: (i, 0)),
                  pl.BlockSpec((None, D_MODEL, tn), lambda j, i: (l, 0, j))],
        out_specs=pl.BlockSpec((tm, tn), lambda j, i: (i, j)),
        out_shape=jax.ShapeDtypeStruct((N_TOK, D_IN), BF16),
        scratch_shapes=[pltpu.VMEM((D_MODEL, tn), BF16)],
        compiler_params=_cparams(("arbitrary", "arbitrary")),
        name="in_proj",
    )(u, w_in)


def _rope(x, cos, sin_signed):
    lane = lax.broadcasted_iota(jnp.int32, x.shape, 1)
    first = (lane & 31) < 16
    partner = jnp.where(first, pltpu.roll(x, LANES - 16, 1), pltpu.roll(x, 16, 1))
    return x * cos + partner * sin_signed


LOG2E = 1.4426950408889634


def _attn_body(q, keys, vals, lam, subln, out_scale):
    lane = lax.broadcasted_iota(jnp.int32, q.shape, 1)
    qs = q * (DA_HEAD_DIM ** -0.5 * LOG2E)
    acc = None
    for m in range(2):
        in_map = (lane < DA_HEAD_DIM) if m == 0 else (lane >= DA_HEAD_DIM)
        qm = jnp.where(in_map, qs, 0.0).astype(BF16)
        s = [_dot_nt(qm, k) for k in keys]
        mx = s[0].max(axis=-1, keepdims=True)
        for si in s[1:]:
            mx = jnp.maximum(mx, si.max(axis=-1, keepdims=True))
        e = [jnp.exp2(si - mx) for si in s]
        den = e[0].sum(axis=-1, keepdims=True)
        for ei in e[1:]:
            den = den + ei.sum(axis=-1, keepdims=True)
        pv = _dot(e[0].astype(BF16), vals[0])
        for ei, v in zip(e[1:], vals[1:]):
            pv = pv + _dot(ei.astype(BF16), v)
        coef = (1.0 / den) if m == 0 else (-lam[:, 0:1] / den)
        acc = pv * coef if acc is None else acc + pv * coef
    y = acc * lax.rsqrt(jnp.mean(acc * acc, axis=-1, keepdims=True) + NORM_EPS)
    return y * subln * out_scale


def _attn_ctx_kernel(lam_ref, sub_ref, q_ref, k_ref, v_ref, o_ref, *, out_scale):
    for hd in range(DA_HEADS):
        cols = slice(hd * LANES, (hd + 1) * LANES)
        o_ref[:, cols] = _attn_body(q_ref[:, cols].astype(F32), [k_ref[:, cols]], [v_ref[:, cols]],
                                    lam_ref[...], sub_ref[...], out_scale)


def _attn_lat_kernel(lam_ref, sub_ref, q_ref, k_ref, v_ref, kc_ref, vc_ref, cos_ref, sin_ref,
                     o_ref, k_scr, kc_scr, vc_scr, *, out_scale, tq):
    k_scr[...] = _rope(k_ref[...].astype(F32), cos_ref[...], sin_ref[...]).astype(BF16)
    kc_scr[...] = kc_ref[...].astype(BF16)
    vc_scr[...] = vc_ref[...].astype(BF16)

    def q_block(qi, carry):
        rows = pl.ds(pl.multiple_of(qi * tq, tq), tq)
        q = _rope(q_ref[rows, :].astype(F32), cos_ref[rows, :], sin_ref[rows, :])
        o_ref[rows, :] = _attn_body(q, [k_scr[...], kc_scr[...]], [v_ref[...], vc_scr[...]],
                                    lam_ref[...], sub_ref[...], out_scale)
        return carry

    lax.fori_loop(0, DEC_SEQ // tq, q_block, 0)


def _rope_tables():
    t = np.arange(DEC_SEQ)
    row = (t // GRID_W).astype(np.float64)
    col = (t % GRID_W).astype(np.float64)
    d = DA_HEAD_DIM // 2
    inv = ROPE_BASE ** (-np.arange(0, d, 2, dtype=np.float64) / d)
    a_row = row[:, None] * inv[None, :]
    a_col = col[:, None] * inv[None, :]
    cos = np.concatenate([np.cos(a_row), np.cos(a_row), np.cos(a_col), np.cos(a_col)], -1)
    sin = np.concatenate([-np.sin(a_row), np.sin(a_row), -np.sin(a_col), np.sin(a_col)], -1)
    return (jnp.asarray(np.tile(cos, (1, 2)), F32), jnp.asarray(np.tile(sin, (1, 2)), F32))


def _attention(h, cache_k, cache_v, lam, subln, l, out_scale):
    lam_v = jnp.full((1, LANES), lam, F32)
    sub_v = subln.reshape(1, DA_V_DIM)
    small = pl.BlockSpec((1, LANES), lambda *a: (0, 0))
    att_ctx = pl.pallas_call(
        functools.partial(_attn_ctx_kernel, out_scale=out_scale),
        grid=(BATCH,),
        in_specs=[small, small,
                  pl.BlockSpec((SEQ, BRANCH_W), lambda b: (b, 0)),
                  pl.BlockSpec((SEQ, BRANCH_W), lambda b: (b, 1)),
                  pl.BlockSpec((SEQ, BRANCH_W), lambda b: (b, 2))],
        out_specs=pl.BlockSpec((SEQ, BRANCH_W), lambda b: (b, 0)),
        out_shape=jax.ShapeDtypeStruct((N_CTX, BRANCH_W), F32),
        compiler_params=_cparams(("arbitrary",)),
        name="attn_ctx",
    )(lam_v, sub_v, h, h, h)

    tq = 256
    cos, sin = _rope_tables()
    row0 = N_CTX // DEC_SEQ
    seq_blk = lambda cb: pl.BlockSpec((DEC_SEQ, LANES), lambda b, hd: (row0 + b, cb + hd))
    cache_blk = pl.BlockSpec((None, None, PAST_LEN, LANES), lambda b, hd: (b, l, 0, hd))
    table = pl.BlockSpec((DEC_SEQ, LANES), lambda b, hd: (0, 0))
    att_lat = pl.pallas_call(
        functools.partial(_attn_lat_kernel, out_scale=out_scale, tq=tq),
        grid=(DEC_BATCH, DA_HEADS),
        in_specs=[small, small, seq_blk(0), seq_blk(4), seq_blk(8), cache_blk, cache_blk, table, table],
        out_specs=pl.BlockSpec((DEC_SEQ, LANES), lambda b, hd: (b, hd)),
        out_shape=jax.ShapeDtypeStruct((N_LAT, BRANCH_W), F32),
        scratch_shapes=[pltpu.VMEM((DEC_SEQ, LANES), BF16), pltpu.VMEM((PAST_LEN, LANES), BF16),
                        pltpu.VMEM((PAST_LEN, LANES), BF16)],
        compiler_params=_cparams(("arbitrary", "arbitrary")),
        name="attn_lat",
    )(lam_v, sub_v, h, h, h, cache_k, cache_v, cos, sin)
    return att_ctx, att_lat


def _gelu_tanh(x):
    return 0.5 * x * (1.0 + jnp.tanh(math.sqrt(2.0 / math.pi) * (x + 0.044715 * (x * x * x))))


def _rg_kernel(*refs, seq, has_h0):
    if has_h0:
        rx_ref, gate_ref, cw_ref, cb_ref, wg_ref, bg_ref, lam_ref, h0_ref = refs[:8]
        rest = refs[8:]
    else:
        rx_ref, gate_ref, cw_ref, cb_ref, wg_ref, bg_ref, lam_ref = refs[:7]
        h0_ref = None
        rest = refs[7:]
    out_ref, hl_ref, a_scr, b_scr, h_scr = rest

    x = rx_ref[...].astype(F32)
    row = lax.broadcasted_iota(jnp.int32, x.shape, 0)
    xr = cb_ref[...] + cw_ref[2:3, :] * x
    for j in (0, 1, 3):
        d = j - RG_CONV_W // 2
        shifted = pltpu.roll(x, (-d) % seq, 0)
        valid = (row + d >= 0) & (row + d < seq)
        xr = xr + cw_ref[j:j + 1, :] * jnp.where(valid, shifted, 0.0)

    g = _dot(xr.astype(BF16), wg_ref[...]) + bg_ref[...]
    for dr in range(2):
        r = _sigmoid(g[:, (2 * dr) * RG_WIDTH:(2 * dr + 1) * RG_WIDTH])
        i = _sigmoid(g[:, (2 * dr + 1) * RG_WIDTH:(2 * dr + 2) * RG_WIDTH])
        lam = lam_ref[dr:dr + 1, :]
        softplus_neg = jnp.maximum(-lam, 0.0) + jnp.log(1.0 + jnp.exp(-jnp.abs(lam)))
        log_a = (-RG_C * softplus_neg) * r
        a = jnp.exp(log_a)
        a_scr[dr] = a
        b_scr[dr] = jnp.sqrt(1.0 - a * a) * i * xr

    if has_h0:
        hf0, hb0 = h0_ref[0:1, :], h0_ref[1:2, :]
    else:
        hf0 = hb0 = jnp.zeros((1, RG_WIDTH), F32)

    def step(t, carry):
        hf, hb = carry
        tb = seq - 1 - t
        hf = a_scr[0, pl.ds(t, 1), :] * hf + b_scr[0, pl.ds(t, 1), :]
        h_scr[0, pl.ds(t, 1), :] = hf
        hb = a_scr[1, pl.ds(tb, 1), :] * hb + b_scr[1, pl.ds(tb, 1), :]
        h_scr[1, pl.ds(tb, 1), :] = hb
        return hf, hb

    hf, hb = lax.fori_loop(0, seq, step, (hf0, hb0), unroll=8)
    hl_ref[0:1, :] = hf
    hl_ref[1:2, :] = hb
    out_ref[...] = (h_scr[0] + h_scr[1]) * _gelu_tanh(gate_ref[...].astype(F32))


def _rglru_call(h, conv_w, conv_b, wg, bg, lam, h0, nseq, seq, row_block0):
    has_h0 = h0 is not None
    full = lambda shape: pl.BlockSpec(shape, lambda b: (0,) * len(shape))
    in_specs = [pl.BlockSpec((seq, RG_WIDTH), lambda b: (row_block0 + b, 3)),
                pl.BlockSpec((seq, RG_WIDTH), lambda b: (row_block0 + b, 4)),
                full((RG_CONV_W, RG_WIDTH)), full((1, RG_WIDTH)),
                full((RG_WIDTH, 4 * RG_WIDTH)), full((1, 4 * RG_WIDTH)), full((2, RG_WIDTH))]
    args = [h, h, conv_w, conv_b.reshape(1, RG_WIDTH), wg, bg, lam]
    if has_h0:
        in_specs.append(pl.BlockSpec((None, 2, RG_WIDTH), lambda b: (b, 0, 0)))
        args.append(h0)
    return pl.pallas_call(
        functools.partial(_rg_kernel, seq=seq, has_h0=has_h0),
        grid=(nseq,),
        in_specs=in_specs,
        out_specs=[pl.BlockSpec((seq, RG_WIDTH), lambda b: (b, 0)),
                   pl.BlockSpec((None, 2, RG_WIDTH), lambda b: (b, 0, 0))],
        out_shape=[jax.ShapeDtypeStruct((nseq * seq, RG_WIDTH), F32),
                   jax.ShapeDtypeStruct((nseq, 2, RG_WIDTH), F32)],
        scratch_shapes=[pltpu.VMEM((2, seq, RG_WIDTH), F32)] * 3,
        compiler_params=_cparams(("arbitrary",)),
        name="rglru_lat" if has_h0 else "rglru_ctx",
    )(*args)


def _rg_gate_dense(gate_w, gate_b):
    eye = jnp.eye(RG_BLOCKS, dtype=F32)
    dense = jnp.einsum('dgncf,nm->dgncmf', gate_w, eye).reshape(2, 2, RG_WIDTH, RG_WIDTH)
    wg = jnp.transpose(dense, (2, 0, 1, 3)).reshape(RG_WIDTH, 4 * RG_WIDTH)
    return wg.astype(BF16), gate_b.reshape(1, 4 * RG_WIDTH)


def _hg_masks():
    c = HG_CHUNK
    t = np.arange(c)[:, None]
    s = np.arange(c)[None, :]
    lvl = np.zeros((2, len(HG_LEVELS), c, c), np.float32)
    for n, h in enumerate(HG_LEVELS):
        same = (t // (2 * h)) == (s // (2 * h))
        lvl[0, n] = same & ((t // h) % 2 == 1) & ((s // h) % 2 == 0)
        lvl[1, n] = same & ((t // h) % 2 == 0) & ((s // h) % 2 == 1)
    return jnp.asarray(lvl)


def _hg_chunk(q_raw, z, v, lb, st, lvl, rev):
    c = HG_CHUNK
    q = q_raw * _sigmoid(q_raw)
    sig = _sigmoid(z)
    f = lb + (1.0 - lb) * sig
    kk = (1.0 - lb) * (1.0 - sig)

    rowk = lax.broadcasted_iota(jnp.int32, (c, HG_KEY), 0)
    pos = rowk if not rev else (c - 1) - rowk
    earlier = lambda x, d: pltpu.roll(x, d if not rev else c - d, 0)
    later = lambda x, d: pltpu.roll(x, c - d if not rev else d, 0)

    def block_row(x, n, p):
        r = p if not rev else n - 1 - p
        x3 = x.reshape(c // n, n, HG_KEY)
        return jnp.broadcast_to(x3[:, r:r + 1, :], x3.shape).reshape(c, HG_KEY)

    pair = f * earlier(f, 1)
    f_next = later(f, 1)
    p4 = pos & 3
    a = {1: f, 2: jnp.where((pos & 1) == 1, pair, f)}
    b = {2: jnp.where((pos & 1) == 0, f_next, 1.0)}
    a[4] = a[2] * jnp.where(p4 == 2, earlier(pair, 1), jnp.where(p4 == 3, earlier(pair, 2), 1.0))
    b[4] = jnp.where(p4 == 3, 1.0, jnp.where(p4 == 2, f_next,
                                             jnp.where(p4 == 1, later(pair, 2), f_next * later(pair, 3))))
    h = 4
    while h < c:
        in_later_half = (pos & h) != 0
        a[2 * h] = jnp.where(in_later_half, a[h] * block_row(a[h], 2 * h, h - 1), a[h])
        b[2 * h] = jnp.where(in_later_half, b[h], b[h] * block_row(a[h], 2 * h, 2 * h - 1))
        h *= 2

    row = lax.broadcasted_iota(jnp.int32, (c, c), 0)
    col = lax.broadcasted_iota(jnp.int32, (c, c), 1)
    scores = jnp.where(row == col, jnp.sum(q * kk, axis=-1, keepdims=True), 0.0)
    for n, h in enumerate(HG_LEVELS):
        qa = (q * a[h]).astype(BF16)
        kb = (kk if h == 1 else kk * b[h]).astype(BF16)
        scores = scores + lvl[n] * _dot_nt(qa, kb)

    vb = v.astype(BF16)
    o = _dot(scores.astype(BF16), vb)
    o = o + _dot_nt((q * a[c]).astype(BF16), st.astype(BF16))
    total = a[c][c - 1:c, :] if not rev else a[c][0:1, :]
    st_new = st * total + _dot(v.T.astype(BF16), (kk * b[c]).astype(BF16))
    return o, st_new


def _hg_kernel(*refs, has_s0, nchunk):
    if has_s0:
        (qf_ref, qb_ref, zf_ref, zb_ref, vf_ref, vb_ref, lb_ref, lvl_ref, s0_ref,
         of_ref, ob_ref, sfin_ref, st_scr) = refs
    else:
        (qf_ref, qb_ref, zf_ref, zb_ref, vf_ref, vb_ref, lb_ref, lvl_ref,
         of_ref, ob_ref, sfin_ref, st_scr) = refs
        s0_ref = None
    ci = pl.program_id(1)

    @pl.when(ci == 0)
    def _():
        for dr in range(2):
            for hd in range(HG_HEADS):
                st_scr[dr, hd] = s0_ref[dr, hd].T if has_s0 else jnp.zeros((HG_VAL, HG_KEY), F32)

    last = ci == nchunk - 1
    for hd in range(HG_HEADS):
        cols = slice(hd * LANES, (hd + 1) * LANES)
        ld = lambda ref: ref[:, cols].astype(F32)
        o_f, st_f = _hg_chunk(ld(qf_ref), ld(zf_ref), ld(vf_ref), lb_ref[0:1, cols], st_scr[0, hd],
                              lvl_ref[0], False)
        o_b, st_b = _hg_chunk(ld(qb_ref), ld(zb_ref), ld(vb_ref), lb_ref[1:2, cols], st_scr[1, hd],
                              lvl_ref[1], True)
        of_ref[:, cols] = o_f
        ob_ref[:, cols] = o_b
        st_scr[0, hd] = st_f
        st_scr[1, hd] = st_b

        @pl.when(last)
        def _():
            sfin_ref[0, hd] = st_f.T
            sfin_ref[1, hd] = st_b.T


def _hgrn_call(h, lbs_l, s0, l, nseq, seq, row0):
    c = HG_CHUNK
    nchunk = seq // c
    has_s0 = s0 is not None
    lvl = _hg_masks()
    rf = lambda b, ci: row0 // c + b * nchunk + ci
    rb = lambda b, ci: row0 // c + b * nchunk + (nchunk - 1 - ci)
    blk = lambda rfun, cb: pl.BlockSpec((c, BRANCH_W), lambda b, ci: (rfun(b, ci), cb))
    const = lambda shape: pl.BlockSpec(shape, lambda b, ci: (0,) * len(shape))
    in_specs = [blk(rf, 5), blk(rb, 5), blk(rf, 6), blk(rb, 7), blk(rf, 8), blk(rb, 8),
                const((2, BRANCH_W)), const((2, len(HG_LEVELS), c, c))]
    args = [h, h, h, h, h, h, lbs_l, lvl]
    if has_s0:
        in_specs.append(pl.BlockSpec((None, None, 2, HG_HEADS, HG_KEY, HG_VAL), lambda b, ci: (b, l, 0, 0, 0, 0)))
        args.append(s0)
    return pl.pallas_call(
        functools.partial(_hg_kernel, has_s0=has_s0, nchunk=nchunk),
        grid=(nseq, nchunk),
        in_specs=in_specs,
        out_specs=[pl.BlockSpec((c, BRANCH_W), lambda b, ci: (b * nchunk + ci, 0)),
                   pl.BlockSpec((c, BRANCH_W), lambda b, ci: (b * nchunk + (nchunk - 1 - ci), 0)),
                   pl.BlockSpec((None, 2, HG_HEADS, HG_KEY, HG_VAL), lambda b, ci: (b, 0, 0, 0, 0))],
        out_shape=[jax.ShapeDtypeStruct((nseq * seq, BRANCH_W), F32),
                   jax.ShapeDtypeStruct((nseq * seq, BRANCH_W), F32),
                   jax.ShapeDtypeStruct((nseq, 2, HG_HEADS, HG_KEY, HG_VAL), F32)],
        scratch_shapes=[pltpu.VMEM((2, HG_HEADS, HG_VAL, HG_KEY), F32)],
        compiler_params=_cparams(("arbitrary", "arbitrary")),
        name="hgrn_lat" if has_s0 else "hgrn_ctx",
    )(*args)


def _layer_norm(y, g, b):
    mu = jnp.mean(y, axis=-1, keepdims=True)
    yc = y - mu
    var = jnp.mean(yc * yc, axis=-1, keepdims=True)
    return yc * lax.rsqrt(var + NORM_EPS) * g + b


def _merge_kernel(attc_ref, attl_ref, rgc_ref, rgl_ref, ofc_ref, ofl_ref, obc_ref, obl_ref,
                  hgate_ref, mg0_ref, mg1_ref, mg2_ref, x_ref, mod_ref,
                  hgn_ref, wbr_ref, wout_ref, lng_ref, lnb_ref, rw_ref, rb_ref, tri_ref,
                  x1_ref, u2_ref, idx_ref, rank_ref, wts_ref, cnt_ref, wbr_bf, wout_bf, cnt_scr, *, ctx_tiles):
    @pl.when(pl.program_id(0) == 0)
    def _():
        wbr_bf[...] = wbr_ref[...].astype(BF16)
        wout_bf[...] = wout_ref[...].astype(BF16)
        cnt_scr[...] = jnp.zeros(cnt_scr.shape, F32)

    is_ctx = pl.program_id(0) < ctx_tiles
    pick = lambda c_ref, l_ref: jnp.where(is_ctx, c_ref[...], l_ref[...])
    att = pick(attc_ref, attl_ref)
    rg = pick(rgc_ref, rgl_ref)
    o = pick(ofc_ref, ofl_ref) + pick(obc_ref, obl_ref)
    hgate = hgate_ref[...].astype(F32)
    hg_parts = []
    for hd in range(HG_HEADS):
        oh = o[:, hd * HG_VAL:(hd + 1) * HG_VAL]
        gh = hgate[:, hd * HG_VAL:(hd + 1) * HG_VAL]
        yh = oh * lax.rsqrt(jnp.mean(oh * oh, axis=-1, keepdims=True) + NORM_EPS) * hgn_ref[...]
        hg_parts.append(yh * (gh * _sigmoid(gh)))
    hg = jnp.concatenate(hg_parts, axis=-1)

    proj = _sigmoid(mg0_ref[...].astype(F32)) * _dot(att.astype(BF16), wbr_bf[0])
    proj = proj + _sigmoid(mg1_ref[...].astype(F32)) * _dot(rg.astype(BF16), wbr_bf[1])
    proj = proj + _sigmoid(mg2_ref[...].astype(F32)) * _dot(hg.astype(BF16), wbr_bf[2])
    mix = _dot(proj.astype(BF16), wout_bf[...])

    x1 = _layer_norm(DN_ALPHA * x_ref[...] + mod_ref[2:3, :] * mix, lng_ref[...], lnb_ref[...])
    x1_ref[...] = x1
    u2 = x1 * (1.0 + mod_ref[4:5, :]) + mod_ref[3:4, :]
    packed = _pack_bf16_pair(u2[:, :D_MODEL // 2], u2[:, D_MODEL // 2:])
    for j in range(u2_ref.shape[0]):
        u2_ref[j] = packed[:, j * u2_ref.shape[2]:(j + 1) * u2_ref.shape[2]]

    u2_hi = u2.astype(BF16)
    u2_lo = (u2 - u2_hi.astype(F32)).astype(BF16)
    logits = (_dot(u2_hi, rw_ref[0]) + _dot(u2_lo, rw_ref[0]) + _dot(u2_hi, rw_ref[1])) + rb_ref[...]
    lane = lax.broadcasted_iota(jnp.int32, logits.shape, 1).astype(F32)
    idx_out = jnp.zeros(logits.shape, F32)
    wts_out = jnp.zeros(logits.shape, F32)
    chosen = jnp.zeros(logits.shape, F32)
    sels = []
    top0 = None
    den = None
    for k in range(TOP_K):
        m = jnp.max(logits, axis=-1, keepdims=True)
        sel = jnp.min(jnp.where(logits == m, lane, float(LANES)), axis=-1, keepdims=True)
        sels.append(sel)
        if k == 0:
            top0 = m
        e = jnp.exp(m - top0)
        den = e if den is None else den + e
        idx_out = jnp.where(lane == k, sel, idx_out)
        wts_out = jnp.where(lane == k, e, wts_out)
        chosen = jnp.where(lane == sel, 1.0, chosen)
        logits = jnp.where(lane == sel, -jnp.inf, logits)
    idx_ref[...] = idx_out.astype(jnp.int32)
    wts_ref[...] = wts_out * (1.0 / den)

    before = _dot(tri_ref[...], chosen.astype(BF16)) + cnt_scr[...]
    rank_out = jnp.zeros(logits.shape, F32)
    for k in range(TOP_K):
        rk = jnp.sum(jnp.where(lane == sels[k], before, 0.0), axis=-1, keepdims=True)
        rank_out = jnp.where(lane == k, rk, rank_out)
    rank_ref[...] = rank_out.astype(jnp.int32)
    cnt_scr[...] = cnt_scr[...] + jnp.sum(chosen, axis=0, keepdims=True)
    cnt_ref[...] = jnp.broadcast_to(cnt_scr[...], cnt_ref.shape).astype(jnp.int32)


def _merge_call(att, rg, o_f, o_b, h, x, mods, p, l):
    tm = 256
    ctx_tiles = N_CTX // tm
    rowb = lambda w, cb: pl.BlockSpec((tm, w), lambda i: (i, cb))
    ctxb = pl.BlockSpec((tm, BRANCH_W), lambda i: (jnp.minimum(i, ctx_tiles - 1), 0))
    latb = pl.BlockSpec((tm, BRANCH_W), lambda i: (jnp.maximum(i - ctx_tiles, 0), 0))
    const = lambda shape: pl.BlockSpec(shape, lambda i: (0,) * len(shape))
    rw = jnp.zeros((D_MODEL, LANES), F32).at[:, :N_EXPERTS].set(p['router_w'][l])
    rb = jnp.full((1, LANES), -1e30, F32).at[0, :N_EXPERTS].set(p['router_b'][l])
    rw_hi = rw.astype(BF16)
    rw = jnp.stack([rw_hi, (rw - rw_hi.astype(F32)).astype(BF16)])
    tri = jnp.asarray(np.tril(np.ones((tm, tm), np.float32), -1), BF16)
    return pl.pallas_call(
        functools.partial(_merge_kernel, ctx_tiles=ctx_tiles),
        grid=(N_TOK // tm,),
        in_specs=[ctxb, latb, ctxb, latb, ctxb, latb, ctxb, latb,
                  rowb(BRANCH_W, 9), rowb(D_MODEL, 5), rowb(D_MODEL, 6), rowb(D_MODEL, 7),
                  rowb(D_MODEL, 0),
                  pl.BlockSpec((None, None, 6, D_MODEL), lambda i: (l, _cond_of_row_tile(i, tm), 0, 0)),
                  const((1, HG_VAL)), const((3, BRANCH_W, D_MODEL)), const((D_MODEL, D_MODEL)),
                  const((1, D_MODEL)), const((1, D_MODEL)), const((2, D_MODEL, LANES)), const((1, LANES)),
                  const((tm, tm))],
        out_specs=[rowb(D_MODEL, 0), pl.BlockSpec((SC_SLABS, tm, SC_SLAB_W), lambda i: (0, i, 0)),
                   rowb(LANES, 0), rowb(LANES, 0), rowb(LANES, 0), const((8, LANES))],
        out_shape=[jax.ShapeDtypeStruct((N_TOK, D_MODEL), F32),
                   jax.ShapeDtypeStruct((SC_SLABS, N_TOK, SC_SLAB_W), F32),
                   jax.ShapeDtypeStruct((N_TOK, LANES), jnp.int32),
                   jax.ShapeDtypeStruct((N_TOK, LANES), jnp.int32),
                   jax.ShapeDtypeStruct((N_TOK, LANES), F32),
                   jax.ShapeDtypeStruct((8, LANES), jnp.int32)],
        scratch_shapes=[pltpu.VMEM((3, BRANCH_W, D_MODEL), BF16), pltpu.VMEM((D_MODEL, D_MODEL), BF16),
                        pltpu.VMEM((1, LANES), F32)],
        compiler_params=_cparams(("arbitrary",)),
        name="merge",
    )(att[0], att[1], rg[0], rg[1], o_f[0], o_f[1], o_b[0], o_b[1], h, h, h, h, x, mods,
      p['hg_norm'][l].reshape(1, HG_VAL), p['w_branch'][l], p['w_out'][l],
      p['ln1_g'][l].reshape(1, D_MODEL), p['ln1_b'][l].reshape(1, D_MODEL), rw, rb, tri)


MOE_NB = 256


def _moe_kernel(te_ref, tf_ref, tsl_ref, tnx_ref, nt_ref, x_ref, w1_hbm, b1_ref, w2_hbm, b2_ref, perm_ref,
                y_ref, w1_f, w2_f, w1_bf, w2_bf, sem, *, layer):
    i = pl.program_id(0)
    half = MOE_NB // 2

    def weight_copies(e, s):
        return (pltpu.make_async_copy(w1_hbm.at[layer, e], w1_f.at[s], sem.at[0, s]),
                pltpu.make_async_copy(w2_hbm.at[layer, e], w2_f.at[s], sem.at[1, s]))

    @pl.when(i == 0)
    def _():
        for cp in weight_copies(te_ref[0], 0):
            cp.start()

    @pl.when(tf_ref[i] == 1)
    def _():
        s = tsl_ref[i]
        for cp in weight_copies(te_ref[i], s):
            cp.wait()
        nxt = tnx_ref[i]

        @pl.when(nxt >= 0)
        def _():
            for cp in weight_copies(nxt, 1 - s):
                cp.start()

        for b in range(2 * D_EXPERT // MOE_NB):
            blk = w1_f[s, :, b * MOE_NB:(b + 1) * MOE_NB].astype(BF16)
            w1_bf[:, b * MOE_NB:(b + 1) * MOE_NB] = _dot(blk, perm_ref[...]).astype(BF16)
        w2_bf[...] = w2_f[s].astype(BF16)

    @pl.when(i < nt_ref[0])
    def _():
        halves = [_unpack_bf16_pair(x_ref[j]) for j in range(x_ref.shape[0])]
        x = jnp.concatenate([lo for lo, _ in halves] + [hi for _, hi in halves], axis=-1).astype(BF16)
        h = _dot(x, w1_bf[...]) + b1_ref[...]
        acts = []
        for b in range(2 * D_EXPERT // MOE_NB):
            glu = jnp.minimum(h[:, b * MOE_NB:b * MOE_NB + half], SWIGLU_LIMIT)
            lin = jnp.clip(h[:, b * MOE_NB + half:(b + 1) * MOE_NB], -SWIGLU_LIMIT, SWIGLU_LIMIT)
            acts.append((glu * _sigmoid(SWIGLU_ALPHA * glu) * (lin + 1.0)).astype(BF16))
        act = jnp.concatenate(acts, axis=-1)
        y = _dot(act, w2_bf[...]) + b2_ref[...]
        y_ref[...] = _pack_bf16_pair(y[:, :D_MODEL // 2], y[:, D_MODEL // 2:])

    @pl.when(i >= nt_ref[0])
    def _():
        y_ref[...] = jnp.zeros(y_ref.shape, F32)


def _moe_perm():
    half = MOE_NB // 2
    pm = np.zeros((MOE_NB, MOE_NB), np.float32)
    pm[2 * np.arange(half), np.arange(half)] = 1.0
    pm[2 * np.arange(half) + 1, half + np.arange(half)] = 1.0
    return jnp.asarray(pm, BF16)


def _moe_call(x_sorted, sched, w1, b1p, w2, b2, l):
    tm = MOE_TM
    emap = lambda i, te, *_: (l, te[i], 0, 0)
    grid_spec = pltpu.PrefetchScalarGridSpec(
        num_scalar_prefetch=5,
        grid=(MOE_TILES,),
        in_specs=[pl.BlockSpec((SC_SLABS, tm, SC_SLAB_W), lambda i, *_: (0, i, 0)),
                  pl.BlockSpec(memory_space=pl.ANY),
                  pl.BlockSpec((None, None, 1, 2 * D_EXPERT), emap),
                  pl.BlockSpec(memory_space=pl.ANY),
                  pl.BlockSpec((None, None, 1, D_MODEL), emap),
                  pl.BlockSpec((MOE_NB, MOE_NB), lambda i, *_: (0, 0))],
        out_specs=pl.BlockSpec((tm, D_MODEL // 2), lambda i, *_: (i, 0)),
        scratch_shapes=[pltpu.VMEM((2, D_MODEL, 2 * D_EXPERT), F32), pltpu.VMEM((2, D_EXPERT, D_MODEL), F32),
                        pltpu.VMEM((D_MODEL, 2 * D_EXPERT), BF16), pltpu.VMEM((D_EXPERT, D_MODEL), BF16),
                        pltpu.SemaphoreType.DMA((2, 2))],
    )
    return pl.pallas_call(
        functools.partial(_moe_kernel, layer=l),
        grid_spec=grid_spec,
        out_shape=jax.ShapeDtypeStruct((MOE_TILES * tm, D_MODEL // 2), F32),
        compiler_params=_cparams(("arbitrary",)),
        name="moe",
    )(*sched, x_sorted, w1, b1p, w2, b2, _moe_perm())


def _route(idx, rank, counts):
    tm = MOE_TM
    tiles_e = (counts + tm - 1) // tm
    eid = np.arange(N_EXPERTS, dtype=np.int32)
    earlier = (eid[None, :] <= eid[:, None]).astype(np.int32)
    tile_end = jnp.sum(earlier * tiles_e[None, :], axis=1)
    tile_start = tile_end - tiles_e
    lookup = lambda table, keys: jnp.sum(jnp.where(keys[..., None] == eid, table, 0), axis=-1)
    pos_t = lookup(tile_start, idx.T) * tm + rank.T
    n_used = tile_end[N_EXPERTS - 1]
    tile_ids = jnp.arange(MOE_TILES, dtype=jnp.int32)
    tid = jnp.minimum(tile_ids, n_used - 1)
    tile_expert = jnp.sum((tile_end[None, :] <= tid[:, None]).astype(jnp.int32), axis=1)
    tile_first = ((tile_ids == lookup(tile_start, tile_expert)) & (tile_ids < n_used)).astype(jnp.int32)
    has_rows = (tiles_e > 0).astype(jnp.int32)
    slot_e = (jnp.sum(earlier * has_rows[None, :], axis=1) - 1) & 1
    later = jnp.where((eid[None, :] > eid[:, None]) & (has_rows[None, :] > 0), eid[None, :], N_EXPERTS)
    next_e = jnp.min(later, axis=1)
    next_e = jnp.where(next_e >= N_EXPERTS, -1, next_e).astype(jnp.int32)
    sched = (tile_expert, tile_first, lookup(slot_e, tile_expert).astype(jnp.int32),
             lookup(next_e, tile_expert).astype(jnp.int32), n_used.reshape(1))
    return pos_t, sched


SC_WINDOW = 128
SC_SLABS = 2
SC_SLAB_W = D_MODEL // 2 // SC_SLABS


def _dispatch_rows(u2, pos_t):
    mesh = plsc.VectorSubcoreMesh(core_axis_name="core", subcore_axis_name="subcore")

    @functools.partial(pl.kernel, mesh=mesh, scratch_types=[],
                       out_type=jax.ShapeDtypeStruct((SC_SLABS, MOE_TILES * MOE_TM, SC_SLAB_W), F32))
    def dispatch(x_hbm, i_hbm, o_hbm):
        for j in range(SC_SLABS):
            def body(x_vmem, i_vmem, j=j):
                for k in range(TOP_K):
                    pltpu.sync_copy(x_vmem, o_hbm.at[j].at[i_vmem.at[k]])

            pltpu.emit_pipeline(
                body,
                grid=(N_TOK // SC_WINDOW,),
                in_specs=[pl.BlockSpec((SC_WINDOW, SC_SLAB_W), index_map=lambda i: (i, 0)),
                          pl.BlockSpec((TOP_K, SC_WINDOW), index_map=lambda i: (0, i))],
                out_specs=[],
                core_axis_name=("core", "subcore"),
                dimension_semantics=(pltpu.PARALLEL,),
            )(x_hbm.at[j], i_hbm)

    return dispatch(u2, pos_t)


def _final_kernel(yg_ref, wts_ref, x1_ref, mod_ref, modn_ref, lng_ref, lnb_ref, o_ref, un_ref):
    wts = wts_ref[...]
    ffn_lo = ffn_hi = None
    for k in range(TOP_K):
        lo, hi = _unpack_bf16_pair(yg_ref[k])
        w = wts[:, k:k + 1]
        ffn_lo = w * lo if ffn_lo is None else ffn_lo + w * lo
        ffn_hi = w * hi if ffn_hi is None else ffn_hi + w * hi
    ffn = jnp.concatenate([ffn_lo, ffn_hi], axis=-1)
    x2 = _layer_norm(DN_ALPHA * x1_ref[...] + mod_ref[5:6, :] * ffn, lng_ref[...], lnb_ref[...])
    o_ref[...] = x2
    un_ref[...] = (x2 * (1.0 + modn_ref[1:2, :]) + modn_ref[0:1, :]).astype(BF16)


def _final_call(yg, wts, x1, mods, ln_g, ln_b, l):
    tm = 256
    ln = min(l + 1, DEPTH - 1)
    const = lambda shape: pl.BlockSpec(shape, lambda i: (0,) * len(shape))
    modspec = lambda lyr: pl.BlockSpec((None, None, 6, D_MODEL), lambda i: (lyr, _cond_of_row_tile(i, tm), 0, 0))
    return pl.pallas_call(
        _final_kernel,
        grid=(N_TOK // tm,),
        in_specs=[pl.BlockSpec((TOP_K, tm, D_MODEL // 2), lambda i: (0, i, 0)),
                  pl.BlockSpec((tm, LANES), lambda i: (i, 0)),
                  pl.BlockSpec((tm, D_MODEL), lambda i: (i, 0)),
                  modspec(l), modspec(ln),
                  const((1, D_MODEL)), const((1, D_MODEL))],
        out_specs=[pl.BlockSpec((tm, D_MODEL), lambda i: (i, 0)), pl.BlockSpec((tm, D_MODEL), lambda i: (i, 0))],
        out_shape=[jax.ShapeDtypeStruct((N_TOK, D_MODEL), F32), jax.ShapeDtypeStruct((N_TOK, D_MODEL), BF16)],
        compiler_params=_cparams(("arbitrary",)),
        name="final",
    )(yg, wts, x1, mods, mods, ln_g.reshape(1, D_MODEL), ln_b.reshape(1, D_MODEL))


def kernel(x_prompt, x_sample, cache_attn_k, cache_attn_v, state_rglru, state_hgrn, c, c_ctx, w_ada, b_ada, w_in, da_lambda, da_subln, rg_conv_w, rg_conv_b, rg_gate_w, rg_gate_b, rg_lambda, hg_lb, hg_norm, w_branch, w_out, ln1_g, ln1_b, router_w, router_b, w1, b1, w2, b2, ln2_g, ln2_b):
    p = dict(hg_norm=hg_norm, w_branch=w_branch, w_out=w_out, ln1_g=ln1_g, ln1_b=ln1_b,
             router_w=router_w, router_b=router_b)

    x = jnp.concatenate([x_prompt.reshape(N_CTX, D_MODEL), x_sample.reshape(N_LAT, D_MODEL)], axis=0)
    cond = jnp.concatenate([c_ctx[None, :], c, jnp.zeros((N_COND - 1 - DEC_BATCH, D_MODEL), F32)], axis=0)
    mods = _adaln_all(cond, w_ada, b_ada).reshape(DEPTH, N_COND, 6, D_MODEL)

    pr = jax.nn.softmax(hg_lb.astype(F32), axis=0)
    lbs = jnp.cumsum(pr, axis=0) - pr[0]
    dl = da_lambda.astype(F32)
    lam_all = jnp.exp(jnp.sum(dl[:, 0] * dl[:, 1], -1)) - jnp.exp(jnp.sum(dl[:, 2] * dl[:, 3], -1))

    cache_k = cache_attn_k.reshape(DEC_BATCH, DEPTH, PAST_LEN, BRANCH_W)
    cache_v = cache_attn_v.reshape(DEC_BATCH, DEPTH, PAST_LEN, BRANCH_W)

    b1p = b1.reshape(DEPTH, N_EXPERTS, 2 * D_EXPERT // MOE_NB, MOE_NB // 2, 2)
    b1p = jnp.swapaxes(b1p, -1, -2).reshape(DEPTH, N_EXPERTS, 1, 2 * D_EXPERT)
    b2r = b2.reshape(DEPTH, N_EXPERTS, 1, D_MODEL)

    ks, vs, rgs, hgs = [], [], [], []
    u = _modulate(x, mods, 0)
    for l in range(DEPTH):
        lambda_init = 0.8 - 0.6 * math.exp(-0.3 * l)
        h = _in_proj(u, w_in, l)
        ks.append(h[:N_CTX, BRANCH_W:2 * BRANCH_W].astype(F32).reshape(BATCH, SEQ, DA_HEADS, 2, DA_HEAD_DIM))
        vs.append(h[:N_CTX, 2 * BRANCH_W:3 * BRANCH_W].astype(F32).reshape(BATCH, SEQ, DA_HEADS, DA_V_DIM))

        att = _attention(h, cache_k, cache_v, lam_all[l] + lambda_init, da_subln[l], l, 1.0 - lambda_init)

        wg, bg = _rg_gate_dense(rg_gate_w[l], rg_gate_b[l])
        rg_c, hl_c = _rglru_call(h, rg_conv_w[l], rg_conv_b[l], wg, bg, rg_lambda[l], None, BATCH, SEQ, 0)
        rg_l, _ = _rglru_call(h, rg_conv_w[l], rg_conv_b[l], wg, bg, rg_lambda[l], state_rglru[:, l],
                              DEC_BATCH, DEC_SEQ, N_CTX // DEC_SEQ)
        rgs.append(hl_c)

        of_c, ob_c, sl_c = _hgrn_call(h, lbs[l], None, l, BATCH, SEQ, 0)
        of_l, ob_l, _ = _hgrn_call(h, lbs[l], state_hgrn, l, DEC_BATCH, DEC_SEQ, N_CTX)
        hgs.append(sl_c)

        x1, u2, idx, rank, wts, cnt = _merge_call(att, (rg_c, rg_l), (of_c, of_l), (ob_c, ob_l), h, x, mods, p, l)

        pos_t, sched = _route(idx[:, :TOP_K], rank[:, :TOP_K], cnt[0, :N_EXPERTS])
        x_sorted = _dispatch_rows(u2, pos_t)
        y_sorted = _moe_call(x_sorted, sched, w1, b1p, w2, b2r, l)
        yg = y_sorted.at[pos_t.reshape(-1)].get(mode='promise_in_bounds').reshape(TOP_K, N_TOK, D_MODEL // 2)
        x, u = _final_call(yg, wts, x1, mods, ln2_g[l], ln2_b[l], l)

    y_prompt = x[:N_CTX].reshape(BATCH, SEQ, D_MODEL)
    y_sample = x[N_CTX:].reshape(DEC_BATCH, DEC_SEQ, D_MODEL)
    return (y_prompt, y_sample, jnp.stack(ks, axis=1), jnp.stack(vs, axis=1),
            jnp.stack(rgs, axis=1), jnp.stack(hgs, axis=1))
```

```python
import functools
import math

import numpy as np
import jax
import jax.numpy as jnp
from jax import lax
from jax.experimental import pallas as pl
from jax.experimental.pallas import tpu as pltpu
from jax.experimental.pallas import tpu_sc as plsc

F32 = jnp.float32
BF16 = jnp.bfloat16
HIGHEST = lax.Precision.HIGHEST

D_MODEL = 1024
BATCH = 16
SEQ = 256
DEPTH = 4
DEC_BATCH = 4
DEC_SEQ = 1024
PAST_LEN = 256
GRID_W = 64
BRANCH_W = 512
DA_HEADS = 4
DA_HEAD_DIM = 64
DA_V_DIM = 128
ROPE_BASE = 10000.0
RG_WIDTH = 512
RG_BLOCKS = 8
RG_BLOCK_W = 64
RG_CONV_W = 4
RG_C = 8.0
HG_HEADS = 4
HG_KEY = 128
HG_VAL = 128
N_EXPERTS = 32
TOP_K = 4
D_EXPERT = 1024
SWIGLU_ALPHA = 1.702
SWIGLU_LIMIT = 7.0
DN_ALPHA = (2 * DEPTH) ** 0.25
NORM_EPS = 1e-5
D_IN = 10 * BRANCH_W + 3 * D_MODEL

N_CTX = BATCH * SEQ
N_LAT = DEC_BATCH * DEC_SEQ
N_TOK = N_CTX + N_LAT
N_COND = 8

LANES = 128
VMEM_LIMIT = 56 * 1024 * 1024

HG_CHUNK = 128
HG_LEVELS = (1, 2, 4, 8, 16, 32, 64)
MOE_TM = 256
MOE_TILES = (N_TOK * TOP_K) // MOE_TM + N_EXPERTS


def _cparams(sem):
    return pltpu.CompilerParams(dimension_semantics=sem, vmem_limit_bytes=VMEM_LIMIT)


def _sigmoid(x):
    return 0.5 * jnp.tanh(0.5 * x) + 0.5


def _pack_bf16_pair(lo, hi):
    lo_bits = pltpu.bitcast(lo.astype(BF16).astype(F32), jnp.uint32) >> 16
    hi_bits = pltpu.bitcast(hi.astype(BF16).astype(F32), jnp.uint32) & jnp.uint32(0xFFFF0000)
    return pltpu.bitcast(hi_bits | lo_bits, F32)


def _unpack_bf16_pair(words):
    bits = pltpu.bitcast(words, jnp.uint32)
    return pltpu.bitcast(bits << 16, F32), pltpu.bitcast(bits & jnp.uint32(0xFFFF0000), F32)


def _dot(a, b):
    return jnp.dot(a, b, preferred_element_type=F32)


def _dot_nt(a, b):
    return lax.dot_general(a, b, (((1,), (1,)), ((), ())), preferred_element_type=F32)


def _cond_of_row_tile(i, tm):
    r = i * tm
    return jnp.where(r < N_CTX, 0, 1 + (r - N_CTX) // DEC_SEQ)


def _ada_kernel(c_ref, w_ref, b_ref, o_ref):
    c = c_ref[...]
    s = c * _sigmoid(c)
    o_ref[0] = jnp.dot(s, w_ref[0], precision=HIGHEST, preferred_element_type=F32) + b_ref[0]


def _adaln_all(cond, w_ada, b_ada):
    tn = 1536
    return pl.pallas_call(
        _ada_kernel,
        grid=(DEPTH, 6 * D_MODEL // tn),
        in_specs=[pl.BlockSpec((N_COND, D_MODEL), lambda l, j: (0, 0)),
                  pl.BlockSpec((1, D_MODEL, tn), lambda l, j: (l, 0, j)),
                  pl.BlockSpec((1, 1, tn), lambda l, j: (l, 0, j))],
        out_specs=pl.BlockSpec((1, N_COND, tn), lambda l, j: (l, 0, j)),
        out_shape=jax.ShapeDtypeStruct((DEPTH, N_COND, 6 * D_MODEL), F32),
        compiler_params=_cparams(("arbitrary", "arbitrary")),
        name="adaln",
    )(cond, w_ada, b_ada.reshape(DEPTH, 1, 6 * D_MODEL))


def _modulate_kernel(x_ref, mod_ref, u_ref):
    u_ref[...] = (x_ref[...] * (1.0 + mod_ref[1:2, :]) + mod_ref[0:1, :]).astype(BF16)


def _modulate(x, mods, l):
    tm = 1024
    return pl.pallas_call(
        _modulate_kernel,
        grid=(N_TOK // tm,),
        in_specs=[pl.BlockSpec((tm, D_MODEL), lambda i: (i, 0)),
                  pl.BlockSpec((None, None, 6, D_MODEL), lambda i: (l, _cond_of_row_tile(i, tm), 0, 0))],
        out_specs=pl.BlockSpec((tm, D_MODEL), lambda i: (i, 0)),
        out_shape=jax.ShapeDtypeStruct((N_TOK, D_MODEL), BF16),
        compiler_params=_cparams(("arbitrary",)),
        name="modulate",
    )(x, mods)


def _in_kernel(u_ref, w_ref, o_ref, wbf_ref):
    @pl.when(pl.program_id(1) == 0)
    def _():
        wbf_ref[...] = w_ref[...].astype(BF16)

    o_ref[...] = _dot(u_ref[...], wbf_ref[...]).astype(BF16)


def _in_proj(u, w_in, l):
    tm, tn = 1024, 2048
    return pl.pallas_call(
        _in_kernel,
        grid=(D_IN // tn, N_TOK // tm),
        in_specs=[pl.BlockSpec((tm, D_MODEL), lambda j, i: (i, 0)),
                  pl.BlockSpec((None, D_MODEL, tn), lambda j, i: (l, 0, j))],
        out_specs=pl.BlockSpec((tm, tn), lambda j, i: (i, j)),
        out_shape=jax.ShapeDtypeStruct((N_TOK, D_IN), BF16),
        scratch_shapes=[pltpu.VMEM((D_MODEL, tn), BF16)],
        compiler_params=_cparams(("arbitrary", "arbitrary")),
        name="in_proj",
    )(u, w_in)


def _rope(x, cos, sin_signed):
    lane = lax.broadcasted_iota(jnp.int32, x.shape, 1)
    first = (lane & 31) < 16
    partner = jnp.where(first, pltpu.roll(x, LANES - 16, 1), pltpu.roll(x, 16, 1))
    return x * cos + partner * sin_signed


LOG2E = 1.4426950408889634


def _attn_body(q, keys, vals, lam, subln, out_scale):
    lane = lax.broadcasted_iota(jnp.int32, q.shape, 1)
    qs = q * (DA_HEAD_DIM ** -0.5 * LOG2E)
    acc = None
    for m in range(2):
        in_map = (lane < DA_HEAD_DIM) if m == 0 else (lane >= DA_HEAD_DIM)
        qm = jnp.where(in_map, qs, 0.0).astype(BF16)
        s = [_dot_nt(qm, k) for k in keys]
        mx = s[0].max(axis=-1, keepdims=True)
        for si in s[1:]:
            mx = jnp.maximum(mx, si.max(axis=-1, keepdims=True))
        e = [jnp.exp2(si - mx) for si in s]
        den = e[0].sum(axis=-1, keepdims=True)
        for ei in e[1:]:
            den = den + ei.sum(axis=-1, keepdims=True)
        pv = _dot(e[0].astype(BF16), vals[0])
        for ei, v in zip(e[1:], vals[1:]):
            pv = pv + _dot(ei.astype(BF16), v)
        coef = (1.0 / den) if m == 0 else (-lam[:, 0:1] / den)
        acc = pv * coef if acc is None else acc + pv * coef
    y = acc * lax.rsqrt(jnp.mean(acc * acc, axis=-1, keepdims=True) + NORM_EPS)
    return y * subln * out_scale


def _attn_ctx_kernel(lam_ref, sub_ref, q_ref, k_ref, v_ref, o_ref, *, out_scale):
    for hd in range(DA_HEADS):
        cols = slice(hd * LANES, (hd + 1) * LANES)
        o_ref[:, cols] = _attn_body(q_ref[:, cols].astype(F32), [k_ref[:, cols]], [v_ref[:, cols]],
                                    lam_ref[...], sub_ref[...], out_scale)


def _attn_lat_kernel(lam_ref, sub_ref, q_ref, k_ref, v_ref, kc_ref, vc_ref, cos_ref, sin_ref,
                     o_ref, k_scr, kc_scr, vc_scr, *, out_scale, tq):
    k_scr[...] = _rope(k_ref[...].astype(F32), cos_ref[...], sin_ref[...]).astype(BF16)
    kc_scr[...] = kc_ref[...].astype(BF16)
    vc_scr[...] = vc_ref[...].astype(BF16)

    def q_block(qi, carry):
        rows = pl.ds(pl.multiple_of(qi * tq, tq), tq)
        q = _rope(q_ref[rows, :].astype(F32), cos_ref[rows, :], sin_ref[rows, :])
        o_ref[rows, :] = _attn_body(q, [k_scr[...], kc_scr[...]], [v_ref[...], vc_scr[...]],
                                    lam_ref[...], sub_ref[...], out_scale)
        return carry

    lax.fori_loop(0, DEC_SEQ // tq, q_block, 0)


def _rope_tables():
    t = np.arange(DEC_SEQ)
    row = (t // GRID_W).astype(np.float64)
    col = (t % GRID_W).astype(np.float64)
    d = DA_HEAD_DIM // 2
    inv = ROPE_BASE ** (-np.arange(0, d, 2, dtype=np.float64) / d)
    a_row = row[:, None] * inv[None, :]
    a_col = col[:, None] * inv[None, :]
    cos = np.concatenate([np.cos(a_row), np.cos(a_row), np.cos(a_col), np.cos(a_col)], -1)
    sin = np.concatenate([-np.sin(a_row), np.sin(a_row), -np.sin(a_col), np.sin(a_col)], -1)
    return (jnp.asarray(np.tile(cos, (1, 2)), F32), jnp.asarray(np.tile(sin, (1, 2)), F32))


def _attention(h, cache_k, cache_v, lam, subln, l, out_scale):
    lam_v = jnp.full((1, LANES), lam, F32)
    sub_v = subln.reshape(1, DA_V_DIM)
    small = pl.BlockSpec((1, LANES), lambda *a: (0, 0))
    att_ctx = pl.pallas_call(
        functools.partial(_attn_ctx_kernel, out_scale=out_scale),
        grid=(BATCH,),
        in_specs=[small, small,
                  pl.BlockSpec((SEQ, BRANCH_W), lambda b: (b, 0)),
                  pl.BlockSpec((SEQ, BRANCH_W), lambda b: (b, 1)),
                  pl.BlockSpec((SEQ, BRANCH_W), lambda b: (b, 2))],
        out_specs=pl.BlockSpec((SEQ, BRANCH_W), lambda b: (b, 0)),
        out_shape=jax.ShapeDtypeStruct((N_CTX, BRANCH_W), F32),
        compiler_params=_cparams(("arbitrary",)),
        name="attn_ctx",
    )(lam_v, sub_v, h, h, h)

    tq = 256
    cos, sin = _rope_tables()
    row0 = N_CTX // DEC_SEQ
    seq_blk = lambda cb: pl.BlockSpec((DEC_SEQ, LANES), lambda b, hd: (row0 + b, cb + hd))
    cache_blk = pl.BlockSpec((None, None, PAST_LEN, LANES), lambda b, hd: (b, l, 0, hd))
    table = pl.BlockSpec((DEC_SEQ, LANES), lambda b, hd: (0, 0))
    att_lat = pl.pallas_call(
        functools.partial(_attn_lat_kernel, out_scale=out_scale, tq=tq),
        grid=(DEC_BATCH, DA_HEADS),
        in_specs=[small, small, seq_blk(0), seq_blk(4), seq_blk(8), cache_blk, cache_blk, table, table],
        out_specs=pl.BlockSpec((DEC_SEQ, LANES), lambda b, hd: (b, hd)),
        out_shape=jax.ShapeDtypeStruct((N_LAT, BRANCH_W), F32),
        scratch_shapes=[pltpu.VMEM((DEC_SEQ, LANES), BF16), pltpu.VMEM((PAST_LEN, LANES), BF16),
                        pltpu.VMEM((PAST_LEN, LANES), BF16)],
        compiler_params=_cparams(("arbitrary", "arbitrary")),
        name="attn_lat",
    )(lam_v, sub_v, h, h, h, cache_k, cache_v, cos, sin)
    return att_ctx, att_lat


def _gelu_tanh(x):
    return 0.5 * x * (1.0 + jnp.tanh(math.sqrt(2.0 / math.pi) * (x + 0.044715 * (x * x * x))))


def _rg_kernel(*refs, seq, has_h0):
    if has_h0:
        rx_ref, gate_ref, cw_ref, cb_ref, wg_ref, bg_ref, lam_ref, h0_ref = refs[:8]
        rest = refs[8:]
    else:
        rx_ref, gate_ref, cw_ref, cb_ref, wg_ref, bg_ref, lam_ref = refs[:7]
        h0_ref = None
        rest = refs[7:]
    out_ref, hl_ref, a_scr, b_scr, h_scr = rest

    x = rx_ref[...].astype(F32)
    row = lax.broadcasted_iota(jnp.int32, x.shape, 0)
    xr = cb_ref[...] + cw_ref[2:3, :] * x
    for j in (0, 1, 3):
        d = j - RG_CONV_W // 2
        shifted = pltpu.roll(x, (-d) % seq, 0)
        valid = (row + d >= 0) & (row + d < seq)
        xr = xr + cw_ref[j:j + 1, :] * jnp.where(valid, shifted, 0.0)

    g = _dot(xr.astype(BF16), wg_ref[...]) + bg_ref[...]
    for dr in range(2):
        r = _sigmoid(g[:, (2 * dr) * RG_WIDTH:(2 * dr + 1) * RG_WIDTH])
        i = _sigmoid(g[:, (2 * dr + 1) * RG_WIDTH:(2 * dr + 2) * RG_WIDTH])
        lam = lam_ref[dr:dr + 1, :]
        softplus_neg = jnp.maximum(-lam, 0.0) + jnp.log(1.0 + jnp.exp(-jnp.abs(lam)))
        log_a = (-RG_C * softplus_neg) * r
        a = jnp.exp(log_a)
        a_scr[dr] = a
        b_scr[dr] = jnp.sqrt(1.0 - a * a) * i * xr

    if has_h0:
        hf0, hb0 = h0_ref[0:1, :], h0_ref[1:2, :]
    else:
        hf0 = hb0 = jnp.zeros((1, RG_WIDTH), F32)

    def step(t, carry):
        hf, hb = carry
        tb = seq - 1 - t
        hf = a_scr[0, pl.ds(t, 1), :] * hf + b_scr[0, pl.ds(t, 1), :]
        h_scr[0, pl.ds(t, 1), :] = hf
        hb = a_scr[1, pl.ds(tb, 1), :] * hb + b_scr[1, pl.ds(tb, 1), :]
        h_scr[1, pl.ds(tb, 1), :] = hb
        return hf, hb

    hf, hb = lax.fori_loop(0, seq, step, (hf0, hb0), unroll=8)
    hl_ref[0:1, :] = hf
    hl_ref[1:2, :] = hb
    out_ref[...] = (h_scr[0] + h_scr[1]) * _gelu_tanh(gate_ref[...].astype(F32))


def _rglru_call(h, conv_w, conv_b, wg, bg, lam, h0, nseq, seq, row_block0):
    has_h0 = h0 is not None
    full = lambda shape: pl.BlockSpec(shape, lambda b: (0,) * len(shape))
    in_specs = [pl.BlockSpec((seq, RG_WIDTH), lambda b: (row_block0 + b, 3)),
                pl.BlockSpec((seq, RG_WIDTH), lambda b: (row_block0 + b, 4)),
                full((RG_CONV_W, RG_WIDTH)), full((1, RG_WIDTH)),
                full((RG_WIDTH, 4 * RG_WIDTH)), full((1, 4 * RG_WIDTH)), full((2, RG_WIDTH))]
    args = [h, h, conv_w, conv_b.reshape(1, RG_WIDTH), wg, bg, lam]
    if has_h0:
        in_specs.append(pl.BlockSpec((None, 2, RG_WIDTH), lambda b: (b, 0, 0)))
        args.append(h0)
    return pl.pallas_call(
        functools.partial(_rg_kernel, seq=seq, has_h0=has_h0),
        grid=(nseq,),
        in_specs=in_specs,
        out_specs=[pl.BlockSpec((seq, RG_WIDTH), lambda b: (b, 0)),
                   pl.BlockSpec((None, 2, RG_WIDTH), lambda b: (b, 0, 0))],
        out_shape=[jax.ShapeDtypeStruct((nseq * seq, RG_WIDTH), F32),
                   jax.ShapeDtypeStruct((nseq, 2, RG_WIDTH), F32)],
        scratch_shapes=[pltpu.VMEM((2, seq, RG_WIDTH), F32)] * 3,
        compiler_params=_cparams(("arbitrary",)),
        name="rglru_lat" if has_h0 else "rglru_ctx",
    )(*args)


def _rg_gate_dense(gate_w, gate_b):
    eye = jnp.eye(RG_BLOCKS, dtype=F32)
    dense = jnp.einsum('dgncf,nm->dgncmf', gate_w, eye).reshape(2, 2, RG_WIDTH, RG_WIDTH)
    wg = jnp.transpose(dense, (2, 0, 1, 3)).reshape(RG_WIDTH, 4 * RG_WIDTH)
    return wg.astype(BF16), gate_b.reshape(1, 4 * RG_WIDTH)


def _hg_masks():
    c = HG_CHUNK
    t = np.arange(c)[:, None]
    s = np.arange(c)[None, :]
    lvl = np.zeros((2, len(HG_LEVELS), c, c), np.float32)
    for n, h in enumerate(HG_LEVELS):
        same = (t // (2 * h)) == (s // (2 * h))
        lvl[0, n] = same & ((t // h) % 2 == 1) & ((s // h) % 2 == 0)
        lvl[1, n] = same & ((t // h) % 2 == 0) & ((s // h) % 2 == 1)
    return jnp.asarray(lvl)


def _hg_chunk(q_raw, z, v, lb, st, lvl, rev):
    c = HG_CHUNK
    q = q_raw * _sigmoid(q_raw)
    sig = _sigmoid(z)
    f = lb + (1.0 - lb) * sig
    kk = (1.0 - lb) * (1.0 - sig)

    rowk = lax.broadcasted_iota(jnp.int32, (c, HG_KEY), 0)
    pos = rowk if not rev else (c - 1) - rowk
    earlier = lambda x, d: pltpu.roll(x, d if not rev else c - d, 0)
    later = lambda x, d: pltpu.roll(x, c - d if not rev else d, 0)

    def block_row(x, n, p):
        r = p if not rev else n - 1 - p
        x3 = x.reshape(c // n, n, HG_KEY)
        return jnp.broadcast_to(x3[:, r:r + 1, :], x3.shape).reshape(c, HG_KEY)

    pair = f * earlier(f, 1)
    f_next = later(f, 1)
    p4 = pos & 3
    a = {1: f, 2: jnp.where((pos & 1) == 1, pair, f)}
    b = {2: jnp.where((pos & 1) == 0, f_next, 1.0)}
    a[4] = a[2] * jnp.where(p4 == 2, earlier(pair, 1), jnp.where(p4 == 3, earlier(pair, 2), 1.0))
    b[4] = jnp.where(p4 == 3, 1.0, jnp.where(p4 == 2, f_next,
                                             jnp.where(p4 == 1, later(pair, 2), f_next * later(pair, 3))))
    h = 4
    while h < c:
        in_later_half = (pos & h) != 0
        a[2 * h] = jnp.where(in_later_half, a[h] * block_row(a[h], 2 * h, h - 1), a[h])
        b[2 * h] = jnp.where(in_later_half, b[h], b[h] * block_row(a[h], 2 * h, 2 * h - 1))
        h *= 2

    row = lax.broadcasted_iota(jnp.int32, (c, c), 0)
    col = lax.broadcasted_iota(jnp.int32, (c, c), 1)
    scores = jnp.where(row == col, jnp.sum(q * kk, axis=-1, keepdims=True), 0.0)
    for n, h in enumerate(HG_LEVELS):
        qa = (q * a[h]).astype(BF16)
        kb = (kk if h == 1 else kk * b[h]).astype(BF16)
        scores = scores + lvl[n] * _dot_nt(qa, kb)

    vb = v.astype(BF16)
    o = _dot(scores.astype(BF16), vb)
    o = o + _dot_nt((q * a[c]).astype(BF16), st.astype(BF16))
    total = a[c][c - 1:c, :] if not rev else a[c][0:1, :]
    st_new = st * total + _dot(v.T.astype(BF16), (kk * b[c]).astype(BF16))
    return o, st_new


def _hg_kernel(*refs, has_s0, nchunk):
    if has_s0:
        (qf_ref, qb_ref, zf_ref, zb_ref, vf_ref, vb_ref, lb_ref, lvl_ref, s0_ref,
         of_ref, ob_ref, sfin_ref, st_scr) = refs
    else:
        (qf_ref, qb_ref, zf_ref, zb_ref, vf_ref, vb_ref, lb_ref, lvl_ref,
         of_ref, ob_ref, sfin_ref, st_scr) = refs
        s0_ref = None
    ci = pl.program_id(1)

    @pl.when(ci == 0)
    def _():
        for dr in range(2):
            for hd in range(HG_HEADS):
                st_scr[dr, hd] = s0_ref[dr, hd].T if has_s0 else jnp.zeros((HG_VAL, HG_KEY), F32)

    last = ci == nchunk - 1
    for hd in range(HG_HEADS):
        cols = slice(hd * LANES, (hd + 1) * LANES)
        ld = lambda ref: ref[:, cols].astype(F32)
        o_f, st_f = _hg_chunk(ld(qf_ref), ld(zf_ref), ld(vf_ref), lb_ref[0:1, cols], st_scr[0, hd],
                              lvl_ref[0], False)
        o_b, st_b = _hg_chunk(ld(qb_ref), ld(zb_ref), ld(vb_ref), lb_ref[1:2, cols], st_scr[1, hd],
                              lvl_ref[1], True)
        of_ref[:, cols] = o_f
        ob_ref[:, cols] = o_b
        st_scr[0, hd] = st_f
        st_scr[1, hd] = st_b

        @pl.when(last)
        def _():
            sfin_ref[0, hd] = st_f.T
            sfin_ref[1, hd] = st_b.T


def _hgrn_call(h, lbs_l, s0, l, nseq, seq, row0):
    c = HG_CHUNK
    nchunk = seq // c
    has_s0 = s0 is not None
    lvl = _hg_masks()
    rf = lambda b, ci: row0 // c + b * nchunk + ci
    rb = lambda b, ci: row0 // c + b * nchunk + (nchunk - 1 - ci)
    blk = lambda rfun, cb: pl.BlockSpec((c, BRANCH_W), lambda b, ci: (rfun(b, ci), cb))
    const = lambda shape: pl.BlockSpec(shape, lambda b, ci: (0,) * len(shape))
    in_specs = [blk(rf, 5), blk(rb, 5), blk(rf, 6), blk(rb, 7), blk(rf, 8), blk(rb, 8),
                const((2, BRANCH_W)), const((2, len(HG_LEVELS), c, c))]
    args = [h, h, h, h, h, h, lbs_l, lvl]
    if has_s0:
        in_specs.append(pl.BlockSpec((None, None, 2, HG_HEADS, HG_KEY, HG_VAL), lambda b, ci: (b, l, 0, 0, 0, 0)))
        args.append(s0)
    return pl.pallas_call(
        functools.partial(_hg_kernel, has_s0=has_s0, nchunk=nchunk),
        grid=(nseq, nchunk),
        in_specs=in_specs,
        out_specs=[pl.BlockSpec((c, BRANCH_W), lambda b, ci: (b * nchunk + ci, 0)),
                   pl.BlockSpec((c, BRANCH_W), lambda b, ci: (b * nchunk + (nchunk - 1 - ci), 0)),
                   pl.BlockSpec((None, 2, HG_HEADS, HG_KEY, HG_VAL), lambda b, ci: (b, 0, 0, 0, 0))],
        out_shape=[jax.ShapeDtypeStruct((nseq * seq, BRANCH_W), F32),
                   jax.ShapeDtypeStruct((nseq * seq, BRANCH_W), F32),
                   jax.ShapeDtypeStruct((nseq, 2, HG_HEADS, HG_KEY, HG_VAL), F32)],
        scratch_shapes=[pltpu.VMEM((2, HG_HEADS, HG_VAL, HG_KEY), F32)],
        compiler_params=_cparams(("arbitrary", "arbitrary")),
        name="hgrn_lat" if has_s0 else "hgrn_ctx",
    )(*args)


def _layer_norm(y, g, b):
    mu = jnp.mean(y, axis=-1, keepdims=True)
    yc = y - mu
    var = jnp.mean(yc * yc, axis=-1, keepdims=True)
    return yc * lax.rsqrt(var + NORM_EPS) * g + b


def _merge_kernel(attc_ref, attl_ref, rgc_ref, rgl_ref, ofc_ref, ofl_ref, obc_ref, obl_ref,
                  hgate_ref, mg0_ref, mg1_ref, mg2_ref, x_ref, mod_ref,
                  hgn_ref, wbr_ref, wout_ref, lng_ref, lnb_ref, rw_ref, rb_ref, tri_ref,
                  x1_ref, u2_ref, idx_ref, rank_ref, wts_ref, cnt_ref, wbr_bf, wout_bf, cnt_scr, *, ctx_tiles):
    @pl.when(pl.program_id(0) == 0)
    def _():
        wbr_bf[...] = wbr_ref[...].astype(BF16)
        wout_bf[...] = wout_ref[...].astype(BF16)
        cnt_scr[...] = jnp.zeros(cnt_scr.shape, F32)

    is_ctx = pl.program_id(0) < ctx_tiles
    pick = lambda c_ref, l_ref: jnp.where(is_ctx, c_ref[...], l_ref[...])
    att = pick(attc_ref, attl_ref)
    rg = pick(rgc_ref, rgl_ref)
    o = pick(ofc_ref, ofl_ref) + pick(obc_ref, obl_ref)
    hgate = hgate_ref[...].astype(F32)
    hg_parts = []
    for hd in range(HG_HEADS):
        oh = o[:, hd * HG_VAL:(hd + 1) * HG_VAL]
        gh = hgate[:, hd * HG_VAL:(hd + 1) * HG_VAL]
        yh = oh * lax.rsqrt(jnp.mean(oh * oh, axis=-1, keepdims=True) + NORM_EPS) * hgn_ref[...]
        hg_parts.append(yh * (gh * _sigmoid(gh)))
    hg = jnp.concatenate(hg_parts, axis=-1)

    proj = _sigmoid(mg0_ref[...].astype(F32)) * _dot(att.astype(BF16), wbr_bf[0])
    proj = proj + _sigmoid(mg1_ref[...].astype(F32)) * _dot(rg.astype(BF16), wbr_bf[1])
    proj = proj + _sigmoid(mg2_ref[...].astype(F32)) * _dot(hg.astype(BF16), wbr_bf[2])
    mix = _dot(proj.astype(BF16), wout_bf[...])

    x1 = _layer_norm(DN_ALPHA * x_ref[...] + mod_ref[2:3, :] * mix, lng_ref[...], lnb_ref[...])
    x1_ref[...] = x1
    u2 = x1 * (1.0 + mod_ref[4:5, :]) + mod_ref[3:4, :]
    packed = _pack_bf16_pair(u2[:, :D_MODEL // 2], u2[:, D_MODEL // 2:])
    for j in range(u2_ref.shape[0]):
        u2_ref[j] = packed[:, j * u2_ref.shape[2]:(j + 1) * u2_ref.shape[2]]

    u2_hi = u2.astype(BF16)
    u2_lo = (u2 - u2_hi.astype(F32)).astype(BF16)
    logits = (_dot(u2_hi, rw_ref[0]) + _dot(u2_lo, rw_ref[0]) + _dot(u2_hi, rw_ref[1])) + rb_ref[...]
    lane = lax.broadcasted_iota(jnp.int32, logits.shape, 1).astype(F32)
    idx_out = jnp.zeros(logits.shape, F32)
    wts_out = jnp.zeros(logits.shape, F32)
    chosen = jnp.zeros(logits.shape, F32)
    sels = []
    top0 = None
    den = None
    for k in range(TOP_K):
        m = jnp.max(logits, axis=-1, keepdims=True)
        sel = jnp.min(jnp.where(logits == m, lane, float(LANES)), axis=-1, keepdims=True)
        sels.append(sel)
        if k == 0:
            top0 = m
        e = jnp.exp(m - top0)
        den = e if den is None else den + e
        idx_out = jnp.where(lane == k, sel, idx_out)
        wts_out = jnp.where(lane == k, e, wts_out)
        chosen = jnp.where(lane == sel, 1.0, chosen)
        logits = jnp.where(lane == sel, -jnp.inf, logits)
    idx_ref[...] = idx_out.astype(jnp.int32)
    wts_ref[...] = wts_out * (1.0 / den)

    before = _dot(tri_ref[...], chosen.astype(BF16)) + cnt_scr[...]
    rank_out = jnp.zeros(logits.shape, F32)
    for k in range(TOP_K):
        rk = jnp.sum(jnp.where(lane == sels[k], before, 0.0), axis=-1, keepdims=True)
        rank_out = jnp.where(lane == k, rk, rank_out)
    rank_ref[...] = rank_out.astype(jnp.int32)
    cnt_scr[...] = cnt_scr[...] + jnp.sum(chosen, axis=0, keepdims=True)
    cnt_ref[...] = jnp.broadcast_to(cnt_scr[...], cnt_ref.shape).astype(jnp.int32)


def _merge_call(att, rg, o_f, o_b, h, x, mods, p, l):
    tm = 256
    ctx_tiles = N_CTX // tm
    rowb = lambda w, cb: pl.BlockSpec((tm, w), lambda i: (i, cb))
    ctxb = pl.BlockSpec((tm, BRANCH_W), lambda i: (jnp.minimum(i, ctx_tiles - 1), 0))
    latb = pl.BlockSpec((tm, BRANCH_W), lambda i: (jnp.maximum(i - ctx_tiles, 0), 0))
    const = lambda shape: pl.BlockSpec(shape, lambda i: (0,) * len(shape))
    rw = jnp.zeros((D_MODEL, LANES), F32).at[:, :N_EXPERTS].set(p['router_w'][l])
    rb = jnp.full((1, LANES), -1e30, F32).at[0, :N_EXPERTS].set(p['router_b'][l])
    rw_hi = rw.astype(BF16)
    rw = jnp.stack([rw_hi, (rw - rw_hi.astype(F32)).astype(BF16)])
    tri = jnp.asarray(np.tril(np.ones((tm, tm), np.float32), -1), BF16)
    return pl.pallas_call(
        functools.partial(_merge_kernel, ctx_tiles=ctx_tiles),
        grid=(N_TOK // tm,),
        in_specs=[ctxb, latb, ctxb, latb, ctxb, latb, ctxb, latb,
                  rowb(BRANCH_W, 9), rowb(D_MODEL, 5), rowb(D_MODEL, 6), rowb(D_MODEL, 7),
                  rowb(D_MODEL, 0),
                  pl.BlockSpec((None, None, 6, D_MODEL), lambda i: (l, _cond_of_row_tile(i, tm), 0, 0)),
                  const((1, HG_VAL)), const((3, BRANCH_W, D_MODEL)), const((D_MODEL, D_MODEL)),
                  const((1, D_MODEL)), const((1, D_MODEL)), const((2, D_MODEL, LANES)), const((1, LANES)),
                  const((tm, tm))],
        out_specs=[rowb(D_MODEL, 0), pl.BlockSpec((SC_SLABS, tm, SC_SLAB_W), lambda i: (0, i, 0)),
                   rowb(LANES, 0), rowb(LANES, 0), rowb(LANES, 0), const((8, LANES))],
        out_shape=[jax.ShapeDtypeStruct((N_TOK, D_MODEL), F32),
                   jax.ShapeDtypeStruct((SC_SLABS, N_TOK, SC_SLAB_W), F32),
                   jax.ShapeDtypeStruct((N_TOK, LANES), jnp.int32),
                   jax.ShapeDtypeStruct((N_TOK, LANES), jnp.int32),
                   jax.ShapeDtypeStruct((N_TOK, LANES), F32),
                   jax.ShapeDtypeStruct((8, LANES), jnp.int32)],
        scratch_shapes=[pltpu.VMEM((3, BRANCH_W, D_MODEL), BF16), pltpu.VMEM((D_MODEL, D_MODEL), BF16),
                        pltpu.VMEM((1, LANES), F32)],
        compiler_params=_cparams(("arbitrary",)),
        name="merge",
    )(att[0], att[1], rg[0], rg[1], o_f[0], o_f[1], o_b[0], o_b[1], h, h, h, h, x, mods,
      p['hg_norm'][l].reshape(1, HG_VAL), p['w_branch'][l], p['w_out'][l],
      p['ln1_g'][l].reshape(1, D_MODEL), p['ln1_b'][l].reshape(1, D_MODEL), rw, rb, tri)


MOE_NB = 256


MOE_SLOTS = 3


def _moe_kernel(te_ref, tf_ref, tsl_ref, tnx_ref, tn2_ref, nt_ref, x_ref, w1_hbm, b1_ref, w2_hbm, b2_ref, perm_ref,
                y_ref, w1_f, w2_f, w1_bf, w2_bf, sem, *, layer):
    i = pl.program_id(0)
    half = MOE_NB // 2

    def weight_copies(e, s):
        return (pltpu.make_async_copy(w1_hbm.at[layer, e], w1_f.at[s], sem.at[0, s]),
                pltpu.make_async_copy(w2_hbm.at[layer, e], w2_f.at[s], sem.at[1, s]))

    @pl.when(i == 0)
    def _():
        for cp in weight_copies(te_ref[0], 0):
            cp.start()

        @pl.when(tnx_ref[0] >= 0)
        def _():
            for cp in weight_copies(tnx_ref[0], 1):
                cp.start()

    @pl.when(tf_ref[i] == 1)
    def _():
        s = tsl_ref[i]
        for cp in weight_copies(te_ref[i], s):
            cp.wait()
        ahead = tn2_ref[i]

        @pl.when(ahead >= 0)
        def _():
            for cp in weight_copies(ahead, jnp.where(s == 0, MOE_SLOTS - 1, s - 1)):
                cp.start()

        for b in range(2 * D_EXPERT // MOE_NB):
            blk = w1_f[s, :, b * MOE_NB:(b + 1) * MOE_NB].astype(BF16)
            w1_bf[:, b * MOE_NB:(b + 1) * MOE_NB] = _dot(blk, perm_ref[...]).astype(BF16)
        w2_bf[...] = w2_f[s].astype(BF16)

    @pl.when(i < nt_ref[0])
    def _():
        halves = [_unpack_bf16_pair(x_ref[j]) for j in range(x_ref.shape[0])]
        x = jnp.concatenate([lo for lo, _ in halves] + [hi for _, hi in halves], axis=-1).astype(BF16)
        h = _dot(x, w1_bf[...]) + b1_ref[...]
        acts = []
        for b in range(2 * D_EXPERT // MOE_NB):
            glu = jnp.minimum(h[:, b * MOE_NB:b * MOE_NB + half], SWIGLU_LIMIT)
            lin = jnp.clip(h[:, b * MOE_NB + half:(b + 1) * MOE_NB], -SWIGLU_LIMIT, SWIGLU_LIMIT)
            acts.append((glu * _sigmoid(SWIGLU_ALPHA * glu) * (lin + 1.0)).astype(BF16))
        act = jnp.concatenate(acts, axis=-1)
        y = _dot(act, w2_bf[...]) + b2_ref[...]
        y_ref[...] = _pack_bf16_pair(y[:, :D_MODEL // 2], y[:, D_MODEL // 2:])


def _moe_perm():
    half = MOE_NB // 2
    pm = np.zeros((MOE_NB, MOE_NB), np.float32)
    pm[2 * np.arange(half), np.arange(half)] = 1.0
    pm[2 * np.arange(half) + 1, half + np.arange(half)] = 1.0
    return jnp.asarray(pm, BF16)


def _moe_call(x_sorted, sched, w1, b1p, w2, b2, l):
    tm = MOE_TM
    emap = lambda i, te, *_: (l, te[i], 0, 0)
    tile = lambda i, nt: jnp.minimum(i, nt[0] - 1)
    grid_spec = pltpu.PrefetchScalarGridSpec(
        num_scalar_prefetch=6,
        grid=(MOE_TILES,),
        in_specs=[pl.BlockSpec((SC_SLABS, tm, SC_SLAB_W), lambda i, *s: (0, tile(i, s[-1]), 0)),
                  pl.BlockSpec(memory_space=pl.ANY),
                  pl.BlockSpec((None, None, 1, 2 * D_EXPERT), emap),
                  pl.BlockSpec(memory_space=pl.ANY),
                  pl.BlockSpec((None, None, 1, D_MODEL), emap),
                  pl.BlockSpec((MOE_NB, MOE_NB), lambda i, *_: (0, 0))],
        out_specs=pl.BlockSpec((tm, D_MODEL // 2), lambda i, *s: (tile(i, s[-1]), 0)),
        scratch_shapes=[pltpu.VMEM((MOE_SLOTS, D_MODEL, 2 * D_EXPERT), F32),
                        pltpu.VMEM((MOE_SLOTS, D_EXPERT, D_MODEL), F32),
                        pltpu.VMEM((D_MODEL, 2 * D_EXPERT), BF16), pltpu.VMEM((D_EXPERT, D_MODEL), BF16),
                        pltpu.SemaphoreType.DMA((2, MOE_SLOTS))],
    )
    return pl.pallas_call(
        functools.partial(_moe_kernel, layer=l),
        grid_spec=grid_spec,
        out_shape=jax.ShapeDtypeStruct((MOE_TILES * tm, D_MODEL // 2), F32),
        compiler_params=_cparams(("arbitrary",)),
        name="moe",
    )(*sched, x_sorted, w1, b1p, w2, b2, _moe_perm())


def _route(idx, rank, counts):
    tm = MOE_TM
    tiles_e = (counts + tm - 1) // tm
    eid = np.arange(N_EXPERTS, dtype=np.int32)
    earlier = (eid[None, :] <= eid[:, None]).astype(np.int32)
    tile_end = jnp.sum(earlier * tiles_e[None, :], axis=1)
    tile_start = tile_end - tiles_e
    lookup = lambda table, keys: jnp.sum(jnp.where(keys[..., None] == eid, table, 0), axis=-1)
    pos_t = lookup(tile_start, idx.T) * tm + rank.T
    n_used = tile_end[N_EXPERTS - 1]
    tile_ids = jnp.arange(MOE_TILES, dtype=jnp.int32)
    tid = jnp.minimum(tile_ids, n_used - 1)
    tile_expert = jnp.sum((tile_end[None, :] <= tid[:, None]).astype(jnp.int32), axis=1)
    tile_first = ((tile_ids == lookup(tile_start, tile_expert)) & (tile_ids < n_used)).astype(jnp.int32)
    has_rows = (tiles_e > 0).astype(jnp.int32)
    slot_e = (jnp.sum(earlier * has_rows[None, :], axis=1) - 1) % MOE_SLOTS
    later = jnp.where((eid[None, :] > eid[:, None]) & (has_rows[None, :] > 0), eid[None, :], N_EXPERTS)
    next_e = jnp.min(later, axis=1)
    next2_e = jnp.where(next_e >= N_EXPERTS, N_EXPERTS, lookup(next_e, jnp.minimum(next_e, N_EXPERTS - 1)))
    to_id = lambda e: jnp.where(e >= N_EXPERTS, -1, e).astype(jnp.int32)
    sched = (tile_expert, tile_first, lookup(slot_e, tile_expert).astype(jnp.int32),
             lookup(to_id(next_e), tile_expert).astype(jnp.int32),
             lookup(to_id(next2_e), tile_expert).astype(jnp.int32), n_used.reshape(1))
    return pos_t, sched


SC_WINDOW = 128
SC_SLABS = 2
SC_SLAB_W = D_MODEL // 2 // SC_SLABS


def _dispatch_rows(u2, pos_t):
    mesh = plsc.VectorSubcoreMesh(core_axis_name="core", subcore_axis_name="subcore")

    @functools.partial(pl.kernel, mesh=mesh, scratch_types=[],
                       out_type=jax.ShapeDtypeStruct((SC_SLABS, MOE_TILES * MOE_TM, SC_SLAB_W), F32))
    def dispatch(x_hbm, i_hbm, o_hbm):
        for j in range(SC_SLABS):
            def body(x_vmem, i_vmem, j=j):
                for k in range(TOP_K):
                    pltpu.sync_copy(x_vmem, o_hbm.at[j].at[i_vmem.at[k]])

            pltpu.emit_pipeline(
                body,
                grid=(N_TOK // SC_WINDOW,),
                in_specs=[pl.BlockSpec((SC_WINDOW, SC_SLAB_W), index_map=lambda i: (i, 0)),
                          pl.BlockSpec((TOP_K, SC_WINDOW), index_map=lambda i: (0, i))],
                out_specs=[],
                core_axis_name=("core", "subcore"),
                dimension_semantics=(pltpu.PARALLEL,),
            )(x_hbm.at[j], i_hbm)

    return dispatch(u2, pos_t)


def _final_kernel(yg_ref, wts_ref, x1_ref, mod_ref, modn_ref, lng_ref, lnb_ref, o_ref, un_ref):
    wts = wts_ref[...]
    ffn_lo = ffn_hi = None
    for k in range(TOP_K):
        lo, hi = _unpack_bf16_pair(yg_ref[k])
        w = wts[:, k:k + 1]
        ffn_lo = w * lo if ffn_lo is None else ffn_lo + w * lo
        ffn_hi = w * hi if ffn_hi is None else ffn_hi + w * hi
    ffn = jnp.concatenate([ffn_lo, ffn_hi], axis=-1)
    x2 = _layer_norm(DN_ALPHA * x1_ref[...] + mod_ref[5:6, :] * ffn, lng_ref[...], lnb_ref[...])
    o_ref[...] = x2
    un_ref[...] = (x2 * (1.0 + modn_ref[1:2, :]) + modn_ref[0:1, :]).astype(BF16)


def _final_call(yg, wts, x1, mods, ln_g, ln_b, l):
    tm = 256
    ln = min(l + 1, DEPTH - 1)
    const = lambda shape: pl.BlockSpec(shape, lambda i: (0,) * len(shape))
    modspec = lambda lyr: pl.BlockSpec((None, None, 6, D_MODEL), lambda i: (lyr, _cond_of_row_tile(i, tm), 0, 0))
    return pl.pallas_call(
        _final_kernel,
        grid=(N_TOK // tm,),
        in_specs=[pl.BlockSpec((TOP_K, tm, D_MODEL // 2), lambda i: (0, i, 0)),
                  pl.BlockSpec((tm, LANES), lambda i: (i, 0)),
                  pl.BlockSpec((tm, D_MODEL), lambda i: (i, 0)),
                  modspec(l), modspec(ln),
                  const((1, D_MODEL)), const((1, D_MODEL))],
        out_specs=[pl.BlockSpec((tm, D_MODEL), lambda i: (i, 0)), pl.BlockSpec((tm, D_MODEL), lambda i: (i, 0))],
        out_shape=[jax.ShapeDtypeStruct((N_TOK, D_MODEL), F32), jax.ShapeDtypeStruct((N_TOK, D_MODEL), BF16)],
        compiler_params=_cparams(("arbitrary",)),
        name="final",
    )(yg, wts, x1, mods, mods, ln_g.reshape(1, D_MODEL), ln_b.reshape(1, D_MODEL))


def kernel(x_prompt, x_sample, cache_attn_k, cache_attn_v, state_rglru, state_hgrn, c, c_ctx, w_ada, b_ada, w_in, da_lambda, da_subln, rg_conv_w, rg_conv_b, rg_gate_w, rg_gate_b, rg_lambda, hg_lb, hg_norm, w_branch, w_out, ln1_g, ln1_b, router_w, router_b, w1, b1, w2, b2, ln2_g, ln2_b):
    p = dict(hg_norm=hg_norm, w_branch=w_branch, w_out=w_out, ln1_g=ln1_g, ln1_b=ln1_b,
             router_w=router_w, router_b=router_b)

    x = jnp.concatenate([x_prompt.reshape(N_CTX, D_MODEL), x_sample.reshape(N_LAT, D_MODEL)], axis=0)
    cond = jnp.concatenate([c_ctx[None, :], c, jnp.zeros((N_COND - 1 - DEC_BATCH, D_MODEL), F32)], axis=0)
    mods = _adaln_all(cond, w_ada, b_ada).reshape(DEPTH, N_COND, 6, D_MODEL)

    pr = jax.nn.softmax(hg_lb.astype(F32), axis=0)
    lbs = jnp.cumsum(pr, axis=0) - pr[0]
    dl = da_lambda.astype(F32)
    lam_all = jnp.exp(jnp.sum(dl[:, 0] * dl[:, 1], -1)) - jnp.exp(jnp.sum(dl[:, 2] * dl[:, 3], -1))

    cache_k = cache_attn_k.reshape(DEC_BATCH, DEPTH, PAST_LEN, BRANCH_W)
    cache_v = cache_attn_v.reshape(DEC_BATCH, DEPTH, PAST_LEN, BRANCH_W)

    b1p = b1.reshape(DEPTH, N_EXPERTS, 2 * D_EXPERT // MOE_NB, MOE_NB // 2, 2)
    b1p = jnp.swapaxes(b1p, -1, -2).reshape(DEPTH, N_EXPERTS, 1, 2 * D_EXPERT)
    b2r = b2.reshape(DEPTH, N_EXPERTS, 1, D_MODEL)

    ks, vs, rgs, hgs = [], [], [], []
    u = _modulate(x, mods, 0)
    for l in range(DEPTH):
        lambda_init = 0.8 - 0.6 * math.exp(-0.3 * l)
        h = _in_proj(u, w_in, l)
        ks.append(h[:N_CTX, BRANCH_W:2 * BRANCH_W].astype(F32).reshape(BATCH, SEQ, DA_HEADS, 2, DA_HEAD_DIM))
        vs.append(h[:N_CTX, 2 * BRANCH_W:3 * BRANCH_W].astype(F32).reshape(BATCH, SEQ, DA_HEADS, DA_V_DIM))

        att = _attention(h, cache_k, cache_v, lam_all[l] + lambda_init, da_subln[l], l, 1.0 - lambda_init)

        wg, bg = _rg_gate_dense(rg_gate_w[l], rg_gate_b[l])
        rg_c, hl_c = _rglru_call(h, rg_conv_w[l], rg_conv_b[l], wg, bg, rg_lambda[l], None, BATCH, SEQ, 0)
        rg_l, _ = _rglru_call(h, rg_conv_w[l], rg_conv_b[l], wg, bg, rg_lambda[l], state_rglru[:, l],
                              DEC_BATCH, DEC_SEQ, N_CTX // DEC_SEQ)
        rgs.append(hl_c)

        of_c, ob_c, sl_c = _hgrn_call(h, lbs[l], None, l, BATCH, SEQ, 0)
        of_l, ob_l, _ = _hgrn_call(h, lbs[l], state_hgrn, l, DEC_BATCH, DEC_SEQ, N_CTX)
        hgs.append(sl_c)

        x1, u2, idx, rank, wts, cnt = _merge_call(att, (rg_c, rg_l), (of_c, of_l), (ob_c, ob_l), h, x, mods, p, l)

        pos_t, sched = _route(idx[:, :TOP_K], rank[:, :TOP_K], cnt[0, :N_EXPERTS])
        x_sorted = _dispatch_rows(u2, pos_t)
        y_sorted = _moe_call(x_sorted, sched, w1, b1p, w2, b2r, l)
        yg = y_sorted.at[pos_t.reshape(-1)].get(mode='promise_in_bounds').reshape(TOP_K, N_TOK, D_MODEL // 2)
        x, u = _final_call(yg, wts, x1, mods, ln2_g[l], ln2_b[l], l)

    y_prompt = x[:N_CTX].reshape(BATCH, SEQ, D_MODEL)
    y_sample = x[N_CTX:].reshape(DEC_BATCH, DEC_SEQ, D_MODEL)
    return (y_prompt, y_sample, jnp.stack(ks, axis=1), jnp.stack(vs, axis=1),
            jnp.stack(rgs, axis=1), jnp.stack(hgs, axis=1))
```

```python
import functools
import math

import numpy as np
import jax
import jax.numpy as jnp
from jax import lax
from jax.experimental import pallas as pl
from jax.experimental.pallas import tpu as pltpu
from jax.experimental.pallas import tpu_sc as plsc

F32 = jnp.float32
BF16 = jnp.bfloat16
HIGHEST = lax.Precision.HIGHEST

D_MODEL = 1024
BATCH = 16
SEQ = 256
DEPTH = 4
DEC_BATCH = 4
DEC_SEQ = 1024
PAST_LEN = 256
GRID_W = 64
BRANCH_W = 512
DA_HEADS = 4
DA_HEAD_DIM = 64
DA_V_DIM = 128
ROPE_BASE = 10000.0
RG_WIDTH = 512
RG_BLOCKS = 8
RG_BLOCK_W = 64
RG_CONV_W = 4
RG_C = 8.0
HG_HEADS = 4
HG_KEY = 128
HG_VAL = 128
N_EXPERTS = 32
TOP_K = 4
D_EXPERT = 1024
SWIGLU_ALPHA = 1.702
SWIGLU_LIMIT = 7.0
DN_ALPHA = (2 * DEPTH) ** 0.25
NORM_EPS = 1e-5
D_IN = 10 * BRANCH_W + 3 * D_MODEL

N_CTX = BATCH * SEQ
N_LAT = DEC_BATCH * DEC_SEQ
N_TOK = N_CTX + N_LAT
N_COND = 8

LANES = 128
VMEM_LIMIT = 56 * 1024 * 1024

HG_CHUNK = 128
HG_LEVELS = (1, 2, 4, 8, 16, 32, 64)
MOE_TM = 256
MOE_TILES = (N_TOK * TOP_K) // MOE_TM + N_EXPERTS


def _cparams(sem):
    return pltpu.CompilerParams(dimension_semantics=sem, vmem_limit_bytes=VMEM_LIMIT)


def _sigmoid(x):
    return 0.5 * jnp.tanh(0.5 * x) + 0.5


def _pack_bf16_pair(lo, hi):
    lo_bits = pltpu.bitcast(lo.astype(BF16).astype(F32), jnp.uint32) >> 16
    hi_bits = pltpu.bitcast(hi.astype(BF16).astype(F32), jnp.uint32) & jnp.uint32(0xFFFF0000)
    return pltpu.bitcast(hi_bits | lo_bits, F32)


def _unpack_bf16_pair(words):
    bits = pltpu.bitcast(words, jnp.uint32)
    return pltpu.bitcast(bits << 16, F32), pltpu.bitcast(bits & jnp.uint32(0xFFFF0000), F32)


def _dot(a, b):
    return jnp.dot(a, b, preferred_element_type=F32)


def _dot_nt(a, b):
    return lax.dot_general(a, b, (((1,), (1,)), ((), ())), preferred_element_type=F32)


def _cond_of_row_tile(i, tm):
    r = i * tm
    return jnp.where(r < N_CTX, 0, 1 + (r - N_CTX) // DEC_SEQ)


def _ada_kernel(c_ref, w_ref, b_ref, o_ref):
    c = c_ref[...]
    s = c * _sigmoid(c)
    o_ref[0] = jnp.dot(s, w_ref[0], precision=HIGHEST, preferred_element_type=F32) + b_ref[0]


def _adaln_all(cond, w_ada, b_ada):
    tn = 1536
    return pl.pallas_call(
        _ada_kernel,
        grid=(DEPTH, 6 * D_MODEL // tn),
        in_specs=[pl.BlockSpec((N_COND, D_MODEL), lambda l, j: (0, 0)),
                  pl.BlockSpec((1, D_MODEL, tn), lambda l, j: (l, 0, j)),
                  pl.BlockSpec((1, 1, tn), lambda l, j: (l, 0, j))],
        out_specs=pl.BlockSpec((1, N_COND, tn), lambda l, j: (l, 0, j)),
        out_shape=jax.ShapeDtypeStruct((DEPTH, N_COND, 6 * D_MODEL), F32),
        compiler_params=_cparams(("arbitrary", "arbitrary")),
        name="adaln",
    )(cond, w_ada, b_ada.reshape(DEPTH, 1, 6 * D_MODEL))


def _modulate_kernel(xc_ref, xl_ref, mod_ref, u_ref, *, ctx_tiles):
    x = jnp.where(pl.program_id(0) < ctx_tiles, xc_ref[...], xl_ref[...])
    u_ref[...] = (x * (1.0 + mod_ref[1:2, :]) + mod_ref[0:1, :]).astype(BF16)


def _modulate(x_ctx, x_lat, mods, l):
    tm = 1024
    ctx_tiles = N_CTX // tm
    return pl.pallas_call(
        functools.partial(_modulate_kernel, ctx_tiles=ctx_tiles),
        grid=(N_TOK // tm,),
        in_specs=[pl.BlockSpec((tm, D_MODEL), lambda i: (jnp.minimum(i, ctx_tiles - 1), 0)),
                  pl.BlockSpec((tm, D_MODEL), lambda i: (jnp.maximum(i - ctx_tiles, 0), 0)),
                  pl.BlockSpec((None, None, 6, D_MODEL), lambda i: (l, _cond_of_row_tile(i, tm), 0, 0))],
        out_specs=pl.BlockSpec((tm, D_MODEL), lambda i: (i, 0)),
        out_shape=jax.ShapeDtypeStruct((N_TOK, D_MODEL), BF16),
        compiler_params=_cparams(("arbitrary",)),
        name="modulate",
    )(x_ctx, x_lat, mods)


def _in_kernel(u_ref, w_ref, o_ref, wbf_ref):
    @pl.when(pl.program_id(1) == 0)
    def _():
        wbf_ref[...] = w_ref[...].astype(BF16)

    o_ref[...] = _dot(u_ref[...], wbf_ref[...]).astype(BF16)


def _in_proj(u, w_in, l):
    tm, tn = 1024, 2048
    return pl.pallas_call(
        _in_kernel,
        grid=(D_IN // tn, N_TOK // tm),
        in_specs=[pl.BlockSpec((tm, D_MODEL), lambda j, i: (i, 0)),
                  pl.BlockSpec((None, D_MODEL, tn), lambda j, i: (l, 0, j))],
        out_specs=pl.BlockSpec((tm, tn), lambda j, i: (i, j)),
        out_shape=jax.ShapeDtypeStruct((N_TOK, D_IN), BF16),
        scratch_shapes=[pltpu.VMEM((D_MODEL, tn), BF16)],
        compiler_params=_cparams(("arbitrary", "arbitrary")),
        name="in_proj",
    )(u, w_in)


def _rope(x, cos, sin_signed):
    lane = lax.broadcasted_iota(jnp.int32, x.shape, 1)
    first = (lane & 31) < 16
    partner = jnp.where(first, pltpu.roll(x, LANES - 16, 1), pltpu.roll(x, 16, 1))
    return x * cos + partner * sin_signed


LOG2E = 1.4426950408889634


def _attn_body(q, keys, vals, lam, subln, out_scale):
    lane = lax.broadcasted_iota(jnp.int32, q.shape, 1)
    qs = q * (DA_HEAD_DIM ** -0.5 * LOG2E)
    acc = None
    for m in range(2):
        in_map = (lane < DA_HEAD_DIM) if m == 0 else (lane >= DA_HEAD_DIM)
        qm = jnp.where(in_map, qs, 0.0).astype(BF16)
        s = [_dot_nt(qm, k) for k in keys]
        mx = s[0].max(axis=-1, keepdims=True)
        for si in s[1:]:
            mx = jnp.maximum(mx, si.max(axis=-1, keepdims=True))
        e = [jnp.exp2(si - mx) for si in s]
        den = e[0].sum(axis=-1, keepdims=True)
        for ei in e[1:]:
            den = den + ei.sum(axis=-1, keepdims=True)
        pv = _dot(e[0].astype(BF16), vals[0])
        for ei, v in zip(e[1:], vals[1:]):
            pv = pv + _dot(ei.astype(BF16), v)
        coef = (1.0 / den) if m == 0 else (-lam[:, 0:1] / den)
        acc = pv * coef if acc is None else acc + pv * coef
    y = acc * lax.rsqrt(jnp.mean(acc * acc, axis=-1, keepdims=True) + NORM_EPS)
    return y * subln * out_scale


def _attn_ctx_kernel(lam_ref, sub_ref, q_ref, k_ref, v_ref, o_ref, *, out_scale):
    for hd in range(DA_HEADS):
        cols = slice(hd * LANES, (hd + 1) * LANES)
        o_ref[:, cols] = _attn_body(q_ref[:, cols].astype(F32), [k_ref[:, cols]], [v_ref[:, cols]],
                                    lam_ref[...], sub_ref[...], out_scale)


def _attn_lat_kernel(lam_ref, sub_ref, q_ref, k_ref, v_ref, kc_ref, vc_ref, cos_ref, sin_ref,
                     o_ref, k_scr, kc_scr, vc_scr, *, out_scale, tq):
    k_scr[...] = _rope(k_ref[...].astype(F32), cos_ref[...], sin_ref[...]).astype(BF16)
    kc_scr[...] = kc_ref[...].astype(BF16)
    vc_scr[...] = vc_ref[...].astype(BF16)

    def q_block(qi, carry):
        rows = pl.ds(pl.multiple_of(qi * tq, tq), tq)
        q = _rope(q_ref[rows, :].astype(F32), cos_ref[rows, :], sin_ref[rows, :])
        o_ref[rows, :] = _attn_body(q, [k_scr[...], kc_scr[...]], [v_ref[...], vc_scr[...]],
                                    lam_ref[...], sub_ref[...], out_scale)
        return carry

    lax.fori_loop(0, DEC_SEQ // tq, q_block, 0)


def _rope_tables():
    t = np.arange(DEC_SEQ)
    row = (t // GRID_W).astype(np.float64)
    col = (t % GRID_W).astype(np.float64)
    d = DA_HEAD_DIM // 2
    inv = ROPE_BASE ** (-np.arange(0, d, 2, dtype=np.float64) / d)
    a_row = row[:, None] * inv[None, :]
    a_col = col[:, None] * inv[None, :]
    cos = np.concatenate([np.cos(a_row), np.cos(a_row), np.cos(a_col), np.cos(a_col)], -1)
    sin = np.concatenate([-np.sin(a_row), np.sin(a_row), -np.sin(a_col), np.sin(a_col)], -1)
    return (jnp.asarray(np.tile(cos, (1, 2)), F32), jnp.asarray(np.tile(sin, (1, 2)), F32))


def _attention(h, cache_k, cache_v, lam_v, sub_v, l, out_scale):
    small = pl.BlockSpec((None, 1, LANES), lambda *a: (l, 0, 0))
    att_ctx = pl.pallas_call(
        functools.partial(_attn_ctx_kernel, out_scale=out_scale),
        grid=(BATCH,),
        in_specs=[small, small,
                  pl.BlockSpec((SEQ, BRANCH_W), lambda b: (b, 0)),
                  pl.BlockSpec((SEQ, BRANCH_W), lambda b: (b, 1)),
                  pl.BlockSpec((SEQ, BRANCH_W), lambda b: (b, 2))],
        out_specs=pl.BlockSpec((SEQ, BRANCH_W), lambda b: (b, 0)),
        out_shape=jax.ShapeDtypeStruct((N_CTX, BRANCH_W), F32),
        compiler_params=_cparams(("arbitrary",)),
        name="attn_ctx",
    )(lam_v, sub_v, h, h, h)

    tq = 256
    cos, sin = _rope_tables()
    row0 = N_CTX // DEC_SEQ
    seq_blk = lambda cb: pl.BlockSpec((DEC_SEQ, LANES), lambda b, hd: (row0 + b, cb + hd))
    cache_blk = pl.BlockSpec((None, None, PAST_LEN, LANES), lambda b, hd: (b, l, 0, hd))
    table = pl.BlockSpec((DEC_SEQ, LANES), lambda b, hd: (0, 0))
    att_lat = pl.pallas_call(
        functools.partial(_attn_lat_kernel, out_scale=out_scale, tq=tq),
        grid=(DEC_BATCH, DA_HEADS),
        in_specs=[small, small, seq_blk(0), seq_blk(4), seq_blk(8), cache_blk, cache_blk, table, table],
        out_specs=pl.BlockSpec((DEC_SEQ, LANES), lambda b, hd: (b, hd)),
        out_shape=jax.ShapeDtypeStruct((N_LAT, BRANCH_W), F32),
        scratch_shapes=[pltpu.VMEM((DEC_SEQ, LANES), BF16), pltpu.VMEM((PAST_LEN, LANES), BF16),
                        pltpu.VMEM((PAST_LEN, LANES), BF16)],
        compiler_params=_cparams(("arbitrary", "arbitrary")),
        name="attn_lat",
    )(lam_v, sub_v, h, h, h, cache_k, cache_v, cos, sin)
    return att_ctx, att_lat


def _gelu_tanh(x):
    return 0.5 * x * (1.0 + jnp.tanh(math.sqrt(2.0 / math.pi) * (x + 0.044715 * (x * x * x))))


def _rg_kernel(*refs, seq, has_h0):
    if has_h0:
        rx_ref, gate_ref, cw_ref, cb_ref, wg_ref, bg_ref, lam_ref, h0_ref = refs[:8]
        rest = refs[8:]
    else:
        rx_ref, gate_ref, cw_ref, cb_ref, wg_ref, bg_ref, lam_ref = refs[:7]
        h0_ref = None
        rest = refs[7:]
    out_ref, hl_ref, a_scr, b_scr, h_scr = rest

    x = rx_ref[...].astype(F32)
    row = lax.broadcasted_iota(jnp.int32, x.shape, 0)
    xr = cb_ref[...] + cw_ref[2:3, :] * x
    for j in (0, 1, 3):
        d = j - RG_CONV_W // 2
        shifted = pltpu.roll(x, (-d) % seq, 0)
        valid = (row + d >= 0) & (row + d < seq)
        xr = xr + cw_ref[j:j + 1, :] * jnp.where(valid, shifted, 0.0)

    g = _dot(xr.astype(BF16), wg_ref[...]) + bg_ref[...]
    for dr in range(2):
        r = _sigmoid(g[:, (2 * dr) * RG_WIDTH:(2 * dr + 1) * RG_WIDTH])
        i = _sigmoid(g[:, (2 * dr + 1) * RG_WIDTH:(2 * dr + 2) * RG_WIDTH])
        lam = lam_ref[dr:dr + 1, :]
        softplus_neg = jnp.maximum(-lam, 0.0) + jnp.log(1.0 + jnp.exp(-jnp.abs(lam)))
        log_a = (-RG_C * softplus_neg) * r
        a = jnp.exp(log_a)
        a_scr[dr] = a
        b_scr[dr] = jnp.sqrt(1.0 - a * a) * i * xr

    if has_h0:
        hf0, hb0 = h0_ref[0:1, :], h0_ref[1:2, :]
    else:
        hf0 = hb0 = jnp.zeros((1, RG_WIDTH), F32)

    def step(t, carry):
        hf, hb = carry
        tb = seq - 1 - t
        hf = a_scr[0, pl.ds(t, 1), :] * hf + b_scr[0, pl.ds(t, 1), :]
        h_scr[0, pl.ds(t, 1), :] = hf
        hb = a_scr[1, pl.ds(tb, 1), :] * hb + b_scr[1, pl.ds(tb, 1), :]
        h_scr[1, pl.ds(tb, 1), :] = hb
        return hf, hb

    hf, hb = lax.fori_loop(0, seq, step, (hf0, hb0), unroll=8)
    hl_ref[0:1, :] = hf
    hl_ref[1:2, :] = hb
    out_ref[...] = (h_scr[0] + h_scr[1]) * _gelu_tanh(gate_ref[...].astype(F32))


def _rglru_call(h, conv_w, conv_b, wg, bg, lam, h0, l, nseq, seq, row_block0):
    has_h0 = h0 is not None
    layer = lambda shape: pl.BlockSpec((None,) + shape, lambda b: (l,) + (0,) * len(shape))
    in_specs = [pl.BlockSpec((seq, RG_WIDTH), lambda b: (row_block0 + b, 3)),
                pl.BlockSpec((seq, RG_WIDTH), lambda b: (row_block0 + b, 4)),
                layer((RG_CONV_W, RG_WIDTH)), layer((1, RG_WIDTH)),
                layer((RG_WIDTH, 4 * RG_WIDTH)), layer((1, 4 * RG_WIDTH)), layer((2, RG_WIDTH))]
    args = [h, h, conv_w, conv_b, wg, bg, lam]
    if has_h0:
        in_specs.append(pl.BlockSpec((None, None, 2, RG_WIDTH), lambda b: (b, l, 0, 0)))
        args.append(h0)
    return pl.pallas_call(
        functools.partial(_rg_kernel, seq=seq, has_h0=has_h0),
        grid=(nseq,),
        in_specs=in_specs,
        out_specs=[pl.BlockSpec((seq, RG_WIDTH), lambda b: (b, 0)),
                   pl.BlockSpec((None, 2, RG_WIDTH), lambda b: (b, 0, 0))],
        out_shape=[jax.ShapeDtypeStruct((nseq * seq, RG_WIDTH), F32),
                   jax.ShapeDtypeStruct((nseq, 2, RG_WIDTH), F32)],
        scratch_shapes=[pltpu.VMEM((2, seq, RG_WIDTH), F32)] * 3,
        compiler_params=_cparams(("arbitrary",)),
        name="rglru_lat" if has_h0 else "rglru_ctx",
    )(*args)


def _rg_gate_dense(gate_w, gate_b):
    eye = jnp.eye(RG_BLOCKS, dtype=F32)
    dense = jnp.einsum('ldgncf,nm->ldgncmf', gate_w, eye).reshape(DEPTH, 2, 2, RG_WIDTH, RG_WIDTH)
    wg = jnp.transpose(dense, (0, 3, 1, 2, 4)).reshape(DEPTH, RG_WIDTH, 4 * RG_WIDTH)
    return wg.astype(BF16), gate_b.reshape(DEPTH, 1, 4 * RG_WIDTH)


def _hg_masks():
    c = HG_CHUNK
    t = np.arange(c)[:, None]
    s = np.arange(c)[None, :]
    lvl = np.zeros((2, len(HG_LEVELS), c, c), np.float32)
    for n, h in enumerate(HG_LEVELS):
        same = (t // (2 * h)) == (s // (2 * h))
        lvl[0, n] = same & ((t // h) % 2 == 1) & ((s // h) % 2 == 0)
        lvl[1, n] = same & ((t // h) % 2 == 0) & ((s // h) % 2 == 1)
    return jnp.asarray(lvl)


def _hg_chunk(q_raw, z, v, lb, st, lvl, rev):
    c = HG_CHUNK
    q = q_raw * _sigmoid(q_raw)
    sig = _sigmoid(z)
    f = lb + (1.0 - lb) * sig
    kk = (1.0 - lb) * (1.0 - sig)

    rowk = lax.broadcasted_iota(jnp.int32, (c, HG_KEY), 0)
    pos = rowk if not rev else (c - 1) - rowk
    earlier = lambda x, d: pltpu.roll(x, d if not rev else c - d, 0)
    later = lambda x, d: pltpu.roll(x, c - d if not rev else d, 0)

    def block_row(x, n, p):
        r = p if not rev else n - 1 - p
        x3 = x.reshape(c // n, n, HG_KEY)
        return jnp.broadcast_to(x3[:, r:r + 1, :], x3.shape).reshape(c, HG_KEY)

    pair = f * earlier(f, 1)
    f_next = later(f, 1)
    p4 = pos & 3
    a = {1: f, 2: jnp.where((pos & 1) == 1, pair, f)}
    b = {2: jnp.where((pos & 1) == 0, f_next, 1.0)}
    a[4] = a[2] * jnp.where(p4 == 2, earlier(pair, 1), jnp.where(p4 == 3, earlier(pair, 2), 1.0))
    b[4] = jnp.where(p4 == 3, 1.0, jnp.where(p4 == 2, f_next,
                                             jnp.where(p4 == 1, later(pair, 2), f_next * later(pair, 3))))
    h = 4
    while h < c:
        in_later_half = (pos & h) != 0
        a[2 * h] = jnp.where(in_later_half, a[h] * block_row(a[h], 2 * h, h - 1), a[h])
        b[2 * h] = jnp.where(in_later_half, b[h], b[h] * block_row(a[h], 2 * h, 2 * h - 1))
        h *= 2

    row = lax.broadcasted_iota(jnp.int32, (c, c), 0)
    col = lax.broadcasted_iota(jnp.int32, (c, c), 1)
    scores = jnp.where(row == col, jnp.sum(q * kk, axis=-1, keepdims=True), 0.0)
    for n, h in enumerate(HG_LEVELS):
        qa = (q * a[h]).astype(BF16)
        kb = (kk if h == 1 else kk * b[h]).astype(BF16)
        scores = scores + lvl[n] * _dot_nt(qa, kb)

    vb = v.astype(BF16)
    o = _dot(scores.astype(BF16), vb)
    o = o + _dot_nt((q * a[c]).astype(BF16), st.astype(BF16))
    total = a[c][c - 1:c, :] if not rev else a[c][0:1, :]
    st_new = st * total + _dot(v.T.astype(BF16), (kk * b[c]).astype(BF16))
    return o, st_new


def _hg_kernel(*refs, has_s0, nchunk):
    if has_s0:
        (qf_ref, qb_ref, zf_ref, zb_ref, vf_ref, vb_ref, lb_ref, lvl_ref, s0_ref,
         of_ref, ob_ref, sfin_ref, st_scr) = refs
    else:
        (qf_ref, qb_ref, zf_ref, zb_ref, vf_ref, vb_ref, lb_ref, lvl_ref,
         of_ref, ob_ref, sfin_ref, st_scr) = refs
        s0_ref = None
    ci = pl.program_id(1)

    @pl.when(ci == 0)
    def _():
        for dr in range(2):
            for hd in range(HG_HEADS):
                st_scr[dr, hd] = s0_ref[dr, hd].T if has_s0 else jnp.zeros((HG_VAL, HG_KEY), F32)

    last = ci == nchunk - 1
    for hd in range(HG_HEADS):
        cols = slice(hd * LANES, (hd + 1) * LANES)
        ld = lambda ref: ref[:, cols].astype(F32)
        o_f, st_f = _hg_chunk(ld(qf_ref), ld(zf_ref), ld(vf_ref), lb_ref[0:1, cols], st_scr[0, hd],
                              lvl_ref[0], False)
        o_b, st_b = _hg_chunk(ld(qb_ref), ld(zb_ref), ld(vb_ref), lb_ref[1:2, cols], st_scr[1, hd],
                              lvl_ref[1], True)
        of_ref[:, cols] = o_f
        ob_ref[:, cols] = o_b
        st_scr[0, hd] = st_f
        st_scr[1, hd] = st_b

        @pl.when(last)
        def _():
            sfin_ref[0, hd] = st_f.T
            sfin_ref[1, hd] = st_b.T


def _hgrn_call(h, lbs_l, s0, l, nseq, seq, row0):
    c = HG_CHUNK
    nchunk = seq // c
    has_s0 = s0 is not None
    lvl = _hg_masks()
    rf = lambda b, ci: row0 // c + b * nchunk + ci
    rb = lambda b, ci: row0 // c + b * nchunk + (nchunk - 1 - ci)
    blk = lambda rfun, cb: pl.BlockSpec((c, BRANCH_W), lambda b, ci: (rfun(b, ci), cb))
    const = lambda shape: pl.BlockSpec(shape, lambda b, ci: (0,) * len(shape))
    in_specs = [blk(rf, 5), blk(rb, 5), blk(rf, 6), blk(rb, 7), blk(rf, 8), blk(rb, 8),
                pl.BlockSpec((None, 2, BRANCH_W), lambda b, ci: (l, 0, 0)), const((2, len(HG_LEVELS), c, c))]
    args = [h, h, h, h, h, h, lbs_l, lvl]
    if has_s0:
        in_specs.append(pl.BlockSpec((None, None, 2, HG_HEADS, HG_KEY, HG_VAL), lambda b, ci: (b, l, 0, 0, 0, 0)))
        args.append(s0)
    return pl.pallas_call(
        functools.partial(_hg_kernel, has_s0=has_s0, nchunk=nchunk),
        grid=(nseq, nchunk),
        in_specs=in_specs,
        out_specs=[pl.BlockSpec((c, BRANCH_W), lambda b, ci: (b * nchunk + ci, 0)),
                   pl.BlockSpec((c, BRANCH_W), lambda b, ci: (b * nchunk + (nchunk - 1 - ci), 0)),
                   pl.BlockSpec((None, 2, HG_HEADS, HG_KEY, HG_VAL), lambda b, ci: (b, 0, 0, 0, 0))],
        out_shape=[jax.ShapeDtypeStruct((nseq * seq, BRANCH_W), F32),
                   jax.ShapeDtypeStruct((nseq * seq, BRANCH_W), F32),
                   jax.ShapeDtypeStruct((nseq, 2, HG_HEADS, HG_KEY, HG_VAL), F32)],
        scratch_shapes=[pltpu.VMEM((2, HG_HEADS, HG_VAL, HG_KEY), F32)],
        compiler_params=_cparams(("arbitrary", "arbitrary")),
        name="hgrn_lat" if has_s0 else "hgrn_ctx",
    )(*args)


def _layer_norm(y, g, b):
    mu = jnp.mean(y, axis=-1, keepdims=True)
    yc = y - mu
    var = jnp.mean(yc * yc, axis=-1, keepdims=True)
    return yc * lax.rsqrt(var + NORM_EPS) * g + b


def _merge_kernel(attc_ref, attl_ref, rgc_ref, rgl_ref, ofc_ref, ofl_ref, obc_ref, obl_ref,
                  hgate_ref, mg0_ref, mg1_ref, mg2_ref, xc_ref, xl_ref, mod_ref,
                  hgn_ref, wbr_ref, wout_ref, lng_ref, lnb_ref, rw_ref, rb_ref, tri_ref,
                  x1_ref, u2_ref, idx_ref, rank_ref, wts_ref, cnt_ref, wbr_bf, wout_bf, cnt_scr, *, ctx_tiles):
    @pl.when(pl.program_id(0) == 0)
    def _():
        wbr_bf[...] = wbr_ref[...].astype(BF16)
        wout_bf[...] = wout_ref[...].astype(BF16)
        cnt_scr[...] = jnp.zeros(cnt_scr.shape, F32)

    is_ctx = pl.program_id(0) < ctx_tiles
    pick = lambda c_ref, l_ref: jnp.where(is_ctx, c_ref[...], l_ref[...])
    att = pick(attc_ref, attl_ref)
    rg = pick(rgc_ref, rgl_ref)
    o = pick(ofc_ref, ofl_ref) + pick(obc_ref, obl_ref)
    hgate = hgate_ref[...].astype(F32)
    hg_parts = []
    for hd in range(HG_HEADS):
        oh = o[:, hd * HG_VAL:(hd + 1) * HG_VAL]
        gh = hgate[:, hd * HG_VAL:(hd + 1) * HG_VAL]
        yh = oh * lax.rsqrt(jnp.mean(oh * oh, axis=-1, keepdims=True) + NORM_EPS) * hgn_ref[...]
        hg_parts.append(yh * (gh * _sigmoid(gh)))
    hg = jnp.concatenate(hg_parts, axis=-1)

    proj = _sigmoid(mg0_ref[...].astype(F32)) * _dot(att.astype(BF16), wbr_bf[0])
    proj = proj + _sigmoid(mg1_ref[...].astype(F32)) * _dot(rg.astype(BF16), wbr_bf[1])
    proj = proj + _sigmoid(mg2_ref[...].astype(F32)) * _dot(hg.astype(BF16), wbr_bf[2])
    mix = _dot(proj.astype(BF16), wout_bf[...])

    x1 = _layer_norm(DN_ALPHA * pick(xc_ref, xl_ref) + mod_ref[2:3, :] * mix, lng_ref[...], lnb_ref[...])
    x1_ref[...] = x1
    u2 = x1 * (1.0 + mod_ref[4:5, :]) + mod_ref[3:4, :]
    packed = _pack_bf16_pair(u2[:, :D_MODEL // 2], u2[:, D_MODEL // 2:])
    for j in range(u2_ref.shape[0]):
        u2_ref[j] = packed[:, j * u2_ref.shape[2]:(j + 1) * u2_ref.shape[2]]

    u2_hi = u2.astype(BF16)
    u2_lo = (u2 - u2_hi.astype(F32)).astype(BF16)
    logits = (_dot(u2_hi, rw_ref[0]) + _dot(u2_lo, rw_ref[0]) + _dot(u2_hi, rw_ref[1])) + rb_ref[...]
    lane = lax.broadcasted_iota(jnp.int32, logits.shape, 1).astype(F32)
    idx_out = jnp.zeros(logits.shape, F32)
    wts_out = jnp.zeros(logits.shape, F32)
    chosen = jnp.zeros(logits.shape, F32)
    sels = []
    top0 = None
    den = None
    for k in range(TOP_K):
        m = jnp.max(logits, axis=-1, keepdims=True)
        sel = jnp.min(jnp.where(logits == m, lane, float(LANES)), axis=-1, keepdims=True)
        sels.append(sel)
        if k == 0:
            top0 = m
        e = jnp.exp(m - top0)
        den = e if den is None else den + e
        idx_out = jnp.where(lane == k, sel, idx_out)
        wts_out = jnp.where(lane == k, e, wts_out)
        chosen = jnp.where(lane == sel, 1.0, chosen)
        logits = jnp.where(lane == sel, -jnp.inf, logits)
    idx_ref[...] = idx_out.astype(jnp.int32)
    wts_ref[...] = wts_out * (1.0 / den)

    before = _dot(tri_ref[...], chosen.astype(BF16)) + cnt_scr[...]
    rank_out = jnp.zeros(logits.shape, F32)
    for k in range(TOP_K):
        rk = jnp.sum(jnp.where(lane == sels[k], before, 0.0), axis=-1, keepdims=True)
        rank_out = jnp.where(lane == k, rk, rank_out)
    rank_ref[...] = rank_out.astype(jnp.int32)
    cnt_scr[...] = cnt_scr[...] + jnp.sum(chosen, axis=0, keepdims=True)
    cnt_ref[...] = jnp.broadcast_to(cnt_scr[...], cnt_ref.shape).astype(jnp.int32)


def _router_params(router_w, router_b):
    rw = jnp.zeros((DEPTH, D_MODEL, LANES), F32).at[:, :, :N_EXPERTS].set(router_w)
    rb = jnp.full((DEPTH, 1, LANES), -1e30, F32).at[:, 0, :N_EXPERTS].set(router_b)
    rw_hi = rw.astype(BF16)
    return jnp.stack([rw_hi, (rw - rw_hi.astype(F32)).astype(BF16)], axis=1), rb


def _merge_call(att, rg, o_f, o_b, h, x, mods, p, l):
    tm = 256
    ctx_tiles = N_CTX // tm
    rowb = lambda w, cb: pl.BlockSpec((tm, w), lambda i: (i, cb))
    ctxb = lambda w: pl.BlockSpec((tm, w), lambda i: (jnp.minimum(i, ctx_tiles - 1), 0))
    latb = lambda w, base=0: pl.BlockSpec((tm, w), lambda i: (jnp.maximum(i - ctx_tiles, 0) + base, 0))
    const = lambda shape: pl.BlockSpec(shape, lambda i: (0,) * len(shape))
    layer = lambda shape: pl.BlockSpec((None,) + shape, lambda i: (l,) + (0,) * len(shape))
    bw = BRANCH_W
    tri = jnp.asarray(np.tril(np.ones((tm, tm), np.float32), -1), BF16)
    return pl.pallas_call(
        functools.partial(_merge_kernel, ctx_tiles=ctx_tiles),
        grid=(N_TOK // tm,),
        in_specs=[ctxb(bw), latb(bw), ctxb(bw), latb(bw), ctxb(bw), latb(bw), ctxb(bw), latb(bw),
                  rowb(BRANCH_W, 9), rowb(D_MODEL, 5), rowb(D_MODEL, 6), rowb(D_MODEL, 7),
                  ctxb(D_MODEL), latb(D_MODEL, x[2]),
                  pl.BlockSpec((None, None, 6, D_MODEL), lambda i: (l, _cond_of_row_tile(i, tm), 0, 0)),
                  layer((1, HG_VAL)), layer((3, BRANCH_W, D_MODEL)), layer((D_MODEL, D_MODEL)),
                  layer((1, D_MODEL)), layer((1, D_MODEL)), layer((2, D_MODEL, LANES)), layer((1, LANES)),
                  const((tm, tm))],
        out_specs=[rowb(D_MODEL, 0), pl.BlockSpec((SC_SLABS, tm, SC_SLAB_W), lambda i: (0, i, 0)),
                   rowb(LANES, 0), rowb(LANES, 0), rowb(LANES, 0), const((8, LANES))],
        out_shape=[jax.ShapeDtypeStruct((N_TOK, D_MODEL), F32),
                   jax.ShapeDtypeStruct((SC_SLABS, N_TOK, SC_SLAB_W), F32),
                   jax.ShapeDtypeStruct((N_TOK, LANES), jnp.int32),
                   jax.ShapeDtypeStruct((N_TOK, LANES), jnp.int32),
                   jax.ShapeDtypeStruct((N_TOK, LANES), F32),
                   jax.ShapeDtypeStruct((8, LANES), jnp.int32)],
        scratch_shapes=[pltpu.VMEM((3, BRANCH_W, D_MODEL), BF16), pltpu.VMEM((D_MODEL, D_MODEL), BF16),
                        pltpu.VMEM((1, LANES), F32)],
        compiler_params=_cparams(("arbitrary",)),
        name="merge",
    )(att[0], att[1], rg[0], rg[1], o_f[0], o_f[1], o_b[0], o_b[1], h, h, h, h, x[0], x[1], mods,
      p['hg_norm'], p['w_branch'], p['w_out'], p['ln1_g'], p['ln1_b'], p['rw'], p['rb'], tri)


MOE_NB = 256


MOE_SLOTS = 3


def _moe_kernel(te_ref, tf_ref, tsl_ref, tnx_ref, tn2_ref, nt_ref, x_ref, w1_hbm, b1_ref, w2_hbm, b2_ref, perm_ref,
                y_ref, w1_f, w2_f, w1_bf, w2_bf, sem, *, layer):
    i = pl.program_id(0)
    half = MOE_NB // 2

    def weight_copies(e, s):
        return (pltpu.make_async_copy(w1_hbm.at[layer, e], w1_f.at[s], sem.at[0, s]),
                pltpu.make_async_copy(w2_hbm.at[layer, e], w2_f.at[s], sem.at[1, s]))

    @pl.when(i == 0)
    def _():
        for cp in weight_copies(te_ref[0], 0):
            cp.start()

        @pl.when(tnx_ref[0] >= 0)
        def _():
            for cp in weight_copies(tnx_ref[0], 1):
                cp.start()

    @pl.when(tf_ref[i] == 1)
    def _():
        s = tsl_ref[i]
        for cp in weight_copies(te_ref[i], s):
            cp.wait()
        ahead = tn2_ref[i]

        @pl.when(ahead >= 0)
        def _():
            for cp in weight_copies(ahead, jnp.where(s == 0, MOE_SLOTS - 1, s - 1)):
                cp.start()

        for b in range(2 * D_EXPERT // MOE_NB):
            blk = w1_f[s, :, b * MOE_NB:(b + 1) * MOE_NB].astype(BF16)
            w1_bf[:, b * MOE_NB:(b + 1) * MOE_NB] = _dot(blk, perm_ref[...]).astype(BF16)
        w2_bf[...] = w2_f[s].astype(BF16)

    @pl.when(i < nt_ref[0])
    def _():
        halves = [_unpack_bf16_pair(x_ref[j]) for j in range(x_ref.shape[0])]
        x = jnp.concatenate([lo for lo, _ in halves] + [hi for _, hi in halves], axis=-1).astype(BF16)
        h = _dot(x, w1_bf[...]) + b1_ref[...]
        acts = []
        for b in range(2 * D_EXPERT // MOE_NB):
            glu = jnp.minimum(h[:, b * MOE_NB:b * MOE_NB + half], SWIGLU_LIMIT)
            lin = jnp.clip(h[:, b * MOE_NB + half:(b + 1) * MOE_NB], -SWIGLU_LIMIT, SWIGLU_LIMIT)
            acts.append((glu * _sigmoid(SWIGLU_ALPHA * glu) * (lin + 1.0)).astype(BF16))
        act = jnp.concatenate(acts, axis=-1)
        y = _dot(act, w2_bf[...]) + b2_ref[...]
        y_ref[...] = _pack_bf16_pair(y[:, :D_MODEL // 2], y[:, D_MODEL // 2:])


def _moe_perm():
    half = MOE_NB // 2
    pm = np.zeros((MOE_NB, MOE_NB), np.float32)
    pm[2 * np.arange(half), np.arange(half)] = 1.0
    pm[2 * np.arange(half) + 1, half + np.arange(half)] = 1.0
    return jnp.asarray(pm, BF16)


def _moe_call(x_sorted, sched, w1, b1p, w2, b2, l):
    tm = MOE_TM
    emap = lambda i, te, *_: (l, te[i], 0, 0)
    tile = lambda i, nt: jnp.minimum(i, nt[0] - 1)
    grid_spec = pltpu.PrefetchScalarGridSpec(
        num_scalar_prefetch=6,
        grid=(MOE_TILES,),
        in_specs=[pl.BlockSpec((SC_SLABS, tm, SC_SLAB_W), lambda i, *s: (0, tile(i, s[-1]), 0)),
                  pl.BlockSpec(memory_space=pl.ANY),
                  pl.BlockSpec((None, None, 1, 2 * D_EXPERT), emap),
                  pl.BlockSpec(memory_space=pl.ANY),
                  pl.BlockSpec((None, None, 1, D_MODEL), emap),
                  pl.BlockSpec((MOE_NB, MOE_NB), lambda i, *_: (0, 0))],
        out_specs=pl.BlockSpec((tm, D_MODEL // 2), lambda i, *s: (tile(i, s[-1]), 0)),
        scratch_shapes=[pltpu.VMEM((MOE_SLOTS, D_MODEL, 2 * D_EXPERT), F32),
                        pltpu.VMEM((MOE_SLOTS, D_EXPERT, D_MODEL), F32),
                        pltpu.VMEM((D_MODEL, 2 * D_EXPERT), BF16), pltpu.VMEM((D_EXPERT, D_MODEL), BF16),
                        pltpu.SemaphoreType.DMA((2, MOE_SLOTS))],
    )
    return pl.pallas_call(
        functools.partial(_moe_kernel, layer=l),
        grid_spec=grid_spec,
        out_shape=jax.ShapeDtypeStruct((MOE_TILES * tm, D_MODEL // 2), F32),
        compiler_params=_cparams(("arbitrary",)),
        name="moe",
    )(*sched, x_sorted, w1, b1p, w2, b2, _moe_perm())


def _route(idx, rank, counts):
    tm = MOE_TM
    tiles_e = (counts + tm - 1) // tm
    eid = np.arange(N_EXPERTS, dtype=np.int32)
    earlier = (eid[None, :] <= eid[:, None]).astype(np.int32)
    tile_end = jnp.sum(earlier * tiles_e[None, :], axis=1)
    tile_start = tile_end - tiles_e
    lookup = lambda table, keys: jnp.sum(jnp.where(keys[..., None] == eid, table, 0), axis=-1)
    pos_t = lookup(tile_start, idx.T) * tm + rank.T
    n_used = tile_end[N_EXPERTS - 1]
    tile_ids = jnp.arange(MOE_TILES, dtype=jnp.int32)
    tid = jnp.minimum(tile_ids, n_used - 1)
    tile_expert = jnp.sum((tile_end[None, :] <= tid[:, None]).astype(jnp.int32), axis=1)
    tile_first = ((tile_ids == lookup(tile_start, tile_expert)) & (tile_ids < n_used)).astype(jnp.int32)
    has_rows = (tiles_e > 0).astype(jnp.int32)
    slot_e = (jnp.sum(earlier * has_rows[None, :], axis=1) - 1) % MOE_SLOTS
    later = jnp.where((eid[None, :] > eid[:, None]) & (has_rows[None, :] > 0), eid[None, :], N_EXPERTS)
    next_e = jnp.min(later, axis=1)
    next2_e = jnp.where(next_e >= N_EXPERTS, N_EXPERTS, lookup(next_e, jnp.minimum(next_e, N_EXPERTS - 1)))
    to_id = lambda e: jnp.where(e >= N_EXPERTS, -1, e).astype(jnp.int32)
    sched = (tile_expert, tile_first, lookup(slot_e, tile_expert).astype(jnp.int32),
             lookup(to_id(next_e), tile_expert).astype(jnp.int32),
             lookup(to_id(next2_e), tile_expert).astype(jnp.int32), n_used.reshape(1))
    return pos_t, sched


SC_WINDOW = 128
SC_SLABS = 2
SC_SLAB_W = D_MODEL // 2 // SC_SLABS


def _dispatch_rows(u2, pos_t):
    mesh = plsc.VectorSubcoreMesh(core_axis_name="core", subcore_axis_name="subcore")

    @functools.partial(pl.kernel, mesh=mesh, scratch_types=[],
                       out_type=jax.ShapeDtypeStruct((SC_SLABS, MOE_TILES * MOE_TM, SC_SLAB_W), F32))
    def dispatch(x_hbm, i_hbm, o_hbm):
        for j in range(SC_SLABS):
            def body(x_vmem, i_vmem, j=j):
                for k in range(TOP_K):
                    pltpu.sync_copy(x_vmem, o_hbm.at[j].at[i_vmem.at[k]])

            pltpu.emit_pipeline(
                body,
                grid=(N_TOK // SC_WINDOW,),
                in_specs=[pl.BlockSpec((SC_WINDOW, SC_SLAB_W), index_map=lambda i: (i, 0)),
                          pl.BlockSpec((TOP_K, SC_WINDOW), index_map=lambda i: (0, i))],
                out_specs=[],
                core_axis_name=("core", "subcore"),
                dimension_semantics=(pltpu.PARALLEL,),
            )(x_hbm.at[j], i_hbm)

    return dispatch(u2, pos_t)


def _final_kernel(yg_ref, wts_ref, x1_ref, mod_ref, modn_ref, lng_ref, lnb_ref, o_ref, un_ref):
    wts = wts_ref[...]
    ffn_lo = ffn_hi = None
    for k in range(TOP_K):
        lo, hi = _unpack_bf16_pair(yg_ref[k])
        w = wts[:, k:k + 1]
        ffn_lo = w * lo if ffn_lo is None else ffn_lo + w * lo
        ffn_hi = w * hi if ffn_hi is None else ffn_hi + w * hi
    ffn = jnp.concatenate([ffn_lo, ffn_hi], axis=-1)
    x2 = _layer_norm(DN_ALPHA * x1_ref[...] + mod_ref[5:6, :] * ffn, lng_ref[...], lnb_ref[...])
    o_ref[...] = x2
    un_ref[...] = (x2 * (1.0 + modn_ref[1:2, :]) + modn_ref[0:1, :]).astype(BF16)


def _final_call(yg, wts, x1, mods, ln_g, ln_b, l):
    tm = 256
    ln = min(l + 1, DEPTH - 1)
    lnspec = pl.BlockSpec((None, 1, D_MODEL), lambda i: (l, 0, 0))
    modspec = lambda lyr: pl.BlockSpec((None, None, 6, D_MODEL), lambda i: (lyr, _cond_of_row_tile(i, tm), 0, 0))
    return pl.pallas_call(
        _final_kernel,
        grid=(N_TOK // tm,),
        in_specs=[pl.BlockSpec((TOP_K, tm, D_MODEL // 2), lambda i: (0, i, 0)),
                  pl.BlockSpec((tm, LANES), lambda i: (i, 0)),
                  pl.BlockSpec((tm, D_MODEL), lambda i: (i, 0)),
                  modspec(l), modspec(ln), lnspec, lnspec],
        out_specs=[pl.BlockSpec((tm, D_MODEL), lambda i: (i, 0)), pl.BlockSpec((tm, D_MODEL), lambda i: (i, 0))],
        out_shape=[jax.ShapeDtypeStruct((N_TOK, D_MODEL), F32), jax.ShapeDtypeStruct((N_TOK, D_MODEL), BF16)],
        compiler_params=_cparams(("arbitrary",)),
        name="final",
    )(yg, wts, x1, mods, mods, ln_g, ln_b)


def kernel(x_prompt, x_sample, cache_attn_k, cache_attn_v, state_rglru, state_hgrn, c, c_ctx, w_ada, b_ada, w_in, da_lambda, da_subln, rg_conv_w, rg_conv_b, rg_gate_w, rg_gate_b, rg_lambda, hg_lb, hg_norm, w_branch, w_out, ln1_g, ln1_b, router_w, router_b, w1, b1, w2, b2, ln2_g, ln2_b):
    rw, rb = _router_params(router_w, router_b)
    p = dict(hg_norm=hg_norm.reshape(DEPTH, 1, HG_VAL), w_branch=w_branch, w_out=w_out,
             ln1_g=ln1_g.reshape(DEPTH, 1, D_MODEL), ln1_b=ln1_b.reshape(DEPTH, 1, D_MODEL), rw=rw, rb=rb)
    ln2_g = ln2_g.reshape(DEPTH, 1, D_MODEL)
    ln2_b = ln2_b.reshape(DEPTH, 1, D_MODEL)
    rg_conv_b = rg_conv_b.reshape(DEPTH, 1, RG_WIDTH)
    wg, bg = _rg_gate_dense(rg_gate_w, rg_gate_b)

    x_ctx = x_prompt.reshape(N_CTX, D_MODEL)
    x_lat = x_sample.reshape(N_LAT, D_MODEL)
    cond = jnp.concatenate([c_ctx[None, :], c, jnp.zeros((N_COND - 1 - DEC_BATCH, D_MODEL), F32)], axis=0)
    mods = _adaln_all(cond, w_ada, b_ada).reshape(DEPTH, N_COND, 6, D_MODEL)

    pr = jax.nn.softmax(hg_lb.astype(F32), axis=0)
    lbs = jnp.cumsum(pr, axis=0) - pr[0]
    dl = da_lambda.astype(F32)
    lambda_init = [0.8 - 0.6 * math.exp(-0.3 * l) for l in range(DEPTH)]
    lam_all = (jnp.exp(jnp.sum(dl[:, 0] * dl[:, 1], -1)) - jnp.exp(jnp.sum(dl[:, 2] * dl[:, 3], -1))
               + jnp.asarray(lambda_init, F32))
    lam_v = jnp.broadcast_to(lam_all[:, None, None], (DEPTH, 1, LANES))
    sub_v = da_subln.reshape(DEPTH, 1, DA_V_DIM)

    cache_k = cache_attn_k.reshape(DEC_BATCH, DEPTH, PAST_LEN, BRANCH_W)
    cache_v = cache_attn_v.reshape(DEC_BATCH, DEPTH, PAST_LEN, BRANCH_W)

    b1p = b1.reshape(DEPTH, N_EXPERTS, 2 * D_EXPERT // MOE_NB, MOE_NB // 2, 2)
    b1p = jnp.swapaxes(b1p, -1, -2).reshape(DEPTH, N_EXPERTS, 1, 2 * D_EXPERT)
    b2r = b2.reshape(DEPTH, N_EXPERTS, 1, D_MODEL)

    ks, vs, rgs, hgs = [], [], [], []
    u = _modulate(x_ctx, x_lat, mods, 0)
    x_pair = (x_ctx, x_lat, 0)
    for l in range(DEPTH):
        h = _in_proj(u, w_in, l)
        ks.append(h[:N_CTX, BRANCH_W:2 * BRANCH_W].astype(F32).reshape(BATCH, SEQ, DA_HEADS, 2, DA_HEAD_DIM))
        vs.append(h[:N_CTX, 2 * BRANCH_W:3 * BRANCH_W].astype(F32).reshape(BATCH, SEQ, DA_HEADS, DA_V_DIM))

        att = _attention(h, cache_k, cache_v, lam_v, sub_v, l, 1.0 - lambda_init[l])

        rg_c, hl_c = _rglru_call(h, rg_conv_w, rg_conv_b, wg, bg, rg_lambda, None, l, BATCH, SEQ, 0)
        rg_l, _ = _rglru_call(h, rg_conv_w, rg_conv_b, wg, bg, rg_lambda, state_rglru, l,
                              DEC_BATCH, DEC_SEQ, N_CTX // DEC_SEQ)
        rgs.append(hl_c)

        of_c, ob_c, sl_c = _hgrn_call(h, lbs, None, l, BATCH, SEQ, 0)
        of_l, ob_l, _ = _hgrn_call(h, lbs, state_hgrn, l, DEC_BATCH, DEC_SEQ, N_CTX)
        hgs.append(sl_c)

        x1, u2, idx, rank, wts, cnt = _merge_call(att, (rg_c, rg_l), (of_c, of_l), (ob_c, ob_l), h, x_pair,
                                                  mods, p, l)

        pos_t, sched = _route(idx[:, :TOP_K], rank[:, :TOP_K], cnt[0, :N_EXPERTS])
        x_sorted = _dispatch_rows(u2, pos_t)
        y_sorted = _moe_call(x_sorted, sched, w1, b1p, w2, b2r, l)
        yg = y_sorted.at[pos_t.reshape(-1)].get(mode='promise_in_bounds').reshape(TOP_K, N_TOK, D_MODEL // 2)
        x, u = _final_call(yg, wts, x1, mods, ln2_g, ln2_b, l)
        x_pair = (x, x, N_CTX // 256)

    y_prompt = x[:N_CTX].reshape(BATCH, SEQ, D_MODEL)
    y_sample = x[N_CTX:].reshape(DEC_BATCH, DEC_SEQ, D_MODEL)
    return (y_prompt, y_sample, jnp.stack(ks, axis=1), jnp.stack(vs, axis=1),
            jnp.stack(rgs, axis=1), jnp.stack(hgs, axis=1))
```

```python
import functools
import math

import numpy as np
import jax
import jax.numpy as jnp
from jax import lax
from jax.experimental import pallas as pl
from jax.experimental.pallas import tpu as pltpu
from jax.experimental.pallas import tpu_sc as plsc

F32 = jnp.float32
BF16 = jnp.bfloat16
HIGHEST = lax.Precision.HIGHEST

D_MODEL = 1024
BATCH = 16
SEQ = 256
DEPTH = 4
DEC_BATCH = 4
DEC_SEQ = 1024
PAST_LEN = 256
GRID_W = 64
BRANCH_W = 512
DA_HEADS = 4
DA_HEAD_DIM = 64
DA_V_DIM = 128
ROPE_BASE = 10000.0
RG_WIDTH = 512
RG_BLOCKS = 8
RG_BLOCK_W = 64
RG_CONV_W = 4
RG_C = 8.0
HG_HEADS = 4
HG_KEY = 128
HG_VAL = 128
N_EXPERTS = 32
TOP_K = 4
D_EXPERT = 1024
SWIGLU_ALPHA = 1.702
SWIGLU_LIMIT = 7.0
DN_ALPHA = (2 * DEPTH) ** 0.25
NORM_EPS = 1e-5
D_IN = 10 * BRANCH_W + 3 * D_MODEL

N_CTX = BATCH * SEQ
N_LAT = DEC_BATCH * DEC_SEQ
N_TOK = N_CTX + N_LAT
N_COND = 8

LANES = 128
VMEM_LIMIT = 56 * 1024 * 1024

HG_CHUNK = 128
HG_LEVELS = (1, 2, 4, 8, 16, 32, 64)
MOE_TM = 256
MERGE_TM = 256
MOE_TILES = (N_TOK * TOP_K) // MOE_TM + N_EXPERTS


def _cparams(sem):
    return pltpu.CompilerParams(dimension_semantics=sem, vmem_limit_bytes=VMEM_LIMIT)


def _sigmoid(x):
    return 0.5 * jnp.tanh(0.5 * x) + 0.5


def _pack_bf16_pair(lo, hi):
    lo_bits = pltpu.bitcast(lo.astype(BF16).astype(F32), jnp.uint32) >> 16
    hi_bits = pltpu.bitcast(hi.astype(BF16).astype(F32), jnp.uint32) & jnp.uint32(0xFFFF0000)
    return pltpu.bitcast(hi_bits | lo_bits, F32)


def _unpack_bf16_pair(words):
    bits = pltpu.bitcast(words, jnp.uint32)
    return pltpu.bitcast(bits << 16, F32), pltpu.bitcast(bits & jnp.uint32(0xFFFF0000), F32)


def _dot(a, b):
    return jnp.dot(a, b, preferred_element_type=F32)


def _dot_nt(a, b):
    return lax.dot_general(a, b, (((1,), (1,)), ((), ())), preferred_element_type=F32)


def _cond_of_row_tile(i, tm):
    r = i * tm
    return jnp.where(r < N_CTX, 0, 1 + (r - N_CTX) // DEC_SEQ)


def _ada_kernel(c_ref, w_ref, b_ref, o_ref):
    c = c_ref[...]
    s = c * _sigmoid(c)
    o_ref[0] = jnp.dot(s, w_ref[0], precision=HIGHEST, preferred_element_type=F32) + b_ref[0]


def _adaln_all(cond, w_ada, b_ada):
    tn = 1536
    return pl.pallas_call(
        _ada_kernel,
        grid=(DEPTH, 6 * D_MODEL // tn),
        in_specs=[pl.BlockSpec((N_COND, D_MODEL), lambda l, j: (0, 0)),
                  pl.BlockSpec((1, D_MODEL, tn), lambda l, j: (l, 0, j)),
                  pl.BlockSpec((1, 1, tn), lambda l, j: (l, 0, j))],
        out_specs=pl.BlockSpec((1, N_COND, tn), lambda l, j: (l, 0, j)),
        out_shape=jax.ShapeDtypeStruct((DEPTH, N_COND, 6 * D_MODEL), F32),
        compiler_params=_cparams(("arbitrary", "arbitrary")),
        name="adaln",
    )(cond, w_ada, b_ada.reshape(DEPTH, 1, 6 * D_MODEL))


def _modulate_kernel(xc_ref, xl_ref, mod_ref, u_ref, *, ctx_tiles):
    x = jnp.where(pl.program_id(0) < ctx_tiles, xc_ref[...], xl_ref[...])
    u_ref[...] = (x * (1.0 + mod_ref[1:2, :]) + mod_ref[0:1, :]).astype(BF16)


def _modulate(x_ctx, x_lat, mods, l):
    tm = 1024
    ctx_tiles = N_CTX // tm
    return pl.pallas_call(
        functools.partial(_modulate_kernel, ctx_tiles=ctx_tiles),
        grid=(N_TOK // tm,),
        in_specs=[pl.BlockSpec((tm, D_MODEL), lambda i: (jnp.minimum(i, ctx_tiles - 1), 0)),
                  pl.BlockSpec((tm, D_MODEL), lambda i: (jnp.maximum(i - ctx_tiles, 0), 0)),
                  pl.BlockSpec((None, None, 6, D_MODEL), lambda i: (l, _cond_of_row_tile(i, tm), 0, 0))],
        out_specs=pl.BlockSpec((tm, D_MODEL), lambda i: (i, 0)),
        out_shape=jax.ShapeDtypeStruct((N_TOK, D_MODEL), BF16),
        compiler_params=_cparams(("arbitrary",)),
        name="modulate",
    )(x_ctx, x_lat, mods)


def _in_kernel(u_ref, w_ref, o_ref, wbf_ref):
    @pl.when(pl.program_id(1) == 0)
    def _():
        wbf_ref[...] = w_ref[...].astype(BF16)

    o_ref[...] = _dot(u_ref[...], wbf_ref[...]).astype(BF16)


def _in_proj(u, w_in, l):
    tm, tn = 1024, 2048
    return pl.pallas_call(
        _in_kernel,
        grid=(D_IN // tn, N_TOK // tm),
        in_specs=[pl.BlockSpec((tm, D_MODEL), lambda j, i: (i, 0)),
                  pl.BlockSpec((None, D_MODEL, tn), lambda j, i: (l, 0, j))],
        out_specs=pl.BlockSpec((tm, tn), lambda j, i: (i, j)),
        out_shape=jax.ShapeDtypeStruct((N_TOK, D_IN), BF16),
        scratch_shapes=[pltpu.VMEM((D_MODEL, tn), BF16)],
        compiler_params=_cparams(("arbitrary", "arbitrary")),
        name="in_proj",
    )(u, w_in)


def _rope(x, cos, sin_signed):
    lane = lax.broadcasted_iota(jnp.int32, x.shape, 1)
    first = (lane & 31) < 16
    partner = jnp.where(first, pltpu.roll(x, LANES - 16, 1), pltpu.roll(x, 16, 1))
    return x * cos + partner * sin_signed


LOG2E = 1.4426950408889634


def _attn_body(q, keys, vals, lam, subln, out_scale):
    lane = lax.broadcasted_iota(jnp.int32, q.shape, 1)
    qs = q * (DA_HEAD_DIM ** -0.5 * LOG2E)
    acc = None
    for m in range(2):
        in_map = (lane < DA_HEAD_DIM) if m == 0 else (lane >= DA_HEAD_DIM)
        qm = jnp.where(in_map, qs, 0.0).astype(BF16)
        s = [_dot_nt(qm, k) for k in keys]
        mx = s[0].max(axis=-1, keepdims=True)
        for si in s[1:]:
            mx = jnp.maximum(mx, si.max(axis=-1, keepdims=True))
        e = [jnp.exp2(si - mx) for si in s]
        den = e[0].sum(axis=-1, keepdims=True)
        for ei in e[1:]:
            den = den + ei.sum(axis=-1, keepdims=True)
        pv = _dot(e[0].astype(BF16), vals[0])
        for ei, v in zip(e[1:], vals[1:]):
            pv = pv + _dot(ei.astype(BF16), v)
        coef = (1.0 / den) if m == 0 else (-lam[:, 0:1] / den)
        acc = pv * coef if acc is None else acc + pv * coef
    y = acc * lax.rsqrt(jnp.mean(acc * acc, axis=-1, keepdims=True) + NORM_EPS)
    return y * subln * out_scale


def _attn_ctx_kernel(lam_ref, sub_ref, q_ref, k_ref, v_ref, *rest, out_scale):
    o_ref, kout_ref, vout_ref = rest[-3:]
    kout_ref[...] = k_ref[...].astype(F32)
    vout_ref[...] = v_ref[...].astype(F32)
    for hd in range(DA_HEADS):
        cols = slice(hd * LANES, (hd + 1) * LANES)
        o_ref[:, cols] = _attn_body(q_ref[:, cols].astype(F32), [k_ref[:, cols]], [v_ref[:, cols]],
                                    lam_ref[...], sub_ref[...], out_scale)


def _attn_lat_kernel(lam_ref, sub_ref, q_ref, k_ref, v_ref, kc_ref, vc_ref, cos_ref, sin_ref,
                     o_ref, k_scr, kc_scr, vc_scr, *, out_scale, tq):
    k_scr[...] = _rope(k_ref[...].astype(F32), cos_ref[...], sin_ref[...]).astype(BF16)
    kc_scr[...] = kc_ref[...].astype(BF16)
    vc_scr[...] = vc_ref[...].astype(BF16)

    def q_block(qi, carry):
        rows = pl.ds(pl.multiple_of(qi * tq, tq), tq)
        q = _rope(q_ref[rows, :].astype(F32), cos_ref[rows, :], sin_ref[rows, :])
        o_ref[rows, :] = _attn_body(q, [k_scr[...], kc_scr[...]], [v_ref[...], vc_scr[...]],
                                    lam_ref[...], sub_ref[...], out_scale)
        return carry

    lax.fori_loop(0, DEC_SEQ // tq, q_block, 0)


def _rope_tables():
    t = np.arange(DEC_SEQ)
    row = (t // GRID_W).astype(np.float64)
    col = (t % GRID_W).astype(np.float64)
    d = DA_HEAD_DIM // 2
    inv = ROPE_BASE ** (-np.arange(0, d, 2, dtype=np.float64) / d)
    a_row = row[:, None] * inv[None, :]
    a_col = col[:, None] * inv[None, :]
    cos = np.concatenate([np.cos(a_row), np.cos(a_row), np.cos(a_col), np.cos(a_col)], -1)
    sin = np.concatenate([-np.sin(a_row), np.sin(a_row), -np.sin(a_col), np.sin(a_col)], -1)
    return (jnp.asarray(np.tile(cos, (1, 2)), F32), jnp.asarray(np.tile(sin, (1, 2)), F32))


def _attention(h, cache_k, cache_v, lam_v, sub_v, l, out_scale, kv_prev):
    small = pl.BlockSpec((None, 1, LANES), lambda *a: (l, 0, 0))
    kv_blk = pl.BlockSpec((None, None, SEQ, BRANCH_W), lambda b: (b, l, 0, 0))
    kv_shape = jax.ShapeDtypeStruct((BATCH, DEPTH, SEQ, BRANCH_W), F32)
    carried = () if kv_prev is None else tuple(kv_prev)
    att_ctx, k_all, v_all = pl.pallas_call(
        functools.partial(_attn_ctx_kernel, out_scale=out_scale),
        grid=(BATCH,),
        in_specs=[small, small,
                  pl.BlockSpec((SEQ, BRANCH_W), lambda b: (b, 0)),
                  pl.BlockSpec((SEQ, BRANCH_W), lambda b: (b, 1)),
                  pl.BlockSpec((SEQ, BRANCH_W), lambda b: (b, 2))]
                 + [pl.BlockSpec(memory_space=pl.ANY)] * len(carried),
        out_specs=[pl.BlockSpec((SEQ, BRANCH_W), lambda b: (b, 0)), kv_blk, kv_blk],
        out_shape=[jax.ShapeDtypeStruct((N_CTX, BRANCH_W), F32), kv_shape, kv_shape],
        input_output_aliases={5: 1, 6: 2} if carried else {},
        compiler_params=_cparams(("arbitrary",)),
        name="attn_ctx",
    )(lam_v, sub_v, h, h, h, *carried)

    tq = 256
    cos, sin = _rope_tables()
    row0 = N_CTX // DEC_SEQ
    seq_blk = lambda cb: pl.BlockSpec((DEC_SEQ, LANES), lambda b, hd: (row0 + b, cb + hd))
    cache_blk = pl.BlockSpec((None, None, PAST_LEN, LANES), lambda b, hd: (b, l, 0, hd))
    table = pl.BlockSpec((DEC_SEQ, LANES), lambda b, hd: (0, 0))
    att_lat = pl.pallas_call(
        functools.partial(_attn_lat_kernel, out_scale=out_scale, tq=tq),
        grid=(DEC_BATCH, DA_HEADS),
        in_specs=[small, small, seq_blk(0), seq_blk(4), seq_blk(8), cache_blk, cache_blk, table, table],
        out_specs=pl.BlockSpec((DEC_SEQ, LANES), lambda b, hd: (b, hd)),
        out_shape=jax.ShapeDtypeStruct((N_LAT, BRANCH_W), F32),
        scratch_shapes=[pltpu.VMEM((DEC_SEQ, LANES), BF16), pltpu.VMEM((PAST_LEN, LANES), BF16),
                        pltpu.VMEM((PAST_LEN, LANES), BF16)],
        compiler_params=_cparams(("arbitrary", "arbitrary")),
        name="attn_lat",
    )(lam_v, sub_v, h, h, h, cache_k, cache_v, cos, sin)
    return (att_ctx, att_lat), (k_all, v_all)


def _gelu_tanh(x):
    return 0.5 * x * (1.0 + jnp.tanh(math.sqrt(2.0 / math.pi) * (x + 0.044715 * (x * x * x))))


def _rg_kernel(*refs, seq, has_h0):
    if has_h0:
        rx_ref, gate_ref, cw_ref, cb_ref, wg_ref, bg_ref, lam_ref, h0_ref = refs[:8]
        rest = refs[8:]
    else:
        rx_ref, gate_ref, cw_ref, cb_ref, wg_ref, bg_ref, lam_ref = refs[:7]
        h0_ref = None
        rest = refs[7:]
    out_ref, hl_ref, a_scr, b_scr, h_scr = rest

    x = rx_ref[...].astype(F32)
    row = lax.broadcasted_iota(jnp.int32, x.shape, 0)
    xr = cb_ref[...] + cw_ref[2:3, :] * x
    for j in (0, 1, 3):
        d = j - RG_CONV_W // 2
        shifted = pltpu.roll(x, (-d) % seq, 0)
        valid = (row + d >= 0) & (row + d < seq)
        xr = xr + cw_ref[j:j + 1, :] * jnp.where(valid, shifted, 0.0)

    g = _dot(xr.astype(BF16), wg_ref[...]) + bg_ref[...]
    for dr in range(2):
        r = _sigmoid(g[:, (2 * dr) * RG_WIDTH:(2 * dr + 1) * RG_WIDTH])
        i = _sigmoid(g[:, (2 * dr + 1) * RG_WIDTH:(2 * dr + 2) * RG_WIDTH])
        lam = lam_ref[dr:dr + 1, :]
        softplus_neg = jnp.maximum(-lam, 0.0) + jnp.log(1.0 + jnp.exp(-jnp.abs(lam)))
        log_a = (-RG_C * softplus_neg) * r
        a = jnp.exp(log_a)
        a_scr[dr] = a
        b_scr[dr] = jnp.sqrt(1.0 - a * a) * i * xr

    if has_h0:
        hf0, hb0 = h0_ref[0:1, :], h0_ref[1:2, :]
    else:
        hf0 = hb0 = jnp.zeros((1, RG_WIDTH), F32)

    def step(t, carry):
        hf, hb = carry
        tb = seq - 1 - t
        hf = a_scr[0, pl.ds(t, 1), :] * hf + b_scr[0, pl.ds(t, 1), :]
        h_scr[0, pl.ds(t, 1), :] = hf
        hb = a_scr[1, pl.ds(tb, 1), :] * hb + b_scr[1, pl.ds(tb, 1), :]
        h_scr[1, pl.ds(tb, 1), :] = hb
        return hf, hb

    hf, hb = lax.fori_loop(0, seq, step, (hf0, hb0), unroll=8)
    hl_ref[0:1, :] = hf
    hl_ref[1:2, :] = hb
    out_ref[...] = (h_scr[0] + h_scr[1]) * _gelu_tanh(gate_ref[...].astype(F32))


def _rglru_call(h, conv_w, conv_b, wg, bg, lam, h0, l, nseq, seq, row_block0):
    has_h0 = h0 is not None
    layer = lambda shape: pl.BlockSpec((None,) + shape, lambda b: (l,) + (0,) * len(shape))
    in_specs = [pl.BlockSpec((seq, RG_WIDTH), lambda b: (row_block0 + b, 3)),
                pl.BlockSpec((seq, RG_WIDTH), lambda b: (row_block0 + b, 4)),
                layer((RG_CONV_W, RG_WIDTH)), layer((1, RG_WIDTH)),
                layer((RG_WIDTH, 4 * RG_WIDTH)), layer((1, 4 * RG_WIDTH)), layer((2, RG_WIDTH))]
    args = [h, h, conv_w, conv_b, wg, bg, lam]
    if has_h0:
        in_specs.append(pl.BlockSpec((None, None, 2, RG_WIDTH), lambda b: (b, l, 0, 0)))
        args.append(h0)
    return pl.pallas_call(
        functools.partial(_rg_kernel, seq=seq, has_h0=has_h0),
        grid=(nseq,),
        in_specs=in_specs,
        out_specs=[pl.BlockSpec((seq, RG_WIDTH), lambda b: (b, 0)),
                   pl.BlockSpec((None, 2, RG_WIDTH), lambda b: (b, 0, 0))],
        out_shape=[jax.ShapeDtypeStruct((nseq * seq, RG_WIDTH), F32),
                   jax.ShapeDtypeStruct((nseq, 2, RG_WIDTH), F32)],
        scratch_shapes=[pltpu.VMEM((2, seq, RG_WIDTH), F32)] * 3,
        compiler_params=_cparams(("arbitrary",)),
        name="rglru_lat" if has_h0 else "rglru_ctx",
    )(*args)


def _rg_gate_dense(gate_w, gate_b):
    eye = jnp.eye(RG_BLOCKS, dtype=BF16)
    blocks = jnp.transpose(gate_w, (0, 3, 4, 1, 2, 5)).astype(BF16)
    wg = blocks[:, :, :, :, :, None, :] * eye[None, :, None, None, None, :, None]
    return wg.reshape(DEPTH, RG_WIDTH, 4 * RG_WIDTH), gate_b.reshape(DEPTH, 1, 4 * RG_WIDTH)


def _hg_masks():
    c = HG_CHUNK
    t = np.arange(c)[:, None]
    s = np.arange(c)[None, :]
    lvl = np.zeros((2, len(HG_LEVELS), c, c), np.float32)
    for n, h in enumerate(HG_LEVELS):
        same = (t // (2 * h)) == (s // (2 * h))
        lvl[0, n] = same & ((t // h) % 2 == 1) & ((s // h) % 2 == 0)
        lvl[1, n] = same & ((t // h) % 2 == 0) & ((s // h) % 2 == 1)
    return jnp.asarray(lvl)


def _hg_chunk(q_raw, z, v, lb, st, lvl, rev):
    c = HG_CHUNK
    q = q_raw * _sigmoid(q_raw)
    sig = _sigmoid(z)
    f = lb + (1.0 - lb) * sig
    kk = (1.0 - lb) * (1.0 - sig)

    rowk = lax.broadcasted_iota(jnp.int32, (c, HG_KEY), 0)
    pos = rowk if not rev else (c - 1) - rowk
    earlier = lambda x, d: pltpu.roll(x, d if not rev else c - d, 0)
    later = lambda x, d: pltpu.roll(x, c - d if not rev else d, 0)

    def block_row(x, n, p):
        r = p if not rev else n - 1 - p
        x3 = x.reshape(c // n, n, HG_KEY)
        return jnp.broadcast_to(x3[:, r:r + 1, :], x3.shape).reshape(c, HG_KEY)

    pair = f * earlier(f, 1)
    f_next = later(f, 1)
    p4 = pos & 3
    a = {1: f, 2: jnp.where((pos & 1) == 1, pair, f)}
    b = {2: jnp.where((pos & 1) == 0, f_next, 1.0)}
    a[4] = a[2] * jnp.where(p4 == 2, earlier(pair, 1), jnp.where(p4 == 3, earlier(pair, 2), 1.0))
    b[4] = jnp.where(p4 == 3, 1.0, jnp.where(p4 == 2, f_next,
                                             jnp.where(p4 == 1, later(pair, 2), f_next * later(pair, 3))))
    h = 4
    while h < c:
        in_later_half = (pos & h) != 0
        a[2 * h] = jnp.where(in_later_half, a[h] * block_row(a[h], 2 * h, h - 1), a[h])
        b[2 * h] = jnp.where(in_later_half, b[h], b[h] * block_row(a[h], 2 * h, 2 * h - 1))
        h *= 2

    row = lax.broadcasted_iota(jnp.int32, (c, c), 0)
    col = lax.broadcasted_iota(jnp.int32, (c, c), 1)
    scores = jnp.where(row == col, jnp.sum(q * kk, axis=-1, keepdims=True), 0.0)
    for n, h in enumerate(HG_LEVELS):
        qa = (q * a[h]).astype(BF16)
        kb = (kk if h == 1 else kk * b[h]).astype(BF16)
        scores = scores + lvl[n] * _dot_nt(qa, kb)

    vb = v.astype(BF16)
    o = _dot(scores.astype(BF16), vb)
    o = o + _dot_nt((q * a[c]).astype(BF16), st.astype(BF16))
    total = a[c][c - 1:c, :] if not rev else a[c][0:1, :]
    st_new = st * total + _dot(v.T.astype(BF16), (kk * b[c]).astype(BF16))
    return o, st_new


def _hg_kernel(*refs, has_s0, nchunk):
    if has_s0:
        (qf_ref, qb_ref, zf_ref, zb_ref, vf_ref, vb_ref, lb_ref, lvl_ref, s0_ref,
         of_ref, ob_ref, sfin_ref, st_scr) = refs
    else:
        qf_ref, qb_ref, zf_ref, zb_ref, vf_ref, vb_ref, lb_ref, lvl_ref = refs[:8]
        of_ref, ob_ref, sfin_ref, st_scr = refs[-4:]
        s0_ref = None
    ci = pl.program_id(1)

    @pl.when(ci == 0)
    def _():
        for dr in range(2):
            for hd in range(HG_HEADS):
                st_scr[dr, hd] = s0_ref[dr, hd].T if has_s0 else jnp.zeros((HG_VAL, HG_KEY), F32)

    last = ci == nchunk - 1
    for hd in range(HG_HEADS):
        cols = slice(hd * LANES, (hd + 1) * LANES)
        ld = lambda ref: ref[:, cols].astype(F32)
        o_f, st_f = _hg_chunk(ld(qf_ref), ld(zf_ref), ld(vf_ref), lb_ref[0:1, cols], st_scr[0, hd],
                              lvl_ref[0], False)
        o_b, st_b = _hg_chunk(ld(qb_ref), ld(zb_ref), ld(vb_ref), lb_ref[1:2, cols], st_scr[1, hd],
                              lvl_ref[1], True)
        of_ref[:, cols] = o_f
        ob_ref[:, cols] = o_b
        st_scr[0, hd] = st_f
        st_scr[1, hd] = st_b

        @pl.when(last)
        def _():
            sfin_ref[0, hd] = st_f.T
            sfin_ref[1, hd] = st_b.T


def _hgrn_call(h, lbs_l, s0, l, nseq, seq, row0, state_prev=None):
    c = HG_CHUNK
    nchunk = seq // c
    has_s0 = s0 is not None
    lvl = _hg_masks()
    rf = lambda b, ci: row0 // c + b * nchunk + ci
    rb = lambda b, ci: row0 // c + b * nchunk + (nchunk - 1 - ci)
    blk = lambda rfun, cb: pl.BlockSpec((c, BRANCH_W), lambda b, ci: (rfun(b, ci), cb))
    const = lambda shape: pl.BlockSpec(shape, lambda b, ci: (0,) * len(shape))
    in_specs = [blk(rf, 5), blk(rb, 5), blk(rf, 6), blk(rb, 7), blk(rf, 8), blk(rb, 8),
                pl.BlockSpec((None, 2, BRANCH_W), lambda b, ci: (l, 0, 0)), const((2, len(HG_LEVELS), c, c))]
    args = [h, h, h, h, h, h, lbs_l, lvl]
    if has_s0:
        in_specs.append(pl.BlockSpec((None, None, 2, HG_HEADS, HG_KEY, HG_VAL), lambda b, ci: (b, l, 0, 0, 0, 0)))
        args.append(s0)
        state_spec = pl.BlockSpec((None, 2, HG_HEADS, HG_KEY, HG_VAL), lambda b, ci: (b, 0, 0, 0, 0))
        state_shape = jax.ShapeDtypeStruct((nseq, 2, HG_HEADS, HG_KEY, HG_VAL), F32)
    else:
        state_spec = pl.BlockSpec((None, None, 2, HG_HEADS, HG_KEY, HG_VAL), lambda b, ci: (b, l, 0, 0, 0, 0))
        state_shape = jax.ShapeDtypeStruct((nseq, DEPTH, 2, HG_HEADS, HG_KEY, HG_VAL), F32)
    aliases = {}
    if state_prev is not None:
        aliases = {len(args): 2}
        in_specs.append(pl.BlockSpec(memory_space=pl.ANY))
        args.append(state_prev)
    return pl.pallas_call(
        functools.partial(_hg_kernel, has_s0=has_s0, nchunk=nchunk),
        grid=(nseq, nchunk),
        in_specs=in_specs,
        out_specs=[pl.BlockSpec((c, BRANCH_W), lambda b, ci: (b * nchunk + ci, 0)),
                   pl.BlockSpec((c, BRANCH_W), lambda b, ci: (b * nchunk + (nchunk - 1 - ci), 0)),
                   state_spec],
        out_shape=[jax.ShapeDtypeStruct((nseq * seq, BRANCH_W), F32),
                   jax.ShapeDtypeStruct((nseq * seq, BRANCH_W), F32),
                   state_shape],
        input_output_aliases=aliases,
        scratch_shapes=[pltpu.VMEM((2, HG_HEADS, HG_VAL, HG_KEY), F32)],
        compiler_params=_cparams(("arbitrary", "arbitrary")),
        name="hgrn_lat" if has_s0 else "hgrn_ctx",
    )(*args)


def _layer_norm(y, g, b):
    mu = jnp.mean(y, axis=-1, keepdims=True)
    yc = y - mu
    var = jnp.mean(yc * yc, axis=-1, keepdims=True)
    return yc * lax.rsqrt(var + NORM_EPS) * g + b


def _merge_kernel(attc_ref, attl_ref, rgc_ref, rgl_ref, ofc_ref, ofl_ref, obc_ref, obl_ref,
                  hgate_ref, mg0_ref, mg1_ref, mg2_ref, xc_ref, xl_ref, mod_ref,
                  hgn_ref, wbr_ref, wout_ref, lng_ref, lnb_ref, rw_ref, rb_ref, tri_ref,
                  x1_ref, u2_ref, idx_ref, rank_ref, wts_ref, cnt_ref, wbr_bf, wout_bf, cnt_scr, *, ctx_tiles):
    @pl.when(pl.program_id(0) == 0)
    def _():
        wbr_bf[...] = wbr_ref[...].astype(BF16)
        wout_bf[...] = wout_ref[...].astype(BF16)
        cnt_scr[...] = jnp.zeros(cnt_scr.shape, F32)

    is_ctx = pl.program_id(0) < ctx_tiles
    pick = lambda c_ref, l_ref: jnp.where(is_ctx, c_ref[...], l_ref[...])
    att = pick(attc_ref, attl_ref)
    rg = pick(rgc_ref, rgl_ref)
    o = pick(ofc_ref, ofl_ref) + pick(obc_ref, obl_ref)
    hgate = hgate_ref[...].astype(F32)
    hg_parts = []
    for hd in range(HG_HEADS):
        oh = o[:, hd * HG_VAL:(hd + 1) * HG_VAL]
        gh = hgate[:, hd * HG_VAL:(hd + 1) * HG_VAL]
        yh = oh * lax.rsqrt(jnp.mean(oh * oh, axis=-1, keepdims=True) + NORM_EPS) * hgn_ref[...]
        hg_parts.append(yh * (gh * _sigmoid(gh)))
    hg = jnp.concatenate(hg_parts, axis=-1)

    proj = _sigmoid(mg0_ref[...].astype(F32)) * _dot(att.astype(BF16), wbr_bf[0])
    proj = proj + _sigmoid(mg1_ref[...].astype(F32)) * _dot(rg.astype(BF16), wbr_bf[1])
    proj = proj + _sigmoid(mg2_ref[...].astype(F32)) * _dot(hg.astype(BF16), wbr_bf[2])
    mix = _dot(proj.astype(BF16), wout_bf[...])

    x1 = _layer_norm(DN_ALPHA * pick(xc_ref, xl_ref) + mod_ref[2:3, :] * mix, lng_ref[...], lnb_ref[...])
    x1_ref[...] = x1
    u2 = x1 * (1.0 + mod_ref[4:5, :]) + mod_ref[3:4, :]
    packed = _pack_bf16_pair(u2[:, :D_MODEL // 2], u2[:, D_MODEL // 2:])
    for j in range(u2_ref.shape[0]):
        u2_ref[j] = packed[:, j * u2_ref.shape[2]:(j + 1) * u2_ref.shape[2]]

    u2_hi = u2.astype(BF16)
    u2_lo = (u2 - u2_hi.astype(F32)).astype(BF16)
    logits = (_dot(u2_hi, rw_ref[0]) + _dot(u2_lo, rw_ref[0]) + _dot(u2_hi, rw_ref[1])) + rb_ref[...]
    lane = lax.broadcasted_iota(jnp.int32, logits.shape, 1).astype(F32)
    idx_out = jnp.zeros(logits.shape, F32)
    wts_out = jnp.zeros(logits.shape, F32)
    chosen = jnp.zeros(logits.shape, F32)
    sels = []
    top0 = None
    den = None
    for k in range(TOP_K):
        m = jnp.max(logits, axis=-1, keepdims=True)
        sel = jnp.min(jnp.where(logits == m, lane, float(LANES)), axis=-1, keepdims=True)
        sels.append(sel)
        if k == 0:
            top0 = m
        e = jnp.exp(m - top0)
        den = e if den is None else den + e
        idx_out = jnp.where(lane == k, sel, idx_out)
        wts_out = jnp.where(lane == k, e, wts_out)
        chosen = jnp.where(lane == sel, 1.0, chosen)
        logits = jnp.where(lane == sel, -jnp.inf, logits)
    idx_ref[...] = idx_out.astype(jnp.int32)
    wts_ref[...] = wts_out * (1.0 / den)

    before = _dot(tri_ref[...], chosen.astype(BF16)) + cnt_scr[...]
    rank_out = jnp.zeros(logits.shape, F32)
    for k in range(TOP_K):
        rk = jnp.sum(jnp.where(lane == sels[k], before, 0.0), axis=-1, keepdims=True)
        rank_out = jnp.where(lane == k, rk, rank_out)
    rank_ref[...] = rank_out.astype(jnp.int32)
    cnt_scr[...] = cnt_scr[...] + jnp.sum(chosen, axis=0, keepdims=True)
    cnt_ref[...] = jnp.broadcast_to(cnt_scr[...], cnt_ref.shape).astype(jnp.int32)


def _router_params(router_w, router_b):
    rw = jnp.zeros((DEPTH, D_MODEL, LANES), F32).at[:, :, :N_EXPERTS].set(router_w)
    rb = jnp.full((DEPTH, 1, LANES), -1e30, F32).at[:, 0, :N_EXPERTS].set(router_b)
    rw_hi = rw.astype(BF16)
    return jnp.stack([rw_hi, (rw - rw_hi.astype(F32)).astype(BF16)], axis=1), rb


def _merge_call(att, rg, o_f, o_b, h, x, mods, p, l):
    tm = MERGE_TM
    ctx_tiles = N_CTX // tm
    rowb = lambda w, cb: pl.BlockSpec((tm, w), lambda i: (i, cb))
    ctxb = lambda w: pl.BlockSpec((tm, w), lambda i: (jnp.minimum(i, ctx_tiles - 1), 0))
    latb = lambda w, base=0: pl.BlockSpec((tm, w), lambda i: (jnp.maximum(i - ctx_tiles, 0) + base, 0))
    const = lambda shape: pl.BlockSpec(shape, lambda i: (0,) * len(shape))
    layer = lambda shape: pl.BlockSpec((None,) + shape, lambda i: (l,) + (0,) * len(shape))
    bw = BRANCH_W
    tri = jnp.asarray(np.tril(np.ones((tm, tm), np.float32), -1), BF16)
    return pl.pallas_call(
        functools.partial(_merge_kernel, ctx_tiles=ctx_tiles),
        grid=(N_TOK // tm,),
        in_specs=[ctxb(bw), latb(bw), ctxb(bw), latb(bw), ctxb(bw), latb(bw), ctxb(bw), latb(bw),
                  rowb(BRANCH_W, 9), rowb(D_MODEL, 5), rowb(D_MODEL, 6), rowb(D_MODEL, 7),
                  ctxb(D_MODEL), latb(D_MODEL, x[2]),
                  pl.BlockSpec((None, None, 6, D_MODEL), lambda i: (l, _cond_of_row_tile(i, tm), 0, 0)),
                  layer((1, HG_VAL)), layer((3, BRANCH_W, D_MODEL)), layer((D_MODEL, D_MODEL)),
                  layer((1, D_MODEL)), layer((1, D_MODEL)), layer((2, D_MODEL, LANES)), layer((1, LANES)),
                  const((tm, tm))],
        out_specs=[rowb(D_MODEL, 0), pl.BlockSpec((SC_SLABS, tm, SC_SLAB_W), lambda i: (0, i, 0)),
                   rowb(LANES, 0), rowb(LANES, 0), rowb(LANES, 0), const((8, LANES))],
        out_shape=[jax.ShapeDtypeStruct((N_TOK, D_MODEL), F32),
                   jax.ShapeDtypeStruct((SC_SLABS, N_TOK, SC_SLAB_W), F32),
                   jax.ShapeDtypeStruct((N_TOK, LANES), jnp.int32),
                   jax.ShapeDtypeStruct((N_TOK, LANES), jnp.int32),
                   jax.ShapeDtypeStruct((N_TOK, LANES), F32),
                   jax.ShapeDtypeStruct((8, LANES), jnp.int32)],
        scratch_shapes=[pltpu.VMEM((3, BRANCH_W, D_MODEL), BF16), pltpu.VMEM((D_MODEL, D_MODEL), BF16),
                        pltpu.VMEM((1, LANES), F32)],
        compiler_params=_cparams(("arbitrary",)),
        name="merge",
    )(att[0], att[1], rg[0], rg[1], o_f[0], o_f[1], o_b[0], o_b[1], h, h, h, h, x[0], x[1], mods,
      p['hg_norm'], p['w_branch'], p['w_out'], p['ln1_g'], p['ln1_b'], p['rw'], p['rb'], tri)


MOE_NB = 256


MOE_SLOTS = 3


def _moe_kernel(te_ref, tf_ref, tsl_ref, tnx_ref, tn2_ref, nt_ref, x_ref, w1_hbm, b1_ref, w2_hbm, b2_ref, perm_ref,
                y_ref, w1_f, w2_f, w1_bf, w2_bf, sem, *, layer):
    i = pl.program_id(0)
    half = MOE_NB // 2

    def weight_copies(e, s):
        return (pltpu.make_async_copy(w1_hbm.at[layer, e], w1_f.at[s], sem.at[0, s]),
                pltpu.make_async_copy(w2_hbm.at[layer, e], w2_f.at[s], sem.at[1, s]))

    @pl.when(i == 0)
    def _():
        for cp in weight_copies(te_ref[0], 0):
            cp.start()

        @pl.when(tnx_ref[0] >= 0)
        def _():
            for cp in weight_copies(tnx_ref[0], 1):
                cp.start()

    @pl.when(tf_ref[i] == 1)
    def _():
        s = tsl_ref[i]
        for cp in weight_copies(te_ref[i], s):
            cp.wait()
        ahead = tn2_ref[i]

        @pl.when(ahead >= 0)
        def _():
            for cp in weight_copies(ahead, jnp.where(s == 0, MOE_SLOTS - 1, s - 1)):
                cp.start()

        for b in range(2 * D_EXPERT // MOE_NB):
            blk = w1_f[s, :, b * MOE_NB:(b + 1) * MOE_NB].astype(BF16)
            w1_bf[:, b * MOE_NB:(b + 1) * MOE_NB] = _dot(blk, perm_ref[...]).astype(BF16)
        w2_bf[...] = w2_f[s].astype(BF16)

    @pl.when(i < nt_ref[0])
    def _():
        halves = [_unpack_bf16_pair(x_ref[j]) for j in range(x_ref.shape[0])]
        x = jnp.concatenate([lo for lo, _ in halves] + [hi for _, hi in halves], axis=-1).astype(BF16)
        h = _dot(x, w1_bf[...]) + b1_ref[...]
        acts = []
        for b in range(2 * D_EXPERT // MOE_NB):
            glu = jnp.minimum(h[:, b * MOE_NB:b * MOE_NB + half], SWIGLU_LIMIT)
            lin = jnp.clip(h[:, b * MOE_NB + half:(b + 1) * MOE_NB], -SWIGLU_LIMIT, SWIGLU_LIMIT)
            acts.append((glu * _sigmoid(SWIGLU_ALPHA * glu) * (lin + 1.0)).astype(BF16))
        act = jnp.concatenate(acts, axis=-1)
        y = _dot(act, w2_bf[...]) + b2_ref[...]
        y_ref[...] = _pack_bf16_pair(y[:, :D_MODEL // 2], y[:, D_MODEL // 2:])


def _moe_perm():
    half = MOE_NB // 2
    pm = np.zeros((MOE_NB, MOE_NB), np.float32)
    pm[2 * np.arange(half), np.arange(half)] = 1.0
    pm[2 * np.arange(half) + 1, half + np.arange(half)] = 1.0
    return jnp.asarray(pm, BF16)


def _moe_call(x_sorted, sched, w1, b1p, w2, b2, l):
    tm = MOE_TM
    emap = lambda i, te, *_: (l, te[i], 0, 0)
    tile = lambda i, nt: jnp.minimum(i, nt[0] - 1)
    grid_spec = pltpu.PrefetchScalarGridSpec(
        num_scalar_prefetch=6,
        grid=(MOE_TILES,),
        in_specs=[pl.BlockSpec((SC_SLABS, tm, SC_SLAB_W), lambda i, *s: (0, tile(i, s[-1]), 0)),
                  pl.BlockSpec(memory_space=pl.ANY),
                  pl.BlockSpec((None, None, 1, 2 * D_EXPERT), emap),
                  pl.BlockSpec(memory_space=pl.ANY),
                  pl.BlockSpec((None, None, 1, D_MODEL), emap),
                  pl.BlockSpec((MOE_NB, MOE_NB), lambda i, *_: (0, 0))],
        out_specs=pl.BlockSpec((tm, D_MODEL // 2), lambda i, *s: (tile(i, s[-1]), 0)),
        scratch_shapes=[pltpu.VMEM((MOE_SLOTS, D_MODEL, 2 * D_EXPERT), F32),
                        pltpu.VMEM((MOE_SLOTS, D_EXPERT, D_MODEL), F32),
                        pltpu.VMEM((D_MODEL, 2 * D_EXPERT), BF16), pltpu.VMEM((D_EXPERT, D_MODEL), BF16),
                        pltpu.SemaphoreType.DMA((2, MOE_SLOTS))],
    )
    return pl.pallas_call(
        functools.partial(_moe_kernel, layer=l),
        grid_spec=grid_spec,
        out_shape=jax.ShapeDtypeStruct((MOE_TILES * tm, D_MODEL // 2), F32),
        compiler_params=_cparams(("arbitrary",)),
        name="moe",
    )(*sched, x_sorted, w1, b1p, w2, b2, _moe_perm())


def _route(idx, rank, counts):
    tm = MOE_TM
    tiles_e = (counts + tm - 1) // tm
    eid = np.arange(N_EXPERTS, dtype=np.int32)
    earlier = (eid[None, :] <= eid[:, None]).astype(np.int32)
    tile_end = jnp.sum(earlier * tiles_e[None, :], axis=1)
    tile_start = tile_end - tiles_e
    lookup = lambda table, keys: jnp.sum(jnp.where(keys[..., None] == eid, table, 0), axis=-1)
    pos_t = lookup(tile_start, idx.T) * tm + rank.T
    n_used = tile_end[N_EXPERTS - 1]
    tile_ids = jnp.arange(MOE_TILES, dtype=jnp.int32)
    tid = jnp.minimum(tile_ids, n_used - 1)
    tile_expert = jnp.sum((tile_end[None, :] <= tid[:, None]).astype(jnp.int32), axis=1)
    tile_first = ((tile_ids == lookup(tile_start, tile_expert)) & (tile_ids < n_used)).astype(jnp.int32)
    has_rows = (tiles_e > 0).astype(jnp.int32)
    slot_e = (jnp.sum(earlier * has_rows[None, :], axis=1) - 1) % MOE_SLOTS
    later = jnp.where((eid[None, :] > eid[:, None]) & (has_rows[None, :] > 0), eid[None, :], N_EXPERTS)
    next_e = jnp.min(later, axis=1)
    next2_e = jnp.where(next_e >= N_EXPERTS, N_EXPERTS, lookup(next_e, jnp.minimum(next_e, N_EXPERTS - 1)))
    to_id = lambda e: jnp.where(e >= N_EXPERTS, -1, e).astype(jnp.int32)
    sched = (tile_expert, tile_first, lookup(slot_e, tile_expert).astype(jnp.int32),
             lookup(to_id(next_e), tile_expert).astype(jnp.int32),
             lookup(to_id(next2_e), tile_expert).astype(jnp.int32), n_used.reshape(1))
    return pos_t, sched


SC_WINDOW = 128
SC_SLABS = 2
SC_SLAB_W = D_MODEL // 2 // SC_SLABS


def _dispatch_rows(u2, pos_t):
    mesh = plsc.VectorSubcoreMesh(core_axis_name="core", subcore_axis_name="subcore")

    @functools.partial(pl.kernel, mesh=mesh, scratch_types=[],
                       out_type=jax.ShapeDtypeStruct((SC_SLABS, MOE_TILES * MOE_TM, SC_SLAB_W), F32))
    def dispatch(x_hbm, i_hbm, o_hbm):
        for j in range(SC_SLABS):
            def body(x_vmem, i_vmem, j=j):
                for k in range(TOP_K):
                    pltpu.sync_copy(x_vmem, o_hbm.at[j].at[i_vmem.at[k]])

            pltpu.emit_pipeline(
                body,
                grid=(N_TOK // SC_WINDOW,),
                in_specs=[pl.BlockSpec((SC_WINDOW, SC_SLAB_W), index_map=lambda i: (i, 0)),
                          pl.BlockSpec((TOP_K, SC_WINDOW), index_map=lambda i: (0, i))],
                out_specs=[],
                core_axis_name=("core", "subcore"),
                dimension_semantics=(pltpu.PARALLEL,),
            )(x_hbm.at[j], i_hbm)

    return dispatch(u2, pos_t)


def _final_kernel(yg_ref, wts_ref, x1_ref, mod_ref, modn_ref, lng_ref, lnb_ref, o_ref, un_ref, *, ctx_tiles):
    wts = wts_ref[...]
    ffn_lo = ffn_hi = None
    for k in range(TOP_K):
        lo, hi = _unpack_bf16_pair(yg_ref[k])
        w = wts[:, k:k + 1]
        ffn_lo = w * lo if ffn_lo is None else ffn_lo + w * lo
        ffn_hi = w * hi if ffn_hi is None else ffn_hi + w * hi
    ffn = jnp.concatenate([ffn_lo, ffn_hi], axis=-1)
    x2 = _layer_norm(DN_ALPHA * x1_ref[...] + mod_ref[5:6, :] * ffn, lng_ref[...], lnb_ref[...])
    if ctx_tiles is None:
        o_ref[...] = x2
        un_ref[...] = (x2 * (1.0 + modn_ref[1:2, :]) + modn_ref[0:1, :]).astype(BF16)
    else:
        @pl.when(pl.program_id(0) < ctx_tiles)
        def _():
            o_ref[...] = x2

        @pl.when(pl.program_id(0) >= ctx_tiles)
        def _():
            un_ref[...] = x2


def _final_call(yg, wts, x1, mods, ln_g, ln_b, l):
    tm = 256
    last = l == DEPTH - 1
    ln = min(l + 1, DEPTH - 1)
    ctx_tiles = N_CTX // tm
    lnspec = pl.BlockSpec((None, 1, D_MODEL), lambda i: (l, 0, 0))
    modspec = lambda lyr: pl.BlockSpec((None, None, 6, D_MODEL), lambda i: (lyr, _cond_of_row_tile(i, tm), 0, 0))
    if last:
        out_specs = [pl.BlockSpec((tm, D_MODEL), lambda i: (jnp.minimum(i, ctx_tiles - 1), 0)),
                     pl.BlockSpec((tm, D_MODEL), lambda i: (jnp.maximum(i - ctx_tiles, 0), 0))]
        out_shape = [jax.ShapeDtypeStruct((N_CTX, D_MODEL), F32), jax.ShapeDtypeStruct((N_LAT, D_MODEL), F32)]
    else:
        out_specs = [pl.BlockSpec((tm, D_MODEL), lambda i: (i, 0)), pl.BlockSpec((tm, D_MODEL), lambda i: (i, 0))]
        out_shape = [jax.ShapeDtypeStruct((N_TOK, D_MODEL), F32), jax.ShapeDtypeStruct((N_TOK, D_MODEL), BF16)]
    return pl.pallas_call(
        functools.partial(_final_kernel, ctx_tiles=ctx_tiles if last else None),
        grid=(N_TOK // tm,),
        in_specs=[pl.BlockSpec((TOP_K, tm, D_MODEL // 2), lambda i: (0, i, 0)),
                  pl.BlockSpec((tm, LANES), lambda i: (i, 0)),
                  pl.BlockSpec((tm, D_MODEL), lambda i: (i, 0)),
                  modspec(l), modspec(ln), lnspec, lnspec],
        out_specs=out_specs,
        out_shape=out_shape,
        compiler_params=_cparams(("arbitrary",)),
        name="final",
    )(yg, wts, x1, mods, mods, ln_g, ln_b)


def kernel(x_prompt, x_sample, cache_attn_k, cache_attn_v, state_rglru, state_hgrn, c, c_ctx, w_ada, b_ada, w_in, da_lambda, da_subln, rg_conv_w, rg_conv_b, rg_gate_w, rg_gate_b, rg_lambda, hg_lb, hg_norm, w_branch, w_out, ln1_g, ln1_b, router_w, router_b, w1, b1, w2, b2, ln2_g, ln2_b):
    rw, rb = _router_params(router_w, router_b)
    p = dict(hg_norm=hg_norm.reshape(DEPTH, 1, HG_VAL), w_branch=w_branch, w_out=w_out,
             ln1_g=ln1_g.reshape(DEPTH, 1, D_MODEL), ln1_b=ln1_b.reshape(DEPTH, 1, D_MODEL), rw=rw, rb=rb)
    ln2_g = ln2_g.reshape(DEPTH, 1, D_MODEL)
    ln2_b = ln2_b.reshape(DEPTH, 1, D_MODEL)
    rg_conv_b = rg_conv_b.reshape(DEPTH, 1, RG_WIDTH)
    wg, bg = _rg_gate_dense(rg_gate_w, rg_gate_b)

    x_ctx = x_prompt.reshape(N_CTX, D_MODEL)
    x_lat = x_sample.reshape(N_LAT, D_MODEL)
    cond = jnp.concatenate([c_ctx[None, :], c, jnp.zeros((N_COND - 1 - DEC_BATCH, D_MODEL), F32)], axis=0)
    mods = _adaln_all(cond, w_ada, b_ada).reshape(DEPTH, N_COND, 6, D_MODEL)

    pr = jax.nn.softmax(hg_lb.astype(F32), axis=0)
    lbs = jnp.cumsum(pr, axis=0) - pr[0]
    dl = da_lambda.astype(F32)
    lambda_init = [0.8 - 0.6 * math.exp(-0.3 * l) for l in range(DEPTH)]
    lam_all = (jnp.exp(jnp.sum(dl[:, 0] * dl[:, 1], -1)) - jnp.exp(jnp.sum(dl[:, 2] * dl[:, 3], -1))
               + jnp.asarray(lambda_init, F32))
    lam_v = jnp.broadcast_to(lam_all[:, None, None], (DEPTH, 1, LANES))
    sub_v = da_subln.reshape(DEPTH, 1, DA_V_DIM)

    cache_k = cache_attn_k.reshape(DEC_BATCH, DEPTH, PAST_LEN, BRANCH_W)
    cache_v = cache_attn_v.reshape(DEC_BATCH, DEPTH, PAST_LEN, BRANCH_W)

    b1p = b1.reshape(DEPTH, N_EXPERTS, 2 * D_EXPERT // MOE_NB, MOE_NB // 2, 2)
    b1p = jnp.swapaxes(b1p, -1, -2).reshape(DEPTH, N_EXPERTS, 1, 2 * D_EXPERT)
    b2r = b2.reshape(DEPTH, N_EXPERTS, 1, D_MODEL)

    rgs = []
    kv_all = None
    hg_all = None
    u = _modulate(x_ctx, x_lat, mods, 0)
    x_pair = (x_ctx, x_lat, 0)
    for l in range(DEPTH):
        h = _in_proj(u, w_in, l)
        att, kv_all = _attention(h, cache_k, cache_v, lam_v, sub_v, l, 1.0 - lambda_init[l], kv_all)

        rg_c, hl_c = _rglru_call(h, rg_conv_w, rg_conv_b, wg, bg, rg_lambda, None, l, BATCH, SEQ, 0)
        rg_l, _ = _rglru_call(h, rg_conv_w, rg_conv_b, wg, bg, rg_lambda, state_rglru, l,
                              DEC_BATCH, DEC_SEQ, N_CTX // DEC_SEQ)
        rgs.append(hl_c)

        of_c, ob_c, hg_all = _hgrn_call(h, lbs, None, l, BATCH, SEQ, 0, hg_all)
        of_l, ob_l, _ = _hgrn_call(h, lbs, state_hgrn, l, DEC_BATCH, DEC_SEQ, N_CTX)

        x1, u2, idx, rank, wts, cnt = _merge_call(att, (rg_c, rg_l), (of_c, of_l), (ob_c, ob_l), h, x_pair,
                                                  mods, p, l)

        pos_t, sched = _route(idx[:, :TOP_K], rank[:, :TOP_K], cnt[0, :N_EXPERTS])
        x_sorted = _dispatch_rows(u2, pos_t)
        y_sorted = _moe_call(x_sorted, sched, w1, b1p, w2, b2r, l)
        yg = y_sorted.at[pos_t.reshape(-1)].get(mode='promise_in_bounds').reshape(TOP_K, N_TOK, D_MODEL // 2)
        x, u = _final_call(yg, wts, x1, mods, ln2_g, ln2_b, l)
        x_pair = (x, x, N_CTX // MERGE_TM)

    return (x.reshape(BATCH, SEQ, D_MODEL), u.reshape(DEC_BATCH, DEC_SEQ, D_MODEL),
            kv_all[0].reshape(BATCH, DEPTH, SEQ, DA_HEADS, 2, DA_HEAD_DIM),
            kv_all[1].reshape(BATCH, DEPTH, SEQ, DA_HEADS, DA_V_DIM),
            jnp.stack(rgs, axis=1), hg_all)
```

```python
import functools
import math

import numpy as np
import jax
import jax.numpy as jnp
from jax import lax
from jax.experimental import pallas as pl
from jax.experimental.pallas import tpu as pltpu
from jax.experimental.pallas import tpu_sc as plsc

F32 = jnp.float32
BF16 = jnp.bfloat16
HIGHEST = lax.Precision.HIGHEST

D_MODEL = 1024
BATCH = 16
SEQ = 256
DEPTH = 4
DEC_BATCH = 4
DEC_SEQ = 1024
PAST_LEN = 256
GRID_W = 64
BRANCH_W = 512
DA_HEADS = 4
DA_HEAD_DIM = 64
DA_V_DIM = 128
ROPE_BASE = 10000.0
RG_WIDTH = 512
RG_BLOCKS = 8
RG_BLOCK_W = 64
RG_CONV_W = 4
RG_C = 8.0
HG_HEADS = 4
HG_KEY = 128
HG_VAL = 128
N_EXPERTS = 32
TOP_K = 4
D_EXPERT = 1024
SWIGLU_ALPHA = 1.702
SWIGLU_LIMIT = 7.0
DN_ALPHA = (2 * DEPTH) ** 0.25
NORM_EPS = 1e-5
D_IN = 10 * BRANCH_W + 3 * D_MODEL

N_CTX = BATCH * SEQ
N_LAT = DEC_BATCH * DEC_SEQ
N_TOK = N_CTX + N_LAT
N_COND = 8

LANES = 128
VMEM_LIMIT = 56 * 1024 * 1024

HG_CHUNK = 128
HG_STEP_CHUNKS = 2
HG_LEVELS = (1, 2, 4, 8, 16, 32, 64)
MOE_TM = 256
MERGE_TM = 256
MOE_TILES = (N_TOK * TOP_K) // MOE_TM + N_EXPERTS


def _cparams(sem):
    return pltpu.CompilerParams(dimension_semantics=sem, vmem_limit_bytes=VMEM_LIMIT)


def _sigmoid(x):
    return 0.5 * jnp.tanh(0.5 * x) + 0.5


def _pack_bf16_pair(lo, hi):
    lo_bits = pltpu.bitcast(lo.astype(BF16).astype(F32), jnp.uint32) >> 16
    hi_bits = pltpu.bitcast(hi.astype(BF16).astype(F32), jnp.uint32) & jnp.uint32(0xFFFF0000)
    return pltpu.bitcast(hi_bits | lo_bits, F32)


def _unpack_bf16_pair(words):
    bits = pltpu.bitcast(words, jnp.uint32)
    return pltpu.bitcast(bits << 16, F32), pltpu.bitcast(bits & jnp.uint32(0xFFFF0000), F32)


def _dot(a, b):
    return jnp.dot(a, b, preferred_element_type=F32)


def _dot_nt(a, b):
    return lax.dot_general(a, b, (((1,), (1,)), ((), ())), preferred_element_type=F32)


def _cond_of_row_tile(i, tm):
    r = i * tm
    return jnp.where(r < N_CTX, 0, 1 + (r - N_CTX) // DEC_SEQ)


def _ada_kernel(c_ref, w_ref, b_ref, o_ref):
    c = c_ref[...]
    s = c * _sigmoid(c)
    o_ref[0] = jnp.dot(s, w_ref[0], precision=HIGHEST, preferred_element_type=F32) + b_ref[0]


def _adaln_all(cond, w_ada, b_ada):
    tn = 1536
    return pl.pallas_call(
        _ada_kernel,
        grid=(DEPTH, 6 * D_MODEL // tn),
        in_specs=[pl.BlockSpec((N_COND, D_MODEL), lambda l, j: (0, 0)),
                  pl.BlockSpec((1, D_MODEL, tn), lambda l, j: (l, 0, j)),
                  pl.BlockSpec((1, 1, tn), lambda l, j: (l, 0, j))],
        out_specs=pl.BlockSpec((1, N_COND, tn), lambda l, j: (l, 0, j)),
        out_shape=jax.ShapeDtypeStruct((DEPTH, N_COND, 6 * D_MODEL), F32),
        compiler_params=_cparams(("arbitrary", "arbitrary")),
        name="adaln",
    )(cond, w_ada, b_ada.reshape(DEPTH, 1, 6 * D_MODEL))


def _modulate_kernel(xc_ref, xl_ref, mod_ref, u_ref, *, ctx_tiles):
    x = jnp.where(pl.program_id(0) < ctx_tiles, xc_ref[...], xl_ref[...])
    u_ref[...] = (x * (1.0 + mod_ref[1:2, :]) + mod_ref[0:1, :]).astype(BF16)


def _modulate(x_ctx, x_lat, mods, l):
    tm = 1024
    ctx_tiles = N_CTX // tm
    return pl.pallas_call(
        functools.partial(_modulate_kernel, ctx_tiles=ctx_tiles),
        grid=(N_TOK // tm,),
        in_specs=[pl.BlockSpec((tm, D_MODEL), lambda i: (jnp.minimum(i, ctx_tiles - 1), 0)),
                  pl.BlockSpec((tm, D_MODEL), lambda i: (jnp.maximum(i - ctx_tiles, 0), 0)),
                  pl.BlockSpec((None, None, 6, D_MODEL), lambda i: (l, _cond_of_row_tile(i, tm), 0, 0))],
        out_specs=pl.BlockSpec((tm, D_MODEL), lambda i: (i, 0)),
        out_shape=jax.ShapeDtypeStruct((N_TOK, D_MODEL), BF16),
        compiler_params=_cparams(("arbitrary",)),
        name="modulate",
    )(x_ctx, x_lat, mods)


def _in_kernel(u_ref, w_ref, o_ref, wbf_ref):
    @pl.when(pl.program_id(1) == 0)
    def _():
        wbf_ref[...] = w_ref[...].astype(BF16)

    o_ref[...] = _dot(u_ref[...], wbf_ref[...]).astype(BF16)


def _in_proj(u, w_in, l):
    tm, tn = 1024, 2048
    return pl.pallas_call(
        _in_kernel,
        grid=(D_IN // tn, N_TOK // tm),
        in_specs=[pl.BlockSpec((tm, D_MODEL), lambda j, i: (i, 0)),
                  pl.BlockSpec((None, D_MODEL, tn), lambda j, i: (l, 0, j))],
        out_specs=pl.BlockSpec((tm, tn), lambda j, i: (i, j)),
        out_shape=jax.ShapeDtypeStruct((N_TOK, D_IN), BF16),
        scratch_shapes=[pltpu.VMEM((D_MODEL, tn), BF16)],
        compiler_params=_cparams(("arbitrary", "arbitrary")),
        name="in_proj",
    )(u, w_in)


def _rope(x, cos, sin_signed):
    lane = lax.broadcasted_iota(jnp.int32, x.shape, 1)
    first = (lane & 31) < 16
    partner = jnp.where(first, pltpu.roll(x, LANES - 16, 1), pltpu.roll(x, 16, 1))
    return x * cos + partner * sin_signed


LOG2E = 1.4426950408889634


def _attn_body(q, keys, vals, lam, subln, out_scale):
    lane = lax.broadcasted_iota(jnp.int32, q.shape, 1)
    qs = q * (DA_HEAD_DIM ** -0.5 * LOG2E)
    acc = None
    for m in range(2):
        in_map = (lane < DA_HEAD_DIM) if m == 0 else (lane >= DA_HEAD_DIM)
        qm = jnp.where(in_map, qs, 0.0).astype(BF16)
        s = [_dot_nt(qm, k) for k in keys]
        mx = s[0].max(axis=-1, keepdims=True)
        for si in s[1:]:
            mx = jnp.maximum(mx, si.max(axis=-1, keepdims=True))
        e = [jnp.exp2(si - mx) for si in s]
        den = e[0].sum(axis=-1, keepdims=True)
        for ei in e[1:]:
            den = den + ei.sum(axis=-1, keepdims=True)
        pv = _dot(e[0].astype(BF16), vals[0])
        for ei, v in zip(e[1:], vals[1:]):
            pv = pv + _dot(ei.astype(BF16), v)
        coef = (1.0 / den) if m == 0 else (-lam[:, 0:1] / den)
        acc = pv * coef if acc is None else acc + pv * coef
    y = acc * lax.rsqrt(jnp.mean(acc * acc, axis=-1, keepdims=True) + NORM_EPS)
    return y * subln * out_scale


def _attn_ctx_kernel(lam_ref, sub_ref, q_ref, k_ref, v_ref, *rest, out_scale):
    o_ref, kout_ref, vout_ref = rest[-3:]
    kout_ref[...] = k_ref[...].astype(F32)
    vout_ref[...] = v_ref[...].astype(F32)
    for hd in range(DA_HEADS):
        cols = slice(hd * LANES, (hd + 1) * LANES)
        o_ref[:, cols] = _attn_body(q_ref[:, cols].astype(F32), [k_ref[:, cols]], [v_ref[:, cols]],
                                    lam_ref[...], sub_ref[...], out_scale)


def _attn_lat_kernel(lam_ref, sub_ref, q_ref, k_ref, v_ref, kc_ref, vc_ref, cos_ref, sin_ref,
                     o_ref, k_scr, kc_scr, vc_scr, *, out_scale, tq):
    k_scr[...] = _rope(k_ref[...].astype(F32), cos_ref[...], sin_ref[...]).astype(BF16)
    kc_scr[...] = kc_ref[...].astype(BF16)
    vc_scr[...] = vc_ref[...].astype(BF16)

    def q_block(qi, carry):
        rows = pl.ds(pl.multiple_of(qi * tq, tq), tq)
        q = _rope(q_ref[rows, :].astype(F32), cos_ref[rows, :], sin_ref[rows, :])
        o_ref[rows, :] = _attn_body(q, [k_scr[...], kc_scr[...]], [v_ref[...], vc_scr[...]],
                                    lam_ref[...], sub_ref[...], out_scale)
        return carry

    lax.fori_loop(0, DEC_SEQ // tq, q_block, 0)


def _rope_tables():
    t = np.arange(DEC_SEQ)
    row = (t // GRID_W).astype(np.float64)
    col = (t % GRID_W).astype(np.float64)
    d = DA_HEAD_DIM // 2
    inv = ROPE_BASE ** (-np.arange(0, d, 2, dtype=np.float64) / d)
    a_row = row[:, None] * inv[None, :]
    a_col = col[:, None] * inv[None, :]
    cos = np.concatenate([np.cos(a_row), np.cos(a_row), np.cos(a_col), np.cos(a_col)], -1)
    sin = np.concatenate([-np.sin(a_row), np.sin(a_row), -np.sin(a_col), np.sin(a_col)], -1)
    return (jnp.asarray(np.tile(cos, (1, 2)), F32), jnp.asarray(np.tile(sin, (1, 2)), F32))


def _attention(h, cache_k, cache_v, lam_v, sub_v, l, out_scale, kv_prev):
    small = pl.BlockSpec((None, 1, LANES), lambda *a: (l, 0, 0))
    kv_blk = pl.BlockSpec((None, None, SEQ, BRANCH_W), lambda b: (b, l, 0, 0))
    kv_shape = jax.ShapeDtypeStruct((BATCH, DEPTH, SEQ, BRANCH_W), F32)
    carried = () if kv_prev is None else tuple(kv_prev)
    att_ctx, k_all, v_all = pl.pallas_call(
        functools.partial(_attn_ctx_kernel, out_scale=out_scale),
        grid=(BATCH,),
        in_specs=[small, small,
                  pl.BlockSpec((SEQ, BRANCH_W), lambda b: (b, 0)),
                  pl.BlockSpec((SEQ, BRANCH_W), lambda b: (b, 1)),
                  pl.BlockSpec((SEQ, BRANCH_W), lambda b: (b, 2))]
                 + [pl.BlockSpec(memory_space=pl.ANY)] * len(carried),
        out_specs=[pl.BlockSpec((SEQ, BRANCH_W), lambda b: (b, 0)), kv_blk, kv_blk],
        out_shape=[jax.ShapeDtypeStruct((N_CTX, BRANCH_W), F32), kv_shape, kv_shape],
        input_output_aliases={5: 1, 6: 2} if carried else {},
        compiler_params=_cparams(("arbitrary",)),
        name="attn_ctx",
    )(lam_v, sub_v, h, h, h, *carried)

    tq = 256
    cos, sin = _rope_tables()
    row0 = N_CTX // DEC_SEQ
    seq_blk = lambda cb: pl.BlockSpec((DEC_SEQ, LANES), lambda b, hd: (row0 + b, cb + hd))
    cache_blk = pl.BlockSpec((None, None, PAST_LEN, LANES), lambda b, hd: (b, l, 0, hd))
    table = pl.BlockSpec((DEC_SEQ, LANES), lambda b, hd: (0, 0))
    att_lat = pl.pallas_call(
        functools.partial(_attn_lat_kernel, out_scale=out_scale, tq=tq),
        grid=(DEC_BATCH, DA_HEADS),
        in_specs=[small, small, seq_blk(0), seq_blk(4), seq_blk(8), cache_blk, cache_blk, table, table],
        out_specs=pl.BlockSpec((DEC_SEQ, LANES), lambda b, hd: (b, hd)),
        out_shape=jax.ShapeDtypeStruct((N_LAT, BRANCH_W), F32),
        scratch_shapes=[pltpu.VMEM((DEC_SEQ, LANES), BF16), pltpu.VMEM((PAST_LEN, LANES), BF16),
                        pltpu.VMEM((PAST_LEN, LANES), BF16)],
        compiler_params=_cparams(("arbitrary", "arbitrary")),
        name="attn_lat",
    )(lam_v, sub_v, h, h, h, cache_k, cache_v, cos, sin)
    return (att_ctx, att_lat), (k_all, v_all)


def _gelu_tanh(x):
    return 0.5 * x * (1.0 + jnp.tanh(math.sqrt(2.0 / math.pi) * (x + 0.044715 * (x * x * x))))


def _rg_kernel(*refs, seq, has_h0):
    if has_h0:
        rx_ref, gate_ref, cw_ref, cb_ref, wg_ref, bg_ref, lam_ref, h0_ref = refs[:8]
        rest = refs[8:]
    else:
        rx_ref, gate_ref, cw_ref, cb_ref, wg_ref, bg_ref, lam_ref = refs[:7]
        h0_ref = None
        rest = refs[7:]
    out_ref, hl_ref, a_scr, b_scr, h_scr = rest

    x = rx_ref[...].astype(F32)
    row = lax.broadcasted_iota(jnp.int32, x.shape, 0)
    xr = cb_ref[...] + cw_ref[2:3, :] * x
    for j in (0, 1, 3):
        d = j - RG_CONV_W // 2
        shifted = pltpu.roll(x, (-d) % seq, 0)
        valid = (row + d >= 0) & (row + d < seq)
        xr = xr + cw_ref[j:j + 1, :] * jnp.where(valid, shifted, 0.0)

    g = _dot(xr.astype(BF16), wg_ref[...]) + bg_ref[...]
    for dr in range(2):
        r = _sigmoid(g[:, (2 * dr) * RG_WIDTH:(2 * dr + 1) * RG_WIDTH])
        i = _sigmoid(g[:, (2 * dr + 1) * RG_WIDTH:(2 * dr + 2) * RG_WIDTH])
        lam = lam_ref[dr:dr + 1, :]
        softplus_neg = jnp.maximum(-lam, 0.0) + jnp.log(1.0 + jnp.exp(-jnp.abs(lam)))
        log_a = (-RG_C * softplus_neg) * r
        a = jnp.exp(log_a)
        a_scr[dr] = a
        b_scr[dr] = jnp.sqrt(1.0 - a * a) * i * xr

    if has_h0:
        hf0, hb0 = h0_ref[0:1, :], h0_ref[1:2, :]
    else:
        hf0 = hb0 = jnp.zeros((1, RG_WIDTH), F32)

    def step(t, carry):
        hf, hb = carry
        tb = seq - 1 - t
        hf = a_scr[0, pl.ds(t, 1), :] * hf + b_scr[0, pl.ds(t, 1), :]
        h_scr[0, pl.ds(t, 1), :] = hf
        hb = a_scr[1, pl.ds(tb, 1), :] * hb + b_scr[1, pl.ds(tb, 1), :]
        h_scr[1, pl.ds(tb, 1), :] = hb
        return hf, hb

    hf, hb = lax.fori_loop(0, seq, step, (hf0, hb0), unroll=8)
    hl_ref[0:1, :] = hf
    hl_ref[1:2, :] = hb
    out_ref[...] = (h_scr[0] + h_scr[1]) * _gelu_tanh(gate_ref[...].astype(F32))


def _rglru_call(h, conv_w, conv_b, wg, bg, lam, h0, l, nseq, seq, row_block0):
    has_h0 = h0 is not None
    layer = lambda shape: pl.BlockSpec((None,) + shape, lambda b: (l,) + (0,) * len(shape))
    in_specs = [pl.BlockSpec((seq, RG_WIDTH), lambda b: (row_block0 + b, 3)),
                pl.BlockSpec((seq, RG_WIDTH), lambda b: (row_block0 + b, 4)),
                layer((RG_CONV_W, RG_WIDTH)), layer((1, RG_WIDTH)),
                layer((RG_WIDTH, 4 * RG_WIDTH)), layer((1, 4 * RG_WIDTH)), layer((2, RG_WIDTH))]
    args = [h, h, conv_w, conv_b, wg, bg, lam]
    if has_h0:
        in_specs.append(pl.BlockSpec((None, None, 2, RG_WIDTH), lambda b: (b, l, 0, 0)))
        args.append(h0)
    return pl.pallas_call(
        functools.partial(_rg_kernel, seq=seq, has_h0=has_h0),
        grid=(nseq,),
        in_specs=in_specs,
        out_specs=[pl.BlockSpec((seq, RG_WIDTH), lambda b: (b, 0)),
                   pl.BlockSpec((None, 2, RG_WIDTH), lambda b: (b, 0, 0))],
        out_shape=[jax.ShapeDtypeStruct((nseq * seq, RG_WIDTH), F32),
                   jax.ShapeDtypeStruct((nseq, 2, RG_WIDTH), F32)],
        scratch_shapes=[pltpu.VMEM((2, seq, RG_WIDTH), F32)] * 3,
        compiler_params=_cparams(("arbitrary",)),
        name="rglru_lat" if has_h0 else "rglru_ctx",
    )(*args)


def _rg_gate_dense(gate_w, gate_b):
    eye = jnp.eye(RG_BLOCKS, dtype=BF16)
    blocks = jnp.transpose(gate_w, (0, 3, 4, 1, 2, 5)).astype(BF16)
    wg = blocks[:, :, :, :, :, None, :] * eye[None, :, None, None, None, :, None]
    return wg.reshape(DEPTH, RG_WIDTH, 4 * RG_WIDTH), gate_b.reshape(DEPTH, 1, 4 * RG_WIDTH)


def _hg_masks():
    c = HG_CHUNK
    t = np.arange(c)[:, None]
    s = np.arange(c)[None, :]
    lvl = np.zeros((2, len(HG_LEVELS), c, c), np.float32)
    for n, h in enumerate(HG_LEVELS):
        same = (t // (2 * h)) == (s // (2 * h))
        lvl[0, n] = same & ((t // h) % 2 == 1) & ((s // h) % 2 == 0)
        lvl[1, n] = same & ((t // h) % 2 == 0) & ((s // h) % 2 == 1)
    return jnp.asarray(lvl)


def _hg_chunk(q_raw, z, v, lb, st, lvl, rev):
    c = HG_CHUNK
    q = q_raw * _sigmoid(q_raw)
    sig = _sigmoid(z)
    f = lb + (1.0 - lb) * sig
    kk = (1.0 - lb) * (1.0 - sig)

    rowk = lax.broadcasted_iota(jnp.int32, (c, HG_KEY), 0)
    pos = rowk if not rev else (c - 1) - rowk
    earlier = lambda x, d: pltpu.roll(x, d if not rev else c - d, 0)
    later = lambda x, d: pltpu.roll(x, c - d if not rev else d, 0)

    def block_row(x, n, p):
        r = p if not rev else n - 1 - p
        x3 = x.reshape(c // n, n, HG_KEY)
        return jnp.broadcast_to(x3[:, r:r + 1, :], x3.shape).reshape(c, HG_KEY)

    pair = f * earlier(f, 1)
    f_next = later(f, 1)
    p4 = pos & 3
    a = {1: f, 2: jnp.where((pos & 1) == 1, pair, f)}
    b = {2: jnp.where((pos & 1) == 0, f_next, 1.0)}
    a[4] = a[2] * jnp.where(p4 == 2, earlier(pair, 1), jnp.where(p4 == 3, earlier(pair, 2), 1.0))
    b[4] = jnp.where(p4 == 3, 1.0, jnp.where(p4 == 2, f_next,
                                             jnp.where(p4 == 1, later(pair, 2), f_next * later(pair, 3))))
    h = 4
    while h < c:
        in_later_half = (pos & h) != 0
        a[2 * h] = jnp.where(in_later_half, a[h] * block_row(a[h], 2 * h, h - 1), a[h])
        b[2 * h] = jnp.where(in_later_half, b[h], b[h] * block_row(a[h], 2 * h, 2 * h - 1))
        h *= 2

    row = lax.broadcasted_iota(jnp.int32, (c, c), 0)
    col = lax.broadcasted_iota(jnp.int32, (c, c), 1)
    scores = jnp.where(row == col, jnp.sum(q * kk, axis=-1, keepdims=True), 0.0)
    for n, h in enumerate(HG_LEVELS):
        qa = (q * a[h]).astype(BF16)
        kb = (kk if h == 1 else kk * b[h]).astype(BF16)
        scores = scores + lvl[n] * _dot_nt(qa, kb)

    vb = v.astype(BF16)
    o = _dot(scores.astype(BF16), vb)
    o = o + _dot_nt((q * a[c]).astype(BF16), st.astype(BF16))
    total = a[c][c - 1:c, :] if not rev else a[c][0:1, :]
    st_new = st * total + _dot(v.T.astype(BF16), (kk * b[c]).astype(BF16))
    return o, st_new


def _hg_kernel(*refs, has_s0, nchunk):
    if has_s0:
        (qf_ref, qb_ref, zf_ref, zb_ref, vf_ref, vb_ref, lb_ref, lvl_ref, s0_ref,
         of_ref, ob_ref, sfin_ref, st_scr) = refs
    else:
        qf_ref, qb_ref, zf_ref, zb_ref, vf_ref, vb_ref, lb_ref, lvl_ref = refs[:8]
        of_ref, ob_ref, sfin_ref, st_scr = refs[-4:]
        s0_ref = None
    ci = pl.program_id(1)

    @pl.when(ci == 0)
    def _():
        for dr in range(2):
            for hd in range(HG_HEADS):
                st_scr[dr, hd] = s0_ref[dr, hd].T if has_s0 else jnp.zeros((HG_VAL, HG_KEY), F32)

    last = ci == nchunk // HG_STEP_CHUNKS - 1
    c = HG_CHUNK
    for hd in range(HG_HEADS):
        cols = slice(hd * LANES, (hd + 1) * LANES)
        st_f, st_b = st_scr[0, hd], st_scr[1, hd]
        for sc in range(HG_STEP_CHUNKS):
            rf = slice(sc * c, (sc + 1) * c)
            rb = slice((HG_STEP_CHUNKS - 1 - sc) * c, (HG_STEP_CHUNKS - sc) * c)
            ldf = lambda ref: ref[rf, cols].astype(F32)
            ldb = lambda ref: ref[rb, cols].astype(F32)
            o_f, st_f = _hg_chunk(ldf(qf_ref), ldf(zf_ref), ldf(vf_ref), lb_ref[0:1, cols], st_f, lvl_ref[0], False)
            o_b, st_b = _hg_chunk(ldb(qb_ref), ldb(zb_ref), ldb(vb_ref), lb_ref[1:2, cols], st_b, lvl_ref[1], True)
            of_ref[rf, cols] = o_f
            ob_ref[rb, cols] = o_b
        st_scr[0, hd] = st_f
        st_scr[1, hd] = st_b

        @pl.when(last)
        def _():
            sfin_ref[0, hd] = st_f.T
            sfin_ref[1, hd] = st_b.T


def _hgrn_call(h, lbs_l, s0, l, nseq, seq, row0, state_prev=None):
    c = HG_CHUNK
    nchunk = seq // c
    has_s0 = s0 is not None
    lvl = _hg_masks()
    rows = c * HG_STEP_CHUNKS
    nstep = seq // rows
    rf = lambda b, ci: row0 // rows + b * nstep + ci
    rb = lambda b, ci: row0 // rows + b * nstep + (nstep - 1 - ci)
    blk = lambda rfun, cb: pl.BlockSpec((rows, BRANCH_W), lambda b, ci: (rfun(b, ci), cb))
    const = lambda shape: pl.BlockSpec(shape, lambda b, ci: (0,) * len(shape))
    in_specs = [blk(rf, 5), blk(rb, 5), blk(rf, 6), blk(rb, 7), blk(rf, 8), blk(rb, 8),
                pl.BlockSpec((None, 2, BRANCH_W), lambda b, ci: (l, 0, 0)), const((2, len(HG_LEVELS), c, c))]
    args = [h, h, h, h, h, h, lbs_l, lvl]
    if has_s0:
        in_specs.append(pl.BlockSpec((None, None, 2, HG_HEADS, HG_KEY, HG_VAL), lambda b, ci: (b, l, 0, 0, 0, 0)))
        args.append(s0)
        state_spec = pl.BlockSpec((None, 2, HG_HEADS, HG_KEY, HG_VAL), lambda b, ci: (b, 0, 0, 0, 0))
        state_shape = jax.ShapeDtypeStruct((nseq, 2, HG_HEADS, HG_KEY, HG_VAL), F32)
    else:
        state_spec = pl.BlockSpec((None, None, 2, HG_HEADS, HG_KEY, HG_VAL), lambda b, ci: (b, l, 0, 0, 0, 0))
        state_shape = jax.ShapeDtypeStruct((nseq, DEPTH, 2, HG_HEADS, HG_KEY, HG_VAL), F32)
    aliases = {}
    if state_prev is not None:
        aliases = {len(args): 2}
        in_specs.append(pl.BlockSpec(memory_space=pl.ANY))
        args.append(state_prev)
    return pl.pallas_call(
        functools.partial(_hg_kernel, has_s0=has_s0, nchunk=nchunk),
        grid=(nseq, nstep),
        in_specs=in_specs,
        out_specs=[pl.BlockSpec((rows, BRANCH_W), lambda b, ci: (b * nstep + ci, 0)),
                   pl.BlockSpec((rows, BRANCH_W), lambda b, ci: (b * nstep + (nstep - 1 - ci), 0)),
                   state_spec],
        out_shape=[jax.ShapeDtypeStruct((nseq * seq, BRANCH_W), F32),
                   jax.ShapeDtypeStruct((nseq * seq, BRANCH_W), F32),
                   state_shape],
        input_output_aliases=aliases,
        scratch_shapes=[pltpu.VMEM((2, HG_HEADS, HG_VAL, HG_KEY), F32)],
        compiler_params=_cparams(("arbitrary", "arbitrary")),
        name="hgrn_lat" if has_s0 else "hgrn_ctx",
    )(*args)


def _layer_norm(y, g, b):
    mu = jnp.mean(y, axis=-1, keepdims=True)
    yc = y - mu
    var = jnp.mean(yc * yc, axis=-1, keepdims=True)
    return yc * lax.rsqrt(var + NORM_EPS) * g + b


def _merge_kernel(attc_ref, attl_ref, rgc_ref, rgl_ref, ofc_ref, ofl_ref, obc_ref, obl_ref,
                  hgate_ref, mg0_ref, mg1_ref, mg2_ref, xc_ref, xl_ref, mod_ref,
                  hgn_ref, wbr_ref, wout_ref, lng_ref, lnb_ref, rw_ref, rb_ref, tri_ref,
                  x1_ref, u2_ref, idx_ref, rank_ref, wts_ref, cnt_ref, wbr_bf, wout_bf, cnt_scr, *, ctx_tiles):
    @pl.when(pl.program_id(0) == 0)
    def _():
        wbr_bf[...] = wbr_ref[...].astype(BF16)
        wout_bf[...] = wout_ref[...].astype(BF16)
        cnt_scr[...] = jnp.zeros(cnt_scr.shape, F32)

    is_ctx = pl.program_id(0) < ctx_tiles
    pick = lambda c_ref, l_ref: jnp.where(is_ctx, c_ref[...], l_ref[...])
    att = pick(attc_ref, attl_ref)
    rg = pick(rgc_ref, rgl_ref)
    o = pick(ofc_ref, ofl_ref) + pick(obc_ref, obl_ref)
    hgate = hgate_ref[...].astype(F32)
    hg_parts = []
    for hd in range(HG_HEADS):
        oh = o[:, hd * HG_VAL:(hd + 1) * HG_VAL]
        gh = hgate[:, hd * HG_VAL:(hd + 1) * HG_VAL]
        yh = oh * lax.rsqrt(jnp.mean(oh * oh, axis=-1, keepdims=True) + NORM_EPS) * hgn_ref[...]
        hg_parts.append(yh * (gh * _sigmoid(gh)))
    hg = jnp.concatenate(hg_parts, axis=-1)

    proj = _sigmoid(mg0_ref[...].astype(F32)) * _dot(att.astype(BF16), wbr_bf[0])
    proj = proj + _sigmoid(mg1_ref[...].astype(F32)) * _dot(rg.astype(BF16), wbr_bf[1])
    proj = proj + _sigmoid(mg2_ref[...].astype(F32)) * _dot(hg.astype(BF16), wbr_bf[2])
    mix = _dot(proj.astype(BF16), wout_bf[...])

    x1 = _layer_norm(DN_ALPHA * pick(xc_ref, xl_ref) + mod_ref[2:3, :] * mix, lng_ref[...], lnb_ref[...])
    x1_ref[...] = x1
    u2 = x1 * (1.0 + mod_ref[4:5, :]) + mod_ref[3:4, :]
    packed = _pack_bf16_pair(u2[:, :D_MODEL // 2], u2[:, D_MODEL // 2:])
    for j in range(u2_ref.shape[0]):
        u2_ref[j] = packed[:, j * u2_ref.shape[2]:(j + 1) * u2_ref.shape[2]]

    u2_hi = u2.astype(BF16)
    u2_lo = (u2 - u2_hi.astype(F32)).astype(BF16)
    logits = (_dot(u2_hi, rw_ref[0]) + _dot(u2_lo, rw_ref[0]) + _dot(u2_hi, rw_ref[1])) + rb_ref[...]
    lane = lax.broadcasted_iota(jnp.int32, logits.shape, 1).astype(F32)
    idx_out = jnp.zeros(logits.shape, F32)
    wts_out = jnp.zeros(logits.shape, F32)
    chosen = jnp.zeros(logits.shape, F32)
    sels = []
    top0 = None
    den = None
    for k in range(TOP_K):
        m = jnp.max(logits, axis=-1, keepdims=True)
        sel = jnp.min(jnp.where(logits == m, lane, float(LANES)), axis=-1, keepdims=True)
        sels.append(sel)
        if k == 0:
            top0 = m
        e = jnp.exp(m - top0)
        den = e if den is None else den + e
        idx_out = jnp.where(lane == k, sel, idx_out)
        wts_out = jnp.where(lane == k, e, wts_out)
        chosen = jnp.where(lane == sel, 1.0, chosen)
        logits = jnp.where(lane == sel, -jnp.inf, logits)
    idx_ref[...] = idx_out.astype(jnp.int32)
    wts_ref[...] = wts_out * (1.0 / den)

    before = _dot(tri_ref[...], chosen.astype(BF16)) + cnt_scr[...]
    rank_out = jnp.zeros(logits.shape, F32)
    for k in range(TOP_K):
        rk = jnp.sum(jnp.where(lane == sels[k], before, 0.0), axis=-1, keepdims=True)
        rank_out = jnp.where(lane == k, rk, rank_out)
    rank_ref[...] = rank_out.astype(jnp.int32)
    cnt_scr[...] = cnt_scr[...] + jnp.sum(chosen, axis=0, keepdims=True)
    cnt_ref[...] = jnp.broadcast_to(cnt_scr[...], cnt_ref.shape).astype(jnp.int32)


def _router_params(router_w, router_b):
    rw = jnp.zeros((DEPTH, D_MODEL, LANES), F32).at[:, :, :N_EXPERTS].set(router_w)
    rb = jnp.full((DEPTH, 1, LANES), -1e30, F32).at[:, 0, :N_EXPERTS].set(router_b)
    rw_hi = rw.astype(BF16)
    return jnp.stack([rw_hi, (rw - rw_hi.astype(F32)).astype(BF16)], axis=1), rb


def _merge_call(att, rg, o_f, o_b, h, x, mods, p, l):
    tm = MERGE_TM
    ctx_tiles = N_CTX // tm
    rowb = lambda w, cb: pl.BlockSpec((tm, w), lambda i: (i, cb))
    ctxb = lambda w: pl.BlockSpec((tm, w), lambda i: (jnp.minimum(i, ctx_tiles - 1), 0))
    latb = lambda w, base=0: pl.BlockSpec((tm, w), lambda i: (jnp.maximum(i - ctx_tiles, 0) + base, 0))
    const = lambda shape: pl.BlockSpec(shape, lambda i: (0,) * len(shape))
    layer = lambda shape: pl.BlockSpec((None,) + shape, lambda i: (l,) + (0,) * len(shape))
    bw = BRANCH_W
    tri = jnp.asarray(np.tril(np.ones((tm, tm), np.float32), -1), BF16)
    return pl.pallas_call(
        functools.partial(_merge_kernel, ctx_tiles=ctx_tiles),
        grid=(N_TOK // tm,),
        in_specs=[ctxb(bw), latb(bw), ctxb(bw), latb(bw), ctxb(bw), latb(bw), ctxb(bw), latb(bw),
                  rowb(BRANCH_W, 9), rowb(D_MODEL, 5), rowb(D_MODEL, 6), rowb(D_MODEL, 7),
                  ctxb(D_MODEL), latb(D_MODEL, x[2]),
                  pl.BlockSpec((None, None, 6, D_MODEL), lambda i: (l, _cond_of_row_tile(i, tm), 0, 0)),
                  layer((1, HG_VAL)), layer((3, BRANCH_W, D_MODEL)), layer((D_MODEL, D_MODEL)),
                  layer((1, D_MODEL)), layer((1, D_MODEL)), layer((2, D_MODEL, LANES)), layer((1, LANES)),
                  const((tm, tm))],
        out_specs=[rowb(D_MODEL, 0), pl.BlockSpec((SC_SLABS, tm, SC_SLAB_W), lambda i: (0, i, 0)),
                   rowb(LANES, 0), rowb(LANES, 0), rowb(LANES, 0), const((8, LANES))],
        out_shape=[jax.ShapeDtypeStruct((N_TOK, D_MODEL), F32),
                   jax.ShapeDtypeStruct((SC_SLABS, N_TOK, SC_SLAB_W), F32),
                   jax.ShapeDtypeStruct((N_TOK, LANES), jnp.int32),
                   jax.ShapeDtypeStruct((N_TOK, LANES), jnp.int32),
                   jax.ShapeDtypeStruct((N_TOK, LANES), F32),
                   jax.ShapeDtypeStruct((8, LANES), jnp.int32)],
        scratch_shapes=[pltpu.VMEM((3, BRANCH_W, D_MODEL), BF16), pltpu.VMEM((D_MODEL, D_MODEL), BF16),
                        pltpu.VMEM((1, LANES), F32)],
        compiler_params=_cparams(("arbitrary",)),
        name="merge",
    )(att[0], att[1], rg[0], rg[1], o_f[0], o_f[1], o_b[0], o_b[1], h, h, h, h, x[0], x[1], mods,
      p['hg_norm'], p['w_branch'], p['w_out'], p['ln1_g'], p['ln1_b'], p['rw'], p['rb'], tri)


MOE_NB = 256


MOE_SLOTS = 3


def _moe_kernel(te_ref, tf_ref, tsl_ref, tnx_ref, tn2_ref, nt_ref, x_ref, w1_hbm, b1_ref, w2_hbm, b2_ref, perm_ref,
                y_ref, w1_f, w2_f, w1_bf, w2_bf, sem, *, layer):
    i = pl.program_id(0)
    half = MOE_NB // 2

    def weight_copies(e, s):
        return (pltpu.make_async_copy(w1_hbm.at[layer, e], w1_f.at[s], sem.at[0, s]),
                pltpu.make_async_copy(w2_hbm.at[layer, e], w2_f.at[s], sem.at[1, s]))

    @pl.when(i == 0)
    def _():
        for cp in weight_copies(te_ref[0], 0):
            cp.start()

        @pl.when(tnx_ref[0] >= 0)
        def _():
            for cp in weight_copies(tnx_ref[0], 1):
                cp.start()

    @pl.when(tf_ref[i] == 1)
    def _():
        s = tsl_ref[i]
        for cp in weight_copies(te_ref[i], s):
            cp.wait()
        ahead = tn2_ref[i]

        @pl.when(ahead >= 0)
        def _():
            for cp in weight_copies(ahead, jnp.where(s == 0, MOE_SLOTS - 1, s - 1)):
                cp.start()

        for b in range(2 * D_EXPERT // MOE_NB):
            blk = w1_f[s, :, b * MOE_NB:(b + 1) * MOE_NB].astype(BF16)
            w1_bf[:, b * MOE_NB:(b + 1) * MOE_NB] = _dot(blk, perm_ref[...]).astype(BF16)
        w2_bf[...] = w2_f[s].astype(BF16)

    @pl.when(i < nt_ref[0])
    def _():
        halves = [_unpack_bf16_pair(x_ref[j]) for j in range(x_ref.shape[0])]
        x = jnp.concatenate([lo for lo, _ in halves] + [hi for _, hi in halves], axis=-1).astype(BF16)
        h = _dot(x, w1_bf[...]) + b1_ref[...]
        acts = []
        for b in range(2 * D_EXPERT // MOE_NB):
            glu = jnp.minimum(h[:, b * MOE_NB:b * MOE_NB + half], SWIGLU_LIMIT)
            lin = jnp.clip(h[:, b * MOE_NB + half:(b + 1) * MOE_NB], -SWIGLU_LIMIT, SWIGLU_LIMIT)
            acts.append((glu * _sigmoid(SWIGLU_ALPHA * glu) * (lin + 1.0)).astype(BF16))
        act = jnp.concatenate(acts, axis=-1)
        y = _dot(act, w2_bf[...]) + b2_ref[...]
        y_ref[...] = _pack_bf16_pair(y[:, :D_MODEL // 2], y[:, D_MODEL // 2:])


def _moe_perm():
    half = MOE_NB // 2
    pm = np.zeros((MOE_NB, MOE_NB), np.float32)
    pm[2 * np.arange(half), np.arange(half)] = 1.0
    pm[2 * np.arange(half) + 1, half + np.arange(half)] = 1.0
    return jnp.asarray(pm, BF16)


def _moe_call(x_sorted, sched, w1, b1p, w2, b2, l):
    tm = MOE_TM
    emap = lambda i, te, *_: (l, te[i], 0, 0)
    tile = lambda i, nt: jnp.minimum(i, nt[0] - 1)
    grid_spec = pltpu.PrefetchScalarGridSpec(
        num_scalar_prefetch=6,
        grid=(MOE_TILES,),
        in_specs=[pl.BlockSpec((SC_SLABS, tm, SC_SLAB_W), lambda i, *s: (0, tile(i, s[-1]), 0)),
                  pl.BlockSpec(memory_space=pl.ANY),
                  pl.BlockSpec((None, None, 1, 2 * D_EXPERT), emap),
                  pl.BlockSpec(memory_space=pl.ANY),
                  pl.BlockSpec((None, None, 1, D_MODEL), emap),
                  pl.BlockSpec((MOE_NB, MOE_NB), lambda i, *_: (0, 0))],
        out_specs=pl.BlockSpec((tm, D_MODEL // 2), lambda i, *s: (tile(i, s[-1]), 0)),
        scratch_shapes=[pltpu.VMEM((MOE_SLOTS, D_MODEL, 2 * D_EXPERT), F32),
                        pltpu.VMEM((MOE_SLOTS, D_EXPERT, D_MODEL), F32),
                        pltpu.VMEM((D_MODEL, 2 * D_EXPERT), BF16), pltpu.VMEM((D_EXPERT, D_MODEL), BF16),
                        pltpu.SemaphoreType.DMA((2, MOE_SLOTS))],
    )
    return pl.pallas_call(
        functools.partial(_moe_kernel, layer=l),
        grid_spec=grid_spec,
        out_shape=jax.ShapeDtypeStruct((MOE_TILES * tm, D_MODEL // 2), F32),
        compiler_params=_cparams(("arbitrary",)),
        name="moe",
    )(*sched, x_sorted, w1, b1p, w2, b2, _moe_perm())


def _route(idx, rank, counts):
    tm = MOE_TM
    tiles_e = (counts + tm - 1) // tm
    eid = np.arange(N_EXPERTS, dtype=np.int32)
    earlier = (eid[None, :] <= eid[:, None]).astype(np.int32)
    tile_end = jnp.sum(earlier * tiles_e[None, :], axis=1)
    tile_start = tile_end - tiles_e
    lookup = lambda table, keys: jnp.sum(jnp.where(keys[..., None] == eid, table, 0), axis=-1)
    pos_t = lookup(tile_start, idx.T) * tm + rank.T
    n_used = tile_end[N_EXPERTS - 1]
    tile_ids = jnp.arange(MOE_TILES, dtype=jnp.int32)
    tid = jnp.minimum(tile_ids, n_used - 1)
    tile_expert = jnp.sum((tile_end[None, :] <= tid[:, None]).astype(jnp.int32), axis=1)
    tile_first = ((tile_ids == lookup(tile_start, tile_expert)) & (tile_ids < n_used)).astype(jnp.int32)
    has_rows = (tiles_e > 0).astype(jnp.int32)
    slot_e = (jnp.sum(earlier * has_rows[None, :], axis=1) - 1) % MOE_SLOTS
    later = jnp.where((eid[None, :] > eid[:, None]) & (has_rows[None, :] > 0), eid[None, :], N_EXPERTS)
    next_e = jnp.min(later, axis=1)
    next2_e = jnp.where(next_e >= N_EXPERTS, N_EXPERTS, lookup(next_e, jnp.minimum(next_e, N_EXPERTS - 1)))
    to_id = lambda e: jnp.where(e >= N_EXPERTS, -1, e).astype(jnp.int32)
    sched = (tile_expert, tile_first, lookup(slot_e, tile_expert).astype(jnp.int32),
             lookup(to_id(next_e), tile_expert).astype(jnp.int32),
             lookup(to_id(next2_e), tile_expert).astype(jnp.int32), n_used.reshape(1))
    return pos_t, sched


SC_WINDOW = 128
SC_SLABS = 2
SC_SLAB_W = D_MODEL // 2 // SC_SLABS


def _dispatch_rows(u2, pos_t):
    mesh = plsc.VectorSubcoreMesh(core_axis_name="core", subcore_axis_name="subcore")

    @functools.partial(pl.kernel, mesh=mesh, scratch_types=[],
                       out_type=jax.ShapeDtypeStruct((SC_SLABS, MOE_TILES * MOE_TM, SC_SLAB_W), F32))
    def dispatch(x_hbm, i_hbm, o_hbm):
        for j in range(SC_SLABS):
            def body(x_vmem, i_vmem, j=j):
                for k in range(TOP_K):
                    pltpu.sync_copy(x_vmem, o_hbm.at[j].at[i_vmem.at[k]])

            pltpu.emit_pipeline(
                body,
                grid=(N_TOK // SC_WINDOW,),
                in_specs=[pl.BlockSpec((SC_WINDOW, SC_SLAB_W), index_map=lambda i: (i, 0)),
                          pl.BlockSpec((TOP_K, SC_WINDOW), index_map=lambda i: (0, i))],
                out_specs=[],
                core_axis_name=("core", "subcore"),
                dimension_semantics=(pltpu.PARALLEL,),
            )(x_hbm.at[j], i_hbm)

    return dispatch(u2, pos_t)


def _final_kernel(yg_ref, wts_ref, x1_ref, mod_ref, modn_ref, lng_ref, lnb_ref, o_ref, un_ref, *, ctx_tiles):
    wts = wts_ref[...]
    ffn_lo = ffn_hi = None
    for k in range(TOP_K):
        lo, hi = _unpack_bf16_pair(yg_ref[k])
        w = wts[:, k:k + 1]
        ffn_lo = w * lo if ffn_lo is None else ffn_lo + w * lo
        ffn_hi = w * hi if ffn_hi is None else ffn_hi + w * hi
    ffn = jnp.concatenate([ffn_lo, ffn_hi], axis=-1)
    x2 = _layer_norm(DN_ALPHA * x1_ref[...] + mod_ref[5:6, :] * ffn, lng_ref[...], lnb_ref[...])
    if ctx_tiles is None:
        o_ref[...] = x2
        un_ref[...] = (x2 * (1.0 + modn_ref[1:2, :]) + modn_ref[0:1, :]).astype(BF16)
    else:
        @pl.when(pl.program_id(0) < ctx_tiles)
        def _():
            o_ref[...] = x2

        @pl.when(pl.program_id(0) >= ctx_tiles)
        def _():
            un_ref[...] = x2


def _final_call(yg, wts, x1, mods, ln_g, ln_b, l):
    tm = 512
    last = l == DEPTH - 1
    ln = min(l + 1, DEPTH - 1)
    ctx_tiles = N_CTX // tm
    lnspec = pl.BlockSpec((None, 1, D_MODEL), lambda i: (l, 0, 0))
    modspec = lambda lyr: pl.BlockSpec((None, None, 6, D_MODEL), lambda i: (lyr, _cond_of_row_tile(i, tm), 0, 0))
    if last:
        out_specs = [pl.BlockSpec((tm, D_MODEL), lambda i: (jnp.minimum(i, ctx_tiles - 1), 0)),
                     pl.BlockSpec((tm, D_MODEL), lambda i: (jnp.maximum(i - ctx_tiles, 0), 0))]
        out_shape = [jax.ShapeDtypeStruct((N_CTX, D_MODEL), F32), jax.ShapeDtypeStruct((N_LAT, D_MODEL), F32)]
    else:
        out_specs = [pl.BlockSpec((tm, D_MODEL), lambda i: (i, 0)), pl.BlockSpec((tm, D_MODEL), lambda i: (i, 0))]
        out_shape = [jax.ShapeDtypeStruct((N_TOK, D_MODEL), F32), jax.ShapeDtypeStruct((N_TOK, D_MODEL), BF16)]
    return pl.pallas_call(
        functools.partial(_final_kernel, ctx_tiles=ctx_tiles if last else None),
        grid=(N_TOK // tm,),
        in_specs=[pl.BlockSpec((TOP_K, tm, D_MODEL // 2), lambda i: (0, i, 0)),
                  pl.BlockSpec((tm, LANES), lambda i: (i, 0)),
                  pl.BlockSpec((tm, D_MODEL), lambda i: (i, 0)),
                  modspec(l), modspec(ln), lnspec, lnspec],
        out_specs=out_specs,
        out_shape=out_shape,
        compiler_params=_cparams(("arbitrary",)),
        name="final",
    )(yg, wts, x1, mods, mods, ln_g, ln_b)


def kernel(x_prompt, x_sample, cache_attn_k, cache_attn_v, state_rglru, state_hgrn, c, c_ctx, w_ada, b_ada, w_in, da_lambda, da_subln, rg_conv_w, rg_conv_b, rg_gate_w, rg_gate_b, rg_lambda, hg_lb, hg_norm, w_branch, w_out, ln1_g, ln1_b, router_w, router_b, w1, b1, w2, b2, ln2_g, ln2_b):
    rw, rb = _router_params(router_w, router_b)
    p = dict(hg_norm=hg_norm.reshape(DEPTH, 1, HG_VAL), w_branch=w_branch, w_out=w_out,
             ln1_g=ln1_g.reshape(DEPTH, 1, D_MODEL), ln1_b=ln1_b.reshape(DEPTH, 1, D_MODEL), rw=rw, rb=rb)
    ln2_g = ln2_g.reshape(DEPTH, 1, D_MODEL)
    ln2_b = ln2_b.reshape(DEPTH, 1, D_MODEL)
    rg_conv_b = rg_conv_b.reshape(DEPTH, 1, RG_WIDTH)
    wg, bg = _rg_gate_dense(rg_gate_w, rg_gate_b)

    x_ctx = x_prompt.reshape(N_CTX, D_MODEL)
    x_lat = x_sample.reshape(N_LAT, D_MODEL)
    cond = jnp.concatenate([c_ctx[None, :], c, jnp.zeros((N_COND - 1 - DEC_BATCH, D_MODEL), F32)], axis=0)
    mods = _adaln_all(cond, w_ada, b_ada).reshape(DEPTH, N_COND, 6, D_MODEL)

    pr = jax.nn.softmax(hg_lb.astype(F32), axis=0)
    lbs = jnp.cumsum(pr, axis=0) - pr[0]
    dl = da_lambda.astype(F32)
    lambda_init = [0.8 - 0.6 * math.exp(-0.3 * l) for l in range(DEPTH)]
    lam_all = (jnp.exp(jnp.sum(dl[:, 0] * dl[:, 1], -1)) - jnp.exp(jnp.sum(dl[:, 2] * dl[:, 3], -1))
               + jnp.asarray(lambda_init, F32))
    lam_v = jnp.broadcast_to(lam_all[:, None, None], (DEPTH, 1, LANES))
    sub_v = da_subln.reshape(DEPTH, 1, DA_V_DIM)

    cache_k = cache_attn_k.reshape(DEC_BATCH, DEPTH, PAST_LEN, BRANCH_W)
    cache_v = cache_attn_v.reshape(DEC_BATCH, DEPTH, PAST_LEN, BRANCH_W)

    b1p = b1.reshape(DEPTH, N_EXPERTS, 2 * D_EXPERT // MOE_NB, MOE_NB // 2, 2)
    b1p = jnp.swapaxes(b1p, -1, -2).reshape(DEPTH, N_EXPERTS, 1, 2 * D_EXPERT)
    b2r = b2.reshape(DEPTH, N_EXPERTS, 1, D_MODEL)

    rgs = []
    kv_all = None
    hg_all = None
    u = _modulate(x_ctx, x_lat, mods, 0)
    x_pair = (x_ctx, x_lat, 0)
    for l in range(DEPTH):
        h = _in_proj(u, w_in, l)
        att, kv_all = _attention(h, cache_k, cache_v, lam_v, sub_v, l, 1.0 - lambda_init[l], kv_all)

        rg_c, hl_c = _rglru_call(h, rg_conv_w, rg_conv_b, wg, bg, rg_lambda, None, l, BATCH, SEQ, 0)
        rg_l, _ = _rglru_call(h, rg_conv_w, rg_conv_b, wg, bg, rg_lambda, state_rglru, l,
                              DEC_BATCH, DEC_SEQ, N_CTX // DEC_SEQ)
        rgs.append(hl_c)

        of_c, ob_c, hg_all = _hgrn_call(h, lbs, None, l, BATCH, SEQ, 0, hg_all)
        of_l, ob_l, _ = _hgrn_call(h, lbs, state_hgrn, l, DEC_BATCH, DEC_SEQ, N_CTX)

        x1, u2, idx, rank, wts, cnt = _merge_call(att, (rg_c, rg_l), (of_c, of_l), (ob_c, ob_l), h, x_pair,
                                                  mods, p, l)

        pos_t, sched = _route(idx[:, :TOP_K], rank[:, :TOP_K], cnt[0, :N_EXPERTS])
        x_sorted = _dispatch_rows(u2, pos_t)
        y_sorted = _moe_call(x_sorted, sched, w1, b1p, w2, b2r, l)
        yg = y_sorted.at[pos_t.reshape(-1)].get(mode='promise_in_bounds').reshape(TOP_K, N_TOK, D_MODEL // 2)
        x, u = _final_call(yg, wts, x1, mods, ln2_g, ln2_b, l)
        x_pair = (x, x, N_CTX // MERGE_TM)

    return (x.reshape(BATCH, SEQ, D_MODEL), u.reshape(DEC_BATCH, DEC_SEQ, D_MODEL),
            kv_all[0].reshape(BATCH, DEPTH, SEQ, DA_HEADS, 2, DA_HEAD_DIM),
            kv_all[1].reshape(BATCH, DEPTH, SEQ, DA_HEADS, DA_V_DIM),
            jnp.stack(rgs, axis=1), hg_all)
```

```python
import functools
import math

import numpy as np
import jax
import jax.numpy as jnp
from jax import lax
from jax.experimental import pallas as pl
from jax.experimental.pallas import tpu as pltpu
from jax.experimental.pallas import tpu_sc as plsc

F32 = jnp.float32
BF16 = jnp.bfloat16
HIGHEST = lax.Precision.HIGHEST

D_MODEL = 1024
BATCH = 16
SEQ = 256
DEPTH = 4
DEC_BATCH = 4
DEC_SEQ = 1024
PAST_LEN = 256
GRID_W = 64
BRANCH_W = 512
DA_HEADS = 4
DA_HEAD_DIM = 64
DA_V_DIM = 128
ROPE_BASE = 10000.0
RG_WIDTH = 512
RG_BLOCKS = 8
RG_BLOCK_W = 64
RG_CONV_W = 4
RG_C = 8.0
HG_HEADS = 4
HG_KEY = 128
HG_VAL = 128
N_EXPERTS = 32
TOP_K = 4
D_EXPERT = 1024
SWIGLU_ALPHA = 1.702
SWIGLU_LIMIT = 7.0
DN_ALPHA = (2 * DEPTH) ** 0.25
NORM_EPS = 1e-5
D_IN = 10 * BRANCH_W + 3 * D_MODEL

N_CTX = BATCH * SEQ
N_LAT = DEC_BATCH * DEC_SEQ
N_TOK = N_CTX + N_LAT
N_COND = 8

LANES = 128
VMEM_LIMIT = 56 * 1024 * 1024

HG_CHUNK = 128
HG_STEP_CHUNKS = 2
HG_LEVELS = (1, 2, 4, 8, 16, 32, 64)
MOE_TM = 256
MERGE_TM = 256
MOE_TILES = (N_TOK * TOP_K) // MOE_TM + N_EXPERTS


def _cparams(sem):
    return pltpu.CompilerParams(dimension_semantics=sem, vmem_limit_bytes=VMEM_LIMIT)


def _sigmoid(x):
    return 0.5 * jnp.tanh(0.5 * x) + 0.5


def _pack_bf16_pair(lo, hi):
    lo_bits = pltpu.bitcast(lo.astype(BF16).astype(F32), jnp.uint32) >> 16
    hi_bits = pltpu.bitcast(hi.astype(BF16).astype(F32), jnp.uint32) & jnp.uint32(0xFFFF0000)
    return pltpu.bitcast(hi_bits | lo_bits, F32)


def _unpack_bf16_pair(words):
    bits = pltpu.bitcast(words, jnp.uint32)
    return pltpu.bitcast(bits << 16, F32), pltpu.bitcast(bits & jnp.uint32(0xFFFF0000), F32)


def _dot(a, b):
    return jnp.dot(a, b, preferred_element_type=F32)


def _dot_nt(a, b):
    return lax.dot_general(a, b, (((1,), (1,)), ((), ())), preferred_element_type=F32)


def _cond_of_row_tile(i, tm):
    r = i * tm
    return jnp.where(r < N_CTX, 0, 1 + (r - N_CTX) // DEC_SEQ)


def _ada_kernel(c_ref, w_ref, b_ref, o_ref):
    c = c_ref[...]
    s = c * _sigmoid(c)
    o_ref[0] = jnp.dot(s, w_ref[0], precision=HIGHEST, preferred_element_type=F32) + b_ref[0]


def _adaln_all(cond, w_ada, b_ada):
    tn = 1536
    return pl.pallas_call(
        _ada_kernel,
        grid=(DEPTH, 6 * D_MODEL // tn),
        in_specs=[pl.BlockSpec((N_COND, D_MODEL), lambda l, j: (0, 0)),
                  pl.BlockSpec((1, D_MODEL, tn), lambda l, j: (l, 0, j)),
                  pl.BlockSpec((1, 1, tn), lambda l, j: (l, 0, j))],
        out_specs=pl.BlockSpec((1, N_COND, tn), lambda l, j: (l, 0, j)),
        out_shape=jax.ShapeDtypeStruct((DEPTH, N_COND, 6 * D_MODEL), F32),
        compiler_params=_cparams(("arbitrary", "arbitrary")),
        name="adaln",
    )(cond, w_ada, b_ada.reshape(DEPTH, 1, 6 * D_MODEL))


def _modulate_kernel(xc_ref, xl_ref, mod_ref, u_ref, *, ctx_tiles):
    x = jnp.where(pl.program_id(0) < ctx_tiles, xc_ref[...], xl_ref[...])
    u_ref[...] = (x * (1.0 + mod_ref[1:2, :]) + mod_ref[0:1, :]).astype(BF16)


def _modulate(x_ctx, x_lat, mods, l):
    tm = 1024
    ctx_tiles = N_CTX // tm
    return pl.pallas_call(
        functools.partial(_modulate_kernel, ctx_tiles=ctx_tiles),
        grid=(N_TOK // tm,),
        in_specs=[pl.BlockSpec((tm, D_MODEL), lambda i: (jnp.minimum(i, ctx_tiles - 1), 0)),
                  pl.BlockSpec((tm, D_MODEL), lambda i: (jnp.maximum(i - ctx_tiles, 0), 0)),
                  pl.BlockSpec((None, None, 6, D_MODEL), lambda i: (l, _cond_of_row_tile(i, tm), 0, 0))],
        out_specs=pl.BlockSpec((tm, D_MODEL), lambda i: (i, 0)),
        out_shape=jax.ShapeDtypeStruct((N_TOK, D_MODEL), BF16),
        compiler_params=_cparams(("arbitrary",)),
        name="modulate",
    )(x_ctx, x_lat, mods)


def _in_kernel(u_ref, w_ref, o_ref, wbf_ref):
    @pl.when(pl.program_id(1) == 0)
    def _():
        wbf_ref[...] = w_ref[...].astype(BF16)

    o_ref[...] = _dot(u_ref[...], wbf_ref[...]).astype(BF16)


def _in_proj(u, w_in, l):
    tm, tn = 1024, 2048
    return pl.pallas_call(
        _in_kernel,
        grid=(D_IN // tn, N_TOK // tm),
        in_specs=[pl.BlockSpec((tm, D_MODEL), lambda j, i: (i, 0)),
                  pl.BlockSpec((None, D_MODEL, tn), lambda j, i: (l, 0, j))],
        out_specs=pl.BlockSpec((tm, tn), lambda j, i: (i, j)),
        out_shape=jax.ShapeDtypeStruct((N_TOK, D_IN), BF16),
        scratch_shapes=[pltpu.VMEM((D_MODEL, tn), BF16)],
        compiler_params=_cparams(("arbitrary", "arbitrary")),
        name="in_proj",
    )(u, w_in)


def _rope(x, cos, sin_signed):
    lane = lax.broadcasted_iota(jnp.int32, x.shape, 1)
    first = (lane & 31) < 16
    partner = jnp.where(first, pltpu.roll(x, LANES - 16, 1), pltpu.roll(x, 16, 1))
    return x * cos + partner * sin_signed


LOG2E = 1.4426950408889634


def _attn_body(q, keys, vals, lam, subln, out_scale):
    lane = lax.broadcasted_iota(jnp.int32, q.shape, 1)
    qs = q * (DA_HEAD_DIM ** -0.5 * LOG2E)
    acc = None
    for m in range(2):
        in_map = (lane < DA_HEAD_DIM) if m == 0 else (lane >= DA_HEAD_DIM)
        qm = jnp.where(in_map, qs, 0.0).astype(BF16)
        s = [_dot_nt(qm, k) for k in keys]
        mx = s[0].max(axis=-1, keepdims=True)
        for si in s[1:]:
            mx = jnp.maximum(mx, si.max(axis=-1, keepdims=True))
        e = [jnp.exp2(si - mx) for si in s]
        den = e[0].sum(axis=-1, keepdims=True)
        for ei in e[1:]:
            den = den + ei.sum(axis=-1, keepdims=True)
        pv = _dot(e[0].astype(BF16), vals[0])
        for ei, v in zip(e[1:], vals[1:]):
            pv = pv + _dot(ei.astype(BF16), v)
        coef = (1.0 / den) if m == 0 else (-lam[:, 0:1] / den)
        acc = pv * coef if acc is None else acc + pv * coef
    y = acc * lax.rsqrt(jnp.mean(acc * acc, axis=-1, keepdims=True) + NORM_EPS)
    return y * subln * out_scale


def _attn_ctx_kernel(lam_ref, sub_ref, q_ref, k_ref, v_ref, *rest, out_scale):
    o_ref, kout_ref, vout_ref = rest[-3:]
    if len(kout_ref.shape) == 3:
        kout_ref[...] = jnp.zeros(kout_ref.shape, F32)
        vout_ref[...] = jnp.zeros(vout_ref.shape, F32)
        kout_ref, vout_ref = kout_ref.at[0], vout_ref.at[0]
    kout_ref[...] = k_ref[...].astype(F32)
    vout_ref[...] = v_ref[...].astype(F32)
    for hd in range(DA_HEADS):
        cols = slice(hd * LANES, (hd + 1) * LANES)
        o_ref[:, cols] = _attn_body(q_ref[:, cols].astype(F32), [k_ref[:, cols]], [v_ref[:, cols]],
                                    lam_ref[...], sub_ref[...], out_scale)


def _attn_lat_kernel(lam_ref, sub_ref, q_ref, k_ref, v_ref, kc_ref, vc_ref, cos_ref, sin_ref,
                     o_ref, k_scr, kc_scr, vc_scr, *, out_scale, tq):
    k_scr[...] = _rope(k_ref[...].astype(F32), cos_ref[...], sin_ref[...]).astype(BF16)
    kc_scr[...] = kc_ref[...].astype(BF16)
    vc_scr[...] = vc_ref[...].astype(BF16)

    def q_block(qi, carry):
        rows = pl.ds(pl.multiple_of(qi * tq, tq), tq)
        q = _rope(q_ref[rows, :].astype(F32), cos_ref[rows, :], sin_ref[rows, :])
        o_ref[rows, :] = _attn_body(q, [k_scr[...], kc_scr[...]], [v_ref[...], vc_scr[...]],
                                    lam_ref[...], sub_ref[...], out_scale)
        return carry

    lax.fori_loop(0, DEC_SEQ // tq, q_block, 0)


def _rope_tables():
    t = np.arange(DEC_SEQ)
    row = (t // GRID_W).astype(np.float64)
    col = (t % GRID_W).astype(np.float64)
    d = DA_HEAD_DIM // 2
    inv = ROPE_BASE ** (-np.arange(0, d, 2, dtype=np.float64) / d)
    a_row = row[:, None] * inv[None, :]
    a_col = col[:, None] * inv[None, :]
    cos = np.concatenate([np.cos(a_row), np.cos(a_row), np.cos(a_col), np.cos(a_col)], -1)
    sin = np.concatenate([-np.sin(a_row), np.sin(a_row), -np.sin(a_col), np.sin(a_col)], -1)
    return (jnp.asarray(np.tile(cos, (1, 2)), F32), jnp.asarray(np.tile(sin, (1, 2)), F32))


def _attention(h, cache_k, cache_v, lam_v, sub_v, l, out_scale, kv_prev):
    small = pl.BlockSpec((None, 1, LANES), lambda *a: (l, 0, 0))
    if kv_prev is None:
        kv_blk = pl.BlockSpec((None, DEPTH, SEQ, BRANCH_W), lambda b: (b, 0, 0, 0))
    else:
        kv_blk = pl.BlockSpec((None, None, SEQ, BRANCH_W), lambda b: (b, l, 0, 0))
    kv_shape = jax.ShapeDtypeStruct((BATCH, DEPTH, SEQ, BRANCH_W), F32)
    carried = () if kv_prev is None else tuple(kv_prev)
    att_ctx, k_all, v_all = pl.pallas_call(
        functools.partial(_attn_ctx_kernel, out_scale=out_scale),
        grid=(BATCH,),
        in_specs=[small, small,
                  pl.BlockSpec((SEQ, BRANCH_W), lambda b: (b, 0)),
                  pl.BlockSpec((SEQ, BRANCH_W), lambda b: (b, 1)),
                  pl.BlockSpec((SEQ, BRANCH_W), lambda b: (b, 2))]
                 + [pl.BlockSpec(memory_space=pl.ANY)] * len(carried),
        out_specs=[pl.BlockSpec((SEQ, BRANCH_W), lambda b: (b, 0)), kv_blk, kv_blk],
        out_shape=[jax.ShapeDtypeStruct((N_CTX, BRANCH_W), F32), kv_shape, kv_shape],
        input_output_aliases={5: 1, 6: 2} if carried else {},
        compiler_params=_cparams(("arbitrary",)),
        name="attn_ctx",
    )(lam_v, sub_v, h, h, h, *carried)

    tq = 256
    cos, sin = _rope_tables()
    row0 = N_CTX // DEC_SEQ
    seq_blk = lambda cb: pl.BlockSpec((DEC_SEQ, LANES), lambda b, hd: (row0 + b, cb + hd))
    cache_blk = pl.BlockSpec((None, None, PAST_LEN, LANES), lambda b, hd: (b, l, 0, hd))
    table = pl.BlockSpec((DEC_SEQ, LANES), lambda b, hd: (0, 0))
    att_lat = pl.pallas_call(
        functools.partial(_attn_lat_kernel, out_scale=out_scale, tq=tq),
        grid=(DEC_BATCH, DA_HEADS),
        in_specs=[small, small, seq_blk(0), seq_blk(4), seq_blk(8), cache_blk, cache_blk, table, table],
        out_specs=pl.BlockSpec((DEC_SEQ, LANES), lambda b, hd: (b, hd)),
        out_shape=jax.ShapeDtypeStruct((N_LAT, BRANCH_W), F32),
        scratch_shapes=[pltpu.VMEM((DEC_SEQ, LANES), BF16), pltpu.VMEM((PAST_LEN, LANES), BF16),
                        pltpu.VMEM((PAST_LEN, LANES), BF16)],
        compiler_params=_cparams(("arbitrary", "arbitrary")),
        name="attn_lat",
    )(lam_v, sub_v, h, h, h, cache_k, cache_v, cos, sin)
    return (att_ctx, att_lat), (k_all, v_all)


def _gelu_tanh(x):
    return 0.5 * x * (1.0 + jnp.tanh(math.sqrt(2.0 / math.pi) * (x + 0.044715 * (x * x * x))))


def _rg_kernel(*refs, seq, has_h0):
    if has_h0:
        rx_ref, gate_ref, cw_ref, cb_ref, wg_ref, bg_ref, lam_ref, h0_ref = refs[:8]
        rest = refs[8:]
    else:
        rx_ref, gate_ref, cw_ref, cb_ref, wg_ref, bg_ref, lam_ref = refs[:7]
        h0_ref = None
        rest = refs[7:]
    out_ref, hl_ref, a_scr, b_scr, h_scr = rest

    x = rx_ref[...].astype(F32)
    row = lax.broadcasted_iota(jnp.int32, x.shape, 0)
    xr = cb_ref[...] + cw_ref[2:3, :] * x
    for j in (0, 1, 3):
        d = j - RG_CONV_W // 2
        shifted = pltpu.roll(x, (-d) % seq, 0)
        valid = (row + d >= 0) & (row + d < seq)
        xr = xr + cw_ref[j:j + 1, :] * jnp.where(valid, shifted, 0.0)

    g = _dot(xr.astype(BF16), wg_ref[...]) + bg_ref[...]
    for dr in range(2):
        r = _sigmoid(g[:, (2 * dr) * RG_WIDTH:(2 * dr + 1) * RG_WIDTH])
        i = _sigmoid(g[:, (2 * dr + 1) * RG_WIDTH:(2 * dr + 2) * RG_WIDTH])
        lam = lam_ref[dr:dr + 1, :]
        softplus_neg = jnp.maximum(-lam, 0.0) + jnp.log(1.0 + jnp.exp(-jnp.abs(lam)))
        log_a = (-RG_C * softplus_neg) * r
        a = jnp.exp(log_a)
        a_scr[dr] = a
        b_scr[dr] = jnp.sqrt(1.0 - a * a) * i * xr

    if has_h0:
        hf0, hb0 = h0_ref[0:1, :], h0_ref[1:2, :]
    else:
        hf0 = hb0 = jnp.zeros((1, RG_WIDTH), F32)

    def step(t, carry):
        hf, hb = carry
        tb = seq - 1 - t
        hf = a_scr[0, pl.ds(t, 1), :] * hf + b_scr[0, pl.ds(t, 1), :]
        h_scr[0, pl.ds(t, 1), :] = hf
        hb = a_scr[1, pl.ds(tb, 1), :] * hb + b_scr[1, pl.ds(tb, 1), :]
        h_scr[1, pl.ds(tb, 1), :] = hb
        return hf, hb

    hf, hb = lax.fori_loop(0, seq, step, (hf0, hb0), unroll=8)
    hl_ref[0:1, :] = hf
    hl_ref[1:2, :] = hb
    out_ref[...] = (h_scr[0] + h_scr[1]) * _gelu_tanh(gate_ref[...].astype(F32))


def _rglru_call(h, conv_w, conv_b, wg, bg, lam, h0, l, nseq, seq, row_block0):
    has_h0 = h0 is not None
    layer = lambda shape: pl.BlockSpec((None,) + shape, lambda b: (l,) + (0,) * len(shape))
    in_specs = [pl.BlockSpec((seq, RG_WIDTH), lambda b: (row_block0 + b, 3)),
                pl.BlockSpec((seq, RG_WIDTH), lambda b: (row_block0 + b, 4)),
                layer((RG_CONV_W, RG_WIDTH)), layer((1, RG_WIDTH)),
                layer((RG_WIDTH, 4 * RG_WIDTH)), layer((1, 4 * RG_WIDTH)), layer((2, RG_WIDTH))]
    args = [h, h, conv_w, conv_b, wg, bg, lam]
    if has_h0:
        in_specs.append(pl.BlockSpec((None, None, 2, RG_WIDTH), lambda b: (b, l, 0, 0)))
        args.append(h0)
    return pl.pallas_call(
        functools.partial(_rg_kernel, seq=seq, has_h0=has_h0),
        grid=(nseq,),
        in_specs=in_specs,
        out_specs=[pl.BlockSpec((seq, RG_WIDTH), lambda b: (b, 0)),
                   pl.BlockSpec((None, 2, RG_WIDTH), lambda b: (b, 0, 0))],
        out_shape=[jax.ShapeDtypeStruct((nseq * seq, RG_WIDTH), F32),
                   jax.ShapeDtypeStruct((nseq, 2, RG_WIDTH), F32)],
        scratch_shapes=[pltpu.VMEM((2, seq, RG_WIDTH), F32)] * 3,
        compiler_params=_cparams(("arbitrary",)),
        name="rglru_lat" if has_h0 else "rglru_ctx",
    )(*args)


def _rg_gate_dense(gate_w, gate_b):
    eye = jnp.eye(RG_BLOCKS, dtype=BF16)
    blocks = jnp.transpose(gate_w, (0, 3, 4, 1, 2, 5)).astype(BF16)
    wg = blocks[:, :, :, :, :, None, :] * eye[None, :, None, None, None, :, None]
    return wg.reshape(DEPTH, RG_WIDTH, 4 * RG_WIDTH), gate_b.reshape(DEPTH, 1, 4 * RG_WIDTH)


def _hg_masks():
    c = HG_CHUNK
    t = np.arange(c)[:, None]
    s = np.arange(c)[None, :]
    lvl = np.zeros((2, len(HG_LEVELS), c, c), np.float32)
    for n, h in enumerate(HG_LEVELS):
        same = (t // (2 * h)) == (s // (2 * h))
        lvl[0, n] = same & ((t // h) % 2 == 1) & ((s // h) % 2 == 0)
        lvl[1, n] = same & ((t // h) % 2 == 0) & ((s // h) % 2 == 1)
    return jnp.asarray(lvl)


def _hg_chunk(q_raw, z, v, lb, st, lvl, rev):
    c = HG_CHUNK
    q = q_raw * _sigmoid(q_raw)
    sig = _sigmoid(z)
    f = lb + (1.0 - lb) * sig
    kk = (1.0 - lb) * (1.0 - sig)

    rowk = lax.broadcasted_iota(jnp.int32, (c, HG_KEY), 0)
    pos = rowk if not rev else (c - 1) - rowk
    earlier = lambda x, d: pltpu.roll(x, d if not rev else c - d, 0)
    later = lambda x, d: pltpu.roll(x, c - d if not rev else d, 0)

    def block_row(x, n, p):
        r = p if not rev else n - 1 - p
        x3 = x.reshape(c // n, n, HG_KEY)
        return jnp.broadcast_to(x3[:, r:r + 1, :], x3.shape).reshape(c, HG_KEY)

    pair = f * earlier(f, 1)
    f_next = later(f, 1)
    p4 = pos & 3
    a = {1: f, 2: jnp.where((pos & 1) == 1, pair, f)}
    b = {2: jnp.where((pos & 1) == 0, f_next, 1.0)}
    a[4] = a[2] * jnp.where(p4 == 2, earlier(pair, 1), jnp.where(p4 == 3, earlier(pair, 2), 1.0))
    b[4] = jnp.where(p4 == 3, 1.0, jnp.where(p4 == 2, f_next,
                                             jnp.where(p4 == 1, later(pair, 2), f_next * later(pair, 3))))
    h = 4
    while h < c:
        in_later_half = (pos & h) != 0
        a[2 * h] = jnp.where(in_later_half, a[h] * block_row(a[h], 2 * h, h - 1), a[h])
        b[2 * h] = jnp.where(in_later_half, b[h], b[h] * block_row(a[h], 2 * h, 2 * h - 1))
        h *= 2

    row = lax.broadcasted_iota(jnp.int32, (c, c), 0)
    col = lax.broadcasted_iota(jnp.int32, (c, c), 1)
    scores = jnp.where(row == col, jnp.sum(q * kk, axis=-1, keepdims=True), 0.0)
    for n, h in enumerate(HG_LEVELS):
        qa = (q * a[h]).astype(BF16)
        kb = (kk if h == 1 else kk * b[h]).astype(BF16)
        scores = scores + lvl[n] * _dot_nt(qa, kb)

    vb = v.astype(BF16)
    o = _dot(scores.astype(BF16), vb)
    o = o + _dot_nt((q * a[c]).astype(BF16), st.astype(BF16))
    total = a[c][c - 1:c, :] if not rev else a[c][0:1, :]
    st_new = st * total + _dot(v.T.astype(BF16), (kk * b[c]).astype(BF16))
    return o, st_new


def _hg_kernel(*refs, has_s0, nchunk):
    if has_s0:
        (qf_ref, qb_ref, zf_ref, zb_ref, vf_ref, vb_ref, lb_ref, lvl_ref, s0_ref,
         of_ref, ob_ref, sfin_ref, st_scr) = refs
    else:
        qf_ref, qb_ref, zf_ref, zb_ref, vf_ref, vb_ref, lb_ref, lvl_ref = refs[:8]
        of_ref, ob_ref, sfin_ref, st_scr = refs[-4:]
        s0_ref = None
    ci = pl.program_id(1)

    @pl.when(ci == 0)
    def _():
        for dr in range(2):
            for hd in range(HG_HEADS):
                st_scr[dr, hd] = s0_ref[dr, hd].T if has_s0 else jnp.zeros((HG_VAL, HG_KEY), F32)

    last = ci == nchunk // HG_STEP_CHUNKS - 1
    c = HG_CHUNK
    if len(sfin_ref.shape) == 5:
        @pl.when(last)
        def _():
            sfin_ref[1:] = jnp.zeros((sfin_ref.shape[0] - 1,) + sfin_ref.shape[1:], F32)

        sfin_ref = sfin_ref.at[0]
    for hd in range(HG_HEADS):
        cols = slice(hd * LANES, (hd + 1) * LANES)
        st_f, st_b = st_scr[0, hd], st_scr[1, hd]
        for sc in range(HG_STEP_CHUNKS):
            rf = slice(sc * c, (sc + 1) * c)
            rb = slice((HG_STEP_CHUNKS - 1 - sc) * c, (HG_STEP_CHUNKS - sc) * c)
            ldf = lambda ref: ref[rf, cols].astype(F32)
            ldb = lambda ref: ref[rb, cols].astype(F32)
            o_f, st_f = _hg_chunk(ldf(qf_ref), ldf(zf_ref), ldf(vf_ref), lb_ref[0:1, cols], st_f, lvl_ref[0], False)
            o_b, st_b = _hg_chunk(ldb(qb_ref), ldb(zb_ref), ldb(vb_ref), lb_ref[1:2, cols], st_b, lvl_ref[1], True)
            of_ref[rf, cols] = o_f
            ob_ref[rb, cols] = o_b
        st_scr[0, hd] = st_f
        st_scr[1, hd] = st_b

        @pl.when(last)
        def _():
            sfin_ref[0, hd] = st_f.T
            sfin_ref[1, hd] = st_b.T


def _hgrn_call(h, lbs_l, s0, l, nseq, seq, row0, state_prev=None):
    c = HG_CHUNK
    nchunk = seq // c
    has_s0 = s0 is not None
    lvl = _hg_masks()
    rows = c * HG_STEP_CHUNKS
    nstep = seq // rows
    rf = lambda b, ci: row0 // rows + b * nstep + ci
    rb = lambda b, ci: row0 // rows + b * nstep + (nstep - 1 - ci)
    blk = lambda rfun, cb: pl.BlockSpec((rows, BRANCH_W), lambda b, ci: (rfun(b, ci), cb))
    const = lambda shape: pl.BlockSpec(shape, lambda b, ci: (0,) * len(shape))
    in_specs = [blk(rf, 5), blk(rb, 5), blk(rf, 6), blk(rb, 7), blk(rf, 8), blk(rb, 8),
                pl.BlockSpec((None, 2, BRANCH_W), lambda b, ci: (l, 0, 0)), const((2, len(HG_LEVELS), c, c))]
    args = [h, h, h, h, h, h, lbs_l, lvl]
    if has_s0:
        in_specs.append(pl.BlockSpec((None, None, 2, HG_HEADS, HG_KEY, HG_VAL), lambda b, ci: (b, l, 0, 0, 0, 0)))
        args.append(s0)
        state_spec = pl.BlockSpec((None, 2, HG_HEADS, HG_KEY, HG_VAL), lambda b, ci: (b, 0, 0, 0, 0))
        state_shape = jax.ShapeDtypeStruct((nseq, 2, HG_HEADS, HG_KEY, HG_VAL), F32)
    else:
        if state_prev is None:
            state_spec = pl.BlockSpec((None, DEPTH, 2, HG_HEADS, HG_KEY, HG_VAL), lambda b, ci: (b, 0, 0, 0, 0, 0))
        else:
            state_spec = pl.BlockSpec((None, None, 2, HG_HEADS, HG_KEY, HG_VAL),
                                      lambda b, ci: (b, l, 0, 0, 0, 0))
        state_shape = jax.ShapeDtypeStruct((nseq, DEPTH, 2, HG_HEADS, HG_KEY, HG_VAL), F32)
    aliases = {}
    if state_prev is not None:
        aliases = {len(args): 2}
        in_specs.append(pl.BlockSpec(memory_space=pl.ANY))
        args.append(state_prev)
    return pl.pallas_call(
        functools.partial(_hg_kernel, has_s0=has_s0, nchunk=nchunk),
        grid=(nseq, nstep),
        in_specs=in_specs,
        out_specs=[pl.BlockSpec((rows, BRANCH_W), lambda b, ci: (b * nstep + ci, 0)),
                   pl.BlockSpec((rows, BRANCH_W), lambda b, ci: (b * nstep + (nstep - 1 - ci), 0)),
                   state_spec],
        out_shape=[jax.ShapeDtypeStruct((nseq * seq, BRANCH_W), F32),
                   jax.ShapeDtypeStruct((nseq * seq, BRANCH_W), F32),
                   state_shape],
        input_output_aliases=aliases,
        scratch_shapes=[pltpu.VMEM((2, HG_HEADS, HG_VAL, HG_KEY), F32)],
        compiler_params=_cparams(("arbitrary", "arbitrary")),
        name="hgrn_lat" if has_s0 else "hgrn_ctx",
    )(*args)


def _layer_norm(y, g, b):
    mu = jnp.mean(y, axis=-1, keepdims=True)
    yc = y - mu
    var = jnp.mean(yc * yc, axis=-1, keepdims=True)
    return yc * lax.rsqrt(var + NORM_EPS) * g + b


def _merge_kernel(attc_ref, attl_ref, rgc_ref, rgl_ref, ofc_ref, ofl_ref, obc_ref, obl_ref,
                  hgate_ref, mg0_ref, mg1_ref, mg2_ref, xc_ref, xl_ref, mod_ref,
                  hgn_ref, wbr_ref, wout_ref, lng_ref, lnb_ref, rw_ref, rb_ref, tri_ref,
                  x1_ref, u2_ref, idx_ref, rank_ref, wts_ref, cnt_ref, wbr_bf, wout_bf, cnt_scr, *, ctx_tiles):
    @pl.when(pl.program_id(0) == 0)
    def _():
        wbr_bf[...] = wbr_ref[...].astype(BF16)
        wout_bf[...] = wout_ref[...].astype(BF16)
        cnt_scr[...] = jnp.zeros(cnt_scr.shape, F32)

    is_ctx = pl.program_id(0) < ctx_tiles
    pick = lambda c_ref, l_ref: jnp.where(is_ctx, c_ref[...], l_ref[...])
    att = pick(attc_ref, attl_ref)
    rg = pick(rgc_ref, rgl_ref)
    o = pick(ofc_ref, ofl_ref) + pick(obc_ref, obl_ref)
    hgate = hgate_ref[...].astype(F32)
    hg_parts = []
    for hd in range(HG_HEADS):
        oh = o[:, hd * HG_VAL:(hd + 1) * HG_VAL]
        gh = hgate[:, hd * HG_VAL:(hd + 1) * HG_VAL]
        yh = oh * lax.rsqrt(jnp.mean(oh * oh, axis=-1, keepdims=True) + NORM_EPS) * hgn_ref[...]
        hg_parts.append(yh * (gh * _sigmoid(gh)))
    hg = jnp.concatenate(hg_parts, axis=-1)

    proj = _sigmoid(mg0_ref[...].astype(F32)) * _dot(att.astype(BF16), wbr_bf[0])
    proj = proj + _sigmoid(mg1_ref[...].astype(F32)) * _dot(rg.astype(BF16), wbr_bf[1])
    proj = proj + _sigmoid(mg2_ref[...].astype(F32)) * _dot(hg.astype(BF16), wbr_bf[2])
    mix = _dot(proj.astype(BF16), wout_bf[...])

    x1 = _layer_norm(DN_ALPHA * pick(xc_ref, xl_ref) + mod_ref[2:3, :] * mix, lng_ref[...], lnb_ref[...])
    x1_ref[...] = x1
    u2 = x1 * (1.0 + mod_ref[4:5, :]) + mod_ref[3:4, :]
    packed = _pack_bf16_pair(u2[:, :D_MODEL // 2], u2[:, D_MODEL // 2:])
    for j in range(u2_ref.shape[0]):
        u2_ref[j] = packed[:, j * u2_ref.shape[2]:(j + 1) * u2_ref.shape[2]]

    u2_hi = u2.astype(BF16)
    u2_lo = (u2 - u2_hi.astype(F32)).astype(BF16)
    logits = (_dot(u2_hi, rw_ref[0]) + _dot(u2_lo, rw_ref[0]) + _dot(u2_hi, rw_ref[1])) + rb_ref[...]
    lane = lax.broadcasted_iota(jnp.int32, logits.shape, 1).astype(F32)
    idx_out = jnp.zeros(logits.shape, F32)
    wts_out = jnp.zeros(logits.shape, F32)
    chosen = jnp.zeros(logits.shape, F32)
    sels = []
    top0 = None
    den = None
    for k in range(TOP_K):
        m = jnp.max(logits, axis=-1, keepdims=True)
        sel = jnp.min(jnp.where(logits == m, lane, float(LANES)), axis=-1, keepdims=True)
        sels.append(sel)
        if k == 0:
            top0 = m
        e = jnp.exp(m - top0)
        den = e if den is None else den + e
        idx_out = jnp.where(lane == k, sel, idx_out)
        wts_out = jnp.where(lane == k, e, wts_out)
        chosen = jnp.where(lane == sel, 1.0, chosen)
        logits = jnp.where(lane == sel, -jnp.inf, logits)
    idx_ref[...] = idx_out.astype(jnp.int32)
    wts_ref[...] = wts_out * (1.0 / den)

    before = _dot(tri_ref[...], chosen.astype(BF16)) + cnt_scr[...]
    rank_out = jnp.zeros(logits.shape, F32)
    for k in range(TOP_K):
        rk = jnp.sum(jnp.where(lane == sels[k], before, 0.0), axis=-1, keepdims=True)
        rank_out = jnp.where(lane == k, rk, rank_out)
    rank_ref[...] = rank_out.astype(jnp.int32)
    cnt_scr[...] = cnt_scr[...] + jnp.sum(chosen, axis=0, keepdims=True)
    cnt_ref[...] = jnp.broadcast_to(cnt_scr[...], cnt_ref.shape).astype(jnp.int32)


def _router_params(router_w, router_b):
    rw = jnp.zeros((DEPTH, D_MODEL, LANES), F32).at[:, :, :N_EXPERTS].set(router_w)
    rb = jnp.full((DEPTH, 1, LANES), -1e30, F32).at[:, 0, :N_EXPERTS].set(router_b)
    rw_hi = rw.astype(BF16)
    return jnp.stack([rw_hi, (rw - rw_hi.astype(F32)).astype(BF16)], axis=1), rb


def _merge_call(att, rg, o_f, o_b, h, x, mods, p, l):
    tm = MERGE_TM
    ctx_tiles = N_CTX // tm
    rowb = lambda w, cb: pl.BlockSpec((tm, w), lambda i: (i, cb))
    ctxb = lambda w: pl.BlockSpec((tm, w), lambda i: (jnp.minimum(i, ctx_tiles - 1), 0))
    latb = lambda w, base=0: pl.BlockSpec((tm, w), lambda i: (jnp.maximum(i - ctx_tiles, 0) + base, 0))
    const = lambda shape: pl.BlockSpec(shape, lambda i: (0,) * len(shape))
    layer = lambda shape: pl.BlockSpec((None,) + shape, lambda i: (l,) + (0,) * len(shape))
    bw = BRANCH_W
    tri = jnp.asarray(np.tril(np.ones((tm, tm), np.float32), -1), BF16)
    return pl.pallas_call(
        functools.partial(_merge_kernel, ctx_tiles=ctx_tiles),
        grid=(N_TOK // tm,),
        in_specs=[ctxb(bw), latb(bw), ctxb(bw), latb(bw), ctxb(bw), latb(bw), ctxb(bw), latb(bw),
                  rowb(BRANCH_W, 9), rowb(D_MODEL, 5), rowb(D_MODEL, 6), rowb(D_MODEL, 7),
                  ctxb(D_MODEL), latb(D_MODEL, x[2]),
                  pl.BlockSpec((None, None, 6, D_MODEL), lambda i: (l, _cond_of_row_tile(i, tm), 0, 0)),
                  layer((1, HG_VAL)), layer((3, BRANCH_W, D_MODEL)), layer((D_MODEL, D_MODEL)),
                  layer((1, D_MODEL)), layer((1, D_MODEL)), layer((2, D_MODEL, LANES)), layer((1, LANES)),
                  const((tm, tm))],
        out_specs=[rowb(D_MODEL, 0), pl.BlockSpec((SC_SLABS, tm, SC_SLAB_W), lambda i: (0, i, 0)),
                   rowb(LANES, 0), rowb(LANES, 0), rowb(LANES, 0), const((8, LANES))],
        out_shape=[jax.ShapeDtypeStruct((N_TOK, D_MODEL), F32),
                   jax.ShapeDtypeStruct((SC_SLABS, N_TOK, SC_SLAB_W), F32),
                   jax.ShapeDtypeStruct((N_TOK, LANES), jnp.int32),
                   jax.ShapeDtypeStruct((N_TOK, LANES), jnp.int32),
                   jax.ShapeDtypeStruct((N_TOK, LANES), F32),
                   jax.ShapeDtypeStruct((8, LANES), jnp.int32)],
        scratch_shapes=[pltpu.VMEM((3, BRANCH_W, D_MODEL), BF16), pltpu.VMEM((D_MODEL, D_MODEL), BF16),
                        pltpu.VMEM((1, LANES), F32)],
        compiler_params=_cparams(("arbitrary",)),
        name="merge",
    )(att[0], att[1], rg[0], rg[1], o_f[0], o_f[1], o_b[0], o_b[1], h, h, h, h, x[0], x[1], mods,
      p['hg_norm'], p['w_branch'], p['w_out'], p['ln1_g'], p['ln1_b'], p['rw'], p['rb'], tri)


MOE_NB = 256


MOE_SLOTS = 3


def _moe_kernel(te_ref, tf_ref, tsl_ref, tnx_ref, tn2_ref, nt_ref, x_ref, w1_hbm, b1_ref, w2_hbm, b2_ref, perm_ref,
                y_ref, w1_f, w2_f, w1_bf, w2_bf, sem, *, layer):
    i = pl.program_id(0)
    half = MOE_NB // 2

    def weight_copies(e, s):
        return (pltpu.make_async_copy(w1_hbm.at[layer, e], w1_f.at[s], sem.at[0, s]),
                pltpu.make_async_copy(w2_hbm.at[layer, e], w2_f.at[s], sem.at[1, s]))

    @pl.when(i == 0)
    def _():
        for cp in weight_copies(te_ref[0], 0):
            cp.start()

        @pl.when(tnx_ref[0] >= 0)
        def _():
            for cp in weight_copies(tnx_ref[0], 1):
                cp.start()

    @pl.when(tf_ref[i] == 1)
    def _():
        s = tsl_ref[i]
        for cp in weight_copies(te_ref[i], s):
            cp.wait()
        ahead = tn2_ref[i]

        @pl.when(ahead >= 0)
        def _():
            for cp in weight_copies(ahead, jnp.where(s == 0, MOE_SLOTS - 1, s - 1)):
                cp.start()

        for b in range(2 * D_EXPERT // MOE_NB):
            blk = w1_f[s, :, b * MOE_NB:(b + 1) * MOE_NB].astype(BF16)
            w1_bf[:, b * MOE_NB:(b + 1) * MOE_NB] = _dot(blk, perm_ref[...]).astype(BF16)
        w2_bf[...] = w2_f[s].astype(BF16)

    @pl.when(i < nt_ref[0])
    def _():
        halves = [_unpack_bf16_pair(x_ref[j]) for j in range(x_ref.shape[0])]
        x = jnp.concatenate([lo for lo, _ in halves] + [hi for _, hi in halves], axis=-1).astype(BF16)
        h = _dot(x, w1_bf[...]) + b1_ref[...]
        acts = []
        for b in range(2 * D_EXPERT // MOE_NB):
            glu = jnp.minimum(h[:, b * MOE_NB:b * MOE_NB + half], SWIGLU_LIMIT)
            lin = jnp.clip(h[:, b * MOE_NB + half:(b + 1) * MOE_NB], -SWIGLU_LIMIT, SWIGLU_LIMIT)
            acts.append((glu * _sigmoid(SWIGLU_ALPHA * glu) * (lin + 1.0)).astype(BF16))
        act = jnp.concatenate(acts, axis=-1)
        y = _dot(act, w2_bf[...]) + b2_ref[...]
        y_ref[...] = _pack_bf16_pair(y[:, :D_MODEL // 2], y[:, D_MODEL // 2:])


def _moe_perm():
    half = MOE_NB // 2
    pm = np.zeros((MOE_NB, MOE_NB), np.float32)
    pm[2 * np.arange(half), np.arange(half)] = 1.0
    pm[2 * np.arange(half) + 1, half + np.arange(half)] = 1.0
    return jnp.asarray(pm, BF16)


def _moe_call(x_sorted, sched, w1, b1p, w2, b2, l):
    tm = MOE_TM
    emap = lambda i, te, *_: (l, te[i], 0, 0)
    tile = lambda i, nt: jnp.minimum(i, nt[0] - 1)
    grid_spec = pltpu.PrefetchScalarGridSpec(
        num_scalar_prefetch=6,
        grid=(MOE_TILES,),
        in_specs=[pl.BlockSpec((SC_SLABS, tm, SC_SLAB_W), lambda i, *s: (0, tile(i, s[-1]), 0)),
                  pl.BlockSpec(memory_space=pl.ANY),
                  pl.BlockSpec((None, None, 1, 2 * D_EXPERT), emap),
                  pl.BlockSpec(memory_space=pl.ANY),
                  pl.BlockSpec((None, None, 1, D_MODEL), emap),
                  pl.BlockSpec((MOE_NB, MOE_NB), lambda i, *_: (0, 0))],
        out_specs=pl.BlockSpec((tm, D_MODEL // 2), lambda i, *s: (tile(i, s[-1]), 0)),
        scratch_shapes=[pltpu.VMEM((MOE_SLOTS, D_MODEL, 2 * D_EXPERT), F32),
                        pltpu.VMEM((MOE_SLOTS, D_EXPERT, D_MODEL), F32),
                        pltpu.VMEM((D_MODEL, 2 * D_EXPERT), BF16), pltpu.VMEM((D_EXPERT, D_MODEL), BF16),
                        pltpu.SemaphoreType.DMA((2, MOE_SLOTS))],
    )
    return pl.pallas_call(
        functools.partial(_moe_kernel, layer=l),
        grid_spec=grid_spec,
        out_shape=jax.ShapeDtypeStruct((MOE_TILES * tm, D_MODEL // 2), F32),
        compiler_params=_cparams(("arbitrary",)),
        name="moe",
    )(*sched, x_sorted, w1, b1p, w2, b2, _moe_perm())


def _route(idx, rank, counts):
    tm = MOE_TM
    tiles_e = (counts + tm - 1) // tm
    eid = np.arange(N_EXPERTS, dtype=np.int32)
    earlier = (eid[None, :] <= eid[:, None]).astype(np.int32)
    tile_end = jnp.sum(earlier * tiles_e[None, :], axis=1)
    tile_start = tile_end - tiles_e
    lookup = lambda table, keys: jnp.sum(jnp.where(keys[..., None] == eid, table, 0), axis=-1)
    pos_t = lookup(tile_start, idx.T) * tm + rank.T
    n_used = tile_end[N_EXPERTS - 1]
    tile_ids = jnp.arange(MOE_TILES, dtype=jnp.int32)
    tid = jnp.minimum(tile_ids, n_used - 1)
    tile_expert = jnp.sum((tile_end[None, :] <= tid[:, None]).astype(jnp.int32), axis=1)
    tile_first = ((tile_ids == lookup(tile_start, tile_expert)) & (tile_ids < n_used)).astype(jnp.int32)
    has_rows = (tiles_e > 0).astype(jnp.int32)
    slot_e = (jnp.sum(earlier * has_rows[None, :], axis=1) - 1) % MOE_SLOTS
    later = jnp.where((eid[None, :] > eid[:, None]) & (has_rows[None, :] > 0), eid[None, :], N_EXPERTS)
    next_e = jnp.min(later, axis=1)
    next2_e = jnp.where(next_e >= N_EXPERTS, N_EXPERTS, lookup(next_e, jnp.minimum(next_e, N_EXPERTS - 1)))
    to_id = lambda e: jnp.where(e >= N_EXPERTS, -1, e).astype(jnp.int32)
    sched = (tile_expert, tile_first, lookup(slot_e, tile_expert).astype(jnp.int32),
             lookup(to_id(next_e), tile_expert).astype(jnp.int32),
             lookup(to_id(next2_e), tile_expert).astype(jnp.int32), n_used.reshape(1))
    return pos_t, sched


SC_WINDOW = 128
SC_SLABS = 2
SC_SLAB_W = D_MODEL // 2 // SC_SLABS


def _dispatch_rows(u2, pos_t):
    mesh = plsc.VectorSubcoreMesh(core_axis_name="core", subcore_axis_name="subcore")

    @functools.partial(pl.kernel, mesh=mesh, scratch_types=[],
                       out_type=jax.ShapeDtypeStruct((SC_SLABS, MOE_TILES * MOE_TM, SC_SLAB_W), F32))
    def dispatch(x_hbm, i_hbm, o_hbm):
        for j in range(SC_SLABS):
            def body(x_vmem, i_vmem, j=j):
                for k in range(TOP_K):
                    pltpu.sync_copy(x_vmem, o_hbm.at[j].at[i_vmem.at[k]])

            pltpu.emit_pipeline(
                body,
                grid=(N_TOK // SC_WINDOW,),
                in_specs=[pl.BlockSpec((SC_WINDOW, SC_SLAB_W), index_map=lambda i: (i, 0)),
                          pl.BlockSpec((TOP_K, SC_WINDOW), index_map=lambda i: (0, i))],
                out_specs=[],
                core_axis_name=("core", "subcore"),
                dimension_semantics=(pltpu.PARALLEL,),
            )(x_hbm.at[j], i_hbm)

    return dispatch(u2, pos_t)


def _final_kernel(yg_ref, wts_ref, x1_ref, mod_ref, modn_ref, lng_ref, lnb_ref, o_ref, un_ref, *, ctx_tiles):
    wts = wts_ref[...]
    ffn_lo = ffn_hi = None
    for k in range(TOP_K):
        lo, hi = _unpack_bf16_pair(yg_ref[k])
        w = wts[:, k:k + 1]
        ffn_lo = w * lo if ffn_lo is None else ffn_lo + w * lo
        ffn_hi = w * hi if ffn_hi is None else ffn_hi + w * hi
    ffn = jnp.concatenate([ffn_lo, ffn_hi], axis=-1)
    x2 = _layer_norm(DN_ALPHA * x1_ref[...] + mod_ref[5:6, :] * ffn, lng_ref[...], lnb_ref[...])
    if ctx_tiles is None:
        o_ref[...] = x2
        un_ref[...] = (x2 * (1.0 + modn_ref[1:2, :]) + modn_ref[0:1, :]).astype(BF16)
    else:
        @pl.when(pl.program_id(0) < ctx_tiles)
        def _():
            o_ref[...] = x2

        @pl.when(pl.program_id(0) >= ctx_tiles)
        def _():
            un_ref[...] = x2


def _final_call(yg, wts, x1, mods, ln_g, ln_b, l):
    tm = 512
    last = l == DEPTH - 1
    ln = min(l + 1, DEPTH - 1)
    ctx_tiles = N_CTX // tm
    lnspec = pl.BlockSpec((None, 1, D_MODEL), lambda i: (l, 0, 0))
    modspec = lambda lyr: pl.BlockSpec((None, None, 6, D_MODEL), lambda i: (lyr, _cond_of_row_tile(i, tm), 0, 0))
    if last:
        out_specs = [pl.BlockSpec((tm, D_MODEL), lambda i: (jnp.minimum(i, ctx_tiles - 1), 0)),
                     pl.BlockSpec((tm, D_MODEL), lambda i: (jnp.maximum(i - ctx_tiles, 0), 0))]
        out_shape = [jax.ShapeDtypeStruct((N_CTX, D_MODEL), F32), jax.ShapeDtypeStruct((N_LAT, D_MODEL), F32)]
    else:
        out_specs = [pl.BlockSpec((tm, D_MODEL), lambda i: (i, 0)), pl.BlockSpec((tm, D_MODEL), lambda i: (i, 0))]
        out_shape = [jax.ShapeDtypeStruct((N_TOK, D_MODEL), F32), jax.ShapeDtypeStruct((N_TOK, D_MODEL), BF16)]
    return pl.pallas_call(
        functools.partial(_final_kernel, ctx_tiles=ctx_tiles if last else None),
        grid=(N_TOK // tm,),
        in_specs=[pl.BlockSpec((TOP_K, tm, D_MODEL // 2), lambda i: (0, i, 0)),
                  pl.BlockSpec((tm, LANES), lambda i: (i, 0)),
                  pl.BlockSpec((tm, D_MODEL), lambda i: (i, 0)),
                  modspec(l), modspec(ln), lnspec, lnspec],
        out_specs=out_specs,
        out_shape=out_shape,
        compiler_params=_cparams(("arbitrary",)),
        name="final",
    )(yg, wts, x1, mods, mods, ln_g, ln_b)


def kernel(x_prompt, x_sample, cache_attn_k, cache_attn_v, state_rglru, state_hgrn, c, c_ctx, w_ada, b_ada, w_in, da_lambda, da_subln, rg_conv_w, rg_conv_b, rg_gate_w, rg_gate_b, rg_lambda, hg_lb, hg_norm, w_branch, w_out, ln1_g, ln1_b, router_w, router_b, w1, b1, w2, b2, ln2_g, ln2_b):
    rw, rb = _router_params(router_w, router_b)
    p = dict(hg_norm=hg_norm.reshape(DEPTH, 1, HG_VAL), w_branch=w_branch, w_out=w_out,
             ln1_g=ln1_g.reshape(DEPTH, 1, D_MODEL), ln1_b=ln1_b.reshape(DEPTH, 1, D_MODEL), rw=rw, rb=rb)
    ln2_g = ln2_g.reshape(DEPTH, 1, D_MODEL)
    ln2_b = ln2_b.reshape(DEPTH, 1, D_MODEL)
    rg_conv_b = rg_conv_b.reshape(DEPTH, 1, RG_WIDTH)
    wg, bg = _rg_gate_dense(rg_gate_w, rg_gate_b)

    x_ctx = x_prompt.reshape(N_CTX, D_MODEL)
    x_lat = x_sample.reshape(N_LAT, D_MODEL)
    cond = jnp.concatenate([c_ctx[None, :], c, jnp.zeros((N_COND - 1 - DEC_BATCH, D_MODEL), F32)], axis=0)
    mods = _adaln_all(cond, w_ada, b_ada).reshape(DEPTH, N_COND, 6, D_MODEL)

    pr = jax.nn.softmax(hg_lb.astype(F32), axis=0)
    lbs = jnp.cumsum(pr, axis=0) - pr[0]
    dl = da_lambda.astype(F32)
    lambda_init = [0.8 - 0.6 * math.exp(-0.3 * l) for l in range(DEPTH)]
    lam_all = (jnp.exp(jnp.sum(dl[:, 0] * dl[:, 1], -1)) - jnp.exp(jnp.sum(dl[:, 2] * dl[:, 3], -1))
               + jnp.asarray(lambda_init, F32))
    lam_v = jnp.broadcast_to(lam_all[:, None, None], (DEPTH, 1, LANES))
    sub_v = da_subln.reshape(DEPTH, 1, DA_V_DIM)

    cache_k = cache_attn_k.reshape(DEC_BATCH, DEPTH, PAST_LEN, BRANCH_W)
    cache_v = cache_attn_v.reshape(DEC_BATCH, DEPTH, PAST_LEN, BRANCH_W)

    b1p = b1.reshape(DEPTH, N_EXPERTS, 2 * D_EXPERT // MOE_NB, MOE_NB // 2, 2)
    b1p = jnp.swapaxes(b1p, -1, -2).reshape(DEPTH, N_EXPERTS, 1, 2 * D_EXPERT)
    b2r = b2.reshape(DEPTH, N_EXPERTS, 1, D_MODEL)

    rgs = []
    kv_all = None
    hg_all = None
    u = _modulate(x_ctx, x_lat, mods, 0)
    x_pair = (x_ctx, x_lat, 0)
    for l in range(DEPTH):
        h = _in_proj(u, w_in, l)
        att, kv_all = _attention(h, cache_k, cache_v, lam_v, sub_v, l, 1.0 - lambda_init[l], kv_all)

        rg_c, hl_c = _rglru_call(h, rg_conv_w, rg_conv_b, wg, bg, rg_lambda, None, l, BATCH, SEQ, 0)
        rg_l, _ = _rglru_call(h, rg_conv_w, rg_conv_b, wg, bg, rg_lambda, state_rglru, l,
                              DEC_BATCH, DEC_SEQ, N_CTX // DEC_SEQ)
        rgs.append(hl_c)

        of_c, ob_c, hg_all = _hgrn_call(h, lbs, None, l, BATCH, SEQ, 0, hg_all)
        of_l, ob_l, _ = _hgrn_call(h, lbs, state_hgrn, l, DEC_BATCH, DEC_SEQ, N_CTX)

        x1, u2, idx, rank, wts, cnt = _merge_call(att, (rg_c, rg_l), (of_c, of_l), (ob_c, ob_l), h, x_pair,
                                                  mods, p, l)

        pos_t, sched = _route(idx[:, :TOP_K], rank[:, :TOP_K], cnt[0, :N_EXPERTS])
        x_sorted = _dispatch_rows(u2, pos_t)
        y_sorted = _moe_call(x_sorted, sched, w1, b1p, w2, b2r, l)
        yg = y_sorted.at[pos_t.reshape(-1)].get(mode='promise_in_bounds').reshape(TOP_K, N_TOK, D_MODEL // 2)
        x, u = _final_call(yg, wts, x1, mods, ln2_g, ln2_b, l)
        x_pair = (x, x, N_CTX // MERGE_TM)

    return (x.reshape(BATCH, SEQ, D_MODEL), u.reshape(DEC_BATCH, DEC_SEQ, D_MODEL),
            kv_all[0].reshape(BATCH, DEPTH, SEQ, DA_HEADS, 2, DA_HEAD_DIM),
            kv_all[1].reshape(BATCH, DEPTH, SEQ, DA_HEADS, DA_V_DIM),
            jnp.stack(rgs, axis=1), hg_all)
```

```python
import functools
import math

import numpy as np
import jax
import jax.numpy as jnp
from jax import lax
from jax.experimental import pallas as pl
from jax.experimental.pallas import tpu as pltpu
from jax.experimental.pallas import tpu_sc as plsc

F32 = jnp.float32
BF16 = jnp.bfloat16
HIGHEST = lax.Precision.HIGHEST

D_MODEL = 1024
BATCH = 16
SEQ = 256
DEPTH = 4
DEC_BATCH = 4
DEC_SEQ = 1024
PAST_LEN = 256
GRID_W = 64
BRANCH_W = 512
DA_HEADS = 4
DA_HEAD_DIM = 64
DA_V_DIM = 128
ROPE_BASE = 10000.0
RG_WIDTH = 512
RG_BLOCKS = 8
RG_BLOCK_W = 64
RG_CONV_W = 4
RG_C = 8.0
HG_HEADS = 4
HG_KEY = 128
HG_VAL = 128
N_EXPERTS = 32
TOP_K = 4
D_EXPERT = 1024
SWIGLU_ALPHA = 1.702
SWIGLU_LIMIT = 7.0
DN_ALPHA = (2 * DEPTH) ** 0.25
NORM_EPS = 1e-5
D_IN = 10 * BRANCH_W + 3 * D_MODEL

N_CTX = BATCH * SEQ
N_LAT = DEC_BATCH * DEC_SEQ
N_TOK = N_CTX + N_LAT
N_COND = 8

LANES = 128
VMEM_LIMIT = 56 * 1024 * 1024

HG_CHUNK = 128
HG_STEP_CHUNKS = 2
HG_LEVELS = (1, 2, 4, 8, 16, 32, 64)
MOE_TM = 256
MERGE_TM = 512
MOE_TILES = (N_TOK * TOP_K) // MOE_TM + N_EXPERTS


MERGE_VMEM_LIMIT = 60 * 1024 * 1024


def _cparams(sem, vmem_limit=VMEM_LIMIT):
    return pltpu.CompilerParams(dimension_semantics=sem, vmem_limit_bytes=vmem_limit)


def _sigmoid(x):
    return 0.5 * jnp.tanh(0.5 * x) + 0.5


def _pack_bf16_pair(lo, hi):
    lo_bits = pltpu.bitcast(lo.astype(BF16).astype(F32), jnp.uint32) >> 16
    hi_bits = pltpu.bitcast(hi.astype(BF16).astype(F32), jnp.uint32) & jnp.uint32(0xFFFF0000)
    return pltpu.bitcast(hi_bits | lo_bits, F32)


def _unpack_bf16_pair(words):
    bits = pltpu.bitcast(words, jnp.uint32)
    return pltpu.bitcast(bits << 16, F32), pltpu.bitcast(bits & jnp.uint32(0xFFFF0000), F32)


def _dot(a, b):
    return jnp.dot(a, b, preferred_element_type=F32)


def _dot_nt(a, b):
    return lax.dot_general(a, b, (((1,), (1,)), ((), ())), preferred_element_type=F32)


def _cond_of_row_tile(i, tm):
    r = i * tm
    return jnp.where(r < N_CTX, 0, 1 + (r - N_CTX) // DEC_SEQ)


def _ada_kernel(c_ref, w_ref, b_ref, o_ref):
    c = c_ref[...]
    s = c * _sigmoid(c)
    o_ref[0] = jnp.dot(s, w_ref[0], precision=HIGHEST, preferred_element_type=F32) + b_ref[0]


def _adaln_all(cond, w_ada, b_ada):
    tn = 1536
    return pl.pallas_call(
        _ada_kernel,
        grid=(DEPTH, 6 * D_MODEL // tn),
        in_specs=[pl.BlockSpec((N_COND, D_MODEL), lambda l, j: (0, 0)),
                  pl.BlockSpec((1, D_MODEL, tn), lambda l, j: (l, 0, j)),
                  pl.BlockSpec((1, 1, tn), lambda l, j: (l, 0, j))],
        out_specs=pl.BlockSpec((1, N_COND, tn), lambda l, j: (l, 0, j)),
        out_shape=jax.ShapeDtypeStruct((DEPTH, N_COND, 6 * D_MODEL), F32),
        compiler_params=_cparams(("arbitrary", "arbitrary")),
        name="adaln",
    )(cond, w_ada, b_ada.reshape(DEPTH, 1, 6 * D_MODEL))


def _modulate_kernel(xc_ref, xl_ref, mod_ref, u_ref, *, ctx_tiles):
    x = jnp.where(pl.program_id(0) < ctx_tiles, xc_ref[...], xl_ref[...])
    u_ref[...] = (x * (1.0 + mod_ref[1:2, :]) + mod_ref[0:1, :]).astype(BF16)


def _modulate(x_ctx, x_lat, mods, l):
    tm = 1024
    ctx_tiles = N_CTX // tm
    return pl.pallas_call(
        functools.partial(_modulate_kernel, ctx_tiles=ctx_tiles),
        grid=(N_TOK // tm,),
        in_specs=[pl.BlockSpec((tm, D_MODEL), lambda i: (jnp.minimum(i, ctx_tiles - 1), 0)),
                  pl.BlockSpec((tm, D_MODEL), lambda i: (jnp.maximum(i - ctx_tiles, 0), 0)),
                  pl.BlockSpec((None, None, 6, D_MODEL), lambda i: (l, _cond_of_row_tile(i, tm), 0, 0))],
        out_specs=pl.BlockSpec((tm, D_MODEL), lambda i: (i, 0)),
        out_shape=jax.ShapeDtypeStruct((N_TOK, D_MODEL), BF16),
        compiler_params=_cparams(("arbitrary",)),
        name="modulate",
    )(x_ctx, x_lat, mods)


def _in_kernel(u_ref, w_ref, o_ref, wbf_ref):
    @pl.when(pl.program_id(1) == 0)
    def _():
        wbf_ref[...] = w_ref[...].astype(BF16)

    o_ref[...] = _dot(u_ref[...], wbf_ref[...]).astype(BF16)


def _in_proj(u, w_in, l):
    tm, tn = 1024, 2048
    return pl.pallas_call(
        _in_kernel,
        grid=(D_IN // tn, N_TOK // tm),
        in_specs=[pl.BlockSpec((tm, D_MODEL), lambda j, i: (i, 0)),
                  pl.BlockSpec((None, D_MODEL, tn), lambda j, i: (l, 0, j))],
        out_specs=pl.BlockSpec((tm, tn), lambda j, i: (i, j)),
        out_shape=jax.ShapeDtypeStruct((N_TOK, D_IN), BF16),
        scratch_shapes=[pltpu.VMEM((D_MODEL, tn), BF16)],
        compiler_params=_cparams(("arbitrary", "arbitrary")),
        name="in_proj",
    )(u, w_in)


def _rope(x, cos, sin_signed):
    lane = lax.broadcasted_iota(jnp.int32, x.shape, 1)
    first = (lane & 31) < 16
    partner = jnp.where(first, pltpu.roll(x, LANES - 16, 1), pltpu.roll(x, 16, 1))
    return x * cos + partner * sin_signed


LOG2E = 1.4426950408889634


def _attn_body(q, keys, vals, lam, subln, out_scale):
    lane = lax.broadcasted_iota(jnp.int32, q.shape, 1)
    qs = q * (DA_HEAD_DIM ** -0.5 * LOG2E)
    acc = None
    for m in range(2):
        in_map = (lane < DA_HEAD_DIM) if m == 0 else (lane >= DA_HEAD_DIM)
        qm = jnp.where(in_map, qs, 0.0).astype(BF16)
        s = [_dot_nt(qm, k) for k in keys]
        mx = s[0].max(axis=-1, keepdims=True)
        for si in s[1:]:
            mx = jnp.maximum(mx, si.max(axis=-1, keepdims=True))
        e = [jnp.exp2(si - mx) for si in s]
        den = e[0].sum(axis=-1, keepdims=True)
        for ei in e[1:]:
            den = den + ei.sum(axis=-1, keepdims=True)
        pv = _dot(e[0].astype(BF16), vals[0])
        for ei, v in zip(e[1:], vals[1:]):
            pv = pv + _dot(ei.astype(BF16), v)
        coef = (1.0 / den) if m == 0 else (-lam[:, 0:1] / den)
        acc = pv * coef if acc is None else acc + pv * coef
    y = acc * lax.rsqrt(jnp.mean(acc * acc, axis=-1, keepdims=True) + NORM_EPS)
    return y * subln * out_scale


def _attn_ctx_kernel(lam_ref, sub_ref, q_ref, k_ref, v_ref, *rest, out_scale):
    o_ref, kout_ref, vout_ref = rest[-3:]
    if len(kout_ref.shape) == 3:
        kout_ref[...] = jnp.zeros(kout_ref.shape, F32)
        vout_ref[...] = jnp.zeros(vout_ref.shape, F32)
        kout_ref, vout_ref = kout_ref.at[0], vout_ref.at[0]
    kout_ref[...] = k_ref[...].astype(F32)
    vout_ref[...] = v_ref[...].astype(F32)
    for hd in range(DA_HEADS):
        cols = slice(hd * LANES, (hd + 1) * LANES)
        o_ref[:, cols] = _attn_body(q_ref[:, cols].astype(F32), [k_ref[:, cols]], [v_ref[:, cols]],
                                    lam_ref[...], sub_ref[...], out_scale)


def _attn_lat_kernel(lam_ref, sub_ref, q_ref, k_ref, v_ref, kc_ref, vc_ref, cos_ref, sin_ref,
                     o_ref, k_scr, kc_scr, vc_scr, *, out_scale, tq):
    k_scr[...] = _rope(k_ref[...].astype(F32), cos_ref[...], sin_ref[...]).astype(BF16)
    kc_scr[...] = kc_ref[...].astype(BF16)
    vc_scr[...] = vc_ref[...].astype(BF16)

    def q_block(qi, carry):
        rows = pl.ds(pl.multiple_of(qi * tq, tq), tq)
        q = _rope(q_ref[rows, :].astype(F32), cos_ref[rows, :], sin_ref[rows, :])
        o_ref[rows, :] = _attn_body(q, [k_scr[...], kc_scr[...]], [v_ref[...], vc_scr[...]],
                                    lam_ref[...], sub_ref[...], out_scale)
        return carry

    lax.fori_loop(0, DEC_SEQ // tq, q_block, 0)


def _rope_tables():
    t = np.arange(DEC_SEQ)
    row = (t // GRID_W).astype(np.float64)
    col = (t % GRID_W).astype(np.float64)
    d = DA_HEAD_DIM // 2
    inv = ROPE_BASE ** (-np.arange(0, d, 2, dtype=np.float64) / d)
    a_row = row[:, None] * inv[None, :]
    a_col = col[:, None] * inv[None, :]
    cos = np.concatenate([np.cos(a_row), np.cos(a_row), np.cos(a_col), np.cos(a_col)], -1)
    sin = np.concatenate([-np.sin(a_row), np.sin(a_row), -np.sin(a_col), np.sin(a_col)], -1)
    return (jnp.asarray(np.tile(cos, (1, 2)), F32), jnp.asarray(np.tile(sin, (1, 2)), F32))


def _attention(h, cache_k, cache_v, lam_v, sub_v, l, out_scale, kv_prev):
    small = pl.BlockSpec((None, 1, LANES), lambda *a: (l, 0, 0))
    if kv_prev is None:
        kv_blk = pl.BlockSpec((None, DEPTH, SEQ, BRANCH_W), lambda b: (b, 0, 0, 0))
    else:
        kv_blk = pl.BlockSpec((None, None, SEQ, BRANCH_W), lambda b: (b, l, 0, 0))
    kv_shape = jax.ShapeDtypeStruct((BATCH, DEPTH, SEQ, BRANCH_W), F32)
    carried = () if kv_prev is None else tuple(kv_prev)
    att_ctx, k_all, v_all = pl.pallas_call(
        functools.partial(_attn_ctx_kernel, out_scale=out_scale),
        grid=(BATCH,),
        in_specs=[small, small,
                  pl.BlockSpec((SEQ, BRANCH_W), lambda b: (b, 0)),
                  pl.BlockSpec((SEQ, BRANCH_W), lambda b: (b, 1)),
                  pl.BlockSpec((SEQ, BRANCH_W), lambda b: (b, 2))]
                 + [pl.BlockSpec(memory_space=pl.ANY)] * len(carried),
        out_specs=[pl.BlockSpec((SEQ, BRANCH_W), lambda b: (b, 0)), kv_blk, kv_blk],
        out_shape=[jax.ShapeDtypeStruct((N_CTX, BRANCH_W), F32), kv_shape, kv_shape],
        input_output_aliases={5: 1, 6: 2} if carried else {},
        compiler_params=_cparams(("arbitrary",)),
        name="attn_ctx",
    )(lam_v, sub_v, h, h, h, *carried)

    tq = 256
    cos, sin = _rope_tables()
    row0 = N_CTX // DEC_SEQ
    seq_blk = lambda cb: pl.BlockSpec((DEC_SEQ, LANES), lambda b, hd: (row0 + b, cb + hd))
    cache_blk = pl.BlockSpec((None, None, PAST_LEN, LANES), lambda b, hd: (b, l, 0, hd))
    table = pl.BlockSpec((DEC_SEQ, LANES), lambda b, hd: (0, 0))
    att_lat = pl.pallas_call(
        functools.partial(_attn_lat_kernel, out_scale=out_scale, tq=tq),
        grid=(DEC_BATCH, DA_HEADS),
        in_specs=[small, small, seq_blk(0), seq_blk(4), seq_blk(8), cache_blk, cache_blk, table, table],
        out_specs=pl.BlockSpec((DEC_SEQ, LANES), lambda b, hd: (b, hd)),
        out_shape=jax.ShapeDtypeStruct((N_LAT, BRANCH_W), F32),
        scratch_shapes=[pltpu.VMEM((DEC_SEQ, LANES), BF16), pltpu.VMEM((PAST_LEN, LANES), BF16),
                        pltpu.VMEM((PAST_LEN, LANES), BF16)],
        compiler_params=_cparams(("arbitrary", "arbitrary")),
        name="attn_lat",
    )(lam_v, sub_v, h, h, h, cache_k, cache_v, cos, sin)
    return (att_ctx, att_lat), (k_all, v_all)


def _gelu_tanh(x):
    return 0.5 * x * (1.0 + jnp.tanh(math.sqrt(2.0 / math.pi) * (x + 0.044715 * (x * x * x))))


def _rg_kernel(*refs, seq, has_h0):
    if has_h0:
        rx_ref, gate_ref, cw_ref, cb_ref, wg_ref, bg_ref, lam_ref, h0_ref = refs[:8]
        rest = refs[8:]
    else:
        rx_ref, gate_ref, cw_ref, cb_ref, wg_ref, bg_ref, lam_ref = refs[:7]
        h0_ref = None
        rest = refs[7:]
    out_ref, hl_ref, a_scr, b_scr, h_scr = rest

    x = rx_ref[...].astype(F32)
    row = lax.broadcasted_iota(jnp.int32, x.shape, 0)
    xr = cb_ref[...] + cw_ref[2:3, :] * x
    for j in (0, 1, 3):
        d = j - RG_CONV_W // 2
        shifted = pltpu.roll(x, (-d) % seq, 0)
        valid = (row + d >= 0) & (row + d < seq)
        xr = xr + cw_ref[j:j + 1, :] * jnp.where(valid, shifted, 0.0)

    g = _dot(xr.astype(BF16), wg_ref[...]) + bg_ref[...]
    for dr in range(2):
        r = _sigmoid(g[:, (2 * dr) * RG_WIDTH:(2 * dr + 1) * RG_WIDTH])
        i = _sigmoid(g[:, (2 * dr + 1) * RG_WIDTH:(2 * dr + 2) * RG_WIDTH])
        lam = lam_ref[dr:dr + 1, :]
        softplus_neg = jnp.maximum(-lam, 0.0) + jnp.log(1.0 + jnp.exp(-jnp.abs(lam)))
        log_a = (-RG_C * softplus_neg) * r
        a = jnp.exp(log_a)
        a_scr[dr] = a
        b_scr[dr] = jnp.sqrt(1.0 - a * a) * i * xr

    if has_h0:
        hf0, hb0 = h0_ref[0:1, :], h0_ref[1:2, :]
    else:
        hf0 = hb0 = jnp.zeros((1, RG_WIDTH), F32)

    def step(t, carry):
        hf, hb = carry
        tb = seq - 1 - t
        hf = a_scr[0, pl.ds(t, 1), :] * hf + b_scr[0, pl.ds(t, 1), :]
        h_scr[0, pl.ds(t, 1), :] = hf
        hb = a_scr[1, pl.ds(tb, 1), :] * hb + b_scr[1, pl.ds(tb, 1), :]
        h_scr[1, pl.ds(tb, 1), :] = hb
        return hf, hb

    hf, hb = lax.fori_loop(0, seq, step, (hf0, hb0), unroll=8)
    hl_ref[0:1, :] = hf
    hl_ref[1:2, :] = hb
    out_ref[...] = (h_scr[0] + h_scr[1]) * _gelu_tanh(gate_ref[...].astype(F32))


def _rglru_call(h, conv_w, conv_b, wg, bg, lam, h0, l, nseq, seq, row_block0):
    has_h0 = h0 is not None
    layer = lambda shape: pl.BlockSpec((None,) + shape, lambda b: (l,) + (0,) * len(shape))
    in_specs = [pl.BlockSpec((seq, RG_WIDTH), lambda b: (row_block0 + b, 3)),
                pl.BlockSpec((seq, RG_WIDTH), lambda b: (row_block0 + b, 4)),
                layer((RG_CONV_W, RG_WIDTH)), layer((1, RG_WIDTH)),
                layer((RG_WIDTH, 4 * RG_WIDTH)), layer((1, 4 * RG_WIDTH)), layer((2, RG_WIDTH))]
    args = [h, h, conv_w, conv_b, wg, bg, lam]
    if has_h0:
        in_specs.append(pl.BlockSpec((None, None, 2, RG_WIDTH), lambda b: (b, l, 0, 0)))
        args.append(h0)
    return pl.pallas_call(
        functools.partial(_rg_kernel, seq=seq, has_h0=has_h0),
        grid=(nseq,),
        in_specs=in_specs,
        out_specs=[pl.BlockSpec((seq, RG_WIDTH), lambda b: (b, 0)),
                   pl.BlockSpec((None, 2, RG_WIDTH), lambda b: (b, 0, 0))],
        out_shape=[jax.ShapeDtypeStruct((nseq * seq, RG_WIDTH), F32),
                   jax.ShapeDtypeStruct((nseq, 2, RG_WIDTH), F32)],
        scratch_shapes=[pltpu.VMEM((2, seq, RG_WIDTH), F32)] * 3,
        compiler_params=_cparams(("arbitrary",)),
        name="rglru_lat" if has_h0 else "rglru_ctx",
    )(*args)


def _rg_gate_dense(gate_w, gate_b):
    eye = jnp.eye(RG_BLOCKS, dtype=BF16)
    blocks = jnp.transpose(gate_w, (0, 3, 4, 1, 2, 5)).astype(BF16)
    wg = blocks[:, :, :, :, :, None, :] * eye[None, :, None, None, None, :, None]
    return wg.reshape(DEPTH, RG_WIDTH, 4 * RG_WIDTH), gate_b.reshape(DEPTH, 1, 4 * RG_WIDTH)


def _hg_masks():
    c = HG_CHUNK
    t = np.arange(c)[:, None]
    s = np.arange(c)[None, :]
    lvl = np.zeros((2, len(HG_LEVELS), c, c), np.float32)
    for n, h in enumerate(HG_LEVELS):
        same = (t // (2 * h)) == (s // (2 * h))
        lvl[0, n] = same & ((t // h) % 2 == 1) & ((s // h) % 2 == 0)
        lvl[1, n] = same & ((t // h) % 2 == 0) & ((s // h) % 2 == 1)
    return jnp.asarray(lvl)


def _hg_chunk(q_raw, z, v, lb, st, lvl, rev):
    c = HG_CHUNK
    q = q_raw * _sigmoid(q_raw)
    sig = _sigmoid(z)
    f = lb + (1.0 - lb) * sig
    kk = (1.0 - lb) * (1.0 - sig)

    rowk = lax.broadcasted_iota(jnp.int32, (c, HG_KEY), 0)
    pos = rowk if not rev else (c - 1) - rowk
    earlier = lambda x, d: pltpu.roll(x, d if not rev else c - d, 0)
    later = lambda x, d: pltpu.roll(x, c - d if not rev else d, 0)

    def block_row(x, n, p):
        r = p if not rev else n - 1 - p
        x3 = x.reshape(c // n, n, HG_KEY)
        return jnp.broadcast_to(x3[:, r:r + 1, :], x3.shape).reshape(c, HG_KEY)

    pair = f * earlier(f, 1)
    f_next = later(f, 1)
    p4 = pos & 3
    a = {1: f, 2: jnp.where((pos & 1) == 1, pair, f)}
    b = {2: jnp.where((pos & 1) == 0, f_next, 1.0)}
    a[4] = a[2] * jnp.where(p4 == 2, earlier(pair, 1), jnp.where(p4 == 3, earlier(pair, 2), 1.0))
    b[4] = jnp.where(p4 == 3, 1.0, jnp.where(p4 == 2, f_next,
                                             jnp.where(p4 == 1, later(pair, 2), f_next * later(pair, 3))))
    h = 4
    while h < c:
        in_later_half = (pos & h) != 0
        a[2 * h] = jnp.where(in_later_half, a[h] * block_row(a[h], 2 * h, h - 1), a[h])
        b[2 * h] = jnp.where(in_later_half, b[h], b[h] * block_row(a[h], 2 * h, 2 * h - 1))
        h *= 2

    row = lax.broadcasted_iota(jnp.int32, (c, c), 0)
    col = lax.broadcasted_iota(jnp.int32, (c, c), 1)
    scores = jnp.where(row == col, jnp.sum(q * kk, axis=-1, keepdims=True), 0.0)
    for n, h in enumerate(HG_LEVELS):
        qa = (q * a[h]).astype(BF16)
        kb = (kk if h == 1 else kk * b[h]).astype(BF16)
        scores = scores + lvl[n] * _dot_nt(qa, kb)

    vb = v.astype(BF16)
    o = _dot(scores.astype(BF16), vb)
    o = o + _dot_nt((q * a[c]).astype(BF16), st.astype(BF16))
    total = a[c][c - 1:c, :] if not rev else a[c][0:1, :]
    st_new = st * total + _dot(v.T.astype(BF16), (kk * b[c]).astype(BF16))
    return o, st_new


def _hg_kernel(*refs, has_s0, nchunk):
    if has_s0:
        (qf_ref, qb_ref, zf_ref, zb_ref, vf_ref, vb_ref, lb_ref, lvl_ref, s0_ref,
         of_ref, ob_ref, sfin_ref, st_scr) = refs
    else:
        qf_ref, qb_ref, zf_ref, zb_ref, vf_ref, vb_ref, lb_ref, lvl_ref = refs[:8]
        of_ref, ob_ref, sfin_ref, st_scr = refs[-4:]
        s0_ref = None
    ci = pl.program_id(1)

    @pl.when(ci == 0)
    def _():
        for dr in range(2):
            for hd in range(HG_HEADS):
                st_scr[dr, hd] = s0_ref[dr, hd].T if has_s0 else jnp.zeros((HG_VAL, HG_KEY), F32)

    last = ci == nchunk // HG_STEP_CHUNKS - 1
    c = HG_CHUNK
    if len(sfin_ref.shape) == 5:
        @pl.when(last)
        def _():
            sfin_ref[1:] = jnp.zeros((sfin_ref.shape[0] - 1,) + sfin_ref.shape[1:], F32)

        sfin_ref = sfin_ref.at[0]
    for hd in range(HG_HEADS):
        cols = slice(hd * LANES, (hd + 1) * LANES)
        st_f, st_b = st_scr[0, hd], st_scr[1, hd]
        for sc in range(HG_STEP_CHUNKS):
            rf = slice(sc * c, (sc + 1) * c)
            rb = slice((HG_STEP_CHUNKS - 1 - sc) * c, (HG_STEP_CHUNKS - sc) * c)
            ldf = lambda ref: ref[rf, cols].astype(F32)
            ldb = lambda ref: ref[rb, cols].astype(F32)
            o_f, st_f = _hg_chunk(ldf(qf_ref), ldf(zf_ref), ldf(vf_ref), lb_ref[0:1, cols], st_f, lvl_ref[0], False)
            o_b, st_b = _hg_chunk(ldb(qb_ref), ldb(zb_ref), ldb(vb_ref), lb_ref[1:2, cols], st_b, lvl_ref[1], True)
            of_ref[rf, cols] = o_f
            ob_ref[rb, cols] = o_b
        st_scr[0, hd] = st_f
        st_scr[1, hd] = st_b

        @pl.when(last)
        def _():
            sfin_ref[0, hd] = st_f.T
            sfin_ref[1, hd] = st_b.T


def _hgrn_call(h, lbs_l, s0, l, nseq, seq, row0, state_prev=None):
    c = HG_CHUNK
    nchunk = seq // c
    has_s0 = s0 is not None
    lvl = _hg_masks()
    rows = c * HG_STEP_CHUNKS
    nstep = seq // rows
    rf = lambda b, ci: row0 // rows + b * nstep + ci
    rb = lambda b, ci: row0 // rows + b * nstep + (nstep - 1 - ci)
    blk = lambda rfun, cb: pl.BlockSpec((rows, BRANCH_W), lambda b, ci: (rfun(b, ci), cb))
    const = lambda shape: pl.BlockSpec(shape, lambda b, ci: (0,) * len(shape))
    in_specs = [blk(rf, 5), blk(rb, 5), blk(rf, 6), blk(rb, 7), blk(rf, 8), blk(rb, 8),
                pl.BlockSpec((None, 2, BRANCH_W), lambda b, ci: (l, 0, 0)), const((2, len(HG_LEVELS), c, c))]
    args = [h, h, h, h, h, h, lbs_l, lvl]
    if has_s0:
        in_specs.append(pl.BlockSpec((None, None, 2, HG_HEADS, HG_KEY, HG_VAL), lambda b, ci: (b, l, 0, 0, 0, 0)))
        args.append(s0)
        state_spec = pl.BlockSpec((None, 2, HG_HEADS, HG_KEY, HG_VAL), lambda b, ci: (b, 0, 0, 0, 0))
        state_shape = jax.ShapeDtypeStruct((nseq, 2, HG_HEADS, HG_KEY, HG_VAL), F32)
    else:
        if state_prev is None:
            state_spec = pl.BlockSpec((None, DEPTH, 2, HG_HEADS, HG_KEY, HG_VAL), lambda b, ci: (b, 0, 0, 0, 0, 0))
        else:
            state_spec = pl.BlockSpec((None, None, 2, HG_HEADS, HG_KEY, HG_VAL),
                                      lambda b, ci: (b, l, 0, 0, 0, 0))
        state_shape = jax.ShapeDtypeStruct((nseq, DEPTH, 2, HG_HEADS, HG_KEY, HG_VAL), F32)
    aliases = {}
    if state_prev is not None:
        aliases = {len(args): 2}
        in_specs.append(pl.BlockSpec(memory_space=pl.ANY))
        args.append(state_prev)
    return pl.pallas_call(
        functools.partial(_hg_kernel, has_s0=has_s0, nchunk=nchunk),
        grid=(nseq, nstep),
        in_specs=in_specs,
        out_specs=[pl.BlockSpec((rows, BRANCH_W), lambda b, ci: (b * nstep + ci, 0)),
                   pl.BlockSpec((rows, BRANCH_W), lambda b, ci: (b * nstep + (nstep - 1 - ci), 0)),
                   state_spec],
        out_shape=[jax.ShapeDtypeStruct((nseq * seq, BRANCH_W), F32),
                   jax.ShapeDtypeStruct((nseq * seq, BRANCH_W), F32),
                   state_shape],
        input_output_aliases=aliases,
        scratch_shapes=[pltpu.VMEM((2, HG_HEADS, HG_VAL, HG_KEY), F32)],
        compiler_params=_cparams(("arbitrary", "arbitrary")),
        name="hgrn_lat" if has_s0 else "hgrn_ctx",
    )(*args)


def _layer_norm(y, g, b):
    mu = jnp.mean(y, axis=-1, keepdims=True)
    yc = y - mu
    var = jnp.mean(yc * yc, axis=-1, keepdims=True)
    return yc * lax.rsqrt(var + NORM_EPS) * g + b


def _merge_kernel(attc_ref, attl_ref, rgc_ref, rgl_ref, ofc_ref, ofl_ref, obc_ref, obl_ref,
                  hgate_ref, mg0_ref, mg1_ref, mg2_ref, xc_ref, xl_ref, mod_ref,
                  hgn_ref, wbr_ref, wout_ref, lng_ref, lnb_ref, rw_ref, rb_ref, tri_ref,
                  x1_ref, u2_ref, idx_ref, rank_ref, wts_ref, cnt_ref, wbr_bf, wout_bf, cnt_scr, *, ctx_tiles):
    @pl.when(pl.program_id(0) == 0)
    def _():
        wbr_bf[...] = wbr_ref[...].astype(BF16)
        wout_bf[...] = wout_ref[...].astype(BF16)
        cnt_scr[...] = jnp.zeros(cnt_scr.shape, F32)

    is_ctx = pl.program_id(0) < ctx_tiles
    pick = lambda c_ref, l_ref: jnp.where(is_ctx, c_ref[...], l_ref[...])
    att = pick(attc_ref, attl_ref)
    rg = pick(rgc_ref, rgl_ref)
    o = pick(ofc_ref, ofl_ref) + pick(obc_ref, obl_ref)
    hgate = hgate_ref[...].astype(F32)
    hg_parts = []
    for hd in range(HG_HEADS):
        oh = o[:, hd * HG_VAL:(hd + 1) * HG_VAL]
        gh = hgate[:, hd * HG_VAL:(hd + 1) * HG_VAL]
        yh = oh * lax.rsqrt(jnp.mean(oh * oh, axis=-1, keepdims=True) + NORM_EPS) * hgn_ref[...]
        hg_parts.append(yh * (gh * _sigmoid(gh)))
    hg = jnp.concatenate(hg_parts, axis=-1)

    proj = _sigmoid(mg0_ref[...].astype(F32)) * _dot(att.astype(BF16), wbr_bf[0])
    proj = proj + _sigmoid(mg1_ref[...].astype(F32)) * _dot(rg.astype(BF16), wbr_bf[1])
    proj = proj + _sigmoid(mg2_ref[...].astype(F32)) * _dot(hg.astype(BF16), wbr_bf[2])
    mix = _dot(proj.astype(BF16), wout_bf[...])

    x1 = _layer_norm(DN_ALPHA * pick(xc_ref, xl_ref) + mod_ref[2:3, :] * mix, lng_ref[...], lnb_ref[...])
    x1_ref[...] = x1
    u2 = x1 * (1.0 + mod_ref[4:5, :]) + mod_ref[3:4, :]
    packed = _pack_bf16_pair(u2[:, :D_MODEL // 2], u2[:, D_MODEL // 2:])
    for j in range(u2_ref.shape[0]):
        u2_ref[j] = packed[:, j * u2_ref.shape[2]:(j + 1) * u2_ref.shape[2]]

    u2_hi = u2.astype(BF16)
    u2_lo = (u2 - u2_hi.astype(F32)).astype(BF16)
    logits = (_dot(u2_hi, rw_ref[0]) + _dot(u2_lo, rw_ref[0]) + _dot(u2_hi, rw_ref[1])) + rb_ref[...]
    lane = lax.broadcasted_iota(jnp.int32, logits.shape, 1).astype(F32)
    idx_out = jnp.zeros(logits.shape, F32)
    wts_out = jnp.zeros(logits.shape, F32)
    chosen = jnp.zeros(logits.shape, F32)
    sels = []
    top0 = None
    den = None
    for k in range(TOP_K):
        m = jnp.max(logits, axis=-1, keepdims=True)
        sel = jnp.min(jnp.where(logits == m, lane, float(LANES)), axis=-1, keepdims=True)
        sels.append(sel)
        if k == 0:
            top0 = m
        e = jnp.exp(m - top0)
        den = e if den is None else den + e
        idx_out = jnp.where(lane == k, sel, idx_out)
        wts_out = jnp.where(lane == k, e, wts_out)
        chosen = jnp.where(lane == sel, 1.0, chosen)
        logits = jnp.where(lane == sel, -jnp.inf, logits)
    idx_ref[...] = idx_out.astype(jnp.int32)
    wts_ref[...] = wts_out * (1.0 / den)

    before = _dot(tri_ref[...], chosen.astype(BF16)) + cnt_scr[...]
    rank_out = jnp.zeros(logits.shape, F32)
    for k in range(TOP_K):
        rk = jnp.sum(jnp.where(lane == sels[k], before, 0.0), axis=-1, keepdims=True)
        rank_out = jnp.where(lane == k, rk, rank_out)
    rank_ref[...] = rank_out.astype(jnp.int32)
    cnt_scr[...] = cnt_scr[...] + jnp.sum(chosen, axis=0, keepdims=True)
    cnt_ref[...] = jnp.broadcast_to(cnt_scr[...], cnt_ref.shape).astype(jnp.int32)


def _router_params(router_w, router_b):
    rw = jnp.zeros((DEPTH, D_MODEL, LANES), F32).at[:, :, :N_EXPERTS].set(router_w)
    rb = jnp.full((DEPTH, 1, LANES), -1e30, F32).at[:, 0, :N_EXPERTS].set(router_b)
    rw_hi = rw.astype(BF16)
    return jnp.stack([rw_hi, (rw - rw_hi.astype(F32)).astype(BF16)], axis=1), rb


def _merge_call(att, rg, o_f, o_b, h, x, mods, p, l):
    tm = MERGE_TM
    ctx_tiles = N_CTX // tm
    rowb = lambda w, cb: pl.BlockSpec((tm, w), lambda i: (i, cb))
    ctxb = lambda w: pl.BlockSpec((tm, w), lambda i: (jnp.minimum(i, ctx_tiles - 1), 0))
    latb = lambda w, base=0: pl.BlockSpec((tm, w), lambda i: (jnp.maximum(i - ctx_tiles, 0) + base, 0))
    const = lambda shape: pl.BlockSpec(shape, lambda i: (0,) * len(shape))
    layer = lambda shape: pl.BlockSpec((None,) + shape, lambda i: (l,) + (0,) * len(shape),
                                       pipeline_mode=pl.Buffered(1))
    bw = BRANCH_W
    tri = jnp.asarray(np.tril(np.ones((tm, tm), np.float32), -1), BF16)
    return pl.pallas_call(
        functools.partial(_merge_kernel, ctx_tiles=ctx_tiles),
        grid=(N_TOK // tm,),
        in_specs=[ctxb(bw), latb(bw), ctxb(bw), latb(bw), ctxb(bw), latb(bw), ctxb(bw), latb(bw),
                  rowb(BRANCH_W, 9), rowb(D_MODEL, 5), rowb(D_MODEL, 6), rowb(D_MODEL, 7),
                  ctxb(D_MODEL), latb(D_MODEL, x[2]),
                  pl.BlockSpec((None, None, 6, D_MODEL), lambda i: (l, _cond_of_row_tile(i, tm), 0, 0)),
                  layer((1, HG_VAL)), layer((3, BRANCH_W, D_MODEL)), layer((D_MODEL, D_MODEL)),
                  layer((1, D_MODEL)), layer((1, D_MODEL)), layer((2, D_MODEL, LANES)), layer((1, LANES)),
                  const((tm, tm))],
        out_specs=[rowb(D_MODEL, 0), pl.BlockSpec((SC_SLABS, tm, SC_SLAB_W), lambda i: (0, i, 0)),
                   rowb(LANES, 0), rowb(LANES, 0), rowb(LANES, 0), const((8, LANES))],
        out_shape=[jax.ShapeDtypeStruct((N_TOK, D_MODEL), F32),
                   jax.ShapeDtypeStruct((SC_SLABS, N_TOK, SC_SLAB_W), F32),
                   jax.ShapeDtypeStruct((N_TOK, LANES), jnp.int32),
                   jax.ShapeDtypeStruct((N_TOK, LANES), jnp.int32),
                   jax.ShapeDtypeStruct((N_TOK, LANES), F32),
                   jax.ShapeDtypeStruct((8, LANES), jnp.int32)],
        scratch_shapes=[pltpu.VMEM((3, BRANCH_W, D_MODEL), BF16), pltpu.VMEM((D_MODEL, D_MODEL), BF16),
                        pltpu.VMEM((1, LANES), F32)],
        compiler_params=_cparams(("arbitrary",), MERGE_VMEM_LIMIT),
        name="merge",
    )(att[0], att[1], rg[0], rg[1], o_f[0], o_f[1], o_b[0], o_b[1], h, h, h, h, x[0], x[1], mods,
      p['hg_norm'], p['w_branch'], p['w_out'], p['ln1_g'], p['ln1_b'], p['rw'], p['rb'], tri)


MOE_NB = 256


MOE_SLOTS = 3


def _moe_kernel(te_ref, tf_ref, tsl_ref, tnx_ref, tn2_ref, nt_ref, x_ref, w1_hbm, b1_ref, w2_hbm, b2_ref, perm_ref,
                y_ref, w1_f, w2_f, w1_bf, w2_bf, sem, *, layer):
    i = pl.program_id(0)
    half = MOE_NB // 2

    def weight_copies(e, s):
        return (pltpu.make_async_copy(w1_hbm.at[layer, e], w1_f.at[s], sem.at[0, s]),
                pltpu.make_async_copy(w2_hbm.at[layer, e], w2_f.at[s], sem.at[1, s]))

    @pl.when(i == 0)
    def _():
        for cp in weight_copies(te_ref[0], 0):
            cp.start()

        @pl.when(tnx_ref[0] >= 0)
        def _():
            for cp in weight_copies(tnx_ref[0], 1):
                cp.start()

    @pl.when(tf_ref[i] == 1)
    def _():
        s = tsl_ref[i]
        for cp in weight_copies(te_ref[i], s):
            cp.wait()
        ahead = tn2_ref[i]

        @pl.when(ahead >= 0)
        def _():
            for cp in weight_copies(ahead, jnp.where(s == 0, MOE_SLOTS - 1, s - 1)):
                cp.start()

        for b in range(2 * D_EXPERT // MOE_NB):
            blk = w1_f[s, :, b * MOE_NB:(b + 1) * MOE_NB].astype(BF16)
            w1_bf[:, b * MOE_NB:(b + 1) * MOE_NB] = _dot(blk, perm_ref[...]).astype(BF16)
        w2_bf[...] = w2_f[s].astype(BF16)

    @pl.when(i < nt_ref[0])
    def _():
        halves = [_unpack_bf16_pair(x_ref[j]) for j in range(x_ref.shape[0])]
        x = jnp.concatenate([lo for lo, _ in halves] + [hi for _, hi in halves], axis=-1).astype(BF16)
        h = _dot(x, w1_bf[...]) + b1_ref[...]
        acts = []
        for b in range(2 * D_EXPERT // MOE_NB):
            glu = jnp.minimum(h[:, b * MOE_NB:b * MOE_NB + half], SWIGLU_LIMIT)
            lin = jnp.clip(h[:, b * MOE_NB + half:(b + 1) * MOE_NB], -SWIGLU_LIMIT, SWIGLU_LIMIT)
            acts.append((glu * _sigmoid(SWIGLU_ALPHA * glu) * (lin + 1.0)).astype(BF16))
        act = jnp.concatenate(acts, axis=-1)
        y = _dot(act, w2_bf[...]) + b2_ref[...]
        y_ref[...] = _pack_bf16_pair(y[:, :D_MODEL // 2], y[:, D_MODEL // 2:])


def _moe_perm():
    half = MOE_NB // 2
    pm = np.zeros((MOE_NB, MOE_NB), np.float32)
    pm[2 * np.arange(half), np.arange(half)] = 1.0
    pm[2 * np.arange(half) + 1, half + np.arange(half)] = 1.0
    return jnp.asarray(pm, BF16)


def _moe_call(x_sorted, sched, w1, b1p, w2, b2, l):
    tm = MOE_TM
    emap = lambda i, te, *_: (l, te[i], 0, 0)
    tile = lambda i, nt: jnp.minimum(i, nt[0] - 1)
    grid_spec = pltpu.PrefetchScalarGridSpec(
        num_scalar_prefetch=6,
        grid=(MOE_TILES,),
        in_specs=[pl.BlockSpec((SC_SLABS, tm, SC_SLAB_W), lambda i, *s: (0, tile(i, s[-1]), 0)),
                  pl.BlockSpec(memory_space=pl.ANY),
                  pl.BlockSpec((None, None, 1, 2 * D_EXPERT), emap),
                  pl.BlockSpec(memory_space=pl.ANY),
                  pl.BlockSpec((None, None, 1, D_MODEL), emap),
                  pl.BlockSpec((MOE_NB, MOE_NB), lambda i, *_: (0, 0))],
        out_specs=pl.BlockSpec((tm, D_MODEL // 2), lambda i, *s: (tile(i, s[-1]), 0)),
        scratch_shapes=[pltpu.VMEM((MOE_SLOTS, D_MODEL, 2 * D_EXPERT), F32),
                        pltpu.VMEM((MOE_SLOTS, D_EXPERT, D_MODEL), F32),
                        pltpu.VMEM((D_MODEL, 2 * D_EXPERT), BF16), pltpu.VMEM((D_EXPERT, D_MODEL), BF16),
                        pltpu.SemaphoreType.DMA((2, MOE_SLOTS))],
    )
    return pl.pallas_call(
        functools.partial(_moe_kernel, layer=l),
        grid_spec=grid_spec,
        out_shape=jax.ShapeDtypeStruct((MOE_TILES * tm, D_MODEL // 2), F32),
        compiler_params=_cparams(("arbitrary",)),
        name="moe",
    )(*sched, x_sorted, w1, b1p, w2, b2, _moe_perm())


def _route(idx, rank, counts):
    tm = MOE_TM
    tiles_e = (counts + tm - 1) // tm
    eid = np.arange(N_EXPERTS, dtype=np.int32)
    earlier = (eid[None, :] <= eid[:, None]).astype(np.int32)
    tile_end = jnp.sum(earlier * tiles_e[None, :], axis=1)
    tile_start = tile_end - tiles_e
    lookup = lambda table, keys: jnp.sum(jnp.where(keys[..., None] == eid, table, 0), axis=-1)
    pos_t = lookup(tile_start, idx.T) * tm + rank.T
    n_used = tile_end[N_EXPERTS - 1]
    tile_ids = jnp.arange(MOE_TILES, dtype=jnp.int32)
    tid = jnp.minimum(tile_ids, n_used - 1)
    tile_expert = jnp.sum((tile_end[None, :] <= tid[:, None]).astype(jnp.int32), axis=1)
    tile_first = ((tile_ids == lookup(tile_start, tile_expert)) & (tile_ids < n_used)).astype(jnp.int32)
    has_rows = (tiles_e > 0).astype(jnp.int32)
    slot_e = (jnp.sum(earlier * has_rows[None, :], axis=1) - 1) % MOE_SLOTS
    later = jnp.where((eid[None, :] > eid[:, None]) & (has_rows[None, :] > 0), eid[None, :], N_EXPERTS)
    next_e = jnp.min(later, axis=1)
    next2_e = jnp.where(next_e >= N_EXPERTS, N_EXPERTS, lookup(next_e, jnp.minimum(next_e, N_EXPERTS - 1)))
    to_id = lambda e: jnp.where(e >= N_EXPERTS, -1, e).astype(jnp.int32)
    sched = (tile_expert, tile_first, lookup(slot_e, tile_expert).astype(jnp.int32),
             lookup(to_id(next_e), tile_expert).astype(jnp.int32),
             lookup(to_id(next2_e), tile_expert).astype(jnp.int32), n_used.reshape(1))
    return pos_t, sched


SC_WINDOW = 128
SC_SLABS = 2
SC_SLAB_W = D_MODEL // 2 // SC_SLABS


def _dispatch_rows(u2, pos_t):
    mesh = plsc.VectorSubcoreMesh(core_axis_name="core", subcore_axis_name="subcore")

    @functools.partial(pl.kernel, mesh=mesh, scratch_types=[],
                       out_type=jax.ShapeDtypeStruct((SC_SLABS, MOE_TILES * MOE_TM, SC_SLAB_W), F32))
    def dispatch(x_hbm, i_hbm, o_hbm):
        for j in range(SC_SLABS):
            def body(x_vmem, i_vmem, j=j):
                for k in range(TOP_K):
                    pltpu.sync_copy(x_vmem, o_hbm.at[j].at[i_vmem.at[k]])

            pltpu.emit_pipeline(
                body,
                grid=(N_TOK // SC_WINDOW,),
                in_specs=[pl.BlockSpec((SC_WINDOW, SC_SLAB_W), index_map=lambda i: (i, 0)),
                          pl.BlockSpec((TOP_K, SC_WINDOW), index_map=lambda i: (0, i))],
                out_specs=[],
                core_axis_name=("core", "subcore"),
                dimension_semantics=(pltpu.PARALLEL,),
            )(x_hbm.at[j], i_hbm)

    return dispatch(u2, pos_t)


def _final_kernel(yg_ref, wts_ref, x1_ref, mod_ref, modn_ref, lng_ref, lnb_ref, o_ref, un_ref, *, ctx_tiles):
    wts = wts_ref[...]
    ffn_lo = ffn_hi = None
    for k in range(TOP_K):
        lo, hi = _unpack_bf16_pair(yg_ref[k])
        w = wts[:, k:k + 1]
        ffn_lo = w * lo if ffn_lo is None else ffn_lo + w * lo
        ffn_hi = w * hi if ffn_hi is None else ffn_hi + w * hi
    ffn = jnp.concatenate([ffn_lo, ffn_hi], axis=-1)
    x2 = _layer_norm(DN_ALPHA * x1_ref[...] + mod_ref[5:6, :] * ffn, lng_ref[...], lnb_ref[...])
    if ctx_tiles is None:
        o_ref[...] = x2
        un_ref[...] = (x2 * (1.0 + modn_ref[1:2, :]) + modn_ref[0:1, :]).astype(BF16)
    else:
        @pl.when(pl.program_id(0) < ctx_tiles)
        def _():
            o_ref[...] = x2

        @pl.when(pl.program_id(0) >= ctx_tiles)
        def _():
            un_ref[...] = x2


def _final_call(yg, wts, x1, mods, ln_g, ln_b, l):
    tm = 512
    last = l == DEPTH - 1
    ln = min(l + 1, DEPTH - 1)
    ctx_tiles = N_CTX // tm
    lnspec = pl.BlockSpec((None, 1, D_MODEL), lambda i: (l, 0, 0))
    modspec = lambda lyr: pl.BlockSpec((None, None, 6, D_MODEL), lambda i: (lyr, _cond_of_row_tile(i, tm), 0, 0))
    if last:
        out_specs = [pl.BlockSpec((tm, D_MODEL), lambda i: (jnp.minimum(i, ctx_tiles - 1), 0)),
                     pl.BlockSpec((tm, D_MODEL), lambda i: (jnp.maximum(i - ctx_tiles, 0), 0))]
        out_shape = [jax.ShapeDtypeStruct((N_CTX, D_MODEL), F32), jax.ShapeDtypeStruct((N_LAT, D_MODEL), F32)]
    else:
        out_specs = [pl.BlockSpec((tm, D_MODEL), lambda i: (i, 0)), pl.BlockSpec((tm, D_MODEL), lambda i: (i, 0))]
        out_shape = [jax.ShapeDtypeStruct((N_TOK, D_MODEL), F32), jax.ShapeDtypeStruct((N_TOK, D_MODEL), BF16)]
    return pl.pallas_call(
        functools.partial(_final_kernel, ctx_tiles=ctx_tiles if last else None),
        grid=(N_TOK // tm,),
        in_specs=[pl.BlockSpec((TOP_K, tm, D_MODEL // 2), lambda i: (0, i, 0)),
                  pl.BlockSpec((tm, LANES), lambda i: (i, 0)),
                  pl.BlockSpec((tm, D_MODEL), lambda i: (i, 0)),
                  modspec(l), modspec(ln), lnspec, lnspec],
        out_specs=out_specs,
        out_shape=out_shape,
        compiler_params=_cparams(("arbitrary",)),
        name="final",
    )(yg, wts, x1, mods, mods, ln_g, ln_b)


def kernel(x_prompt, x_sample, cache_attn_k, cache_attn_v, state_rglru, state_hgrn, c, c_ctx, w_ada, b_ada, w_in, da_lambda, da_subln, rg_conv_w, rg_conv_b, rg_gate_w, rg_gate_b, rg_lambda, hg_lb, hg_norm, w_branch, w_out, ln1_g, ln1_b, router_w, router_b, w1, b1, w2, b2, ln2_g, ln2_b):
    rw, rb = _router_params(router_w, router_b)
    p = dict(hg_norm=hg_norm.reshape(DEPTH, 1, HG_VAL), w_branch=w_branch, w_out=w_out,
             ln1_g=ln1_g.reshape(DEPTH, 1, D_MODEL), ln1_b=ln1_b.reshape(DEPTH, 1, D_MODEL), rw=rw, rb=rb)
    ln2_g = ln2_g.reshape(DEPTH, 1, D_MODEL)
    ln2_b = ln2_b.reshape(DEPTH, 1, D_MODEL)
    rg_conv_b = rg_conv_b.reshape(DEPTH, 1, RG_WIDTH)
    wg, bg = _rg_gate_dense(rg_gate_w, rg_gate_b)

    x_ctx = x_prompt.reshape(N_CTX, D_MODEL)
    x_lat = x_sample.reshape(N_LAT, D_MODEL)
    cond = jnp.concatenate([c_ctx[None, :], c, jnp.zeros((N_COND - 1 - DEC_BATCH, D_MODEL), F32)], axis=0)
    mods = _adaln_all(cond, w_ada, b_ada).reshape(DEPTH, N_COND, 6, D_MODEL)

    pr = jax.nn.softmax(hg_lb.astype(F32), axis=0)
    lbs = jnp.cumsum(pr, axis=0) - pr[0]
    dl = da_lambda.astype(F32)
    lambda_init = [0.8 - 0.6 * math.exp(-0.3 * l) for l in range(DEPTH)]
    lam_all = (jnp.exp(jnp.sum(dl[:, 0] * dl[:, 1], -1)) - jnp.exp(jnp.sum(dl[:, 2] * dl[:, 3], -1))
               + jnp.asarray(lambda_init, F32))
    lam_v = jnp.broadcast_to(lam_all[:, None, None], (DEPTH, 1, LANES))
    sub_v = da_subln.reshape(DEPTH, 1, DA_V_DIM)

    cache_k = cache_attn_k.reshape(DEC_BATCH, DEPTH, PAST_LEN, BRANCH_W)
    cache_v = cache_attn_v.reshape(DEC_BATCH, DEPTH, PAST_LEN, BRANCH_W)

    b1p = b1.reshape(DEPTH, N_EXPERTS, 2 * D_EXPERT // MOE_NB, MOE_NB // 2, 2)
    b1p = jnp.swapaxes(b1p, -1, -2).reshape(DEPTH, N_EXPERTS, 1, 2 * D_EXPERT)
    b2r = b2.reshape(DEPTH, N_EXPERTS, 1, D_MODEL)

    rgs = []
    kv_all = None
    hg_all = None
    u = _modulate(x_ctx, x_lat, mods, 0)
    x_pair = (x_ctx, x_lat, 0)
    for l in range(DEPTH):
        h = _in_proj(u, w_in, l)
        att, kv_all = _attention(h, cache_k, cache_v, lam_v, sub_v, l, 1.0 - lambda_init[l], kv_all)

        rg_c, hl_c = _rglru_call(h, rg_conv_w, rg_conv_b, wg, bg, rg_lambda, None, l, BATCH, SEQ, 0)
        rg_l, _ = _rglru_call(h, rg_conv_w, rg_conv_b, wg, bg, rg_lambda, state_rglru, l,
                              DEC_BATCH, DEC_SEQ, N_CTX // DEC_SEQ)
        rgs.append(hl_c)

        of_c, ob_c, hg_all = _hgrn_call(h, lbs, None, l, BATCH, SEQ, 0, hg_all)
        of_l, ob_l, _ = _hgrn_call(h, lbs, state_hgrn, l, DEC_BATCH, DEC_SEQ, N_CTX)

        x1, u2, idx, rank, wts, cnt = _merge_call(att, (rg_c, rg_l), (of_c, of_l), (ob_c, ob_l), h, x_pair,
                                                  mods, p, l)

        pos_t, sched = _route(idx[:, :TOP_K], rank[:, :TOP_K], cnt[0, :N_EXPERTS])
        x_sorted = _dispatch_rows(u2, pos_t)
        y_sorted = _moe_call(x_sorted, sched, w1, b1p, w2, b2r, l)
        yg = y_sorted.at[pos_t.reshape(-1)].get(mode='promise_in_bounds').reshape(TOP_K, N_TOK, D_MODEL // 2)
        x, u = _final_call(yg, wts, x1, mods, ln2_g, ln2_b, l)
        x_pair = (x, x, N_CTX // MERGE_TM)

    return (x.reshape(BATCH, SEQ, D_MODEL), u.reshape(DEC_BATCH, DEC_SEQ, D_MODEL),
            kv_all[0].reshape(BATCH, DEPTH, SEQ, DA_HEADS, 2, DA_HEAD_DIM),
            kv_all[1].reshape(BATCH, DEPTH, SEQ, DA_HEADS, DA_V_DIM),
            jnp.stack(rgs, axis=1), hg_all)
```
